```python
import jax
import jax.numpy as jnp
from jax import lax
import numpy as np

D_MODEL = 1024
BATCH = 16
SEQ = 2048
DEPTH = 4

GRID_W = 64
CTX_LEN = 256
N_MIXERS = 3
NORM_EPS = 1e-6
ROPE_THETA = 10000.0
Q_BLOCK = 128

NA_HEADS = 16
NA_HEAD_DIM = D_MODEL // NA_HEADS
NA_WIN_ROWS = 8
NA_WIN_COLS = 16

SW_Q_HEADS = 16
SW_KV_HEADS = 4
SW_HEAD_DIM = D_MODEL // SW_Q_HEADS
SW_WINDOW = 128

GA_Q_HEADS = 8
GA_KV_HEADS = 4
GA_HEAD_DIM = D_MODEL // GA_Q_HEADS

MIXER_HEADS = ((NA_HEADS, NA_HEADS, NA_HEAD_DIM),
               (SW_Q_HEADS, SW_KV_HEADS, SW_HEAD_DIM),
               (GA_Q_HEADS, GA_KV_HEADS, GA_HEAD_DIM))

D_FF = 3584
N_EXPERTS = 8
TOP_K = 2
D_FF_EXPERT = 3584

kernel_name = 'hybrid_natten_swa_axial_moe_dit'


def rms_norm(x, gain):
    xf = x.astype(jnp.float32)
    y = xf * lax.rsqrt(jnp.mean(xf * xf, axis=-1, keepdims=True) + NORM_EPS)
    return (y * gain.astype(jnp.float32)).astype(x.dtype)


def ada_ln(cond, w_mod, b_mod):
    return jnp.split(jax.nn.silu(cond) @ w_mod + b_mod, 6, axis=-1)


def project_q(h, w_q, n_heads, head_dim, q_gain):
    b, t, _ = h.shape
    return rms_norm((h @ w_q).reshape(b, t, n_heads, head_dim), q_gain)


def project_kv(h, w_kv, n_kv, head_dim, k_gain):
    b, t, _ = h.shape
    kv = (h @ w_kv).reshape(b, t, 2, n_kv, head_dim)
    return rms_norm(kv[:, :, 0], k_gain), kv[:, :, 1]


def axial_rope_tables(n_tokens, head_dim):
    n_freq = head_dim // 4
    inv_freq = ROPE_THETA ** (-jnp.arange(n_freq, dtype=jnp.float32) / n_freq)
    t = jnp.arange(n_tokens)
    row = (t // GRID_W).astype(jnp.float32)
    col = (t % GRID_W).astype(jnp.float32)
    ang = jnp.concatenate([row[:, None] * inv_freq, col[:, None] * inv_freq], axis=-1)
    return jnp.cos(ang), jnp.sin(ang)


def apply_rope(x, cos, sin):
    half = x.shape[-1] // 2
    xf = x.astype(jnp.float32)
    x1, x2 = xf[..., :half], xf[..., half:]
    cs, sn = cos[:, None, :], sin[:, None, :]
    return jnp.concatenate([x1 * cs - x2 * sn, x1 * sn + x2 * cs], axis=-1).astype(x.dtype)


def context_attention(qc, kc, vc, sink):
    b, l, hq, dh = qc.shape
    hkv = kc.shape[2]
    g = hq // hkv
    q = qc.reshape(b, l, hkv, g, dh)
    logits = jnp.einsum('bqkgd,bckd->bkgqc', q, kc).astype(jnp.float32) * dh ** -0.5
    if sink is None:
        p = jax.nn.softmax(logits, axis=-1)
    else:
        sink_col = jnp.broadcast_to(sink.reshape(hkv, g)[None, :, :, None, None].astype(jnp.float32),
                                    logits.shape[:-1] + (1,))
        p = jax.nn.softmax(jnp.concatenate([logits, sink_col], axis=-1), axis=-1)[..., :-1]
    o = jnp.einsum('bkgqc,bckd->bqkgd', p.astype(vc.dtype), vc)
    return o.reshape(b, l, hq * dh)


def neighbourhood_attention(q, k, v, kc, vc, rel_bias):
    b, s, h, dh = q.shape
    rows = s // GRID_W
    win_r = min(NA_WIN_ROWS, rows)
    scale = dh ** -0.5
    qg = q.reshape(b, rows, GRID_W, h, dh)
    kg = k.reshape(b, rows, GRID_W, h, dh)
    vg = v.reshape(b, rows, GRID_W, h, dh)
    col = jnp.arange(GRID_W)
    col_start = jnp.clip(col - NA_WIN_COLS // 2, 0, GRID_W - NA_WIN_COLS)
    in_win = (col[None, :] >= col_start[:, None]) & (col[None, :] < col_start[:, None] + NA_WIN_COLS)
    col_idx = jnp.clip(col[None, :] - col[:, None] + NA_WIN_COLS - 1, 0, 2 * NA_WIN_COLS - 2)
    col_bias = rel_bias[:, :, col_idx]

    def row_block(r):
        r0 = jnp.clip(r - win_r // 2, 0, rows - win_r)
        q_r = lax.dynamic_index_in_dim(qg, r, axis=1, keepdims=False)
        k_band = lax.dynamic_slice_in_dim(kg, r0, win_r, axis=1)
        v_band = lax.dynamic_slice_in_dim(vg, r0, win_r, axis=1)
        bias = jnp.take(col_bias, r0 + jnp.arange(win_r) - r + NA_WIN_ROWS - 1, axis=1)
        s_lat = jnp.einsum('bqhd,brkhd->bhqrk', q_r, k_band).astype(jnp.float32) * scale
        s_lat = jnp.where(in_win[:, None, :], s_lat + bias.transpose(0, 2, 1, 3).astype(jnp.float32), -jnp.inf)
        s_ctx = jnp.einsum('bqhd,bchd->bhqc', q_r, kc).astype(jnp.float32) * scale
        n_lat = win_r * GRID_W
        p = jax.nn.softmax(jnp.concatenate([s_lat.reshape(b, h, GRID_W, n_lat), s_ctx], axis=-1), axis=-1)
        p = p.astype(v.dtype)
        p_lat = p[..., :n_lat].reshape(b, h, GRID_W, win_r, GRID_W)
        return (jnp.einsum('bhqrk,brkhd->bqhd', p_lat, v_band)
                + jnp.einsum('bhqc,bchd->bqhd', p[..., n_lat:], vc))

    out = lax.map(row_block, jnp.arange(rows))
    return out.transpose(1, 0, 2, 3, 4).reshape(b, s, h * dh)


def sliding_window_attention(q, k, v, kc, vc, sink):
    b, s, hq, dh = q.shape
    hkv = k.shape[2]
    g = hq // hkv
    nb = s // Q_BLOCK
    span = Q_BLOCK + 2 * SW_WINDOW
    scale = dh ** -0.5
    kp = jnp.pad(k, ((0, 0), (SW_WINDOW, SW_WINDOW), (0, 0), (0, 0)))
    vp = jnp.pad(v, ((0, 0), (SW_WINDOW, SW_WINDOW), (0, 0), (0, 0)))
    qb = q.reshape(b, nb, Q_BLOCK, hkv, g, dh)
    rel = jnp.arange(span)[None, :] - SW_WINDOW - jnp.arange(Q_BLOCK)[:, None]
    in_band = jnp.abs(rel) <= SW_WINDOW
    sink_col = sink.reshape(hkv, g)[None, :, :, None, None].astype(jnp.float32)

    def block(i):
        q_i = lax.dynamic_index_in_dim(qb, i, axis=1, keepdims=False)
        k_i = lax.dynamic_slice_in_dim(kp, i * Q_BLOCK, span, axis=1)
        v_i = lax.dynamic_slice_in_dim(vp, i * Q_BLOCK, span, axis=1)
        key_pos = i * Q_BLOCK - SW_WINDOW + jnp.arange(span)
        valid = in_band & ((key_pos >= 0) & (key_pos < s))[None, :]
        s_lat = jnp.einsum('bqkgd,bskd->bkgqs', q_i, k_i).astype(jnp.float32) * scale
        s_lat = jnp.where(valid, s_lat, -jnp.inf)
        s_ctx = jnp.einsum('bqkgd,bckd->bkgqc', q_i, kc).astype(jnp.float32) * scale
        logits = jnp.concatenate([s_lat, s_ctx, jnp.broadcast_to(sink_col, s_lat.shape[:-1] + (1,))], axis=-1)
        p = jax.nn.softmax(logits, axis=-1).astype(v.dtype)
        o = (jnp.einsum('bkgqs,bskd->bqkgd', p[..., :span], v_i)
             + jnp.einsum('bkgqc,bckd->bqkgd', p[..., span:-1], vc))
        return o.reshape(b, Q_BLOCK, hq * dh)

    out = lax.map(block, jnp.arange(nb))
    return out.transpose(1, 0, 2, 3).reshape(b, s, hq * dh)


def blocked_global_attention(q, k, v, kc, vc):
    b, s, hq, dh = q.shape
    hkv = k.shape[2]
    g = hq // hkv
    nb = s // Q_BLOCK
    scale = dh ** -0.5
    k_all = jnp.concatenate([k, kc], axis=1)
    v_all = jnp.concatenate([v, vc], axis=1)
    qb = q.reshape(b, nb, Q_BLOCK, hkv, g, dh).swapaxes(0, 1)

    def block(q_i):
        logits = jnp.einsum('bqkgd,bskd->bkgqs', q_i, k_all).astype(jnp.float32) * scale
        p = jax.nn.softmax(logits, axis=-1).astype(v.dtype)
        return jnp.einsum('bkgqs,bskd->bqkgd', p, v_all).reshape(b, Q_BLOCK, hq * dh)

    out = lax.map(block, qb)
    return out.transpose(1, 0, 2, 3).reshape(b, s, hq * dh)


def swiglu(h, w_in, w_out):
    gate, up = jnp.split(h @ w_in, 2, axis=-1)
    return (jax.nn.silu(gate) * up) @ w_out


def moe_swiglu(h, w_router, b_router, w_exp_in, w_exp_out):
    logits = (h @ w_router).astype(jnp.float32) + b_router.astype(jnp.float32)
    top_val, top_idx = lax.top_k(logits, TOP_K)
    top_w = jax.nn.softmax(top_val, axis=-1)
    gates = jnp.sum(jax.nn.one_hot(top_idx, N_EXPERTS, dtype=jnp.float32) * top_w[..., None], axis=-2)
    gates = gates.astype(h.dtype)
    out = jnp.zeros_like(h)
    for e in range(N_EXPERTS):
        out = out + gates[..., e:e + 1] * swiglu(h, w_exp_in[e], w_exp_out[e])
    return out


def channel_mixer(h, ffn):
    return swiglu(h, *ffn) if len(ffn) == 2 else moe_swiglu(h, *ffn)


def hybrid_layer(x, ctx, c, c_ctx, w_mod, b_mod, norm1, norm2, w_qkv, q_norm, k_norm,
                 mixer_param, w_o, ffn, mixer, last):
    s = x.shape[1]
    n_q, n_kv, dh = MIXER_HEADS[mixer]
    w_q, w_kv = w_qkv[:, :n_q * dh], w_qkv[:, n_q * dh:]
    sh1, sc1, g1, sh2, sc2, g2 = (m[:, None, :] for m in ada_ln(c, w_mod, b_mod))
    csh1, csc1, cg1, csh2, csc2, cg2 = ada_ln(c_ctx, w_mod, b_mod)
    h = rms_norm(x, norm1) * (1 + sc1) + sh1
    hc = rms_norm(ctx, norm1) * (1 + csc1) + csh1
    q = project_q(h, w_q, n_q, dh, q_norm)
    k, v = project_kv(h, w_kv, n_kv, dh, k_norm)
    kc, vc = project_kv(hc, w_kv, n_kv, dh, k_norm)
    if mixer == 0:
        y = neighbourhood_attention(q, k, v, kc, vc, mixer_param)
    else:
        cos, sin = axial_rope_tables(s, dh)
        q, k = apply_rope(q, cos, sin), apply_rope(k, cos, sin)
        if mixer == 1:
            y = sliding_window_attention(q, k, v, kc, vc, mixer_param)
        else:
            y = blocked_global_attention(q, k, v, kc, vc)
    x = x + g1 * (y @ w_o)
    x = x + g2 * channel_mixer(rms_norm(x, norm2) * (1 + sc2) + sh2, ffn)
    if last:
        return x, ctx
    qc = project_q(hc, w_q, n_q, dh, q_norm)
    yc = context_attention(qc, kc, vc, mixer_param if mixer == 1 else None)
    ctx = ctx + cg1 * (yc @ w_o)
    ctx = ctx + cg2 * channel_mixer(rms_norm(ctx, norm2) * (1 + csc2) + csh2, ffn)
    return x, ctx


def setup_inputs(seed: int = 0) -> dict:
    key = jax.random.key(seed)
    k_x, k_c, k_ctx, k_cctx, k_layers = jax.random.split(key, 5)
    normal = jax.random.normal
    f32 = jnp.float32
    inputs = {
        'x': normal(k_x, (BATCH, SEQ, D_MODEL), f32),
        'c': normal(k_c, (BATCH, D_MODEL), f32),
        'ctx': normal(k_ctx, (BATCH, CTX_LEN, D_MODEL), f32),
        'c_ctx': normal(k_cctx, (D_MODEL,), f32),
    }
    layer_keys = jax.random.split(k_layers, DEPTH)
    for i in range(DEPTH):
        mixer = i % N_MIXERS
        n_q, n_kv, dh = MIXER_HEADS[mixer]
        ks = jax.random.split(layer_keys[i], 14)
        p = 'l%d_' % i
        inputs[p + 'w_mod'] = normal(ks[0], (D_MODEL, 6 * D_MODEL), f32) * (0.5 * D_MODEL ** -0.5)
        inputs[p + 'b_mod'] = 0.01 * normal(ks[1], (6 * D_MODEL,), f32)
        inputs[p + 'norm1'] = 1.0 + 0.02 * normal(ks[2], (D_MODEL,), f32)
        inputs[p + 'norm2'] = 1.0 + 0.02 * normal(ks[3], (D_MODEL,), f32)
        inputs[p + 'w_qkv'] = normal(ks[4], (D_MODEL, (n_q + 2 * n_kv) * dh), f32) * D_MODEL ** -0.5
        inputs[p + 'q_norm'] = 1.0 + 0.02 * normal(ks[5], (dh,), f32)
        inputs[p + 'k_norm'] = 1.0 + 0.02 * normal(ks[6], (dh,), f32)
        if mixer == 0:
            inputs[p + 'rel_bias'] = 0.1 * normal(ks[7], (NA_HEADS, 2 * NA_WIN_ROWS - 1, 2 * NA_WIN_COLS - 1), f32)
        elif mixer == 1:
            inputs[p + 'sink'] = 0.5 * normal(ks[7], (SW_Q_HEADS,), f32)
        inputs[p + 'w_o'] = normal(ks[8], (n_q * dh, D_MODEL), f32) * (n_q * dh) ** -0.5
        if i % 2 == 0:
            inputs[p + 'w_ffn_in'] = normal(ks[9], (D_MODEL, 2 * D_FF), f32) * D_MODEL ** -0.5
            inputs[p + 'w_ffn_out'] = normal(ks[10], (D_FF, D_MODEL), f32) * D_FF ** -0.5
        else:
            inputs[p + 'w_router'] = normal(ks[9], (D_MODEL, N_EXPERTS), f32) * D_MODEL ** -0.5
            inputs[p + 'b_router'] = 0.01 * normal(ks[10], (N_EXPERTS,), f32)
            inputs[p + 'w_exp_in'] = normal(ks[11], (N_EXPERTS, D_MODEL, 2 * D_FF_EXPERT), f32) * D_MODEL ** -0.5
            inputs[p + 'w_exp_out'] = normal(ks[12], (N_EXPERTS, D_FF_EXPERT, D_MODEL), f32) * D_FF_EXPERT ** -0.5
    return inputs


def reference(x, c, ctx, c_ctx,
              l0_w_mod, l0_b_mod, l0_norm1, l0_norm2, l0_w_qkv, l0_q_norm, l0_k_norm, l0_rel_bias, l0_w_o,
              l0_w_ffn_in, l0_w_ffn_out,
              l1_w_mod, l1_b_mod, l1_norm1, l1_norm2, l1_w_qkv, l1_q_norm, l1_k_norm, l1_sink, l1_w_o,
              l1_w_router, l1_b_router, l1_w_exp_in, l1_w_exp_out,
              l2_w_mod, l2_b_mod, l2_norm1, l2_norm2, l2_w_qkv, l2_q_norm, l2_k_norm, l2_w_o,
              l2_w_ffn_in, l2_w_ffn_out,
              l3_w_mod, l3_b_mod, l3_norm1, l3_norm2, l3_w_qkv, l3_q_norm, l3_k_norm, l3_rel_bias, l3_w_o,
              l3_w_router, l3_b_router, l3_w_exp_in, l3_w_exp_out):
    layers = (
        (l0_w_mod, l0_b_mod, l0_norm1, l0_norm2, l0_w_qkv, l0_q_norm, l0_k_norm, l0_rel_bias, l0_w_o,
         (l0_w_ffn_in, l0_w_ffn_out)),
        (l1_w_mod, l1_b_mod, l1_norm1, l1_norm2, l1_w_qkv, l1_q_norm, l1_k_norm, l1_sink, l1_w_o,
         (l1_w_router, l1_b_router, l1_w_exp_in, l1_w_exp_out)),
        (l2_w_mod, l2_b_mod, l2_norm1, l2_norm2, l2_w_qkv, l2_q_norm, l2_k_norm, None, l2_w_o,
         (l2_w_ffn_in, l2_w_ffn_out)),
        (l3_w_mod, l3_b_mod, l3_norm1, l3_norm2, l3_w_qkv, l3_q_norm, l3_k_norm, l3_rel_bias, l3_w_o,
         (l3_w_router, l3_b_router, l3_w_exp_in, l3_w_exp_out)),
    )
    for i in range(DEPTH):
        x, ctx = hybrid_layer(x, ctx, c, c_ctx, *layers[i], mixer=i % N_MIXERS, last=(i == DEPTH - 1))
    return x
```

```python
import functools

import jax
import jax.numpy as jnp
from jax import lax
from jax.experimental import pallas as pl
from jax.experimental.pallas import tpu as pltpu

F32 = jnp.float32
BF16 = jnp.bfloat16

GRID_W = 64
NORM_EPS = 1e-6
ROPE_THETA = 10000.0
NA_WIN_ROWS = 8
NA_WIN_COLS = 16
SW_WINDOW = 128
N_EXPERTS = 8
MIXER_HEADS = ((16, 16, 64), (16, 4, 64), (8, 4, 128))

LANES = 128
VMEM_LIMIT_BYTES = 56 * 1024 * 1024
NEG_BIG = -1e30

SWA_TQ = 128
GLB_TQ = 256
MOE_TM = 1024
FFN_TF = 512


def _params(*sem):
    return pltpu.CompilerParams(dimension_semantics=sem, vmem_limit_bytes=VMEM_LIMIT_BYTES)


def _row_tile(n_lat_per_sample, n_ctx_rows, cap):
    for tm in (1024, 512, 256, 128):
        if tm <= cap and n_lat_per_sample % tm == 0 and n_ctx_rows % tm == 0:
            return tm
    raise ValueError("no row tile fits")


def _split_bf16(a):
    hi = a.astype(BF16)
    lo = (a - hi.astype(F32)).astype(BF16)
    return hi, lo


def _dot(a, b):
    return jnp.dot(a, b, preferred_element_type=F32)


def _silu(g):
    return g / (1.0 + jnp.exp(-g))


def _rms_mod(x, gain, scale, shift):
    ms = jnp.mean(x * x, axis=-1, keepdims=True)
    return x * lax.rsqrt(ms + NORM_EPS) * gain * (1.0 + scale) + shift


def _adaln_kernel(c_ref, w_ref, b_ref, o_ref):
    a_hi, a_lo = _split_bf16(_silu(c_ref[...]))
    w_hi, w_lo = _split_bf16(w_ref[...])
    o_ref[...] = _dot(a_hi, w_hi) + _dot(a_hi, w_lo) + _dot(a_lo, w_hi) + b_ref[...]


def _adaln(cond, w_mod, b_mod):
    r, d = cond.shape
    n = w_mod.shape[1]
    tn = 1536
    out = pl.pallas_call(
        _adaln_kernel,
        grid=(n // tn,),
        in_specs=[pl.BlockSpec((r, d), lambda j: (0, 0)),
                  pl.BlockSpec((d, tn), lambda j: (0, j)),
                  pl.BlockSpec((1, tn), lambda j: (0, j))],
        out_specs=pl.BlockSpec((r, tn), lambda j: (0, j)),
        out_shape=jax.ShapeDtypeStruct((r, n), F32),
        compiler_params=_params("arbitrary"),
        name="adaln",
    )(cond, w_mod, b_mod.reshape(1, n))
    return out.reshape(r, 6, d)


def _rot_half(z, dh):
    if dh == LANES:
        return pltpu.roll(z, LANES // 2, axis=1)
    lane = lax.broadcasted_iota(jnp.int32, z.shape, 1)
    from_right = pltpu.roll(z, LANES - dh // 2, axis=1)
    from_left = pltpu.roll(z, dh // 2, axis=1)
    return jnp.where((lane % dh) < dh // 2, from_right, from_left)


def _qkv_kernel(*refs, n_q, n_kv, dh, rope):
    if rope:
        x_ref, mod_ref, n1_ref, w_ref, g_ref, bd_ref, cos_ref, sin_ref, q_ref, k_ref, v_ref = refs
    else:
        x_ref, mod_ref, n1_ref, w_ref, g_ref, bd_ref, q_ref, k_ref, v_ref = refs
    h = _rms_mod(x_ref[...], n1_ref[...], mod_ref[1:2, :], mod_ref[0:1, :]).astype(BF16)
    bd = bd_ref[...]
    cw = 2 * LANES
    for c in range((n_q + n_kv) // cw):
        y = _dot(h, w_ref[:, c * cw:(c + 1) * cw])
        hi, lo = _split_bf16(y * y)
        ss = _dot(hi, bd) + _dot(lo, bd)
        y = y * lax.rsqrt(ss * (1.0 / dh) + NORM_EPS) * g_ref[:, c * cw:(c + 1) * cw]
        if rope:
            halves = []
            for s in range(2):
                z = y[:, s * LANES:(s + 1) * LANES]
                halves.append(z * cos_ref[...] + _rot_half(z, dh) * sin_ref[...])
            y = jnp.concatenate(halves, axis=1)
        y = y.astype(BF16)
        if c * cw < n_q:
            q_ref[:, c * cw:(c + 1) * cw] = y
        else:
            k_ref[:, c * cw - n_q:(c + 1) * cw - n_q] = y
    for c in range(n_kv // cw):
        lo_col = n_q + n_kv + c * cw
        v_ref[:, c * cw:(c + 1) * cw] = _dot(h, w_ref[:, lo_col:lo_col + cw]).astype(BF16)


def _rope_tables(s_len, dh, tm):
    n_freq = dh // 4
    inv_freq = ROPE_THETA ** (-jnp.arange(n_freq, dtype=F32) / n_freq)
    t = jnp.arange(s_len)
    row = (t // GRID_W).astype(F32)
    col = (t % GRID_W).astype(F32)
    ang = jnp.concatenate([row[:, None] * inv_freq, col[:, None] * inv_freq], axis=-1)
    cos, sin = jnp.cos(ang), jnp.sin(ang)
    reps = LANES // dh
    cos_t = jnp.tile(jnp.concatenate([cos, cos], axis=-1), (1, reps))
    sin_t = jnp.tile(jnp.concatenate([-sin, sin], axis=-1), (1, reps))
    cos_t = jnp.concatenate([cos_t, jnp.ones((tm, LANES), F32)], axis=0)
    sin_t = jnp.concatenate([sin_t, jnp.zeros((tm, LANES), F32)], axis=0)
    return cos_t, sin_t


def _qkv(xs, mod, norm1, w_qkv, q_gain, k_gain, dims, mixer):
    b, s_len, n_ctx = dims
    n_rows, d = xs.shape
    n_qh, n_kvh, dh = MIXER_HEADS[mixer]
    n_q, n_kv = n_qh * dh, n_kvh * dh
    rope = mixer != 0
    tm = _row_tile(s_len, b * n_ctx, 512)
    n_lat_tiles = b * s_len // tm
    tiles_per_sample = s_len // tm

    gains = jnp.concatenate([jnp.tile(q_gain * dh ** -0.5, n_qh), jnp.tile(k_gain, n_kvh)]).reshape(1, n_q + n_kv)
    hd = jnp.arange(2 * LANES) // dh
    bd = (hd[:, None] == hd[None, :]).astype(BF16)

    def mod_idx(i):
        return jnp.where(i < n_lat_tiles, i // tiles_per_sample, b)

    in_specs = [pl.BlockSpec((tm, d), lambda i: (i, 0)),
                pl.BlockSpec((None, 6, d), lambda i: (mod_idx(i), 0, 0)),
                pl.BlockSpec((1, d), lambda i: (0, 0)),
                pl.BlockSpec(w_qkv.shape, lambda i: (0, 0)),
                pl.BlockSpec(gains.shape, lambda i: (0, 0)),
                pl.BlockSpec(bd.shape, lambda i: (0, 0))]
    args = [xs, mod, norm1.reshape(1, d), w_qkv.astype(BF16), gains, bd]
    if rope:
        cos_t, sin_t = _rope_tables(s_len, dh, tm)

        def pos_idx(i):
            return jnp.where(i < n_lat_tiles, i % tiles_per_sample, tiles_per_sample)

        in_specs += [pl.BlockSpec((tm, LANES), lambda i: (pos_idx(i), 0))] * 2
        args += [cos_t, sin_t]
    return pl.pallas_call(
        functools.partial(_qkv_kernel, n_q=n_q, n_kv=n_kv, dh=dh, rope=rope),
        grid=(n_rows // tm,),
        in_specs=in_specs,
        out_specs=[pl.BlockSpec((tm, n_q), lambda i: (i, 0)),
                   pl.BlockSpec((tm, n_kv), lambda i: (i, 0)),
                   pl.BlockSpec((tm, n_kv), lambda i: (i, 0))],
        out_shape=[jax.ShapeDtypeStruct((n_rows, n_q), BF16),
                   jax.ShapeDtypeStruct((n_rows, n_kv), BF16),
                   jax.ShapeDtypeStruct((n_rows, n_kv), BF16)],
        compiler_params=_params("arbitrary"),
        name="qkv_m%d" % mixer,
    )(*args)


def _stack_heads(q, dh, nh):
    if dh == LANES:
        return jnp.concatenate([q[:, h * LANES:(h + 1) * LANES] for h in range(nh)], axis=0)
    per_kv = nh // 2
    tq = q.shape[0]
    lane = lax.broadcasted_iota(jnp.int32, (tq, LANES), 1)
    ops = []
    for h in range(nh):
        slot = h // per_kv
        chunk = q[:, (h // 2) * LANES:(h // 2 + 1) * LANES].astype(F32)
        if h % 2 != slot:
            chunk = pltpu.roll(chunk, dh, axis=1)
        keep = lane < dh if slot == 0 else lane >= dh
        ops.append(jnp.where(keep, chunk, 0.0).astype(BF16))
    return jnp.concatenate(ops, axis=0)


def _unstack_heads(o, dh, nh):
    tq = o.shape[0] // nh
    if dh == LANES:
        return jnp.concatenate([o[h * tq:(h + 1) * tq] for h in range(nh)], axis=1)
    per_kv = nh // 2
    lane = lax.broadcasted_iota(jnp.int32, (tq, LANES), 1)
    chunks = []
    for c in range(nh // 2):
        parts = []
        for h in (2 * c, 2 * c + 1):
            oh = o[h * tq:(h + 1) * tq]
            if h % 2 != h // per_kv:
                oh = pltpu.roll(oh, dh, axis=1)
            parts.append(oh)
        chunks.append(jnp.where(lane < dh, parts[0], parts[1]))
    return jnp.concatenate(chunks, axis=1)


def _attend(qs, segments, sink_col=None):
    logits = []
    for k, _, bias in segments:
        s = lax.dot_general(qs, k, (((1,), (1,)), ((), ())), preferred_element_type=F32)
        logits.append(s if bias is None else s + bias)
    m = functools.reduce(jnp.maximum, [jnp.max(s, axis=-1, keepdims=True) for s in logits])
    if sink_col is not None:
        m = jnp.maximum(m, sink_col)
    denom = None
    acc = None
    for s, (_, v, _) in zip(logits, segments):
        p = jnp.exp(s - m)
        part = jnp.sum(p, axis=-1, keepdims=True)
        pv = _dot(p.astype(BF16), v)
        denom = part if denom is None else denom + part
        acc = pv if acc is None else acc + pv
    if sink_col is not None:
        denom = denom + jnp.exp(sink_col - m)
    return acc / denom


def _sink_column(sink_ref, first_head, nh, tq):
    return jnp.concatenate([jnp.full((tq, 1), sink_ref[first_head + h], F32) for h in range(nh)], axis=0)


def _na_kernel(q_ref, k_ref, v_ref, kc_ref, vc_ref, bias_ref, o_ref, *, rows):
    kc, vc = kc_ref[...], vc_ref[...]
    band = NA_WIN_ROWS * GRID_W

    def one_row(r, carry):
        r0 = jnp.clip(r - NA_WIN_ROWS // 2, 0, rows - NA_WIN_ROWS)
        q0 = pl.multiple_of(r * GRID_W, GRID_W)
        k0 = pl.multiple_of(r0 * GRID_W, GRID_W)
        qs = _stack_heads(q_ref[pl.ds(q0, GRID_W), :], 64, 2)
        bias = bias_ref[r0 - r + NA_WIN_ROWS - 1].reshape(2 * GRID_W, band)
        o = _attend(qs, [(k_ref[pl.ds(k0, band), :], v_ref[pl.ds(k0, band), :], bias), (kc, vc, None)])
        o_ref[pl.ds(q0, GRID_W), :] = _unstack_heads(o, 64, 2).astype(BF16)
        return carry

    lax.fori_loop(0, rows, one_row, 0)


def _na_bias_tables(rel_bias):
    n_h = rel_bias.shape[0]
    col = jnp.arange(GRID_W)
    col_start = jnp.clip(col - NA_WIN_COLS // 2, 0, GRID_W - NA_WIN_COLS)
    in_win = (col[None, :] >= col_start[:, None]) & (col[None, :] < col_start[:, None] + NA_WIN_COLS)
    col_idx = jnp.clip(col[None, :] - col[:, None] + NA_WIN_COLS - 1, 0, 2 * NA_WIN_COLS - 2)
    masked = jnp.where(in_win[None, None], rel_bias[:, :, col_idx], NEG_BIG)
    variants = jnp.stack([masked[:, j:j + NA_WIN_ROWS] for j in range(NA_WIN_ROWS)], axis=1)
    variants = variants.transpose(0, 1, 3, 2, 4).reshape(n_h, NA_WIN_ROWS, GRID_W, NA_WIN_ROWS * GRID_W)
    return variants.reshape(n_h // 2, 2, NA_WIN_ROWS, GRID_W, NA_WIN_ROWS * GRID_W).transpose(0, 2, 1, 3, 4)


def _na_attention(q, k, v, rel_bias, dims):
    b, s_len, n_ctx = dims
    rows = s_len // GRID_W
    n_pairs = q.shape[1] // LANES
    bias = _na_bias_tables(rel_bias)
    ctx_blk = b * s_len // n_ctx
    lat = pl.BlockSpec((s_len, LANES), lambda p, i: (i, p))
    ctx = pl.BlockSpec((n_ctx, LANES), lambda p, i: (ctx_blk + i, p))
    return pl.pallas_call(
        functools.partial(_na_kernel, rows=rows),
        grid=(n_pairs, b),
        in_specs=[lat, lat, lat, ctx, ctx,
                  pl.BlockSpec((None,) + bias.shape[1:], lambda p, i: (p, 0, 0, 0, 0))],
        out_specs=lat,
        out_shape=jax.ShapeDtypeStruct((b * s_len, q.shape[1]), BF16),
        compiler_params=_params("arbitrary", "arbitrary"),
        name="attn_na",
    )(q, k, v, k, v, bias)


def _swa_kernel(sink_ref, q_ref, k_ref, v_ref, kc_ref, vc_ref, band_ref, o_ref, *, s_len, nh):
    tq = q_ref.shape[0]
    span = tq + 2 * SW_WINDOW
    p, t = pl.program_id(1), pl.program_id(2)
    k0 = pl.multiple_of(jnp.clip(t * tq - SW_WINDOW, 0, s_len - span), LANES)
    qs = _stack_heads(q_ref[...], 64, nh)
    bias = jnp.concatenate([band_ref[...]] * nh, axis=0)
    o = _attend(qs, [(k_ref[pl.ds(k0, span), :], v_ref[pl.ds(k0, span), :], bias),
                     (kc_ref[...], vc_ref[...], None)],
                sink_col=_sink_column(sink_ref, p * nh, nh, tq))
    o_ref[...] = _unstack_heads(o, 64, nh).astype(BF16)


def _swa_band_tables(tq, s_len):
    span = tq + 2 * SW_WINDOW
    rowq = jnp.arange(tq)[:, None]
    colk = jnp.arange(span)[None, :]
    tables = []
    for shift in (0, SW_WINDOW, 2 * SW_WINDOW):
        rel = colk - shift - rowq
        tables.append(jnp.where(jnp.abs(rel) <= SW_WINDOW, 0.0, NEG_BIG))
    return jnp.stack(tables).astype(F32)


def _swa_attention(q, k, v, sink, dims):
    b, s_len, n_ctx = dims
    tq = SWA_TQ
    n_t = s_len // tq
    n_kvblk = k.shape[1] // LANES
    nh = q.shape[1] // k.shape[1] * 2
    qw = nh * 64
    band = _swa_band_tables(tq, s_len)
    ctx_blk = b * s_len // n_ctx
    qspec = pl.BlockSpec((tq, qw), lambda i, p, t, *_: (i * n_t + t, p))
    lat = pl.BlockSpec((s_len, LANES), lambda i, p, t, *_: (i, p))
    ctx = pl.BlockSpec((n_ctx, LANES), lambda i, p, t, *_: (ctx_blk + i, p))

    def variant(i, p, t, *_):
        return (jnp.where(t == 0, 0, jnp.where(t == n_t - 1, 2, 1)), 0, 0)

    return pl.pallas_call(
        functools.partial(_swa_kernel, s_len=s_len, nh=nh),
        grid_spec=pltpu.PrefetchScalarGridSpec(
            num_scalar_prefetch=1,
            grid=(b, n_kvblk, n_t),
            in_specs=[qspec, lat, lat, ctx, ctx, pl.BlockSpec((None,) + band.shape[1:], variant)],
            out_specs=qspec),
        out_shape=jax.ShapeDtypeStruct((b * s_len, q.shape[1]), BF16),
        compiler_params=_params("arbitrary", "arbitrary", "arbitrary"),
        name="attn_swa",
    )(sink, q, k, v, k, v, band)


def _global_kernel(q_ref, k_ref, v_ref, kc_ref, vc_ref, o_ref, *, nh):
    qs = _stack_heads(q_ref[...], LANES, nh)
    o = _attend(qs, [(k_ref[...], v_ref[...], None), (kc_ref[...], vc_ref[...], None)])
    o_ref[...] = _unstack_heads(o, LANES, nh).astype(BF16)


def _global_attention(q, k, v, dims):
    b, s_len, n_ctx = dims
    tq = GLB_TQ
    n_t = s_len // tq
    n_kvblk = k.shape[1] // LANES
    nh = q.shape[1] // k.shape[1]
    qw = nh * LANES
    ctx_blk = b * s_len // n_ctx
    qspec = pl.BlockSpec((tq, qw), lambda i, p, t: (i * n_t + t, p))
    lat = pl.BlockSpec((s_len, LANES), lambda i, p, t: (i, p))
    ctx = pl.BlockSpec((n_ctx, LANES), lambda i, p, t: (ctx_blk + i, p))
    return pl.pallas_call(
        functools.partial(_global_kernel, nh=nh),
        grid=(b, n_kvblk, n_t),
        in_specs=[qspec, lat, lat, ctx, ctx],
        out_specs=qspec,
        out_shape=jax.ShapeDtypeStruct((b * s_len, q.shape[1]), BF16),
        compiler_params=_params("arbitrary", "arbitrary", "arbitrary"),
        name="attn_global",
    )(q, k, v, k, v)


def _ctx_kernel(sink_ref, q_ref, kc_ref, vc_ref, o_ref, *, dh, nh, use_sink):
    tq = q_ref.shape[0]
    qs = _stack_heads(q_ref[...], dh, nh)
    sink_col = _sink_column(sink_ref, pl.program_id(1) * nh, nh, tq) if use_sink else None
    o = _attend(qs, [(kc_ref[...], vc_ref[...], None)], sink_col=sink_col)
    o_ref[...] = _unstack_heads(o, dh, nh).astype(BF16)


def _ctx_attention(q, k, v, sink, dims, mixer):
    b, s_len, n_ctx = dims
    n_qh, n_kvh, dh = MIXER_HEADS[mixer]
    n_kvblk = k.shape[1] // LANES
    nh = n_qh // n_kvblk
    qw = nh * dh
    ctx_blk = b * s_len // n_ctx
    use_sink = sink is not None
    if not use_sink:
        sink = jnp.zeros((n_qh,), F32)
    return pl.pallas_call(
        functools.partial(_ctx_kernel, dh=dh, nh=nh, use_sink=use_sink),
        grid_spec=pltpu.PrefetchScalarGridSpec(
            num_scalar_prefetch=1,
            grid=(b, n_kvblk),
            in_specs=[pl.BlockSpec((n_ctx, qw), lambda i, p, *_: (ctx_blk + i, p)),
                      pl.BlockSpec((n_ctx, LANES), lambda i, p, *_: (ctx_blk + i, p)),
                      pl.BlockSpec((n_ctx, LANES), lambda i, p, *_: (ctx_blk + i, p))],
            out_specs=pl.BlockSpec((n_ctx, qw), lambda i, p, *_: (i, p))),
        out_shape=jax.ShapeDtypeStruct((b * n_ctx, q.shape[1]), BF16),
        compiler_params=_params("arbitrary", "arbitrary"),
        name="attn_ctx_m%d" % mixer,
    )(sink, q, k, v)


def _oproj_kernel(*refs, n_lat_tiles, with_ctx, moe):
    refs = list(refs)
    ylat_ref = refs.pop(0)
    yctx_ref = refs.pop(0) if with_ctx else None
    x_ref, mod_ref, n2_ref, wo_ref = refs[:4]
    refs = refs[4:]
    if moe:
        wr_ref, br_ref = refs[:2]
        refs = refs[2:]
    xo_ref, h_ref = refs[:2]
    y = ylat_ref[...]
    if with_ctx:
        y = jnp.where(pl.program_id(0) < n_lat_tiles, y, yctx_ref[...])
    x = x_ref[...] + mod_ref[2:3, :] * _dot(y, wo_ref[...])
    xo_ref[...] = x
    h = _rms_mod(x, n2_ref[...], mod_ref[4:5, :], mod_ref[3:4, :])
    h_ref[...] = h.astype(BF16)
    if moe:
        route_ref = refs[2]
        h_hi, h_lo = _split_bf16(h)
        w_hi, w_lo = _split_bf16(wr_ref[...])
        logits = _dot(h_hi, w_hi) + _dot(h_hi, w_lo) + _dot(h_lo, w_hi) + br_ref[...]
        lane = lax.broadcasted_iota(jnp.int32, logits.shape, 1)
        v1 = jnp.max(logits, axis=-1, keepdims=True)
        i1 = jnp.min(jnp.where(logits == v1, lane, LANES), axis=-1, keepdims=True)
        rest = jnp.where(lane == i1, NEG_BIG, logits)
        v2 = jnp.max(rest, axis=-1, keepdims=True)
        i2 = jnp.min(jnp.where(rest == v2, lane, LANES), axis=-1, keepdims=True)
        e = jnp.exp(v2 - v1)
        w1 = 1.0 / (1.0 + e)
        w2 = e / (1.0 + e)
        route = jnp.where(lane == 0, i1.astype(F32), 0.0)
        route = jnp.where(lane == 1, i2.astype(F32), route)
        route = jnp.where(lane == 2, w1, route)
        route = jnp.where(lane == 3, w2, route)
        route_ref[...] = route


def _oproj(y_lat, y_ctx, xs, mod, norm2, w_o, router, dims, last):
    b, s_len, n_ctx = dims
    d = xs.shape[1]
    tm = _row_tile(s_len, b * n_ctx, 512)
    n_lat_tiles = b * s_len // tm
    tiles_per_sample = s_len // tm
    with_ctx = not last
    n_rows = b * s_len if last else xs.shape[0]
    n_tiles = n_rows // tm
    moe = router is not None

    def mod_idx(i):
        return jnp.where(i < n_lat_tiles, i // tiles_per_sample, b)

    row = pl.BlockSpec((tm, d), lambda i: (i, 0))
    in_specs = [pl.BlockSpec((tm, y_lat.shape[1]), lambda i: (jnp.minimum(i, n_lat_tiles - 1), 0))]
    args = [y_lat]
    if with_ctx:
        in_specs.append(pl.BlockSpec((tm, y_ctx.shape[1]), lambda i: (jnp.maximum(i - n_lat_tiles, 0), 0)))
        args.append(y_ctx)
    in_specs += [row,
                 pl.BlockSpec((None, 6, d), lambda i: (mod_idx(i), 0, 0)),
                 pl.BlockSpec((1, d), lambda i: (0, 0)),
                 pl.BlockSpec(w_o.shape, lambda i: (0, 0))]
    args += [xs, mod, norm2.reshape(1, d), w_o.astype(BF16)]
    out_specs = [row, row]
    out_shape = [jax.ShapeDtypeStruct((n_rows, d), F32), jax.ShapeDtypeStruct((n_rows, d), BF16)]
    if moe:
        w_router, b_router = router
        n_e = w_router.shape[1]
        wr = jnp.pad(w_router, ((0, 0), (0, LANES - n_e)))
        br = jnp.pad(b_router, (0, LANES - n_e), constant_values=NEG_BIG).reshape(1, LANES)
        in_specs += [pl.BlockSpec(wr.shape, lambda i: (0, 0)), pl.BlockSpec(br.shape, lambda i: (0, 0))]
        args += [wr, br]
        out_specs.append(pl.BlockSpec((tm, LANES), lambda i: (i, 0)))
        out_shape.append(jax.ShapeDtypeStruct((n_rows, LANES), F32))
    return pl.pallas_call(
        functools.partial(_oproj_kernel, n_lat_tiles=n_lat_tiles, with_ctx=with_ctx, moe=moe),
        grid=(n_tiles,),
        in_specs=in_specs,
        out_specs=out_specs,
        out_shape=out_shape,
        compiler_params=_params("arbitrary"),
        name="oproj_moe" if moe else "oproj",
    )(*args)


def _ffn_kernel(h_ref, wg_ref, wu_ref, wd_ref, x_ref, mod_ref, o_ref, acc_ref):
    f = pl.program_id(1)

    @pl.when(f == 0)
    def _():
        acc_ref[...] = jnp.zeros_like(acc_ref)

    h = h_ref[...]
    a = _silu(_dot(h, wg_ref[...])) * _dot(h, wu_ref[...])
    acc_ref[...] += _dot(a.astype(BF16), wd_ref[...])

    @pl.when(f == pl.num_programs(1) - 1)
    def _():
        o_ref[...] = x_ref[...] + mod_ref[5:6, :] * acc_ref[...]


def _ffn(h, xs, mod, w_in, w_out, dims):
    b, s_len, n_ctx = dims
    n_rows, d = h.shape
    d_ff = w_out.shape[0]
    tm = _row_tile(s_len, b * n_ctx, 1024)
    tf = FFN_TF
    n_f = d_ff // tf
    n_lat_tiles = b * s_len // tm
    tiles_per_sample = s_len // tm

    def mod_idx(i):
        return jnp.where(i < n_lat_tiles, i // tiles_per_sample, b)

    w_in = w_in.astype(BF16)
    return pl.pallas_call(
        _ffn_kernel,
        grid=(n_rows // tm, n_f),
        in_specs=[pl.BlockSpec((tm, d), lambda i, f: (i, 0)),
                  pl.BlockSpec((d, tf), lambda i, f: (0, f)),
                  pl.BlockSpec((d, tf), lambda i, f: (0, n_f + f)),
                  pl.BlockSpec((tf, d), lambda i, f: (f, 0)),
                  pl.BlockSpec((tm, d), lambda i, f: (i, 0)),
                  pl.BlockSpec((None, 6, d), lambda i, f: (mod_idx(i), 0, 0))],
        out_specs=pl.BlockSpec((tm, d), lambda i, f: (i, 0)),
        out_shape=jax.ShapeDtypeStruct((n_rows, d), F32),
        scratch_shapes=[pltpu.VMEM((tm, d), F32)],
        compiler_params=_params("arbitrary", "arbitrary"),
        name="ffn_dense",
    )(h, w_in, w_in, w_out.astype(BF16), xs, mod)


def _moe_ffn_kernel(te_ref, na_ref, h_ref, wg_ref, wu_ref, wd_ref, o_ref, acc_ref):
    i, f = pl.program_id(0), pl.program_id(1)

    @pl.when(i < na_ref[0])
    def _():
        @pl.when(f == 0)
        def _():
            acc_ref[...] = jnp.zeros_like(acc_ref)

        h = h_ref[...]
        a = _silu(_dot(h, wg_ref[...])) * _dot(h, wu_ref[...])
        acc_ref[...] += _dot(a.astype(BF16), wd_ref[...])

        @pl.when(f == pl.num_programs(1) - 1)
        def _():
            o_ref[...] = acc_ref[...].astype(o_ref.dtype)


def _moe_ffn(hs, tile_expert, n_active, w_in, w_out):
    n_rows, d = hs.shape
    d_ff = w_out.shape[1]
    tm, tf = MOE_TM, FFN_TF
    n_f = d_ff // tf

    def row(i, f, te, na):
        return (jnp.minimum(i, na[0] - 1), 0)

    def expert(i, te, na):
        return te[jnp.minimum(i, na[0] - 1)]

    return pl.pallas_call(
        _moe_ffn_kernel,
        grid_spec=pltpu.PrefetchScalarGridSpec(
            num_scalar_prefetch=2,
            grid=(n_rows // tm, n_f),
            in_specs=[pl.BlockSpec((tm, d), row),
                      pl.BlockSpec((None, d, tf), lambda i, f, te, na: (expert(i, te, na), 0, f)),
                      pl.BlockSpec((None, d, tf), lambda i, f, te, na: (expert(i, te, na), 0, n_f + f)),
                      pl.BlockSpec((None, tf, d), lambda i, f, te, na: (expert(i, te, na), f, 0))],
            out_specs=pl.BlockSpec((tm, d), row),
            scratch_shapes=[pltpu.VMEM((tm, d), F32)]),
        out_shape=jax.ShapeDtypeStruct((n_rows, d), BF16),
        compiler_params=_params("arbitrary", "arbitrary"),
        name="moe_ffn",
    )(tile_expert, n_active, hs, w_in, w_in, w_out)


def _combine_kernel(x_ref, a_ref, b_ref, route_ref, mod_ref, o_ref):
    w1 = route_ref[:, 2:3]
    w2 = route_ref[:, 3:4]
    mix = w1 * a_ref[...].astype(F32) + w2 * b_ref[...].astype(F32)
    o_ref[...] = x_ref[...] + mod_ref[5:6, :] * mix


def _combine(xs, ya, yb, route, mod, dims):
    b, s_len, n_ctx = dims
    n_rows, d = ya.shape
    tm = _row_tile(s_len, b * n_ctx, 512)
    n_lat_tiles = b * s_len // tm
    tiles_per_sample = s_len // tm

    def mod_idx(i):
        return jnp.where(i < n_lat_tiles, i // tiles_per_sample, b)

    row = pl.BlockSpec((tm, d), lambda i: (i, 0))
    return pl.pallas_call(
        _combine_kernel,
        grid=(n_rows // tm,),
        in_specs=[row, row, row,
                  pl.BlockSpec((tm, LANES), lambda i: (i, 0)),
                  pl.BlockSpec((None, 6, d), lambda i: (mod_idx(i), 0, 0))],
        out_specs=row,
        out_shape=jax.ShapeDtypeStruct((n_rows, d), F32),
        compiler_params=_params("arbitrary"),
        name="moe_combine",
    )(xs, ya, yb, route, mod)


def _route_plan(idx, tm):
    n = idx.shape[0]
    e_flat = idx.reshape(-1)
    onehot = (e_flat[:, None] == jnp.arange(N_EXPERTS, dtype=jnp.int32)[None, :]).astype(jnp.int32)
    csum = jnp.cumsum(onehot, axis=0)
    rank = jnp.take_along_axis(csum, e_flat[:, None], axis=1)[:, 0] - 1
    counts = csum[-1]
    tiles_per = (counts + tm - 1) // tm
    tile_end = jnp.cumsum(tiles_per)
    tile_start = tile_end - tiles_per
    slot = tile_start[e_flat] * tm + rank
    n_tiles = (2 * n) // tm + N_EXPERTS
    src = jnp.zeros((n_tiles * tm,), jnp.int32).at[slot].set(jnp.arange(2 * n, dtype=jnp.int32) // 2)
    tile_expert = jnp.minimum(jnp.searchsorted(tile_end, jnp.arange(n_tiles, dtype=jnp.int32), side="right"),
                              N_EXPERTS - 1).astype(jnp.int32)
    n_active = tile_end[-1:].astype(jnp.int32)
    return slot.reshape(n, 2), src, tile_expert, n_active


def _moe(h, xs, route, mod, w_exp_in, w_exp_out, dims):
    idx = route[:, 0:2].astype(jnp.int32)
    slot, src, tile_expert, n_active = _route_plan(idx, MOE_TM)
    hs = jnp.take(h, src, axis=0)
    ys = _moe_ffn(hs, tile_expert, n_active, w_exp_in.astype(BF16), w_exp_out.astype(BF16))
    ya = jnp.take(ys, slot[:, 0], axis=0)
    yb = jnp.take(ys, slot[:, 1], axis=0)
    return _combine(xs, ya, yb, route, mod, dims)


def _layer(xs, cond, p, dims, mixer, last):
    b, s_len, n_ctx = dims
    mod = _adaln(cond, p["w_mod"], p["b_mod"])
    q, k, v = _qkv(xs, mod, p["norm1"], p["w_qkv"], p["q_norm"], p["k_norm"], dims, mixer)
    if mixer == 0:
        y_lat = _na_attention(q, k, v, p["rel_bias"], dims)
    elif mixer == 1:
        y_lat = _swa_attention(q, k, v, p["sink"], dims)
    else:
        y_lat = _global_attention(q, k, v, dims)
    y_ctx = None if last else _ctx_attention(q, k, v, p.get("sink"), dims, mixer)
    router = (p["w_router"], p["b_router"]) if "w_router" in p else None
    outs = _oproj(y_lat, y_ctx, xs, mod, p["norm2"], p["w_o"], router, dims, last)
    if router is None:
        xs, h = outs
        return _ffn(h, xs, mod, p["w_ffn_in"], p["w_ffn_out"], dims)
    xs, h, route = outs
    return _moe(h, xs, route, mod, p["w_exp_in"], p["w_exp_out"], dims)


def kernel(x, c, ctx, c_ctx, l0_w_mod, l0_b_mod, l0_norm1, l0_norm2, l0_w_qkv, l0_q_norm, l0_k_norm, l0_rel_bias, l0_w_o, l0_w_ffn_in, l0_w_ffn_out, l1_w_mod, l1_b_mod, l1_norm1, l1_norm2, l1_w_qkv, l1_q_norm, l1_k_norm, l1_sink, l1_w_o, l1_w_router, l1_b_router, l1_w_exp_in, l1_w_exp_out, l2_w_mod, l2_b_mod, l2_norm1, l2_norm2, l2_w_qkv, l2_q_norm, l2_k_norm, l2_w_o, l2_w_ffn_in, l2_w_ffn_out, l3_w_mod, l3_b_mod, l3_norm1, l3_norm2, l3_w_qkv, l3_q_norm, l3_k_norm, l3_rel_bias, l3_w_o, l3_w_router, l3_b_router, l3_w_exp_in, l3_w_exp_out):
    b, s_len, d = x.shape
    n_ctx = ctx.shape[1]
    dims = (b, s_len, n_ctx)
    layers = (
        dict(w_mod=l0_w_mod, b_mod=l0_b_mod, norm1=l0_norm1, norm2=l0_norm2, w_qkv=l0_w_qkv, q_norm=l0_q_norm,
             k_norm=l0_k_norm, rel_bias=l0_rel_bias, w_o=l0_w_o, w_ffn_in=l0_w_ffn_in, w_ffn_out=l0_w_ffn_out),
        dict(w_mod=l1_w_mod, b_mod=l1_b_mod, norm1=l1_norm1, norm2=l1_norm2, w_qkv=l1_w_qkv, q_norm=l1_q_norm,
             k_norm=l1_k_norm, sink=l1_sink, w_o=l1_w_o, w_router=l1_w_router, b_router=l1_b_router,
             w_exp_in=l1_w_exp_in, w_exp_out=l1_w_exp_out),
        dict(w_mod=l2_w_mod, b_mod=l2_b_mod, norm1=l2_norm1, norm2=l2_norm2, w_qkv=l2_w_qkv, q_norm=l2_q_norm,
             k_norm=l2_k_norm, w_o=l2_w_o, w_ffn_in=l2_w_ffn_in, w_ffn_out=l2_w_ffn_out),
        dict(w_mod=l3_w_mod, b_mod=l3_b_mod, norm1=l3_norm1, norm2=l3_norm2, w_qkv=l3_w_qkv, q_norm=l3_q_norm,
             k_norm=l3_k_norm, rel_bias=l3_rel_bias, w_o=l3_w_o, w_router=l3_w_router, b_router=l3_b_router,
             w_exp_in=l3_w_exp_in, w_exp_out=l3_w_exp_out),
    )
    xs = jnp.concatenate([x.reshape(b * s_len, d), ctx.reshape(b * n_ctx, d)], axis=0)
    pad_rows = -(b + 1) % 8
    cond = jnp.concatenate([c, c_ctx[None, :], jnp.zeros((pad_rows, d), F32)], axis=0)
    n_layers = len(layers)
    for i, p in enumerate(layers):
        xs = _layer(xs, cond, p, dims, i % 3, i == n_layers - 1)
    return xs.reshape(b, s_len, d)
```

```python
import functools

import jax
import jax.numpy as jnp
from jax import lax
from jax.experimental import pallas as pl
from jax.experimental.pallas import tpu as pltpu

F32 = jnp.float32
BF16 = jnp.bfloat16

GRID_W = 64
NORM_EPS = 1e-6
ROPE_THETA = 10000.0
NA_WIN_ROWS = 8
NA_WIN_COLS = 16
SW_WINDOW = 128
N_EXPERTS = 8
MIXER_HEADS = ((16, 16, 64), (16, 4, 64), (8, 4, 128))

LANES = 128
VMEM_LIMIT_BYTES = 56 * 1024 * 1024
NEG_BIG = -1e30
LOG2E = 1.4426950408889634

GLB_TQ = 256
GLB_CHUNK = 768
NA_CTX_ROWS = 1024
NA_UNROLL = 4
MOE_TM = 1024
FFN_TF = 512


def _params(*sem):
    return pltpu.CompilerParams(dimension_semantics=sem, vmem_limit_bytes=VMEM_LIMIT_BYTES)


def _row_tile(n_lat_per_sample, n_ctx_rows, cap):
    for tm in (1024, 512, 256, 128):
        if tm <= cap and n_lat_per_sample % tm == 0 and n_ctx_rows % tm == 0:
            return tm
    raise ValueError("no row tile fits")


def _split_bf16(a):
    hi = a.astype(BF16)
    lo = (a - hi.astype(F32)).astype(BF16)
    return hi, lo


def _dot(a, b):
    return jnp.dot(a, b, preferred_element_type=F32)


def _dot_t(a, b):
    return lax.dot_general(a, b, (((1,), (1,)), ((), ())), preferred_element_type=F32)


def _silu(g):
    return g / (1.0 + jnp.exp(-g))


def _rms_mod(x, gain, scale, shift):
    ms = jnp.mean(x * x, axis=-1, keepdims=True)
    return x * lax.rsqrt(ms + NORM_EPS) * gain * (1.0 + scale) + shift


def _adaln_kernel(c_ref, w_ref, b_ref, o_ref):
    a_hi, a_lo = _split_bf16(_silu(c_ref[...]))
    w_hi, w_lo = _split_bf16(w_ref[...])
    o_ref[...] = _dot(a_hi, w_hi) + _dot(a_hi, w_lo) + _dot(a_lo, w_hi) + b_ref[...]


def _adaln(cond, w_mod, b_mod):
    r, d = cond.shape
    n = w_mod.shape[1]
    tn = 1536
    out = pl.pallas_call(
        _adaln_kernel,
        grid=(n // tn,),
        in_specs=[pl.BlockSpec((r, d), lambda j: (0, 0)),
                  pl.BlockSpec((d, tn), lambda j: (0, j)),
                  pl.BlockSpec((1, tn), lambda j: (0, j))],
        out_specs=pl.BlockSpec((r, tn), lambda j: (0, j)),
        out_shape=jax.ShapeDtypeStruct((r, n), F32),
        compiler_params=_params("arbitrary"),
        name="adaln",
    )(cond, w_mod, b_mod.reshape(1, n))
    return out.reshape(r, 6, d)


def _rot_half(z, dh):
    if dh == LANES:
        return pltpu.roll(z, LANES // 2, axis=1)
    lane = lax.broadcasted_iota(jnp.int32, z.shape, 1)
    from_right = pltpu.roll(z, LANES - dh // 2, axis=1)
    from_left = pltpu.roll(z, dh // 2, axis=1)
    return jnp.where((lane % dh) < dh // 2, from_right, from_left)


def _qkv_kernel(*refs, n_q, n_kv, dh, rope):
    if rope:
        x_ref, mod_ref, n1_ref, w_ref, g_ref, bd_ref, cos_ref, sin_ref, q_ref, k_ref, v_ref = refs
    else:
        x_ref, mod_ref, n1_ref, w_ref, g_ref, bd_ref, q_ref, k_ref, v_ref = refs
    h = _rms_mod(x_ref[...], n1_ref[...], mod_ref[1:2, :], mod_ref[0:1, :]).astype(BF16)
    bd = bd_ref[...]
    cw = 2 * LANES
    for c in range((n_q + n_kv) // cw):
        y = _dot(h, w_ref[:, c * cw:(c + 1) * cw])
        hi, lo = _split_bf16(y * y)
        ss = _dot(hi, bd) + _dot(lo, bd)
        y = y * lax.rsqrt(ss * (1.0 / dh) + NORM_EPS) * g_ref[:, c * cw:(c + 1) * cw]
        if rope:
            halves = []
            for s in range(2):
                z = y[:, s * LANES:(s + 1) * LANES]
                halves.append(z * cos_ref[...] + _rot_half(z, dh) * sin_ref[...])
            y = jnp.concatenate(halves, axis=1)
        y = y.astype(BF16)
        if c * cw < n_q:
            q_ref[:, c * cw:(c + 1) * cw] = y
        else:
            k_ref[:, c * cw - n_q:(c + 1) * cw - n_q] = y
    for c in range(n_kv // cw):
        lo_col = n_q + n_kv + c * cw
        v_ref[:, c * cw:(c + 1) * cw] = _dot(h, w_ref[:, lo_col:lo_col + cw]).astype(BF16)


def _rope_tables(s_len, dh, tm):
    n_freq = dh // 4
    inv_freq = ROPE_THETA ** (-jnp.arange(n_freq, dtype=F32) / n_freq)
    t = jnp.arange(s_len)
    row = (t // GRID_W).astype(F32)
    col = (t % GRID_W).astype(F32)
    ang = jnp.concatenate([row[:, None] * inv_freq, col[:, None] * inv_freq], axis=-1)
    cos, sin = jnp.cos(ang), jnp.sin(ang)
    reps = LANES // dh
    cos_t = jnp.tile(jnp.concatenate([cos, cos], axis=-1), (1, reps))
    sin_t = jnp.tile(jnp.concatenate([-sin, sin], axis=-1), (1, reps))
    cos_t = jnp.concatenate([cos_t, jnp.ones((tm, LANES), F32)], axis=0)
    sin_t = jnp.concatenate([sin_t, jnp.zeros((tm, LANES), F32)], axis=0)
    return cos_t, sin_t


def _qkv(xs, mod, norm1, w_qkv, q_gain, k_gain, dims, mixer):
    b, s_len, n_ctx = dims
    n_rows, d = xs.shape
    n_qh, n_kvh, dh = MIXER_HEADS[mixer]
    n_q, n_kv = n_qh * dh, n_kvh * dh
    rope = mixer != 0
    tm = _row_tile(s_len, b * n_ctx, 512)
    n_lat_tiles = b * s_len // tm
    tiles_per_sample = s_len // tm

    gains = jnp.concatenate([jnp.tile(q_gain * (dh ** -0.5 * LOG2E), n_qh),
                             jnp.tile(k_gain, n_kvh)]).reshape(1, n_q + n_kv)
    hd = jnp.arange(2 * LANES) // dh
    bd = (hd[:, None] == hd[None, :]).astype(BF16)

    def mod_idx(i):
        return jnp.where(i < n_lat_tiles, i // tiles_per_sample, b)

    in_specs = [pl.BlockSpec((tm, d), lambda i: (i, 0)),
                pl.BlockSpec((None, 6, d), lambda i: (mod_idx(i), 0, 0)),
                pl.BlockSpec((1, d), lambda i: (0, 0)),
                pl.BlockSpec(w_qkv.shape, lambda i: (0, 0)),
                pl.BlockSpec(gains.shape, lambda i: (0, 0)),
                pl.BlockSpec(bd.shape, lambda i: (0, 0))]
    args = [xs, mod, norm1.reshape(1, d), w_qkv.astype(BF16), gains, bd]
    if rope:
        cos_t, sin_t = _rope_tables(s_len, dh, tm)

        def pos_idx(i):
            return jnp.where(i < n_lat_tiles, i % tiles_per_sample, tiles_per_sample)

        in_specs += [pl.BlockSpec((tm, LANES), lambda i: (pos_idx(i), 0))] * 2
        args += [cos_t, sin_t]
    return pl.pallas_call(
        functools.partial(_qkv_kernel, n_q=n_q, n_kv=n_kv, dh=dh, rope=rope),
        grid=(n_rows // tm,),
        in_specs=in_specs,
        out_specs=[pl.BlockSpec((tm, n_q), lambda i: (i, 0)),
                   pl.BlockSpec((tm, n_kv), lambda i: (i, 0)),
                   pl.BlockSpec((tm, n_kv), lambda i: (i, 0))],
        out_shape=[jax.ShapeDtypeStruct((n_rows, n_q), BF16),
                   jax.ShapeDtypeStruct((n_rows, n_kv), BF16),
                   jax.ShapeDtypeStruct((n_rows, n_kv), BF16)],
        compiler_params=_params("arbitrary"),
        name="qkv_m%d" % mixer,
    )(*args)


def _stack_heads(q, dh, nh):
    if dh == LANES:
        return jnp.concatenate([q[:, h * LANES:(h + 1) * LANES] for h in range(nh)], axis=0)
    per_kv = nh // 2
    tq = q.shape[0]
    lane = lax.broadcasted_iota(jnp.int32, (tq, LANES), 1)
    ops = []
    for h in range(nh):
        slot = h // per_kv
        chunk = q[:, (h // 2) * LANES:(h // 2 + 1) * LANES].astype(F32)
        if h % 2 != slot:
            chunk = pltpu.roll(chunk, dh, axis=1)
        keep = lane < dh if slot == 0 else lane >= dh
        ops.append(jnp.where(keep, chunk, 0.0).astype(BF16))
    return jnp.concatenate(ops, axis=0)


def _unstack_heads(o, dh, nh):
    tq = o.shape[0] // nh
    if dh == LANES:
        return jnp.concatenate([o[h * tq:(h + 1) * tq] for h in range(nh)], axis=1)
    per_kv = nh // 2
    lane = lax.broadcasted_iota(jnp.int32, (tq, LANES), 1)
    chunks = []
    for c in range(nh // 2):
        parts = []
        for h in (2 * c, 2 * c + 1):
            oh = o[h * tq:(h + 1) * tq]
            if h % 2 != h // per_kv:
                oh = pltpu.roll(oh, dh, axis=1)
            parts.append(oh)
        chunks.append(jnp.where(lane < dh, parts[0], parts[1]))
    return jnp.concatenate(chunks, axis=1)


def _lanes(x, n):
    return x if n == LANES else jnp.concatenate([x] * (n // LANES), axis=1)


def _block_max(s):
    return functools.reduce(jnp.maximum, [s[:, j * LANES:(j + 1) * LANES] for j in range(s.shape[1] // LANES)])


def _row_max(blk):
    return jnp.broadcast_to(jnp.max(blk, axis=-1, keepdims=True), blk.shape)


def _ones_ext(v):
    return jnp.concatenate([v, jnp.ones_like(v)], axis=1)


def _sink_rows(sink_ref, first_head, nh, tq):
    return jnp.concatenate([jnp.full((tq, LANES), sink_ref[first_head + h] * LOG2E, F32) for h in range(nh)], axis=0)


def _na_kernel(q_ref, k_ref, v_ref, kc_ref, vc_ref, bias_ref, o_ref,
               qs_ref, vext_ref, mc_ref, numc_ref, lc_ref, *, rows):
    s_len = q_ref.shape[0]
    band = NA_WIN_ROWS * GRID_W
    lane = lax.broadcasted_iota(jnp.int32, (s_len, LANES), 1)
    q = q_ref[...]
    qs_ref[0:s_len, :] = jnp.where(lane < 64, q, jnp.zeros_like(q))
    qs_ref[s_len:, :] = jnp.where(lane >= 64, q, jnp.zeros_like(q))
    vext_ref[:, 0:LANES] = v_ref[...]
    vext_ref[:, LANES:] = jnp.ones((s_len, LANES), BF16)

    kc = kc_ref[...]
    vcx = _ones_ext(vc_ref[...])
    for c in range(2 * s_len // NA_CTX_ROWS):
        rs = slice(c * NA_CTX_ROWS, (c + 1) * NA_CTX_ROWS)
        s = _dot_t(qs_ref[rs, :], kc)
        m = _row_max(_block_max(s))
        acc = _dot(jnp.exp2(s - _lanes(m, s.shape[1])).astype(BF16), vcx)
        mc_ref[rs, :] = m
        numc_ref[rs, :] = acc[:, :LANES]
        lc_ref[rs, :] = acc[:, LANES:]

    lane_q = lax.broadcasted_iota(jnp.int32, (GRID_W, LANES), 1)

    def both_heads(ref, q0):
        return jnp.concatenate([ref[pl.ds(q0, GRID_W), :], ref[pl.ds(s_len + q0, GRID_W), :]], axis=0)

    def one_row(r, carry):
        r0 = jnp.clip(r - NA_WIN_ROWS // 2, 0, rows - NA_WIN_ROWS)
        q0 = pl.multiple_of(r * GRID_W, GRID_W)
        k0 = pl.multiple_of(r0 * GRID_W, GRID_W)
        bias = bias_ref[r0 - r + NA_WIN_ROWS - 1].reshape(2 * GRID_W, band)
        s = _dot_t(both_heads(qs_ref, q0), k_ref[pl.ds(k0, band), :]) + bias
        mc = both_heads(mc_ref, q0)
        m = jnp.maximum(_row_max(_block_max(s)), mc)
        acc = _dot(jnp.exp2(s - _lanes(m, band)).astype(BF16), vext_ref[pl.ds(k0, band), :])
        alpha = jnp.exp2(mc - m)
        num = acc[:, :LANES] + alpha * both_heads(numc_ref, q0)
        den = acc[:, LANES:] + alpha * both_heads(lc_ref, q0)
        o = num / den
        o_ref[pl.ds(q0, GRID_W), :] = jnp.where(lane_q < 64, o[:GRID_W], o[GRID_W:]).astype(BF16)
        return carry

    lax.fori_loop(0, rows, one_row, 0, unroll=NA_UNROLL)


def _na_bias_tables(rel_bias):
    n_h = rel_bias.shape[0]
    col = jnp.arange(GRID_W)
    col_start = jnp.clip(col - NA_WIN_COLS // 2, 0, GRID_W - NA_WIN_COLS)
    in_win = (col[None, :] >= col_start[:, None]) & (col[None, :] < col_start[:, None] + NA_WIN_COLS)
    col_idx = jnp.clip(col[None, :] - col[:, None] + NA_WIN_COLS - 1, 0, 2 * NA_WIN_COLS - 2)
    masked = jnp.where(in_win[None, None], rel_bias[:, :, col_idx] * LOG2E, NEG_BIG)
    variants = jnp.stack([masked[:, j:j + NA_WIN_ROWS] for j in range(NA_WIN_ROWS)], axis=1)
    variants = variants.transpose(0, 1, 3, 2, 4).reshape(n_h, NA_WIN_ROWS, GRID_W, NA_WIN_ROWS * GRID_W)
    return variants.reshape(n_h // 2, 2, NA_WIN_ROWS, GRID_W, NA_WIN_ROWS * GRID_W).transpose(0, 2, 1, 3, 4)


def _na_attention(q, k, v, rel_bias, dims):
    b, s_len, n_ctx = dims
    rows = s_len // GRID_W
    n_pairs = q.shape[1] // LANES
    bias = _na_bias_tables(rel_bias)
    ctx_blk = b * s_len // n_ctx
    lat = pl.BlockSpec((s_len, LANES), lambda p, i: (i, p))
    ctx = pl.BlockSpec((n_ctx, LANES), lambda p, i: (ctx_blk + i, p))
    return pl.pallas_call(
        functools.partial(_na_kernel, rows=rows),
        grid=(n_pairs, b),
        in_specs=[lat, lat, lat, ctx, ctx,
                  pl.BlockSpec((None,) + bias.shape[1:], lambda p, i: (p, 0, 0, 0, 0))],
        out_specs=lat,
        out_shape=jax.ShapeDtypeStruct((b * s_len, q.shape[1]), BF16),
        scratch_shapes=[pltpu.VMEM((2 * s_len, LANES), BF16),
                        pltpu.VMEM((s_len, 2 * LANES), BF16),
                        pltpu.VMEM((2 * s_len, LANES), F32),
                        pltpu.VMEM((2 * s_len, LANES), F32),
                        pltpu.VMEM((2 * s_len, LANES), F32)],
        compiler_params=_params("arbitrary", "arbitrary"),
        name="attn_na",
    )(q, k, v, k, v, bias)


def _swa_kernel(sink_ref, q_ref, k_ref, v_ref, kc_ref, vc_ref, o_ref, vext_ref, *, s_len, nh, n_units):
    tq = q_ref.shape[0]
    p_blk, t = pl.program_id(1), pl.program_id(2)
    n_t = pl.num_programs(2)

    @pl.when(t == 0)
    def _():
        vext_ref[0:s_len, 0:LANES] = v_ref[...]
        vext_ref[s_len:, 0:LANES] = vc_ref[...]
        vext_ref[:, LANES:] = jnp.ones((vext_ref.shape[0], LANES), BF16)

    row = lax.broadcasted_iota(jnp.int32, (tq, tq), 0)
    col = lax.broadcasted_iota(jnp.int32, (tq, tq), 1)
    bias_prev = jnp.where(t > 0, jnp.where(col >= row, 0.0, NEG_BIG), NEG_BIG)
    bias_next = jnp.where(t < n_t - 1, jnp.where(col <= row, 0.0, NEG_BIG), NEG_BIG)
    k_prev = pl.multiple_of(jnp.maximum(t - 1, 0) * tq, tq)
    k_same = pl.multiple_of(t * tq, tq)
    k_next = pl.multiple_of(jnp.minimum(t + 1, n_t - 1) * tq, tq)

    qs_all = _stack_heads(q_ref[...], 64, nh)
    sink_all = _sink_rows(sink_ref, p_blk * nh, nh, tq)
    hu = nh // n_units
    outs = []
    for u in range(n_units):
        qs = qs_all[u * hu * tq:(u + 1) * hu * tq]
        sink = sink_all[u * hu * tq:(u + 1) * hu * tq]

        def banded(k0, bias):
            s = _dot_t(qs, k_ref[pl.ds(k0, tq), :])
            return (s.reshape(hu, tq, tq) + bias[None]).reshape(hu * tq, tq)

        s = jnp.concatenate([banded(k_prev, bias_prev), _dot_t(qs, k_ref[pl.ds(k_same, tq), :]),
                             banded(k_next, bias_next), _dot_t(qs, kc_ref[...])], axis=1)
        m = jnp.maximum(_row_max(_block_max(s)), sink)
        p = jnp.exp2(s - _lanes(m, s.shape[1])).astype(BF16)
        acc = (_dot(p[:, 0:tq], vext_ref[pl.ds(k_prev, tq), :])
               + _dot(p[:, tq:2 * tq], vext_ref[pl.ds(k_same, tq), :])
               + _dot(p[:, 2 * tq:3 * tq], vext_ref[pl.ds(k_next, tq), :])
               + _dot(p[:, 3 * tq:], vext_ref[s_len:, :]))
        outs.append(acc[:, :LANES] / (acc[:, LANES:] + jnp.exp2(sink - m)))
    o_ref[...] = _unstack_heads(jnp.concatenate(outs, axis=0), 64, nh).astype(BF16)


def _swa_attention(q, k, v, sink, dims):
    b, s_len, n_ctx = dims
    tq = SW_WINDOW
    n_t = s_len // tq
    n_kvblk = k.shape[1] // LANES
    nh = q.shape[1] // k.shape[1] * 2
    qw = nh * 64
    ctx_blk = b * s_len // n_ctx
    qspec = pl.BlockSpec((tq, qw), lambda i, p, t, *_: (i * n_t + t, p))
    lat = pl.BlockSpec((s_len, LANES), lambda i, p, t, *_: (i, p))
    ctx = pl.BlockSpec((n_ctx, LANES), lambda i, p, t, *_: (ctx_blk + i, p))
    return pl.pallas_call(
        functools.partial(_swa_kernel, s_len=s_len, nh=nh, n_units=2),
        grid_spec=pltpu.PrefetchScalarGridSpec(
            num_scalar_prefetch=1,
            grid=(b, n_kvblk, n_t),
            in_specs=[qspec, lat, lat, ctx, ctx],
            out_specs=qspec,
            scratch_shapes=[pltpu.VMEM((s_len + n_ctx, 2 * LANES), BF16)]),
        out_shape=jax.ShapeDtypeStruct((b * s_len, q.shape[1]), BF16),
        compiler_params=_params("arbitrary", "arbitrary", "arbitrary"),
        name="attn_swa",
    )(sink, q, k, v, k, v)


def _global_kernel(q_ref, k_ref, v_ref, kc_ref, vc_ref, o_ref, kall_ref, vext_ref, s_ref, *, nh, s_len):
    @pl.when(pl.program_id(2) == 0)
    def _():
        kall_ref[0:s_len, :] = k_ref[...]
        kall_ref[s_len:, :] = kc_ref[...]
        vext_ref[0:s_len, 0:LANES] = v_ref[...]
        vext_ref[s_len:, 0:LANES] = vc_ref[...]
        vext_ref[:, LANES:] = jnp.ones((vext_ref.shape[0], LANES), BF16)

    qs = _stack_heads(q_ref[...], LANES, nh)
    n_chunks = kall_ref.shape[0] // GLB_CHUNK
    mrun = None
    for c in range(n_chunks):
        ks = slice(c * GLB_CHUNK, (c + 1) * GLB_CHUNK)
        s = _dot_t(qs, kall_ref[ks, :])
        s_ref[:, ks] = s
        blk = _block_max(s)
        mrun = blk if mrun is None else jnp.maximum(mrun, blk)
    m = _lanes(_row_max(mrun), GLB_CHUNK)
    acc = None
    for c in range(n_chunks):
        ks = slice(c * GLB_CHUNK, (c + 1) * GLB_CHUNK)
        pv = _dot(jnp.exp2(s_ref[:, ks] - m).astype(BF16), vext_ref[ks, :])
        acc = pv if acc is None else acc + pv
    o_ref[...] = _unstack_heads(acc[:, :LANES] / acc[:, LANES:], LANES, nh).astype(BF16)


def _global_attention(q, k, v, dims):
    b, s_len, n_ctx = dims
    tq = GLB_TQ
    n_t = s_len // tq
    n_kvblk = k.shape[1] // LANES
    nh = q.shape[1] // k.shape[1]
    qw = nh * LANES
    ctx_blk = b * s_len // n_ctx
    n_keys = s_len + n_ctx
    qspec = pl.BlockSpec((tq, qw), lambda i, p, t: (i * n_t + t, p))
    lat = pl.BlockSpec((s_len, LANES), lambda i, p, t: (i, p))
    ctx = pl.BlockSpec((n_ctx, LANES), lambda i, p, t: (ctx_blk + i, p))
    return pl.pallas_call(
        functools.partial(_global_kernel, nh=nh, s_len=s_len),
        grid=(b, n_kvblk, n_t),
        in_specs=[qspec, lat, lat, ctx, ctx],
        out_specs=qspec,
        out_shape=jax.ShapeDtypeStruct((b * s_len, q.shape[1]), BF16),
        scratch_shapes=[pltpu.VMEM((n_keys, LANES), BF16),
                        pltpu.VMEM((n_keys, 2 * LANES), BF16),
                        pltpu.VMEM((nh * tq, n_keys), F32)],
        compiler_params=_params("arbitrary", "arbitrary", "arbitrary"),
        name="attn_global",
    )(q, k, v, k, v)


def _ctx_kernel(sink_ref, q_ref, kc_ref, vc_ref, o_ref, *, dh, nh, use_sink):
    tq = q_ref.shape[0]
    qs = _stack_heads(q_ref[...], dh, nh)
    s = _dot_t(qs, kc_ref[...])
    m = _row_max(_block_max(s))
    if use_sink:
        sink = _sink_rows(sink_ref, pl.program_id(1) * nh, nh, tq)
        m = jnp.maximum(m, sink)
    acc = _dot(jnp.exp2(s - _lanes(m, s.shape[1])).astype(BF16), _ones_ext(vc_ref[...]))
    den = acc[:, LANES:]
    if use_sink:
        den = den + jnp.exp2(sink - m)
    o_ref[...] = _unstack_heads(acc[:, :LANES] / den, dh, nh).astype(BF16)


def _ctx_attention(q, k, v, sink, dims, mixer):
    b, s_len, n_ctx = dims
    n_qh, n_kvh, dh = MIXER_HEADS[mixer]
    n_kvblk = k.shape[1] // LANES
    nh = n_qh // n_kvblk
    qw = nh * dh
    ctx_blk = b * s_len // n_ctx
    use_sink = sink is not None
    if not use_sink:
        sink = jnp.zeros((n_qh,), F32)
    return pl.pallas_call(
        functools.partial(_ctx_kernel, dh=dh, nh=nh, use_sink=use_sink),
        grid_spec=pltpu.PrefetchScalarGridSpec(
            num_scalar_prefetch=1,
            grid=(b, n_kvblk),
            in_specs=[pl.BlockSpec((n_ctx, qw), lambda i, p, *_: (ctx_blk + i, p)),
                      pl.BlockSpec((n_ctx, LANES), lambda i, p, *_: (ctx_blk + i, p)),
                      pl.BlockSpec((n_ctx, LANES), lambda i, p, *_: (ctx_blk + i, p))],
            out_specs=pl.BlockSpec((n_ctx, qw), lambda i, p, *_: (i, p))),
        out_shape=jax.ShapeDtypeStruct((b * n_ctx, q.shape[1]), BF16),
        compiler_params=_params("arbitrary", "arbitrary"),
        name="attn_ctx_m%d" % mixer,
    )(sink, q, k, v)


def _oproj_kernel(*refs, n_lat_tiles, with_ctx, moe):
    refs = list(refs)
    ylat_ref = refs.pop(0)
    yctx_ref = refs.pop(0) if with_ctx else None
    x_ref, mod_ref, n2_ref, wo_ref = refs[:4]
    refs = refs[4:]
    if moe:
        wr_ref, br_ref = refs[:2]
        refs = refs[2:]
    xo_ref, h_ref = refs[:2]
    y = ylat_ref[...]
    if with_ctx:
        y = jnp.where(pl.program_id(0) < n_lat_tiles, y, yctx_ref[...])
    x = x_ref[...] + mod_ref[2:3, :] * _dot(y, wo_ref[...])
    xo_ref[...] = x
    h = _rms_mod(x, n2_ref[...], mod_ref[4:5, :], mod_ref[3:4, :])
    h_ref[...] = h.astype(BF16)
    if moe:
        route_ref = refs[2]
        h_hi, h_lo = _split_bf16(h)
        w_hi, w_lo = _split_bf16(wr_ref[...])
        logits = _dot(h_hi, w_hi) + _dot(h_hi, w_lo) + _dot(h_lo, w_hi) + br_ref[...]
        lane = lax.broadcasted_iota(jnp.int32, logits.shape, 1)
        v1 = jnp.max(logits, axis=-1, keepdims=True)
        i1 = jnp.min(jnp.where(logits == v1, lane, LANES), axis=-1, keepdims=True)
        rest = jnp.where(lane == i1, NEG_BIG, logits)
        v2 = jnp.max(rest, axis=-1, keepdims=True)
        i2 = jnp.min(jnp.where(rest == v2, lane, LANES), axis=-1, keepdims=True)
        e = jnp.exp(v2 - v1)
        w1 = 1.0 / (1.0 + e)
        w2 = e / (1.0 + e)
        route = jnp.where(lane == 0, i1.astype(F32), 0.0)
        route = jnp.where(lane == 1, i2.astype(F32), route)
        route = jnp.where(lane == 2, w1, route)
        route = jnp.where(lane == 3, w2, route)
        route_ref[...] = route


def _oproj(y_lat, y_ctx, xs, mod, norm2, w_o, router, dims, last):
    b, s_len, n_ctx = dims
    d = xs.shape[1]
    tm = _row_tile(s_len, b * n_ctx, 512)
    n_lat_tiles = b * s_len // tm
    tiles_per_sample = s_len // tm
    with_ctx = not last
    n_rows = b * s_len if last else xs.shape[0]
    n_tiles = n_rows // tm
    moe = router is not None

    def mod_idx(i):
        return jnp.where(i < n_lat_tiles, i // tiles_per_sample, b)

    row = pl.BlockSpec((tm, d), lambda i: (i, 0))
    in_specs = [pl.BlockSpec((tm, y_lat.shape[1]), lambda i: (jnp.minimum(i, n_lat_tiles - 1), 0))]
    args = [y_lat]
    if with_ctx:
        in_specs.append(pl.BlockSpec((tm, y_ctx.shape[1]), lambda i: (jnp.maximum(i - n_lat_tiles, 0), 0)))
        args.append(y_ctx)
    in_specs += [row,
                 pl.BlockSpec((None, 6, d), lambda i: (mod_idx(i), 0, 0)),
                 pl.BlockSpec((1, d), lambda i: (0, 0)),
                 pl.BlockSpec(w_o.shape, lambda i: (0, 0))]
    args += [xs, mod, norm2.reshape(1, d), w_o.astype(BF16)]
    out_specs = [row, row]
    out_shape = [jax.ShapeDtypeStruct((n_rows, d), F32), jax.ShapeDtypeStruct((n_rows, d), BF16)]
    if moe:
        w_router, b_router = router
        n_e = w_router.shape[1]
        wr = jnp.pad(w_router, ((0, 0), (0, LANES - n_e)))
        br = jnp.pad(b_router, (0, LANES - n_e), constant_values=NEG_BIG).reshape(1, LANES)
        in_specs += [pl.BlockSpec(wr.shape, lambda i: (0, 0)), pl.BlockSpec(br.shape, lambda i: (0, 0))]
        args += [wr, br]
        out_specs.append(pl.BlockSpec((tm, LANES), lambda i: (i, 0)))
        out_shape.append(jax.ShapeDtypeStruct((n_rows, LANES), F32))
    return pl.pallas_call(
        functools.partial(_oproj_kernel, n_lat_tiles=n_lat_tiles, with_ctx=with_ctx, moe=moe),
        grid=(n_tiles,),
        in_specs=in_specs,
        out_specs=out_specs,
        out_shape=out_shape,
        compiler_params=_params("arbitrary"),
        name="oproj_moe" if moe else "oproj",
    )(*args)


def _ffn_kernel(h_ref, wg_ref, wu_ref, wd_ref, x_ref, mod_ref, o_ref, acc_ref):
    f = pl.program_id(1)

    @pl.when(f == 0)
    def _():
        acc_ref[...] = jnp.zeros_like(acc_ref)

    h = h_ref[...]
    a = _silu(_dot(h, wg_ref[...])) * _dot(h, wu_ref[...])
    acc_ref[...] += _dot(a.astype(BF16), wd_ref[...])

    @pl.when(f == pl.num_programs(1) - 1)
    def _():
        o_ref[...] = x_ref[...] + mod_ref[5:6, :] * acc_ref[...]


def _ffn(h, xs, mod, w_in, w_out, dims):
    b, s_len, n_ctx = dims
    n_rows, d = h.shape
    d_ff = w_out.shape[0]
    tm = _row_tile(s_len, b * n_ctx, 1024)
    tf = FFN_TF
    n_f = d_ff // tf
    n_lat_tiles = b * s_len // tm
    tiles_per_sample = s_len // tm

    def mod_idx(i):
        return jnp.where(i < n_lat_tiles, i // tiles_per_sample, b)

    w_in = w_in.astype(BF16)
    return pl.pallas_call(
        _ffn_kernel,
        grid=(n_rows // tm, n_f),
        in_specs=[pl.BlockSpec((tm, d), lambda i, f: (i, 0)),
                  pl.BlockSpec((d, tf), lambda i, f: (0, f)),
                  pl.BlockSpec((d, tf), lambda i, f: (0, n_f + f)),
                  pl.BlockSpec((tf, d), lambda i, f: (f, 0)),
                  pl.BlockSpec((tm, d), lambda i, f: (i, 0)),
                  pl.BlockSpec((None, 6, d), lambda i, f: (mod_idx(i), 0, 0))],
        out_specs=pl.BlockSpec((tm, d), lambda i, f: (i, 0)),
        out_shape=jax.ShapeDtypeStruct((n_rows, d), F32),
        scratch_shapes=[pltpu.VMEM((tm, d), F32)],
        compiler_params=_params("arbitrary", "arbitrary"),
        name="ffn_dense",
    )(h, w_in, w_in, w_out.astype(BF16), xs, mod)


def _moe_ffn_kernel(te_ref, na_ref, h_ref, wg_ref, wu_ref, wd_ref, o_ref, acc_ref):
    i, f = pl.program_id(0), pl.program_id(1)

    @pl.when(i < na_ref[0])
    def _():
        @pl.when(f == 0)
        def _():
            acc_ref[...] = jnp.zeros_like(acc_ref)

        h = h_ref[...]
        a = _silu(_dot(h, wg_ref[...])) * _dot(h, wu_ref[...])
        acc_ref[...] += _dot(a.astype(BF16), wd_ref[...])

        @pl.when(f == pl.num_programs(1) - 1)
        def _():
            o_ref[...] = acc_ref[...].astype(o_ref.dtype)


def _moe_ffn(hs, tile_expert, n_active, w_in, w_out):
    n_rows, d = hs.shape
    d_ff = w_out.shape[1]
    tm, tf = MOE_TM, FFN_TF
    n_f = d_ff // tf

    def row(i, f, te, na):
        return (jnp.minimum(i, na[0] - 1), 0)

    def expert(i, te, na):
        return te[jnp.minimum(i, na[0] - 1)]

    return pl.pallas_call(
        _moe_ffn_kernel,
        grid_spec=pltpu.PrefetchScalarGridSpec(
            num_scalar_prefetch=2,
            grid=(n_rows // tm, n_f),
            in_specs=[pl.BlockSpec((tm, d), row),
                      pl.BlockSpec((None, d, tf), lambda i, f, te, na: (expert(i, te, na), 0, f)),
                      pl.BlockSpec((None, d, tf), lambda i, f, te, na: (expert(i, te, na), 0, n_f + f)),
                      pl.BlockSpec((None, tf, d), lambda i, f, te, na: (expert(i, te, na), f, 0))],
            out_specs=pl.BlockSpec((tm, d), row),
            scratch_shapes=[pltpu.VMEM((tm, d), F32)]),
        out_shape=jax.ShapeDtypeStruct((n_rows, d), BF16),
        compiler_params=_params("arbitrary", "arbitrary"),
        name="moe_ffn",
    )(tile_expert, n_active, hs, w_in, w_in, w_out)


def _combine_kernel(x_ref, a_ref, b_ref, route_ref, mod_ref, o_ref):
    w1 = route_ref[:, 2:3]
    w2 = route_ref[:, 3:4]
    mix = w1 * a_ref[...].astype(F32) + w2 * b_ref[...].astype(F32)
    o_ref[...] = x_ref[...] + mod_ref[5:6, :] * mix


def _combine(xs, ya, yb, route, mod, dims):
    b, s_len, n_ctx = dims
    n_rows, d = ya.shape
    tm = _row_tile(s_len, b * n_ctx, 512)
    n_lat_tiles = b * s_len // tm
    tiles_per_sample = s_len // tm

    def mod_idx(i):
        return jnp.where(i < n_lat_tiles, i // tiles_per_sample, b)

    row = pl.BlockSpec((tm, d), lambda i: (i, 0))
    return pl.pallas_call(
        _combine_kernel,
        grid=(n_rows // tm,),
        in_specs=[row, row, row,
                  pl.BlockSpec((tm, LANES), lambda i: (i, 0)),
                  pl.BlockSpec((None, 6, d), lambda i: (mod_idx(i), 0, 0))],
        out_specs=row,
        out_shape=jax.ShapeDtypeStruct((n_rows, d), F32),
        compiler_params=_params("arbitrary"),
        name="moe_combine",
    )(xs, ya, yb, route, mod)


def _route_plan(idx, tm):
    n = idx.shape[0]
    e_flat = idx.reshape(-1)
    onehot = (e_flat[:, None] == jnp.arange(N_EXPERTS, dtype=jnp.int32)[None, :]).astype(jnp.int32)
    csum = jnp.cumsum(onehot, axis=0)
    rank = jnp.sum(csum * onehot, axis=1) - 1
    counts = csum[-1]
    tiles_per = (counts + tm - 1) // tm
    tile_end = jnp.cumsum(tiles_per)
    tile_start = tile_end - tiles_per
    slot = jnp.sum(tile_start[None, :] * onehot, axis=1) * tm + rank
    n_tiles = (2 * n) // tm + N_EXPERTS
    src = jnp.zeros((n_tiles * tm,), jnp.int32).at[slot].set(jnp.arange(2 * n, dtype=jnp.int32) // 2)
    tile_ids = jnp.arange(n_tiles, dtype=jnp.int32)
    tile_expert = jnp.minimum(jnp.sum((tile_end[None, :] <= tile_ids[:, None]).astype(jnp.int32), axis=1),
                              N_EXPERTS - 1)
    n_active = tile_end[-1:].astype(jnp.int32)
    return slot.reshape(n, 2), src, tile_expert, n_active


def _moe(h, xs, route, mod, w_exp_in, w_exp_out, dims):
    idx = route[:, 0:2].astype(jnp.int32)
    slot, src, tile_expert, n_active = _route_plan(idx, MOE_TM)
    hs = jnp.take(h, src, axis=0)
    ys = _moe_ffn(hs, tile_expert, n_active, w_exp_in.astype(BF16), w_exp_out.astype(BF16))
    ya = jnp.take(ys, slot[:, 0], axis=0)
    yb = jnp.take(ys, slot[:, 1], axis=0)
    return _combine(xs, ya, yb, route, mod, dims)


def _layer(xs, cond, p, dims, mixer, last):
    b, s_len, n_ctx = dims
    mod = _adaln(cond, p["w_mod"], p["b_mod"])
    q, k, v = _qkv(xs, mod, p["norm1"], p["w_qkv"], p["q_norm"], p["k_norm"], dims, mixer)
    if mixer == 0:
        y_lat = _na_attention(q, k, v, p["rel_bias"], dims)
    elif mixer == 1:
        y_lat = _swa_attention(q, k, v, p["sink"], dims)
    else:
        y_lat = _global_attention(q, k, v, dims)
    y_ctx = None if last else _ctx_attention(q, k, v, p.get("sink"), dims, mixer)
    router = (p["w_router"], p["b_router"]) if "w_router" in p else None
    outs = _oproj(y_lat, y_ctx, xs, mod, p["norm2"], p["w_o"], router, dims, last)
    if router is None:
        xs, h = outs
        return _ffn(h, xs, mod, p["w_ffn_in"], p["w_ffn_out"], dims)
    xs, h, route = outs
    return _moe(h, xs, route, mod, p["w_exp_in"], p["w_exp_out"], dims)


def kernel(x, c, ctx, c_ctx, l0_w_mod, l0_b_mod, l0_norm1, l0_norm2, l0_w_qkv, l0_q_norm, l0_k_norm, l0_rel_bias, l0_w_o, l0_w_ffn_in, l0_w_ffn_out, l1_w_mod, l1_b_mod, l1_norm1, l1_norm2, l1_w_qkv, l1_q_norm, l1_k_norm, l1_sink, l1_w_o, l1_w_router, l1_b_router, l1_w_exp_in, l1_w_exp_out, l2_w_mod, l2_b_mod, l2_norm1, l2_norm2, l2_w_qkv, l2_q_norm, l2_k_norm, l2_w_o, l2_w_ffn_in, l2_w_ffn_out, l3_w_mod, l3_b_mod, l3_norm1, l3_norm2, l3_w_qkv, l3_q_norm, l3_k_norm, l3_rel_bias, l3_w_o, l3_w_router, l3_b_router, l3_w_exp_in, l3_w_exp_out):
    b, s_len, d = x.shape
    n_ctx = ctx.shape[1]
    dims = (b, s_len, n_ctx)
    layers = (
        dict(w_mod=l0_w_mod, b_mod=l0_b_mod, norm1=l0_norm1, norm2=l0_norm2, w_qkv=l0_w_qkv, q_norm=l0_q_norm,
             k_norm=l0_k_norm, rel_bias=l0_rel_bias, w_o=l0_w_o, w_ffn_in=l0_w_ffn_in, w_ffn_out=l0_w_ffn_out),
        dict(w_mod=l1_w_mod, b_mod=l1_b_mod, norm1=l1_norm1, norm2=l1_norm2, w_qkv=l1_w_qkv, q_norm=l1_q_norm,
             k_norm=l1_k_norm, sink=l1_sink, w_o=l1_w_o, w_router=l1_w_router, b_router=l1_b_router,
             w_exp_in=l1_w_exp_in, w_exp_out=l1_w_exp_out),
        dict(w_mod=l2_w_mod, b_mod=l2_b_mod, norm1=l2_norm1, norm2=l2_norm2, w_qkv=l2_w_qkv, q_norm=l2_q_norm,
             k_norm=l2_k_norm, w_o=l2_w_o, w_ffn_in=l2_w_ffn_in, w_ffn_out=l2_w_ffn_out),
        dict(w_mod=l3_w_mod, b_mod=l3_b_mod, norm1=l3_norm1, norm2=l3_norm2, w_qkv=l3_w_qkv, q_norm=l3_q_norm,
             k_norm=l3_k_norm, rel_bias=l3_rel_bias, w_o=l3_w_o, w_router=l3_w_router, b_router=l3_b_router,
             w_exp_in=l3_w_exp_in, w_exp_out=l3_w_exp_out),
    )
    xs = jnp.concatenate([x.reshape(b * s_len, d), ctx.reshape(b * n_ctx, d)], axis=0)
    pad_rows = -(b + 1) % 8
    cond = jnp.concatenate([c, c_ctx[None, :], jnp.zeros((pad_rows, d), F32)], axis=0)
    n_layers = len(layers)
    for i, p in enumerate(layers):
        xs = _layer(xs, cond, p, dims, i % 3, i == n_layers - 1)
    return xs.reshape(b, s_len, d)
```

```python
import functools

import jax
import jax.numpy as jnp
from jax import lax
from jax.experimental import pallas as pl
from jax.experimental.pallas import tpu as pltpu

F32 = jnp.float32
BF16 = jnp.bfloat16

GRID_W = 64
NORM_EPS = 1e-6
ROPE_THETA = 10000.0
NA_WIN_ROWS = 8
NA_WIN_COLS = 16
SW_WINDOW = 128
N_EXPERTS = 8
MIXER_HEADS = ((16, 16, 64), (16, 4, 64), (8, 4, 128))

LANES = 128
VMEM_LIMIT_BYTES = 56 * 1024 * 1024
NEG_BIG = -1e30
LOG2E = 1.4426950408889634

GLB_TQ = 256
GLB_CHUNK = 768
NA_CTX_ROWS = 1024
NA_UNROLL = 4
MOE_TM = 1024
FFN_TF = 512


def _params(*sem):
    return pltpu.CompilerParams(dimension_semantics=sem, vmem_limit_bytes=VMEM_LIMIT_BYTES)


def _row_tile(n_lat_per_sample, n_ctx_rows, cap):
    for tm in (1024, 512, 256, 128):
        if tm <= cap and n_lat_per_sample % tm == 0 and n_ctx_rows % tm == 0:
            return tm
    raise ValueError("no row tile fits")


def _split_bf16(a):
    hi = a.astype(BF16)
    lo = (a - hi.astype(F32)).astype(BF16)
    return hi, lo


def _dot(a, b):
    return jnp.dot(a, b, preferred_element_type=F32)


def _dot_t(a, b):
    return lax.dot_general(a, b, (((1,), (1,)), ((), ())), preferred_element_type=F32)


def _silu(g):
    return g / (1.0 + jnp.exp(-g))


def _rms_mod(x, gain, scale, shift):
    ms = jnp.mean(x * x, axis=-1, keepdims=True)
    return x * lax.rsqrt(ms + NORM_EPS) * gain * (1.0 + scale) + shift


def _adaln_kernel(c_ref, w_ref, b_ref, o_ref):
    a_hi, a_lo = _split_bf16(_silu(c_ref[...]))
    w_hi, w_lo = _split_bf16(w_ref[...])
    o_ref[...] = _dot(a_hi, w_hi) + _dot(a_hi, w_lo) + _dot(a_lo, w_hi) + b_ref[...]


def _adaln(cond, w_mod, b_mod):
    r, d = cond.shape
    n = w_mod.shape[1]
    tn = 1536
    out = pl.pallas_call(
        _adaln_kernel,
        grid=(n // tn,),
        in_specs=[pl.BlockSpec((r, d), lambda j: (0, 0)),
                  pl.BlockSpec((d, tn), lambda j: (0, j)),
                  pl.BlockSpec((1, tn), lambda j: (0, j))],
        out_specs=pl.BlockSpec((r, tn), lambda j: (0, j)),
        out_shape=jax.ShapeDtypeStruct((r, n), F32),
        compiler_params=_params("arbitrary"),
        name="adaln",
    )(cond, w_mod, b_mod.reshape(1, n))
    return out.reshape(r, 6, d)


def _rot_half(z, dh):
    if dh == LANES:
        return pltpu.roll(z, LANES // 2, axis=1)
    lane = lax.broadcasted_iota(jnp.int32, z.shape, 1)
    from_right = pltpu.roll(z, LANES - dh // 2, axis=1)
    from_left = pltpu.roll(z, dh // 2, axis=1)
    return jnp.where((lane % dh) < dh // 2, from_right, from_left)


def _qkv_kernel(*refs, n_q, n_kv, dh, rope):
    if rope:
        x_ref, mod_ref, n1_ref, w_ref, g_ref, bd_ref, cos_ref, sin_ref, q_ref, k_ref, v_ref = refs
    else:
        x_ref, mod_ref, n1_ref, w_ref, g_ref, bd_ref, q_ref, k_ref, v_ref = refs
    h = _rms_mod(x_ref[...], n1_ref[...], mod_ref[1:2, :], mod_ref[0:1, :]).astype(BF16)
    bd = bd_ref[...]
    cw = 2 * LANES
    for c in range((n_q + n_kv) // cw):
        y = _dot(h, w_ref[:, c * cw:(c + 1) * cw])
        hi, lo = _split_bf16(y * y)
        ss = _dot(hi, bd) + _dot(lo, bd)
        y = y * lax.rsqrt(ss * (1.0 / dh) + NORM_EPS) * g_ref[:, c * cw:(c + 1) * cw]
        if rope:
            halves = []
            for s in range(2):
                z = y[:, s * LANES:(s + 1) * LANES]
                halves.append(z * cos_ref[...] + _rot_half(z, dh) * sin_ref[...])
            y = jnp.concatenate(halves, axis=1)
        y = y.astype(BF16)
        if c * cw < n_q:
            q_ref[:, c * cw:(c + 1) * cw] = y
        else:
            k_ref[:, c * cw - n_q:(c + 1) * cw - n_q] = y
    for c in range(n_kv // cw):
        lo_col = n_q + n_kv + c * cw
        v_ref[:, c * cw:(c + 1) * cw] = _dot(h, w_ref[:, lo_col:lo_col + cw]).astype(BF16)


def _rope_tables(s_len, dh, tm):
    n_freq = dh // 4
    inv_freq = ROPE_THETA ** (-jnp.arange(n_freq, dtype=F32) / n_freq)
    t = jnp.arange(s_len)
    row = (t // GRID_W).astype(F32)
    col = (t % GRID_W).astype(F32)
    ang = jnp.concatenate([row[:, None] * inv_freq, col[:, None] * inv_freq], axis=-1)
    cos, sin = jnp.cos(ang), jnp.sin(ang)
    reps = LANES // dh
    cos_t = jnp.tile(jnp.concatenate([cos, cos], axis=-1), (1, reps))
    sin_t = jnp.tile(jnp.concatenate([-sin, sin], axis=-1), (1, reps))
    cos_t = jnp.concatenate([cos_t, jnp.ones((tm, LANES), F32)], axis=0)
    sin_t = jnp.concatenate([sin_t, jnp.zeros((tm, LANES), F32)], axis=0)
    return cos_t, sin_t


def _qkv(xs, mod, norm1, w_qkv, q_gain, k_gain, dims, mixer):
    b, s_len, n_ctx = dims
    n_rows, d = xs.shape
    n_qh, n_kvh, dh = MIXER_HEADS[mixer]
    n_q, n_kv = n_qh * dh, n_kvh * dh
    rope = mixer != 0
    tm = _row_tile(s_len, b * n_ctx, 512)
    n_lat_tiles = b * s_len // tm
    tiles_per_sample = s_len // tm

    gains = jnp.concatenate([jnp.tile(q_gain * (dh ** -0.5 * LOG2E), n_qh),
                             jnp.tile(k_gain, n_kvh)]).reshape(1, n_q + n_kv)
    hd = jnp.arange(2 * LANES) // dh
    bd = (hd[:, None] == hd[None, :]).astype(BF16)

    def mod_idx(i):
        return jnp.where(i < n_lat_tiles, i // tiles_per_sample, b)

    in_specs = [pl.BlockSpec((tm, d), lambda i: (i, 0)),
                pl.BlockSpec((None, 6, d), lambda i: (mod_idx(i), 0, 0)),
                pl.BlockSpec((1, d), lambda i: (0, 0)),
                pl.BlockSpec(w_qkv.shape, lambda i: (0, 0)),
                pl.BlockSpec(gains.shape, lambda i: (0, 0)),
                pl.BlockSpec(bd.shape, lambda i: (0, 0))]
    args = [xs, mod, norm1.reshape(1, d), w_qkv.astype(BF16), gains, bd]
    if rope:
        cos_t, sin_t = _rope_tables(s_len, dh, tm)

        def pos_idx(i):
            return jnp.where(i < n_lat_tiles, i % tiles_per_sample, tiles_per_sample)

        in_specs += [pl.BlockSpec((tm, LANES), lambda i: (pos_idx(i), 0))] * 2
        args += [cos_t, sin_t]
    return pl.pallas_call(
        functools.partial(_qkv_kernel, n_q=n_q, n_kv=n_kv, dh=dh, rope=rope),
        grid=(n_rows // tm,),
        in_specs=in_specs,
        out_specs=[pl.BlockSpec((tm, n_q), lambda i: (i, 0)),
                   pl.BlockSpec((tm, n_kv), lambda i: (i, 0)),
                   pl.BlockSpec((tm, n_kv), lambda i: (i, 0))],
        out_shape=[jax.ShapeDtypeStruct((n_rows, n_q), BF16),
                   jax.ShapeDtypeStruct((n_rows, n_kv), BF16),
                   jax.ShapeDtypeStruct((n_rows, n_kv), BF16)],
        compiler_params=_params("arbitrary"),
        name="qkv_m%d" % mixer,
    )(*args)


def _stack_heads(q, dh, nh):
    if dh == LANES:
        return jnp.concatenate([q[:, h * LANES:(h + 1) * LANES] for h in range(nh)], axis=0)
    per_kv = nh // 2
    tq = q.shape[0]
    lane = lax.broadcasted_iota(jnp.int32, (tq, LANES), 1)
    ops = []
    for h in range(nh):
        slot = h // per_kv
        chunk = q[:, (h // 2) * LANES:(h // 2 + 1) * LANES].astype(F32)
        if h % 2 != slot:
            chunk = pltpu.roll(chunk, dh, axis=1)
        keep = lane < dh if slot == 0 else lane >= dh
        ops.append(jnp.where(keep, chunk, 0.0).astype(BF16))
    return jnp.concatenate(ops, axis=0)


def _unstack_heads(o, dh, nh):
    tq = o.shape[0] // nh
    if dh == LANES:
        return jnp.concatenate([o[h * tq:(h + 1) * tq] for h in range(nh)], axis=1)
    per_kv = nh // 2
    lane = lax.broadcasted_iota(jnp.int32, (tq, LANES), 1)
    chunks = []
    for c in range(nh // 2):
        parts = []
        for h in (2 * c, 2 * c + 1):
            oh = o[h * tq:(h + 1) * tq]
            if h % 2 != h // per_kv:
                oh = pltpu.roll(oh, dh, axis=1)
            parts.append(oh)
        chunks.append(jnp.where(lane < dh, parts[0], parts[1]))
    return jnp.concatenate(chunks, axis=1)


def _lanes(x, n):
    return x if n == LANES else jnp.concatenate([x] * (n // LANES), axis=1)


def _block_max(s):
    return functools.reduce(jnp.maximum, [s[:, j * LANES:(j + 1) * LANES] for j in range(s.shape[1] // LANES)])


def _row_max(blk):
    return jnp.broadcast_to(jnp.max(blk, axis=-1, keepdims=True), blk.shape)


def _dot_row_halves(p, v):
    half = p.shape[0] // 2
    return jnp.concatenate([_dot(p[0:half, :], v), _dot(p[half:, :], v)], axis=0)


def _ones_ext(v):
    return jnp.concatenate([v, jnp.ones_like(v)], axis=1)


def _sink_rows(sink_ref, first_head, nh, tq):
    return jnp.concatenate([jnp.full((tq, LANES), sink_ref[first_head + h] * LOG2E, F32) for h in range(nh)], axis=0)


def _na_kernel(q_ref, k_ref, v_ref, kc_ref, vc_ref, bias_ref, o_ref,
               qs_ref, vext_ref, mc_ref, numc_ref, lc_ref, *, rows):
    s_len = q_ref.shape[0]
    band = NA_WIN_ROWS * GRID_W
    lane = lax.broadcasted_iota(jnp.int32, (s_len, LANES), 1)
    q = q_ref[...]
    qs_ref[0:s_len, :] = jnp.where(lane < 64, q, jnp.zeros_like(q))
    qs_ref[s_len:, :] = jnp.where(lane >= 64, q, jnp.zeros_like(q))
    vext_ref[:, 0:LANES] = v_ref[...]
    vext_ref[:, LANES:] = jnp.ones((s_len, LANES), BF16)

    kc = kc_ref[...]
    vcx = _ones_ext(vc_ref[...])
    for c in range(2 * s_len // NA_CTX_ROWS):
        rs = slice(c * NA_CTX_ROWS, (c + 1) * NA_CTX_ROWS)
        s = _dot_t(qs_ref[rs, :], kc)
        m = _row_max(_block_max(s))
        acc = _dot(jnp.exp2(s - _lanes(m, s.shape[1])).astype(BF16), vcx)
        mc_ref[rs, :] = m
        numc_ref[rs, :] = acc[:, :LANES]
        lc_ref[rs, :] = acc[:, LANES:]

    lane_q = lax.broadcasted_iota(jnp.int32, (GRID_W, LANES), 1)

    def both_heads(ref, q0):
        return jnp.concatenate([ref[pl.ds(q0, GRID_W), :], ref[pl.ds(s_len + q0, GRID_W), :]], axis=0)

    def one_row(r, carry):
        r0 = jnp.clip(r - NA_WIN_ROWS // 2, 0, rows - NA_WIN_ROWS)
        q0 = pl.multiple_of(r * GRID_W, GRID_W)
        k0 = pl.multiple_of(r0 * GRID_W, GRID_W)
        bias = bias_ref[r0 - r + NA_WIN_ROWS - 1].reshape(2 * GRID_W, band)
        s = _dot_t(both_heads(qs_ref, q0), k_ref[pl.ds(k0, band), :]) + bias
        mc = both_heads(mc_ref, q0)
        m = jnp.maximum(_row_max(_block_max(s)), mc)
        acc = _dot(jnp.exp2(s - _lanes(m, band)).astype(BF16), vext_ref[pl.ds(k0, band), :])
        alpha = jnp.exp2(mc - m)
        num = acc[:, :LANES] + alpha * both_heads(numc_ref, q0)
        den = acc[:, LANES:] + alpha * both_heads(lc_ref, q0)
        o = num / den
        o_ref[pl.ds(q0, GRID_W), :] = jnp.where(lane_q < 64, o[:GRID_W], o[GRID_W:]).astype(BF16)
        return carry

    lax.fori_loop(0, rows, one_row, 0, unroll=NA_UNROLL)


def _na_bias_tables(rel_bias):
    n_h = rel_bias.shape[0]
    col = jnp.arange(GRID_W)
    col_start = jnp.clip(col - NA_WIN_COLS // 2, 0, GRID_W - NA_WIN_COLS)
    in_win = (col[None, :] >= col_start[:, None]) & (col[None, :] < col_start[:, None] + NA_WIN_COLS)
    col_idx = jnp.clip(col[None, :] - col[:, None] + NA_WIN_COLS - 1, 0, 2 * NA_WIN_COLS - 2)
    masked = jnp.where(in_win[None, None], rel_bias[:, :, col_idx] * LOG2E, NEG_BIG)
    variants = jnp.stack([masked[:, j:j + NA_WIN_ROWS] for j in range(NA_WIN_ROWS)], axis=1)
    variants = variants.transpose(0, 1, 3, 2, 4).reshape(n_h, NA_WIN_ROWS, GRID_W, NA_WIN_ROWS * GRID_W)
    return variants.reshape(n_h // 2, 2, NA_WIN_ROWS, GRID_W, NA_WIN_ROWS * GRID_W).transpose(0, 2, 1, 3, 4)


def _na_attention(q, k, v, rel_bias, dims):
    b, s_len, n_ctx = dims
    rows = s_len // GRID_W
    n_pairs = q.shape[1] // LANES
    bias = _na_bias_tables(rel_bias)
    ctx_blk = b * s_len // n_ctx
    lat = pl.BlockSpec((s_len, LANES), lambda p, i: (i, p))
    ctx = pl.BlockSpec((n_ctx, LANES), lambda p, i: (ctx_blk + i, p))
    return pl.pallas_call(
        functools.partial(_na_kernel, rows=rows),
        grid=(n_pairs, b),
        in_specs=[lat, lat, lat, ctx, ctx,
                  pl.BlockSpec((None,) + bias.shape[1:], lambda p, i: (p, 0, 0, 0, 0))],
        out_specs=lat,
        out_shape=jax.ShapeDtypeStruct((b * s_len, q.shape[1]), BF16),
        scratch_shapes=[pltpu.VMEM((2 * s_len, LANES), BF16),
                        pltpu.VMEM((s_len, 2 * LANES), BF16),
                        pltpu.VMEM((2 * s_len, LANES), F32),
                        pltpu.VMEM((2 * s_len, LANES), F32),
                        pltpu.VMEM((2 * s_len, LANES), F32)],
        compiler_params=_params("arbitrary", "arbitrary"),
        name="attn_na",
    )(q, k, v, k, v, bias)


def _swa_kernel(sink_ref, q_ref, k_ref, v_ref, kc_ref, vc_ref, o_ref, kwin_ref, vwin_ref, *, nh, n_units):
    tq = q_ref.shape[0]
    n_ctx = kc_ref.shape[0]
    p_blk, t = pl.program_id(1), pl.program_id(2)
    n_t = pl.num_programs(2)

    @pl.when(t == 0)
    def _():
        kwin_ref[3 * tq:, :] = kc_ref[...]
        vwin_ref[3 * tq:, 0:LANES] = vc_ref[...]
        vwin_ref[:, LANES:] = jnp.ones((vwin_ref.shape[0], LANES), BF16)

    starts = (jnp.maximum(t - 1, 0) * tq, t * tq, jnp.minimum(t + 1, n_t - 1) * tq)
    for j, k0 in enumerate(starts):
        k0 = pl.multiple_of(k0, tq)
        kwin_ref[j * tq:(j + 1) * tq, :] = k_ref[pl.ds(k0, tq), :]
        vwin_ref[j * tq:(j + 1) * tq, 0:LANES] = v_ref[pl.ds(k0, tq), :]

    row = lax.broadcasted_iota(jnp.int32, (tq, tq), 0)
    col = lax.broadcasted_iota(jnp.int32, (tq, tq), 1)
    bias_prev = jnp.where(t > 0, jnp.where(col >= row, 0.0, NEG_BIG), NEG_BIG)
    bias_next = jnp.where(t < n_t - 1, jnp.where(col <= row, 0.0, NEG_BIG), NEG_BIG)

    qs_all = _stack_heads(q_ref[...], 64, nh)
    sink_all = _sink_rows(sink_ref, p_blk * nh, nh, tq)
    hu = nh // n_units
    kwin = kwin_ref[...]
    vwin = vwin_ref[...]
    outs = []
    for u in range(n_units):
        qs = qs_all[u * hu * tq:(u + 1) * hu * tq]
        sink = sink_all[u * hu * tq:(u + 1) * hu * tq]
        s = _dot_t(qs, kwin)

        def masked(blk, bias):
            return (blk.reshape(hu, tq, tq) + bias[None]).reshape(hu * tq, tq)

        s = jnp.concatenate([masked(s[:, 0:tq], bias_prev), s[:, tq:2 * tq],
                             masked(s[:, 2 * tq:3 * tq], bias_next), s[:, 3 * tq:]], axis=1)
        m = jnp.maximum(_row_max(_block_max(s)), sink)
        acc = _dot(jnp.exp2(s - _lanes(m, s.shape[1])).astype(BF16), vwin)
        outs.append(acc[:, :LANES] / (acc[:, LANES:] + jnp.exp2(sink - m)))
    o_ref[...] = _unstack_heads(jnp.concatenate(outs, axis=0), 64, nh).astype(BF16)


def _swa_attention(q, k, v, sink, dims):
    b, s_len, n_ctx = dims
    tq = SW_WINDOW
    n_t = s_len // tq
    n_kvblk = k.shape[1] // LANES
    nh = q.shape[1] // k.shape[1] * 2
    qw = nh * 64
    ctx_blk = b * s_len // n_ctx
    qspec = pl.BlockSpec((tq, qw), lambda i, p, t, *_: (i * n_t + t, p))
    lat = pl.BlockSpec((s_len, LANES), lambda i, p, t, *_: (i, p))
    ctx = pl.BlockSpec((n_ctx, LANES), lambda i, p, t, *_: (ctx_blk + i, p))
    return pl.pallas_call(
        functools.partial(_swa_kernel, nh=nh, n_units=2),
        grid_spec=pltpu.PrefetchScalarGridSpec(
            num_scalar_prefetch=1,
            grid=(b, n_kvblk, n_t),
            in_specs=[qspec, lat, lat, ctx, ctx],
            out_specs=qspec,
            scratch_shapes=[pltpu.VMEM((3 * tq + n_ctx, LANES), BF16),
                            pltpu.VMEM((3 * tq + n_ctx, 2 * LANES), BF16)]),
        out_shape=jax.ShapeDtypeStruct((b * s_len, q.shape[1]), BF16),
        compiler_params=_params("arbitrary", "arbitrary", "arbitrary"),
        name="attn_swa",
    )(sink, q, k, v, k, v)


def _global_kernel(q_ref, k_ref, v_ref, kc_ref, vc_ref, o_ref, kall_ref, vext_ref, s_ref, p_ref, *, nh, s_len):
    @pl.when(pl.program_id(2) == 0)
    def _():
        kall_ref[0:s_len, :] = k_ref[...]
        kall_ref[s_len:, :] = kc_ref[...]
        vext_ref[0:s_len, 0:LANES] = v_ref[...]
        vext_ref[s_len:, 0:LANES] = vc_ref[...]
        vext_ref[:, LANES:] = jnp.ones((vext_ref.shape[0], LANES), BF16)

    qs = _stack_heads(q_ref[...], LANES, nh)
    n_chunks = kall_ref.shape[0] // GLB_CHUNK
    mrun = None
    for c in range(n_chunks):
        ks = slice(c * GLB_CHUNK, (c + 1) * GLB_CHUNK)
        s = _dot_t(qs, kall_ref[ks, :])
        s_ref[:, ks] = s
        blk = _block_max(s)
        mrun = blk if mrun is None else jnp.maximum(mrun, blk)
    m = _lanes(_row_max(mrun), GLB_CHUNK)
    for c in range(n_chunks):
        ks = slice(c * GLB_CHUNK, (c + 1) * GLB_CHUNK)
        p_ref[:, ks] = jnp.exp2(s_ref[:, ks] - m).astype(BF16)
    acc = _dot_row_halves(p_ref, vext_ref[...])
    o_ref[...] = _unstack_heads(acc[:, :LANES] / acc[:, LANES:], LANES, nh).astype(BF16)


def _global_attention(q, k, v, dims):
    b, s_len, n_ctx = dims
    tq = GLB_TQ
    n_t = s_len // tq
    n_kvblk = k.shape[1] // LANES
    nh = q.shape[1] // k.shape[1]
    qw = nh * LANES
    ctx_blk = b * s_len // n_ctx
    n_keys = s_len + n_ctx
    qspec = pl.BlockSpec((tq, qw), lambda i, p, t: (i * n_t + t, p))
    lat = pl.BlockSpec((s_len, LANES), lambda i, p, t: (i, p))
    ctx = pl.BlockSpec((n_ctx, LANES), lambda i, p, t: (ctx_blk + i, p))
    return pl.pallas_call(
        functools.partial(_global_kernel, nh=nh, s_len=s_len),
        grid=(b, n_kvblk, n_t),
        in_specs=[qspec, lat, lat, ctx, ctx],
        out_specs=qspec,
        out_shape=jax.ShapeDtypeStruct((b * s_len, q.shape[1]), BF16),
        scratch_shapes=[pltpu.VMEM((n_keys, LANES), BF16),
                        pltpu.VMEM((n_keys, 2 * LANES), BF16),
                        pltpu.VMEM((nh * tq, n_keys), F32),
                        pltpu.VMEM((nh * tq, n_keys), BF16)],
        compiler_params=_params("arbitrary", "arbitrary", "arbitrary"),
        name="attn_global",
    )(q, k, v, k, v)


def _ctx_kernel(sink_ref, q_ref, kc_ref, vc_ref, o_ref, *, dh, nh, use_sink):
    tq = q_ref.shape[0]
    qs = _stack_heads(q_ref[...], dh, nh)
    s = _dot_t(qs, kc_ref[...])
    m = _row_max(_block_max(s))
    if use_sink:
        sink = _sink_rows(sink_ref, pl.program_id(1) * nh, nh, tq)
        m = jnp.maximum(m, sink)
    acc = _dot(jnp.exp2(s - _lanes(m, s.shape[1])).astype(BF16), _ones_ext(vc_ref[...]))
    den = acc[:, LANES:]
    if use_sink:
        den = den + jnp.exp2(sink - m)
    o_ref[...] = _unstack_heads(acc[:, :LANES] / den, dh, nh).astype(BF16)


def _ctx_attention(q, k, v, sink, dims, mixer):
    b, s_len, n_ctx = dims
    n_qh, n_kvh, dh = MIXER_HEADS[mixer]
    n_kvblk = k.shape[1] // LANES
    nh = n_qh // n_kvblk
    qw = nh * dh
    ctx_blk = b * s_len // n_ctx
    use_sink = sink is not None
    if not use_sink:
        sink = jnp.zeros((n_qh,), F32)
    return pl.pallas_call(
        functools.partial(_ctx_kernel, dh=dh, nh=nh, use_sink=use_sink),
        grid_spec=pltpu.PrefetchScalarGridSpec(
            num_scalar_prefetch=1,
            grid=(b, n_kvblk),
            in_specs=[pl.BlockSpec((n_ctx, qw), lambda i, p, *_: (ctx_blk + i, p)),
                      pl.BlockSpec((n_ctx, LANES), lambda i, p, *_: (ctx_blk + i, p)),
                      pl.BlockSpec((n_ctx, LANES), lambda i, p, *_: (ctx_blk + i, p))],
            out_specs=pl.BlockSpec((n_ctx, qw), lambda i, p, *_: (i, p))),
        out_shape=jax.ShapeDtypeStruct((b * n_ctx, q.shape[1]), BF16),
        compiler_params=_params("arbitrary", "arbitrary"),
        name="attn_ctx_m%d" % mixer,
    )(sink, q, k, v)


def _oproj_kernel(*refs, n_lat_tiles, with_ctx, moe):
    refs = list(refs)
    ylat_ref = refs.pop(0)
    yctx_ref = refs.pop(0) if with_ctx else None
    x_ref, mod_ref, n2_ref, wo_ref = refs[:4]
    refs = refs[4:]
    if moe:
        wr_ref, br_ref = refs[:2]
        refs = refs[2:]
    xo_ref, h_ref = refs[:2]
    y = ylat_ref[...]
    if with_ctx:
        y = jnp.where(pl.program_id(0) < n_lat_tiles, y, yctx_ref[...])
    x = x_ref[...] + mod_ref[2:3, :] * _dot(y, wo_ref[...])
    xo_ref[...] = x
    h = _rms_mod(x, n2_ref[...], mod_ref[4:5, :], mod_ref[3:4, :])
    h_ref[...] = h.astype(BF16)
    if moe:
        route_ref, counts_ref, base_ref = refs[2:5]
        h_hi, h_lo = _split_bf16(h)
        w_hi, w_lo = _split_bf16(wr_ref[...])
        logits = _dot(h_hi, w_hi) + _dot(h_hi, w_lo) + _dot(h_lo, w_hi) + br_ref[...]
        lane = lax.broadcasted_iota(jnp.int32, logits.shape, 1)
        v1 = jnp.max(logits, axis=-1, keepdims=True)
        i1 = jnp.min(jnp.where(logits == v1, lane, LANES), axis=-1, keepdims=True)
        rest = jnp.where(lane == i1, NEG_BIG, logits)
        v2 = jnp.max(rest, axis=-1, keepdims=True)
        i2 = jnp.min(jnp.where(rest == v2, lane, LANES), axis=-1, keepdims=True)
        e = jnp.exp(v2 - v1)
        w1 = 1.0 / (1.0 + e)
        w2 = e / (1.0 + e)
        @pl.when(pl.program_id(0) == 0)
        def _():
            base_ref[...] = jnp.zeros_like(base_ref)

        tm = logits.shape[0]
        pick1 = lane == i1
        pick2 = lane == i2
        cnt = jnp.where(pick1, 1.0, 0.0) + jnp.where(pick2, 1.0, 0.0)
        earlier = lax.broadcasted_iota(jnp.int32, (tm, tm), 0) > lax.broadcasted_iota(jnp.int32, (tm, tm), 1)
        before = _dot(jnp.where(earlier, 1.0, 0.0).astype(BF16), cnt.astype(BF16)) + base_ref[0:1, :]
        r1 = jnp.sum(jnp.where(pick1, before, 0.0), axis=-1, keepdims=True)
        r2 = jnp.sum(jnp.where(pick2, before, 0.0), axis=-1, keepdims=True)
        base_ref[...] = base_ref[...] + jnp.sum(cnt, axis=0, keepdims=True)
        counts_ref[...] = base_ref[...]
        route = jnp.where(lane == 0, i1.astype(F32), 0.0)
        route = jnp.where(lane == 1, i2.astype(F32), route)
        route = jnp.where(lane == 2, w1, route)
        route = jnp.where(lane == 3, w2, route)
        route = jnp.where(lane == 4, r1, route)
        route = jnp.where(lane == 5, r2, route)
        route_ref[...] = route


def _oproj(y_lat, y_ctx, xs, mod, norm2, w_o, router, dims, last):
    b, s_len, n_ctx = dims
    d = xs.shape[1]
    tm = _row_tile(s_len, b * n_ctx, 512)
    n_lat_tiles = b * s_len // tm
    tiles_per_sample = s_len // tm
    with_ctx = not last
    n_rows = b * s_len if last else xs.shape[0]
    n_tiles = n_rows // tm
    moe = router is not None

    def mod_idx(i):
        return jnp.where(i < n_lat_tiles, i // tiles_per_sample, b)

    row = pl.BlockSpec((tm, d), lambda i: (i, 0))
    in_specs = [pl.BlockSpec((tm, y_lat.shape[1]), lambda i: (jnp.minimum(i, n_lat_tiles - 1), 0))]
    args = [y_lat]
    if with_ctx:
        in_specs.append(pl.BlockSpec((tm, y_ctx.shape[1]), lambda i: (jnp.maximum(i - n_lat_tiles, 0), 0)))
        args.append(y_ctx)
    in_specs += [row,
                 pl.BlockSpec((None, 6, d), lambda i: (mod_idx(i), 0, 0)),
                 pl.BlockSpec((1, d), lambda i: (0, 0)),
                 pl.BlockSpec(w_o.shape, lambda i: (0, 0))]
    args += [xs, mod, norm2.reshape(1, d), w_o.astype(BF16)]
    out_specs = [row, row]
    out_shape = [jax.ShapeDtypeStruct((n_rows, d), F32), jax.ShapeDtypeStruct((n_rows, d), BF16)]
    if moe:
        w_router, b_router = router
        n_e = w_router.shape[1]
        wr = jnp.pad(w_router, ((0, 0), (0, LANES - n_e)))
        br = jnp.pad(b_router, (0, LANES - n_e), constant_values=NEG_BIG).reshape(1, LANES)
        in_specs += [pl.BlockSpec(wr.shape, lambda i: (0, 0)), pl.BlockSpec(br.shape, lambda i: (0, 0))]
        args += [wr, br]
        out_specs += [pl.BlockSpec((tm, LANES), lambda i: (i, 0)), pl.BlockSpec((8, LANES), lambda i: (0, 0))]
        out_shape += [jax.ShapeDtypeStruct((n_rows, LANES), F32), jax.ShapeDtypeStruct((8, LANES), F32)]
    return pl.pallas_call(
        functools.partial(_oproj_kernel, n_lat_tiles=n_lat_tiles, with_ctx=with_ctx, moe=moe),
        grid=(n_tiles,),
        in_specs=in_specs,
        out_specs=out_specs,
        out_shape=out_shape,
        scratch_shapes=[pltpu.VMEM((8, LANES), F32)] if moe else [],
        compiler_params=_params("arbitrary"),
        name="oproj_moe" if moe else "oproj",
    )(*args)


def _ffn_kernel(h_ref, wg_ref, wu_ref, wd_ref, x_ref, mod_ref, o_ref, acc_ref):
    f = pl.program_id(1)

    @pl.when(f == 0)
    def _():
        acc_ref[...] = jnp.zeros_like(acc_ref)

    h = h_ref[...]
    a = _silu(_dot(h, wg_ref[...])) * _dot(h, wu_ref[...])
    acc_ref[...] += _dot(a.astype(BF16), wd_ref[...])

    @pl.when(f == pl.num_programs(1) - 1)
    def _():
        o_ref[...] = x_ref[...] + mod_ref[5:6, :] * acc_ref[...]


def _ffn(h, xs, mod, w_in, w_out, dims):
    b, s_len, n_ctx = dims
    n_rows, d = h.shape
    d_ff = w_out.shape[0]
    tm = _row_tile(s_len, b * n_ctx, 1024)
    tf = FFN_TF
    n_f = d_ff // tf
    n_lat_tiles = b * s_len // tm
    tiles_per_sample = s_len // tm

    def mod_idx(i):
        return jnp.where(i < n_lat_tiles, i // tiles_per_sample, b)

    w_in = w_in.astype(BF16)
    return pl.pallas_call(
        _ffn_kernel,
        grid=(n_rows // tm, n_f),
        in_specs=[pl.BlockSpec((tm, d), lambda i, f: (i, 0)),
                  pl.BlockSpec((d, tf), lambda i, f: (0, f)),
                  pl.BlockSpec((d, tf), lambda i, f: (0, n_f + f)),
                  pl.BlockSpec((tf, d), lambda i, f: (f, 0)),
                  pl.BlockSpec((tm, d), lambda i, f: (i, 0)),
                  pl.BlockSpec((None, 6, d), lambda i, f: (mod_idx(i), 0, 0))],
        out_specs=pl.BlockSpec((tm, d), lambda i, f: (i, 0)),
        out_shape=jax.ShapeDtypeStruct((n_rows, d), F32),
        scratch_shapes=[pltpu.VMEM((tm, d), F32)],
        compiler_params=_params("arbitrary", "arbitrary"),
        name="ffn_dense",
    )(h, w_in, w_in, w_out.astype(BF16), xs, mod)


def _moe_ffn_kernel(te_ref, na_ref, h_ref, wg_ref, wu_ref, wd_ref, o_ref, acc_ref):
    i, f = pl.program_id(0), pl.program_id(1)

    @pl.when(i < na_ref[0])
    def _():
        @pl.when(f == 0)
        def _():
            acc_ref[...] = jnp.zeros_like(acc_ref)

        h = h_ref[...]
        a = _silu(_dot(h, wg_ref[...].astype(BF16))) * _dot(h, wu_ref[...].astype(BF16))
        acc_ref[...] += _dot(a.astype(BF16), wd_ref[...].astype(BF16))

        @pl.when(f == pl.num_programs(1) - 1)
        def _():
            o_ref[...] = acc_ref[...].astype(o_ref.dtype)


def _moe_ffn(hs, tile_expert, n_active, w_in, w_out):
    n_rows, d = hs.shape
    d_ff = w_out.shape[1]
    tm, tf = MOE_TM, FFN_TF
    n_f = d_ff // tf

    def row(i, f, te, na):
        return (jnp.minimum(i, na[0] - 1), 0)

    def expert(i, te, na):
        return te[jnp.minimum(i, na[0] - 1)]

    return pl.pallas_call(
        _moe_ffn_kernel,
        grid_spec=pltpu.PrefetchScalarGridSpec(
            num_scalar_prefetch=2,
            grid=(n_rows // tm, n_f),
            in_specs=[pl.BlockSpec((tm, d), row),
                      pl.BlockSpec((None, d, tf), lambda i, f, te, na: (expert(i, te, na), 0, f)),
                      pl.BlockSpec((None, d, tf), lambda i, f, te, na: (expert(i, te, na), 0, n_f + f)),
                      pl.BlockSpec((None, tf, d), lambda i, f, te, na: (expert(i, te, na), f, 0))],
            out_specs=pl.BlockSpec((tm, d), row),
            scratch_shapes=[pltpu.VMEM((tm, d), F32)]),
        out_shape=jax.ShapeDtypeStruct((n_rows, d), BF16),
        compiler_params=_params("arbitrary", "arbitrary"),
        name="moe_ffn",
    )(tile_expert, n_active, hs, w_in, w_in, w_out)


def _combine_kernel(x_ref, a_ref, b_ref, route_ref, mod_ref, o_ref):
    w1 = route_ref[:, 2:3]
    w2 = route_ref[:, 3:4]
    mix = w1 * a_ref[...].astype(F32) + w2 * b_ref[...].astype(F32)
    o_ref[...] = x_ref[...] + mod_ref[5:6, :] * mix


def _combine(xs, ya, yb, route, mod, dims):
    b, s_len, n_ctx = dims
    n_rows, d = ya.shape
    tm = _row_tile(s_len, b * n_ctx, 512)
    n_lat_tiles = b * s_len // tm
    tiles_per_sample = s_len // tm

    def mod_idx(i):
        return jnp.where(i < n_lat_tiles, i // tiles_per_sample, b)

    row = pl.BlockSpec((tm, d), lambda i: (i, 0))
    return pl.pallas_call(
        _combine_kernel,
        grid=(n_rows // tm,),
        in_specs=[row, row, row,
                  pl.BlockSpec((tm, LANES), lambda i: (i, 0)),
                  pl.BlockSpec((None, 6, d), lambda i: (mod_idx(i), 0, 0))],
        out_specs=row,
        out_shape=jax.ShapeDtypeStruct((n_rows, d), F32),
        compiler_params=_params("arbitrary"),
        name="moe_combine",
    )(xs, ya, yb, route, mod)


def _route_plan(idx, rank, counts, tm):
    n = idx.shape[0]
    tiles_per = (counts + tm - 1) // tm
    tile_end = jnp.cumsum(tiles_per)
    tile_start = tile_end - tiles_per
    experts = jnp.arange(N_EXPERTS, dtype=jnp.int32)
    start = jnp.sum(jnp.where(idx[:, :, None] == experts, tile_start, 0), axis=-1)
    slot = start * tm + rank
    n_tiles = (2 * n) // tm + N_EXPERTS
    tile_ids = jnp.arange(n_tiles, dtype=jnp.int32)
    tile_expert = jnp.minimum(jnp.sum((tile_end[None, :] <= tile_ids[:, None]).astype(jnp.int32), axis=1),
                              N_EXPERTS - 1)
    return slot, tile_expert, tile_end[-1:].astype(jnp.int32), n_tiles


def _moe(h, xs, route, counts, mod, w_exp_in, w_exp_out, dims):
    idx = route[:, 0:2].astype(jnp.int32)
    rank = route[:, 4:6].astype(jnp.int32)
    slot, tile_expert, n_active, n_tiles = _route_plan(idx, rank, counts[0, :N_EXPERTS].astype(jnp.int32), MOE_TM)
    hs = jnp.zeros((n_tiles * MOE_TM, h.shape[1]), h.dtype)
    hs = hs.at[slot[:, 0]].set(h, unique_indices=True)
    hs = hs.at[slot[:, 1]].set(h, unique_indices=True)
    ys = _moe_ffn(hs, tile_expert, n_active, w_exp_in, w_exp_out)
    ya = jnp.take(ys, slot[:, 0], axis=0)
    yb = jnp.take(ys, slot[:, 1], axis=0)
    return _combine(xs, ya, yb, route, mod, dims)


def _layer(xs, cond, p, dims, mixer, last):
    b, s_len, n_ctx = dims
    mod = _adaln(cond, p["w_mod"], p["b_mod"])
    q, k, v = _qkv(xs, mod, p["norm1"], p["w_qkv"], p["q_norm"], p["k_norm"], dims, mixer)
    if mixer == 0:
        y_lat = _na_attention(q, k, v, p["rel_bias"], dims)
    elif mixer == 1:
        y_lat = _swa_attention(q, k, v, p["sink"], dims)
    else:
        y_lat = _global_attention(q, k, v, dims)
    y_ctx = None if last else _ctx_attention(q, k, v, p.get("sink"), dims, mixer)
    router = (p["w_router"], p["b_router"]) if "w_router" in p else None
    outs = _oproj(y_lat, y_ctx, xs, mod, p["norm2"], p["w_o"], router, dims, last)
    if router is None:
        xs, h = outs
        return _ffn(h, xs, mod, p["w_ffn_in"], p["w_ffn_out"], dims)
    xs, h, route, counts = outs
    return _moe(h, xs, route, counts, mod, p["w_exp_in"], p["w_exp_out"], dims)


def kernel(x, c, ctx, c_ctx, l0_w_mod, l0_b_mod, l0_norm1, l0_norm2, l0_w_qkv, l0_q_norm, l0_k_norm, l0_rel_bias, l0_w_o, l0_w_ffn_in, l0_w_ffn_out, l1_w_mod, l1_b_mod, l1_norm1, l1_norm2, l1_w_qkv, l1_q_norm, l1_k_norm, l1_sink, l1_w_o, l1_w_router, l1_b_router, l1_w_exp_in, l1_w_exp_out, l2_w_mod, l2_b_mod, l2_norm1, l2_norm2, l2_w_qkv, l2_q_norm, l2_k_norm, l2_w_o, l2_w_ffn_in, l2_w_ffn_out, l3_w_mod, l3_b_mod, l3_norm1, l3_norm2, l3_w_qkv, l3_q_norm, l3_k_norm, l3_rel_bias, l3_w_o, l3_w_router, l3_b_router, l3_w_exp_in, l3_w_exp_out):
    b, s_len, d = x.shape
    n_ctx = ctx.shape[1]
    dims = (b, s_len, n_ctx)
    layers = (
        dict(w_mod=l0_w_mod, b_mod=l0_b_mod, norm1=l0_norm1, norm2=l0_norm2, w_qkv=l0_w_qkv, q_norm=l0_q_norm,
             k_norm=l0_k_norm, rel_bias=l0_rel_bias, w_o=l0_w_o, w_ffn_in=l0_w_ffn_in, w_ffn_out=l0_w_ffn_out),
        dict(w_mod=l1_w_mod, b_mod=l1_b_mod, norm1=l1_norm1, norm2=l1_norm2, w_qkv=l1_w_qkv, q_norm=l1_q_norm,
             k_norm=l1_k_norm, sink=l1_sink, w_o=l1_w_o, w_router=l1_w_router, b_router=l1_b_router,
             w_exp_in=l1_w_exp_in, w_exp_out=l1_w_exp_out),
        dict(w_mod=l2_w_mod, b_mod=l2_b_mod, norm1=l2_norm1, norm2=l2_norm2, w_qkv=l2_w_qkv, q_norm=l2_q_norm,
             k_norm=l2_k_norm, w_o=l2_w_o, w_ffn_in=l2_w_ffn_in, w_ffn_out=l2_w_ffn_out),
        dict(w_mod=l3_w_mod, b_mod=l3_b_mod, norm1=l3_norm1, norm2=l3_norm2, w_qkv=l3_w_qkv, q_norm=l3_q_norm,
             k_norm=l3_k_norm, rel_bias=l3_rel_bias, w_o=l3_w_o, w_router=l3_w_router, b_router=l3_b_router,
             w_exp_in=l3_w_exp_in, w_exp_out=l3_w_exp_out),
    )
    xs = jnp.concatenate([x.reshape(b * s_len, d), ctx.reshape(b * n_ctx, d)], axis=0)
    pad_rows = -(b + 1) % 8
    cond = jnp.concatenate([c, c_ctx[None, :], jnp.zeros((pad_rows, d), F32)], axis=0)
    n_layers = len(layers)
    for i, p in enumerate(layers):
        xs = _layer(xs, cond, p, dims, i % 3, i == n_layers - 1)
    return xs.reshape(b, s_len, d)
```

```python
import functools

import jax
import jax.numpy as jnp
from jax import lax
from jax.experimental import pallas as pl
from jax.experimental.pallas import tpu as pltpu

F32 = jnp.float32
BF16 = jnp.bfloat16

GRID_W = 64
NORM_EPS = 1e-6
ROPE_THETA = 10000.0
NA_WIN_ROWS = 8
NA_WIN_COLS = 16
SW_WINDOW = 128
N_EXPERTS = 8
MIXER_HEADS = ((16, 16, 64), (16, 4, 64), (8, 4, 128))

LANES = 128
VMEM_LIMIT_BYTES = 56 * 1024 * 1024
NEG_BIG = -1e30
LOG2E = 1.4426950408889634

QKV_CHUNK = 512
GLB_TQ = 256
GLB_CHUNK = 768
NA_CTX_ROWS = 1024
NA_UNROLL = 8
MOE_TM = 1024
FFN_TF = 512


def _params(*sem):
    return pltpu.CompilerParams(dimension_semantics=sem, vmem_limit_bytes=VMEM_LIMIT_BYTES)


def _row_tile(n_lat_per_sample, n_ctx_rows, cap):
    for tm in (1024, 512, 256, 128):
        if tm <= cap and n_lat_per_sample % tm == 0 and n_ctx_rows % tm == 0:
            return tm
    raise ValueError("no row tile fits")


def _split_bf16(a):
    hi = a.astype(BF16)
    lo = (a - hi.astype(F32)).astype(BF16)
    return hi, lo


def _dot(a, b):
    return jnp.dot(a, b, preferred_element_type=F32)


def _dot_t(a, b):
    return lax.dot_general(a, b, (((1,), (1,)), ((), ())), preferred_element_type=F32)


def _silu(g):
    return g / (1.0 + jnp.exp(-g))


def _rms_mod(x, gain, scale, shift):
    ms = jnp.mean(x * x, axis=-1, keepdims=True)
    return x * lax.rsqrt(ms + NORM_EPS) * gain * (1.0 + scale) + shift


def _adaln_kernel(c_ref, w_ref, b_ref, o_ref):
    a_hi, a_lo = _split_bf16(_silu(c_ref[...]))
    w_hi, w_lo = _split_bf16(w_ref[...])
    o_ref[...] = _dot(a_hi, w_hi) + _dot(a_hi, w_lo) + _dot(a_lo, w_hi) + b_ref[...]


def _adaln(cond, w_mod, b_mod):
    r, d = cond.shape
    n = w_mod.shape[1]
    tn = 1536
    out = pl.pallas_call(
        _adaln_kernel,
        grid=(n // tn,),
        in_specs=[pl.BlockSpec((r, d), lambda j: (0, 0)),
                  pl.BlockSpec((d, tn), lambda j: (0, j)),
                  pl.BlockSpec((1, tn), lambda j: (0, j))],
        out_specs=pl.BlockSpec((r, tn), lambda j: (0, j)),
        out_shape=jax.ShapeDtypeStruct((r, n), F32),
        compiler_params=_params("arbitrary"),
        name="adaln",
    )(cond, w_mod, b_mod.reshape(1, n))
    return out.reshape(r, 6, d)


def _rot_half(z, dh):
    if dh == LANES:
        return pltpu.roll(z, LANES // 2, axis=1)
    lane = lax.broadcasted_iota(jnp.int32, z.shape, 1)
    from_right = pltpu.roll(z, LANES - dh // 2, axis=1)
    from_left = pltpu.roll(z, dh // 2, axis=1)
    return jnp.where((lane % dh) < dh // 2, from_right, from_left)


def _head_mean_sq(z, dh):
    z2 = z * z
    if dh == LANES:
        return jnp.broadcast_to(jnp.sum(z2, axis=-1, keepdims=True), z.shape) * (1.0 / dh)
    low = lax.broadcasted_iota(jnp.int32, z.shape, 1) < dh
    s_low = jnp.sum(jnp.where(low, z2, 0.0), axis=-1, keepdims=True)
    s_high = jnp.sum(jnp.where(low, 0.0, z2), axis=-1, keepdims=True)
    return jnp.where(low, s_low, s_high) * (1.0 / dh)


def _qkv_kernel(*refs, n_q, n_kv, dh, rope):
    if rope:
        x_ref, mod_ref, n1_ref, w_ref, g_ref, cos_ref, sin_ref, q_ref, k_ref, v_ref = refs
    else:
        x_ref, mod_ref, n1_ref, w_ref, g_ref, q_ref, k_ref, v_ref = refs
    h = _rms_mod(x_ref[...], n1_ref[...], mod_ref[1:2, :], mod_ref[0:1, :]).astype(BF16)
    n_qk = n_q + n_kv
    cw = QKV_CHUNK
    for c in range((n_qk + n_kv) // cw):
        y = _dot(h, w_ref[:, c * cw:(c + 1) * cw])
        for s in range(cw // LANES):
            col = c * cw + s * LANES
            z = y[:, s * LANES:(s + 1) * LANES]
            if col < n_qk:
                z = z * lax.rsqrt(_head_mean_sq(z, dh) + NORM_EPS) * g_ref[:, col:col + LANES]
                if rope:
                    z = z * cos_ref[...] + _rot_half(z, dh) * sin_ref[...]
            z = z.astype(BF16)
            if col < n_q:
                q_ref[:, col:col + LANES] = z
            elif col < n_qk:
                k_ref[:, col - n_q:col - n_q + LANES] = z
            else:
                v_ref[:, col - n_qk:col - n_qk + LANES] = z


def _rope_tables(s_len, dh, tm):
    n_freq = dh // 4
    inv_freq = ROPE_THETA ** (-jnp.arange(n_freq, dtype=F32) / n_freq)
    t = jnp.arange(s_len)
    row = (t // GRID_W).astype(F32)
    col = (t % GRID_W).astype(F32)
    ang = jnp.concatenate([row[:, None] * inv_freq, col[:, None] * inv_freq], axis=-1)
    cos, sin = jnp.cos(ang), jnp.sin(ang)
    reps = LANES // dh
    cos_t = jnp.tile(jnp.concatenate([cos, cos], axis=-1), (1, reps))
    sin_t = jnp.tile(jnp.concatenate([-sin, sin], axis=-1), (1, reps))
    cos_t = jnp.concatenate([cos_t, jnp.ones((tm, LANES), F32)], axis=0)
    sin_t = jnp.concatenate([sin_t, jnp.zeros((tm, LANES), F32)], axis=0)
    return cos_t, sin_t


def _qkv(xs, mod, norm1, w_qkv, q_gain, k_gain, dims, mixer):
    b, s_len, n_ctx = dims
    n_rows, d = xs.shape
    n_qh, n_kvh, dh = MIXER_HEADS[mixer]
    n_q, n_kv = n_qh * dh, n_kvh * dh
    rope = mixer != 0
    tm = _row_tile(s_len, b * n_ctx, 512)
    n_lat_tiles = b * s_len // tm
    tiles_per_sample = s_len // tm

    gains = jnp.concatenate([jnp.tile(q_gain * (dh ** -0.5 * LOG2E), n_qh),
                             jnp.tile(k_gain, n_kvh)]).reshape(1, n_q + n_kv)

    def mod_idx(i):
        return jnp.where(i < n_lat_tiles, i // tiles_per_sample, b)

    in_specs = [pl.BlockSpec((tm, d), lambda i: (i, 0)),
                pl.BlockSpec((None, 6, d), lambda i: (mod_idx(i), 0, 0)),
                pl.BlockSpec((1, d), lambda i: (0, 0)),
                pl.BlockSpec(w_qkv.shape, lambda i: (0, 0)),
                pl.BlockSpec(gains.shape, lambda i: (0, 0))]
    args = [xs, mod, norm1.reshape(1, d), w_qkv.astype(BF16), gains]
    if rope:
        cos_t, sin_t = _rope_tables(s_len, dh, tm)

        def pos_idx(i):
            return jnp.where(i < n_lat_tiles, i % tiles_per_sample, tiles_per_sample)

        in_specs += [pl.BlockSpec((tm, LANES), lambda i: (pos_idx(i), 0))] * 2
        args += [cos_t, sin_t]
    return pl.pallas_call(
        functools.partial(_qkv_kernel, n_q=n_q, n_kv=n_kv, dh=dh, rope=rope),
        grid=(n_rows // tm,),
        in_specs=in_specs,
        out_specs=[pl.BlockSpec((tm, n_q), lambda i: (i, 0)),
                   pl.BlockSpec((tm, n_kv), lambda i: (i, 0)),
                   pl.BlockSpec((tm, n_kv), lambda i: (i, 0))],
        out_shape=[jax.ShapeDtypeStruct((n_rows, n_q), BF16),
                   jax.ShapeDtypeStruct((n_rows, n_kv), BF16),
                   jax.ShapeDtypeStruct((n_rows, n_kv), BF16)],
        compiler_params=_params("arbitrary"),
        name="qkv_m%d" % mixer,
    )(*args)


def _stack_heads(q, dh, nh):
    if dh == LANES:
        return jnp.concatenate([q[:, h * LANES:(h + 1) * LANES] for h in range(nh)], axis=0)
    per_kv = nh // 2
    tq = q.shape[0]
    lane = lax.broadcasted_iota(jnp.int32, (tq, LANES), 1)
    ops = []
    for h in range(nh):
        slot = h // per_kv
        chunk = q[:, (h // 2) * LANES:(h // 2 + 1) * LANES].astype(F32)
        if h % 2 != slot:
            chunk = pltpu.roll(chunk, dh, axis=1)
        keep = lane < dh if slot == 0 else lane >= dh
        ops.append(jnp.where(keep, chunk, 0.0).astype(BF16))
    return jnp.concatenate(ops, axis=0)


def _unstack_heads(o, dh, nh):
    tq = o.shape[0] // nh
    if dh == LANES:
        return jnp.concatenate([o[h * tq:(h + 1) * tq] for h in range(nh)], axis=1)
    per_kv = nh // 2
    lane = lax.broadcasted_iota(jnp.int32, (tq, LANES), 1)
    chunks = []
    for c in range(nh // 2):
        parts = []
        for h in (2 * c, 2 * c + 1):
            oh = o[h * tq:(h + 1) * tq]
            if h % 2 != h // per_kv:
                oh = pltpu.roll(oh, dh, axis=1)
            parts.append(oh)
        chunks.append(jnp.where(lane < dh, parts[0], parts[1]))
    return jnp.concatenate(chunks, axis=1)


def _lanes(x, n):
    return x if n == LANES else jnp.concatenate([x] * (n // LANES), axis=1)


def _block_max(s):
    return functools.reduce(jnp.maximum, [s[:, j * LANES:(j + 1) * LANES] for j in range(s.shape[1] // LANES)])


def _row_max(blk):
    return jnp.broadcast_to(jnp.max(blk, axis=-1, keepdims=True), blk.shape)


def _dot_row_halves(p, v):
    half = p.shape[0] // 2
    return jnp.concatenate([_dot(p[0:half, :], v), _dot(p[half:, :], v)], axis=0)


def _ones_ext(v):
    return jnp.concatenate([v, jnp.ones_like(v)], axis=1)


def _sink_rows(sink_ref, first_head, nh, tq):
    return jnp.concatenate([jnp.full((tq, LANES), sink_ref[first_head + h] * LOG2E, F32) for h in range(nh)], axis=0)


def _na_kernel(q_ref, k_ref, v_ref, kc_ref, vc_ref, bias_ref, o_ref,
               qs_ref, vext_ref, mc_ref, numc_ref, lc_ref, *, rows):
    s_len = q_ref.shape[0]
    band = NA_WIN_ROWS * GRID_W
    lane = lax.broadcasted_iota(jnp.int32, (s_len, LANES), 1)
    q = q_ref[...]
    qs_ref[0:s_len, :] = jnp.where(lane < 64, q, jnp.zeros_like(q))
    qs_ref[s_len:, :] = jnp.where(lane >= 64, q, jnp.zeros_like(q))
    vext_ref[:, 0:LANES] = v_ref[...]
    vext_ref[:, LANES:] = jnp.ones((s_len, LANES), BF16)

    kc = kc_ref[...]
    vcx = _ones_ext(vc_ref[...])
    for c in range(2 * s_len // NA_CTX_ROWS):
        rs = slice(c * NA_CTX_ROWS, (c + 1) * NA_CTX_ROWS)
        s = _dot_t(qs_ref[rs, :], kc)
        m = _row_max(_block_max(s))
        acc = _dot(jnp.exp2(s - _lanes(m, s.shape[1])).astype(BF16), vcx)
        mc_ref[rs, :] = m
        numc_ref[rs, :] = acc[:, :LANES]
        lc_ref[rs, :] = acc[:, LANES:]

    lane_q = lax.broadcasted_iota(jnp.int32, (GRID_W, LANES), 1)

    def both_heads(ref, q0):
        return jnp.concatenate([ref[pl.ds(q0, GRID_W), :], ref[pl.ds(s_len + q0, GRID_W), :]], axis=0)

    def one_row(r, carry):
        r0 = jnp.clip(r - NA_WIN_ROWS // 2, 0, rows - NA_WIN_ROWS)
        q0 = pl.multiple_of(r * GRID_W, GRID_W)
        k0 = pl.multiple_of(r0 * GRID_W, GRID_W)
        bias = bias_ref[r0 - r + NA_WIN_ROWS - 1].reshape(2 * GRID_W, band)
        qs = both_heads(qs_ref, q0)
        half = band // 2
        k1 = pl.multiple_of(k0 + half, GRID_W)
        s = jnp.concatenate([_dot_t(qs, k_ref[pl.ds(k0, half), :]), _dot_t(qs, k_ref[pl.ds(k1, half), :])],
                            axis=1) + bias
        mc = both_heads(mc_ref, q0)
        m = jnp.maximum(_row_max(_block_max(s)), mc)
        acc = _dot(jnp.exp2(s - _lanes(m, band)).astype(BF16), vext_ref[pl.ds(k0, band), :])
        alpha = jnp.exp2(mc - m)
        num = acc[:, :LANES] + alpha * both_heads(numc_ref, q0)
        den = acc[:, LANES:] + alpha * both_heads(lc_ref, q0)
        o = num / den
        o_ref[pl.ds(q0, GRID_W), :] = jnp.where(lane_q < 64, o[:GRID_W], o[GRID_W:]).astype(BF16)
        return carry

    lax.fori_loop(0, rows, one_row, 0, unroll=NA_UNROLL)


def _na_bias_tables(rel_bias):
    n_h = rel_bias.shape[0]
    col = jnp.arange(GRID_W)
    col_start = jnp.clip(col - NA_WIN_COLS // 2, 0, GRID_W - NA_WIN_COLS)
    in_win = (col[None, :] >= col_start[:, None]) & (col[None, :] < col_start[:, None] + NA_WIN_COLS)
    col_idx = jnp.clip(col[None, :] - col[:, None] + NA_WIN_COLS - 1, 0, 2 * NA_WIN_COLS - 2)
    masked = jnp.where(in_win[None, None], rel_bias[:, :, col_idx] * LOG2E, NEG_BIG)
    variants = jnp.stack([masked[:, j:j + NA_WIN_ROWS] for j in range(NA_WIN_ROWS)], axis=1)
    variants = variants.transpose(0, 1, 3, 2, 4).reshape(n_h, NA_WIN_ROWS, GRID_W, NA_WIN_ROWS * GRID_W)
    return variants.reshape(n_h // 2, 2, NA_WIN_ROWS, GRID_W, NA_WIN_ROWS * GRID_W).transpose(0, 2, 1, 3, 4)


def _na_attention(q, k, v, rel_bias, dims):
    b, s_len, n_ctx = dims
    rows = s_len // GRID_W
    n_pairs = q.shape[1] // LANES
    bias = _na_bias_tables(rel_bias)
    ctx_blk = b * s_len // n_ctx
    lat = pl.BlockSpec((s_len, LANES), lambda p, i: (i, p))
    ctx = pl.BlockSpec((n_ctx, LANES), lambda p, i: (ctx_blk + i, p))
    return pl.pallas_call(
        functools.partial(_na_kernel, rows=rows),
        grid=(n_pairs, b),
        in_specs=[lat, lat, lat, ctx, ctx,
                  pl.BlockSpec((None,) + bias.shape[1:], lambda p, i: (p, 0, 0, 0, 0))],
        out_specs=lat,
        out_shape=jax.ShapeDtypeStruct((b * s_len, q.shape[1]), BF16),
        scratch_shapes=[pltpu.VMEM((2 * s_len, LANES), BF16),
                        pltpu.VMEM((s_len, 2 * LANES), BF16),
                        pltpu.VMEM((2 * s_len, LANES), F32),
                        pltpu.VMEM((2 * s_len, LANES), F32),
                        pltpu.VMEM((2 * s_len, LANES), F32)],
        compiler_params=_params("arbitrary", "arbitrary"),
        name="attn_na",
    )(q, k, v, k, v, bias)


def _swa_kernel(sink_ref, q_ref, k_ref, v_ref, kc_ref, vc_ref, o_ref, kwin_ref, vwin_ref, *, nh, n_units):
    tq = q_ref.shape[0]
    p_blk, t = pl.program_id(1), pl.program_id(2)
    n_t = pl.num_programs(2)

    @pl.when(t == 0)
    def _():
        kwin_ref[3 * tq:, :] = kc_ref[...]
        vwin_ref[3 * tq:, 0:LANES] = vc_ref[...]
        vwin_ref[:, LANES:] = jnp.ones((vwin_ref.shape[0], LANES), BF16)

    starts = (jnp.maximum(t - 1, 0) * tq, t * tq, jnp.minimum(t + 1, n_t - 1) * tq)
    for j, k0 in enumerate(starts):
        k0 = pl.multiple_of(k0, tq)
        kwin_ref[j * tq:(j + 1) * tq, :] = k_ref[pl.ds(k0, tq), :]
        vwin_ref[j * tq:(j + 1) * tq, 0:LANES] = v_ref[pl.ds(k0, tq), :]

    row = lax.broadcasted_iota(jnp.int32, (tq, tq), 0)
    col = lax.broadcasted_iota(jnp.int32, (tq, tq), 1)
    bias_prev = jnp.where(t > 0, jnp.where(col >= row, 0.0, NEG_BIG), NEG_BIG)
    bias_next = jnp.where(t < n_t - 1, jnp.where(col <= row, 0.0, NEG_BIG), NEG_BIG)

    qs_all = _stack_heads(q_ref[...], 64, nh)
    sink_all = _sink_rows(sink_ref, p_blk * nh, nh, tq)
    hu = nh // n_units
    kwin = kwin_ref[...]
    vwin = vwin_ref[...]
    outs = []
    for u in range(n_units):
        qs = qs_all[u * hu * tq:(u + 1) * hu * tq]
        sink = sink_all[u * hu * tq:(u + 1) * hu * tq]
        s = _dot_t(qs, kwin)

        def masked(blk, bias):
            return (blk.reshape(hu, tq, tq) + bias[None]).reshape(hu * tq, tq)

        s = jnp.concatenate([masked(s[:, 0:tq], bias_prev), s[:, tq:2 * tq],
                             masked(s[:, 2 * tq:3 * tq], bias_next), s[:, 3 * tq:]], axis=1)
        m = jnp.maximum(_row_max(_block_max(s)), sink)
        acc = _dot(jnp.exp2(s - _lanes(m, s.shape[1])).astype(BF16), vwin)
        outs.append(acc[:, :LANES] / (acc[:, LANES:] + jnp.exp2(sink - m)))
    o_ref[...] = _unstack_heads(jnp.concatenate(outs, axis=0), 64, nh).astype(BF16)


def _swa_attention(q, k, v, sink, dims):
    b, s_len, n_ctx = dims
    tq = SW_WINDOW
    n_t = s_len // tq
    n_kvblk = k.shape[1] // LANES
    nh = q.shape[1] // k.shape[1] * 2
    qw = nh * 64
    ctx_blk = b * s_len // n_ctx
    qspec = pl.BlockSpec((tq, qw), lambda i, p, t, *_: (i * n_t + t, p))
    lat = pl.BlockSpec((s_len, LANES), lambda i, p, t, *_: (i, p))
    ctx = pl.BlockSpec((n_ctx, LANES), lambda i, p, t, *_: (ctx_blk + i, p))
    return pl.pallas_call(
        functools.partial(_swa_kernel, nh=nh, n_units=2),
        grid_spec=pltpu.PrefetchScalarGridSpec(
            num_scalar_prefetch=1,
            grid=(b, n_kvblk, n_t),
            in_specs=[qspec, lat, lat, ctx, ctx],
            out_specs=qspec,
            scratch_shapes=[pltpu.VMEM((3 * tq + n_ctx, LANES), BF16),
                            pltpu.VMEM((3 * tq + n_ctx, 2 * LANES), BF16)]),
        out_shape=jax.ShapeDtypeStruct((b * s_len, q.shape[1]), BF16),
        compiler_params=_params("arbitrary", "arbitrary", "arbitrary"),
        name="attn_swa",
    )(sink, q, k, v, k, v)


def _global_kernel(q_ref, k_ref, v_ref, kc_ref, vc_ref, o_ref, kall_ref, vext_ref, s_ref, p_ref, *, nh, s_len):
    @pl.when(pl.program_id(2) == 0)
    def _():
        kall_ref[0:s_len, :] = k_ref[...]
        kall_ref[s_len:, :] = kc_ref[...]
        vext_ref[0:s_len, 0:LANES] = v_ref[...]
        vext_ref[s_len:, 0:LANES] = vc_ref[...]
        vext_ref[:, LANES:] = jnp.ones((vext_ref.shape[0], LANES), BF16)

    qs = _stack_heads(q_ref[...], LANES, nh)
    n_chunks = kall_ref.shape[0] // GLB_CHUNK
    mrun = None
    for c in range(n_chunks):
        ks = slice(c * GLB_CHUNK, (c + 1) * GLB_CHUNK)
        s = _dot_t(qs, kall_ref[ks, :])
        s_ref[:, ks] = s
        blk = _block_max(s)
        mrun = blk if mrun is None else jnp.maximum(mrun, blk)
    m = _lanes(_row_max(mrun), GLB_CHUNK)
    for c in range(n_chunks):
        ks = slice(c * GLB_CHUNK, (c + 1) * GLB_CHUNK)
        p_ref[:, ks] = jnp.exp2(s_ref[:, ks] - m).astype(BF16)
    acc = _dot_row_halves(p_ref, vext_ref[...])
    o_ref[...] = _unstack_heads(acc[:, :LANES] / acc[:, LANES:], LANES, nh).astype(BF16)


def _global_attention(q, k, v, dims):
    b, s_len, n_ctx = dims
    tq = GLB_TQ
    n_t = s_len // tq
    n_kvblk = k.shape[1] // LANES
    nh = q.shape[1] // k.shape[1]
    qw = nh * LANES
    ctx_blk = b * s_len // n_ctx
    n_keys = s_len + n_ctx
    qspec = pl.BlockSpec((tq, qw), lambda i, p, t: (i * n_t + t, p))
    lat = pl.BlockSpec((s_len, LANES), lambda i, p, t: (i, p))
    ctx = pl.BlockSpec((n_ctx, LANES), lambda i, p, t: (ctx_blk + i, p))
    return pl.pallas_call(
        functools.partial(_global_kernel, nh=nh, s_len=s_len),
        grid=(b, n_kvblk, n_t),
        in_specs=[qspec, lat, lat, ctx, ctx],
        out_specs=qspec,
        out_shape=jax.ShapeDtypeStruct((b * s_len, q.shape[1]), BF16),
        scratch_shapes=[pltpu.VMEM((n_keys, LANES), BF16),
                        pltpu.VMEM((n_keys, 2 * LANES), BF16),
                        pltpu.VMEM((nh * tq, n_keys), F32),
                        pltpu.VMEM((nh * tq, n_keys), BF16)],
        compiler_params=_params("arbitrary", "arbitrary", "arbitrary"),
        name="attn_global",
    )(q, k, v, k, v)


def _ctx_kernel(sink_ref, q_ref, kc_ref, vc_ref, o_ref, *, dh, nh, use_sink):
    tq = q_ref.shape[0]
    qs = _stack_heads(q_ref[...], dh, nh)
    s = _dot_t(qs, kc_ref[...])
    m = _row_max(_block_max(s))
    if use_sink:
        sink = _sink_rows(sink_ref, pl.program_id(1) * nh, nh, tq)
        m = jnp.maximum(m, sink)
    acc = _dot(jnp.exp2(s - _lanes(m, s.shape[1])).astype(BF16), _ones_ext(vc_ref[...]))
    den = acc[:, LANES:]
    if use_sink:
        den = den + jnp.exp2(sink - m)
    o_ref[...] = _unstack_heads(acc[:, :LANES] / den, dh, nh).astype(BF16)


def _ctx_attention(q, k, v, sink, dims, mixer):
    b, s_len, n_ctx = dims
    n_qh, n_kvh, dh = MIXER_HEADS[mixer]
    n_kvblk = k.shape[1] // LANES
    nh = n_qh // n_kvblk
    qw = nh * dh
    ctx_blk = b * s_len // n_ctx
    use_sink = sink is not None
    if not use_sink:
        sink = jnp.zeros((n_qh,), F32)
    return pl.pallas_call(
        functools.partial(_ctx_kernel, dh=dh, nh=nh, use_sink=use_sink),
        grid_spec=pltpu.PrefetchScalarGridSpec(
            num_scalar_prefetch=1,
            grid=(b, n_kvblk),
            in_specs=[pl.BlockSpec((n_ctx, qw), lambda i, p, *_: (ctx_blk + i, p)),
                      pl.BlockSpec((n_ctx, LANES), lambda i, p, *_: (ctx_blk + i, p)),
                      pl.BlockSpec((n_ctx, LANES), lambda i, p, *_: (ctx_blk + i, p))],
            out_specs=pl.BlockSpec((n_ctx, qw), lambda i, p, *_: (i, p))),
        out_shape=jax.ShapeDtypeStruct((b * n_ctx, q.shape[1]), BF16),
        compiler_params=_params("arbitrary", "arbitrary"),
        name="attn_ctx_m%d" % mixer,
    )(sink, q, k, v)


def _oproj_kernel(*refs, n_lat_tiles, with_ctx, moe):
    refs = list(refs)
    ylat_ref = refs.pop(0)
    yctx_ref = refs.pop(0) if with_ctx else None
    x_ref, mod_ref, n2_ref, wo_ref = refs[:4]
    refs = refs[4:]
    if moe:
        wr_ref, br_ref = refs[:2]
        refs = refs[2:]
    xo_ref, h_ref = refs[:2]
    y = ylat_ref[...]
    if with_ctx:
        y = jnp.where(pl.program_id(0) < n_lat_tiles, y, yctx_ref[...])
    x = x_ref[...] + mod_ref[2:3, :] * _dot(y, wo_ref[...])
    xo_ref[...] = x
    h = _rms_mod(x, n2_ref[...], mod_ref[4:5, :], mod_ref[3:4, :])
    h_ref[...] = h.astype(BF16)
    if moe:
        route_ref, counts_ref, base_ref = refs[2:5]
        h_hi, h_lo = _split_bf16(h)
        w_hi, w_lo = _split_bf16(wr_ref[...])
        logits = _dot(h_hi, w_hi) + _dot(h_hi, w_lo) + _dot(h_lo, w_hi) + br_ref[...]
        lane = lax.broadcasted_iota(jnp.int32, logits.shape, 1)
        v1 = jnp.max(logits, axis=-1, keepdims=True)
        i1 = jnp.min(jnp.where(logits == v1, lane, LANES), axis=-1, keepdims=True)
        rest = jnp.where(lane == i1, NEG_BIG, logits)
        v2 = jnp.max(rest, axis=-1, keepdims=True)
        i2 = jnp.min(jnp.where(rest == v2, lane, LANES), axis=-1, keepdims=True)
        e = jnp.exp(v2 - v1)
        w1 = 1.0 / (1.0 + e)
        w2 = e / (1.0 + e)
        @pl.when(pl.program_id(0) == 0)
        def _():
            base_ref[...] = jnp.zeros_like(base_ref)

        tm = logits.shape[0]
        pick1 = lane == i1
        pick2 = lane == i2
        cnt = jnp.where(pick1, 1.0, 0.0) + jnp.where(pick2, 1.0, 0.0)
        earlier = lax.broadcasted_iota(jnp.int32, (tm, tm), 0) > lax.broadcasted_iota(jnp.int32, (tm, tm), 1)
        before = _dot(jnp.where(earlier, 1.0, 0.0).astype(BF16), cnt.astype(BF16)) + base_ref[0:1, :]
        r1 = jnp.sum(jnp.where(pick1, before, 0.0), axis=-1, keepdims=True)
        r2 = jnp.sum(jnp.where(pick2, before, 0.0), axis=-1, keepdims=True)
        base_ref[...] = base_ref[...] + jnp.sum(cnt, axis=0, keepdims=True)
        counts_ref[...] = base_ref[...]
        route = jnp.where(lane == 0, i1.astype(F32), 0.0)
        route = jnp.where(lane == 1, i2.astype(F32), route)
        route = jnp.where(lane == 2, w1, route)
        route = jnp.where(lane == 3, w2, route)
        route = jnp.where(lane == 4, r1, route)
        route = jnp.where(lane == 5, r2, route)
        route_ref[...] = route


def _oproj(y_lat, y_ctx, xs, mod, norm2, w_o, router, dims, last):
    b, s_len, n_ctx = dims
    d = xs.shape[1]
    tm = _row_tile(s_len, b * n_ctx, 512)
    n_lat_tiles = b * s_len // tm
    tiles_per_sample = s_len // tm
    with_ctx = not last
    n_rows = b * s_len if last else xs.shape[0]
    n_tiles = n_rows // tm
    moe = router is not None

    def mod_idx(i):
        return jnp.where(i < n_lat_tiles, i // tiles_per_sample, b)

    row = pl.BlockSpec((tm, d), lambda i: (i, 0))
    in_specs = [pl.BlockSpec((tm, y_lat.shape[1]), lambda i: (jnp.minimum(i, n_lat_tiles - 1), 0))]
    args = [y_lat]
    if with_ctx:
        in_specs.append(pl.BlockSpec((tm, y_ctx.shape[1]), lambda i: (jnp.maximum(i - n_lat_tiles, 0), 0)))
        args.append(y_ctx)
    in_specs += [row,
                 pl.BlockSpec((None, 6, d), lambda i: (mod_idx(i), 0, 0)),
                 pl.BlockSpec((1, d), lambda i: (0, 0)),
                 pl.BlockSpec(w_o.shape, lambda i: (0, 0))]
    args += [xs, mod, norm2.reshape(1, d), w_o.astype(BF16)]
    out_specs = [row, row]
    out_shape = [jax.ShapeDtypeStruct((n_rows, d), F32), jax.ShapeDtypeStruct((n_rows, d), BF16)]
    if moe:
        w_router, b_router = router
        n_e = w_router.shape[1]
        wr = jnp.pad(w_router, ((0, 0), (0, LANES - n_e)))
        br = jnp.pad(b_router, (0, LANES - n_e), constant_values=NEG_BIG).reshape(1, LANES)
        in_specs += [pl.BlockSpec(wr.shape, lambda i: (0, 0)), pl.BlockSpec(br.shape, lambda i: (0, 0))]
        args += [wr, br]
        out_specs += [pl.BlockSpec((tm, LANES), lambda i: (i, 0)), pl.BlockSpec((8, LANES), lambda i: (0, 0))]
        out_shape += [jax.ShapeDtypeStruct((n_rows, LANES), F32), jax.ShapeDtypeStruct((8, LANES), F32)]
    return pl.pallas_call(
        functools.partial(_oproj_kernel, n_lat_tiles=n_lat_tiles, with_ctx=with_ctx, moe=moe),
        grid=(n_tiles,),
        in_specs=in_specs,
        out_specs=out_specs,
        out_shape=out_shape,
        scratch_shapes=[pltpu.VMEM((8, LANES), F32)] if moe else [],
        compiler_params=_params("arbitrary"),
        name="oproj_moe" if moe else "oproj",
    )(*args)


def _ffn_kernel(h_ref, wg_ref, wu_ref, wd_ref, x_ref, mod_ref, o_ref, acc_ref):
    f = pl.program_id(1)

    @pl.when(f == 0)
    def _():
        acc_ref[...] = jnp.zeros_like(acc_ref)

    h = h_ref[...]
    a = _silu(_dot(h, wg_ref[...])) * _dot(h, wu_ref[...])
    acc_ref[...] += _dot(a.astype(BF16), wd_ref[...])

    @pl.when(f == pl.num_programs(1) - 1)
    def _():
        o_ref[...] = x_ref[...] + mod_ref[5:6, :] * acc_ref[...]


def _ffn(h, xs, mod, w_in, w_out, dims):
    b, s_len, n_ctx = dims
    n_rows, d = h.shape
    d_ff = w_out.shape[0]
    tm = _row_tile(s_len, b * n_ctx, 1024)
    tf = FFN_TF
    n_f = d_ff // tf
    n_lat_tiles = b * s_len // tm
    tiles_per_sample = s_len // tm

    def mod_idx(i):
        return jnp.where(i < n_lat_tiles, i // tiles_per_sample, b)

    w_in = w_in.astype(BF16)
    return pl.pallas_call(
        _ffn_kernel,
        grid=(n_rows // tm, n_f),
        in_specs=[pl.BlockSpec((tm, d), lambda i, f: (i, 0)),
                  pl.BlockSpec((d, tf), lambda i, f: (0, f)),
                  pl.BlockSpec((d, tf), lambda i, f: (0, n_f + f)),
                  pl.BlockSpec((tf, d), lambda i, f: (f, 0)),
                  pl.BlockSpec((tm, d), lambda i, f: (i, 0)),
                  pl.BlockSpec((None, 6, d), lambda i, f: (mod_idx(i), 0, 0))],
        out_specs=pl.BlockSpec((tm, d), lambda i, f: (i, 0)),
        out_shape=jax.ShapeDtypeStruct((n_rows, d), F32),
        scratch_shapes=[pltpu.VMEM((tm, d), F32)],
        compiler_params=_params("arbitrary", "arbitrary"),
        name="ffn_dense",
    )(h, w_in, w_in, w_out.astype(BF16), xs, mod)


def _moe_ffn_kernel(te_ref, na_ref, h_ref, wg_ref, wu_ref, wd_ref, o_ref, acc_ref):
    i, f = pl.program_id(0), pl.program_id(1)

    @pl.when(i < na_ref[0])
    def _():
        @pl.when(f == 0)
        def _():
            acc_ref[...] = jnp.zeros_like(acc_ref)

        h = h_ref[...]
        a = _silu(_dot(h, wg_ref[...].astype(BF16))) * _dot(h, wu_ref[...].astype(BF16))
        acc_ref[...] += _dot(a.astype(BF16), wd_ref[...].astype(BF16))

        @pl.when(f == pl.num_programs(1) - 1)
        def _():
            o_ref[...] = acc_ref[...].astype(o_ref.dtype)


def _moe_ffn(hs, tile_expert, n_active, w_in, w_out):
    n_rows, d = hs.shape
    d_ff = w_out.shape[1]
    tm, tf = MOE_TM, FFN_TF
    n_f = d_ff // tf

    def row(i, f, te, na):
        return (jnp.minimum(i, na[0] - 1), 0)

    def expert(i, te, na):
        return te[jnp.minimum(i, na[0] - 1)]

    return pl.pallas_call(
        _moe_ffn_kernel,
        grid_spec=pltpu.PrefetchScalarGridSpec(
            num_scalar_prefetch=2,
            grid=(n_rows // tm, n_f),
            in_specs=[pl.BlockSpec((tm, d), row),
                      pl.BlockSpec((None, d, tf), lambda i, f, te, na: (expert(i, te, na), 0, f)),
                      pl.BlockSpec((None, d, tf), lambda i, f, te, na: (expert(i, te, na), 0, n_f + f)),
                      pl.BlockSpec((None, tf, d), lambda i, f, te, na: (expert(i, te, na), f, 0))],
            out_specs=pl.BlockSpec((tm, d), row),
            scratch_shapes=[pltpu.VMEM((tm, d), F32)]),
        out_shape=jax.ShapeDtypeStruct((n_rows, d), BF16),
        compiler_params=_params("arbitrary", "arbitrary"),
        name="moe_ffn",
    )(tile_expert, n_active, hs, w_in, w_in, w_out)


def _combine_kernel(x_ref, a_ref, b_ref, route_ref, mod_ref, o_ref):
    w1 = route_ref[:, 2:3]
    w2 = route_ref[:, 3:4]
    mix = w1 * a_ref[...].astype(F32) + w2 * b_ref[...].astype(F32)
    o_ref[...] = x_ref[...] + mod_ref[5:6, :] * mix


def _combine(xs, ya, yb, route, mod, dims):
    b, s_len, n_ctx = dims
    n_rows, d = ya.shape
    tm = _row_tile(s_len, b * n_ctx, 512)
    n_lat_tiles = b * s_len // tm
    tiles_per_sample = s_len // tm

    def mod_idx(i):
        return jnp.where(i < n_lat_tiles, i // tiles_per_sample, b)

    row = pl.BlockSpec((tm, d), lambda i: (i, 0))
    return pl.pallas_call(
        _combine_kernel,
        grid=(n_rows // tm,),
        in_specs=[row, row, row,
                  pl.BlockSpec((tm, LANES), lambda i: (i, 0)),
                  pl.BlockSpec((None, 6, d), lambda i: (mod_idx(i), 0, 0))],
        out_specs=row,
        out_shape=jax.ShapeDtypeStruct((n_rows, d), F32),
        compiler_params=_params("arbitrary"),
        name="moe_combine",
    )(xs, ya, yb, route, mod)


def _route_plan(idx, rank, counts, tm):
    n = idx.shape[0]
    tiles_per = (counts + tm - 1) // tm
    tile_end = jnp.cumsum(tiles_per)
    tile_start = tile_end - tiles_per
    experts = jnp.arange(N_EXPERTS, dtype=jnp.int32)
    start = jnp.sum(jnp.where(idx[:, :, None] == experts, tile_start, 0), axis=-1)
    slot = start * tm + rank
    n_tiles = (2 * n) // tm + N_EXPERTS
    tile_ids = jnp.arange(n_tiles, dtype=jnp.int32)
    tile_expert = jnp.minimum(jnp.sum((tile_end[None, :] <= tile_ids[:, None]).astype(jnp.int32), axis=1),
                              N_EXPERTS - 1)
    token = jnp.arange(2 * n, dtype=jnp.int32) // 2
    _, token_sorted = lax.sort((slot.reshape(-1), token), num_keys=1)
    first_pair = jnp.cumsum(counts) - counts
    row_in_expert = ((tile_ids - tile_start[tile_expert]) * tm)[:, None] + jnp.arange(tm, dtype=jnp.int32)[None, :]
    pair = jnp.clip(first_pair[tile_expert][:, None] + row_in_expert, 0, 2 * n - 1)
    src = jnp.where(row_in_expert < counts[tile_expert][:, None], token_sorted[pair], 0).reshape(-1)
    return slot, src, tile_expert, tile_end[-1:].astype(jnp.int32)


def _moe(h, xs, route, counts, mod, w_exp_in, w_exp_out, dims):
    idx = route[:, 0:2].astype(jnp.int32)
    rank = route[:, 4:6].astype(jnp.int32)
    slot, src, tile_expert, n_active = _route_plan(idx, rank, counts[0, :N_EXPERTS].astype(jnp.int32), MOE_TM)
    hs = jnp.take(h, src, axis=0)
    ys = _moe_ffn(hs, tile_expert, n_active, w_exp_in, w_exp_out)
    ya = jnp.take(ys, slot[:, 0], axis=0)
    yb = jnp.take(ys, slot[:, 1], axis=0)
    return _combine(xs, ya, yb, route, mod, dims)


def _layer(xs, cond, p, dims, mixer, last):
    b, s_len, n_ctx = dims
    mod = _adaln(cond, p["w_mod"], p["b_mod"])
    q, k, v = _qkv(xs, mod, p["norm1"], p["w_qkv"], p["q_norm"], p["k_norm"], dims, mixer)
    if mixer == 0:
        y_lat = _na_attention(q, k, v, p["rel_bias"], dims)
    elif mixer == 1:
        y_lat = _swa_attention(q, k, v, p["sink"], dims)
    else:
        y_lat = _global_attention(q, k, v, dims)
    y_ctx = None if last else _ctx_attention(q, k, v, p.get("sink"), dims, mixer)
    router = (p["w_router"], p["b_router"]) if "w_router" in p else None
    outs = _oproj(y_lat, y_ctx, xs, mod, p["norm2"], p["w_o"], router, dims, last)
    if router is None:
        xs, h = outs
        return _ffn(h, xs, mod, p["w_ffn_in"], p["w_ffn_out"], dims)
    xs, h, route, counts = outs
    return _moe(h, xs, route, counts, mod, p["w_exp_in"], p["w_exp_out"], dims)


def kernel(x, c, ctx, c_ctx, l0_w_mod, l0_b_mod, l0_norm1, l0_norm2, l0_w_qkv, l0_q_norm, l0_k_norm, l0_rel_bias, l0_w_o, l0_w_ffn_in, l0_w_ffn_out, l1_w_mod, l1_b_mod, l1_norm1, l1_norm2, l1_w_qkv, l1_q_norm, l1_k_norm, l1_sink, l1_w_o, l1_w_router, l1_b_router, l1_w_exp_in, l1_w_exp_out, l2_w_mod, l2_b_mod, l2_norm1, l2_norm2, l2_w_qkv, l2_q_norm, l2_k_norm, l2_w_o, l2_w_ffn_in, l2_w_ffn_out, l3_w_mod, l3_b_mod, l3_norm1, l3_norm2, l3_w_qkv, l3_q_norm, l3_k_norm, l3_rel_bias, l3_w_o, l3_w_router, l3_b_router, l3_w_exp_in, l3_w_exp_out):
    b, s_len, d = x.shape
    n_ctx = ctx.shape[1]
    dims = (b, s_len, n_ctx)
    layers = (
        dict(w_mod=l0_w_mod, b_mod=l0_b_mod, norm1=l0_norm1, norm2=l0_norm2, w_qkv=l0_w_qkv, q_norm=l0_q_norm,
             k_norm=l0_k_norm, rel_bias=l0_rel_bias, w_o=l0_w_o, w_ffn_in=l0_w_ffn_in, w_ffn_out=l0_w_ffn_out),
        dict(w_mod=l1_w_mod, b_mod=l1_b_mod, norm1=l1_norm1, norm2=l1_norm2, w_qkv=l1_w_qkv, q_norm=l1_q_norm,
             k_norm=l1_k_norm, sink=l1_sink, w_o=l1_w_o, w_router=l1_w_router, b_router=l1_b_router,
             w_exp_in=l1_w_exp_in, w_exp_out=l1_w_exp_out),
        dict(w_mod=l2_w_mod, b_mod=l2_b_mod, norm1=l2_norm1, norm2=l2_norm2, w_qkv=l2_w_qkv, q_norm=l2_q_norm,
             k_norm=l2_k_norm, w_o=l2_w_o, w_ffn_in=l2_w_ffn_in, w_ffn_out=l2_w_ffn_out),
        dict(w_mod=l3_w_mod, b_mod=l3_b_mod, norm1=l3_norm1, norm2=l3_norm2, w_qkv=l3_w_qkv, q_norm=l3_q_norm,
             k_norm=l3_k_norm, rel_bias=l3_rel_bias, w_o=l3_w_o, w_router=l3_w_router, b_router=l3_b_router,
             w_exp_in=l3_w_exp_in, w_exp_out=l3_w_exp_out),
    )
    xs = jnp.concatenate([x.reshape(b * s_len, d), ctx.reshape(b * n_ctx, d)], axis=0)
    pad_rows = -(b + 1) % 8
    cond = jnp.concatenate([c, c_ctx[None, :], jnp.zeros((pad_rows, d), F32)], axis=0)
    n_layers = len(layers)
    for i, p in enumerate(layers):
        xs = _layer(xs, cond, p, dims, i % 3, i == n_layers - 1)
    return xs.reshape(b, s_len, d)
```

```python
import functools

import jax
import jax.numpy as jnp
from jax import lax
from jax.experimental import pallas as pl
from jax.experimental.pallas import tpu as pltpu

F32 = jnp.float32
BF16 = jnp.bfloat16

GRID_W = 64
NORM_EPS = 1e-6
ROPE_THETA = 10000.0
NA_WIN_ROWS = 8
NA_WIN_COLS = 16
SW_WINDOW = 128
N_EXPERTS = 8
MIXER_HEADS = ((16, 16, 64), (16, 4, 64), (8, 4, 128))

LANES = 128
VMEM_LIMIT_BYTES = 56 * 1024 * 1024
NEG_BIG = -1e30
LOG2E = 1.4426950408889634

QKV_CHUNK = 512
GLB_TQ = 256
GLB_CHUNK = 768
NA_CTX_ROWS = 1024
NA_UNROLL = 16
MOE_TM = 1024
MOE_CHUNKS = 4
FFN_TF = 512


def _params(*sem):
    return pltpu.CompilerParams(dimension_semantics=sem, vmem_limit_bytes=VMEM_LIMIT_BYTES)


def _row_tile(n_lat_per_sample, n_ctx_rows, cap):
    for tm in (1024, 512, 256, 128):
        if tm <= cap and n_lat_per_sample % tm == 0 and n_ctx_rows % tm == 0:
            return tm
    raise ValueError("no row tile fits")


def _split_bf16(a):
    hi = a.astype(BF16)
    lo = (a - hi.astype(F32)).astype(BF16)
    return hi, lo


def _dot(a, b):
    return jnp.dot(a, b, preferred_element_type=F32)


def _dot_t(a, b):
    return lax.dot_general(a, b, (((1,), (1,)), ((), ())), preferred_element_type=F32)


def _silu(g):
    return g / (1.0 + jnp.exp(-g))


def _rms_mod(x, gain, scale, shift):
    ms = jnp.mean(x * x, axis=-1, keepdims=True)
    return x * lax.rsqrt(ms + NORM_EPS) * gain * (1.0 + scale) + shift


def _adaln_kernel(c_ref, w_ref, b_ref, o_ref):
    a_hi, a_lo = _split_bf16(_silu(c_ref[...]))
    w_hi, w_lo = _split_bf16(w_ref[...])
    o_ref[...] = _dot(a_hi, w_hi) + _dot(a_hi, w_lo) + _dot(a_lo, w_hi) + b_ref[...]


def _adaln(cond, w_mod, b_mod):
    r, d = cond.shape
    n = w_mod.shape[1]
    tn = 1536
    out = pl.pallas_call(
        _adaln_kernel,
        grid=(n // tn,),
        in_specs=[pl.BlockSpec((r, d), lambda j: (0, 0)),
                  pl.BlockSpec((d, tn), lambda j: (0, j)),
                  pl.BlockSpec((1, tn), lambda j: (0, j))],
        out_specs=pl.BlockSpec((r, tn), lambda j: (0, j)),
        out_shape=jax.ShapeDtypeStruct((r, n), F32),
        compiler_params=_params("arbitrary"),
        name="adaln",
    )(cond, w_mod, b_mod.reshape(1, n))
    return out.reshape(r, 6, d)


def _rot_half(z, dh):
    if dh == LANES:
        return pltpu.roll(z, LANES // 2, axis=1)
    lane = lax.broadcasted_iota(jnp.int32, z.shape, 1)
    from_right = pltpu.roll(z, LANES - dh // 2, axis=1)
    from_left = pltpu.roll(z, dh // 2, axis=1)
    return jnp.where((lane % dh) < dh // 2, from_right, from_left)


def _head_mean_sq(z, dh):
    z2 = z * z
    if dh == LANES:
        return jnp.broadcast_to(jnp.sum(z2, axis=-1, keepdims=True), z.shape) * (1.0 / dh)
    low = lax.broadcasted_iota(jnp.int32, z.shape, 1) < dh
    s_low = jnp.sum(jnp.where(low, z2, 0.0), axis=-1, keepdims=True)
    s_high = jnp.sum(jnp.where(low, 0.0, z2), axis=-1, keepdims=True)
    return jnp.where(low, s_low, s_high) * (1.0 / dh)


def _qkv_kernel(*refs, n_q, n_kv, dh, rope):
    if rope:
        x_ref, mod_ref, n1_ref, w_ref, g_ref, cos_ref, sin_ref, q_ref, k_ref, v_ref = refs
    else:
        x_ref, mod_ref, n1_ref, w_ref, g_ref, q_ref, k_ref, v_ref = refs
    h = _rms_mod(x_ref[...], n1_ref[...], mod_ref[1:2, :], mod_ref[0:1, :]).astype(BF16)
    n_qk = n_q + n_kv
    cw = QKV_CHUNK
    for c in range((n_qk + n_kv) // cw):
        y = _dot(h, w_ref[:, c * cw:(c + 1) * cw])
        for s in range(cw // LANES):
            col = c * cw + s * LANES
            z = y[:, s * LANES:(s + 1) * LANES]
            if col < n_qk:
                z = z * lax.rsqrt(_head_mean_sq(z, dh) + NORM_EPS) * g_ref[:, col:col + LANES]
                if rope:
                    z = z * cos_ref[...] + _rot_half(z, dh) * sin_ref[...]
            z = z.astype(BF16)
            if col < n_q:
                q_ref[:, col:col + LANES] = z
            elif col < n_qk:
                k_ref[:, col - n_q:col - n_q + LANES] = z
            else:
                v_ref[:, col - n_qk:col - n_qk + LANES] = z


def _rope_tables(s_len, dh, tm):
    n_freq = dh // 4
    inv_freq = ROPE_THETA ** (-jnp.arange(n_freq, dtype=F32) / n_freq)
    t = jnp.arange(s_len)
    row = (t // GRID_W).astype(F32)
    col = (t % GRID_W).astype(F32)
    ang = jnp.concatenate([row[:, None] * inv_freq, col[:, None] * inv_freq], axis=-1)
    cos, sin = jnp.cos(ang), jnp.sin(ang)
    reps = LANES // dh
    cos_t = jnp.tile(jnp.concatenate([cos, cos], axis=-1), (1, reps))
    sin_t = jnp.tile(jnp.concatenate([-sin, sin], axis=-1), (1, reps))
    cos_t = jnp.concatenate([cos_t, jnp.ones((tm, LANES), F32)], axis=0)
    sin_t = jnp.concatenate([sin_t, jnp.zeros((tm, LANES), F32)], axis=0)
    return cos_t, sin_t


def _qkv(xs, mod, norm1, w_qkv, q_gain, k_gain, dims, mixer):
    b, s_len, n_ctx = dims
    n_rows, d = xs.shape
    n_qh, n_kvh, dh = MIXER_HEADS[mixer]
    n_q, n_kv = n_qh * dh, n_kvh * dh
    rope = mixer != 0
    tm = _row_tile(s_len, b * n_ctx, 512)
    n_lat_tiles = b * s_len // tm
    tiles_per_sample = s_len // tm

    gains = jnp.concatenate([jnp.tile(q_gain * (dh ** -0.5 * LOG2E), n_qh),
                             jnp.tile(k_gain, n_kvh)]).reshape(1, n_q + n_kv)

    def mod_idx(i):
        return jnp.where(i < n_lat_tiles, i // tiles_per_sample, b)

    in_specs = [pl.BlockSpec((tm, d), lambda i: (i, 0)),
                pl.BlockSpec((None, 6, d), lambda i: (mod_idx(i), 0, 0)),
                pl.BlockSpec((1, d), lambda i: (0, 0)),
                pl.BlockSpec(w_qkv.shape, lambda i: (0, 0)),
                pl.BlockSpec(gains.shape, lambda i: (0, 0))]
    args = [xs, mod, norm1.reshape(1, d), w_qkv.astype(BF16), gains]
    if rope:
        cos_t, sin_t = _rope_tables(s_len, dh, tm)

        def pos_idx(i):
            return jnp.where(i < n_lat_tiles, i % tiles_per_sample, tiles_per_sample)

        in_specs += [pl.BlockSpec((tm, LANES), lambda i: (pos_idx(i), 0))] * 2
        args += [cos_t, sin_t]
    return pl.pallas_call(
        functools.partial(_qkv_kernel, n_q=n_q, n_kv=n_kv, dh=dh, rope=rope),
        grid=(n_rows // tm,),
        in_specs=in_specs,
        out_specs=[pl.BlockSpec((tm, n_q), lambda i: (i, 0)),
                   pl.BlockSpec((tm, n_kv), lambda i: (i, 0)),
                   pl.BlockSpec((tm, n_kv), lambda i: (i, 0))],
        out_shape=[jax.ShapeDtypeStruct((n_rows, n_q), BF16),
                   jax.ShapeDtypeStruct((n_rows, n_kv), BF16),
                   jax.ShapeDtypeStruct((n_rows, n_kv), BF16)],
        compiler_params=_params("arbitrary"),
        name="qkv_m%d" % mixer,
    )(*args)


def _stack_heads(q, dh, nh):
    if dh == LANES:
        return jnp.concatenate([q[:, h * LANES:(h + 1) * LANES] for h in range(nh)], axis=0)
    per_kv = nh // 2
    tq = q.shape[0]
    lane = lax.broadcasted_iota(jnp.int32, (tq, LANES), 1)
    ops = []
    for h in range(nh):
        slot = h // per_kv
        chunk = q[:, (h // 2) * LANES:(h // 2 + 1) * LANES].astype(F32)
        if h % 2 != slot:
            chunk = pltpu.roll(chunk, dh, axis=1)
        keep = lane < dh if slot == 0 else lane >= dh
        ops.append(jnp.where(keep, chunk, 0.0).astype(BF16))
    return jnp.concatenate(ops, axis=0)


def _unstack_heads(o, dh, nh):
    tq = o.shape[0] // nh
    if dh == LANES:
        return jnp.concatenate([o[h * tq:(h + 1) * tq] for h in range(nh)], axis=1)
    per_kv = nh // 2
    lane = lax.broadcasted_iota(jnp.int32, (tq, LANES), 1)
    chunks = []
    for c in range(nh // 2):
        parts = []
        for h in (2 * c, 2 * c + 1):
            oh = o[h * tq:(h + 1) * tq]
            if h % 2 != h // per_kv:
                oh = pltpu.roll(oh, dh, axis=1)
            parts.append(oh)
        chunks.append(jnp.where(lane < dh, parts[0], parts[1]))
    return jnp.concatenate(chunks, axis=1)


def _lanes(x, n):
    return x if n == LANES else jnp.concatenate([x] * (n // LANES), axis=1)


def _block_max(s):
    return functools.reduce(jnp.maximum, [s[:, j * LANES:(j + 1) * LANES] for j in range(s.shape[1] // LANES)])


def _row_max(blk):
    return jnp.broadcast_to(jnp.max(blk, axis=-1, keepdims=True), blk.shape)


def _dot_row_halves(p, v):
    half = p.shape[0] // 2
    return jnp.concatenate([_dot(p[0:half, :], v), _dot(p[half:, :], v)], axis=0)


def _ones_ext(v):
    return jnp.concatenate([v, jnp.ones_like(v)], axis=1)


def _sink_rows(sink_ref, first_head, nh, tq):
    return jnp.concatenate([jnp.full((tq, LANES), sink_ref[first_head + h] * LOG2E, F32) for h in range(nh)], axis=0)


def _na_kernel(q_ref, k_ref, v_ref, kc_ref, vc_ref, bias_ref, o_ref,
               qs_ref, vext_ref, mc_ref, numc_ref, lc_ref, *, rows):
    s_len = q_ref.shape[0]
    band = NA_WIN_ROWS * GRID_W
    lane = lax.broadcasted_iota(jnp.int32, (s_len, LANES), 1)
    q = q_ref[...]
    qs_ref[0:s_len, :] = jnp.where(lane < 64, q, jnp.zeros_like(q))
    qs_ref[s_len:, :] = jnp.where(lane >= 64, q, jnp.zeros_like(q))
    vext_ref[:, 0:LANES] = v_ref[...]
    vext_ref[:, LANES:] = jnp.ones((s_len, LANES), BF16)

    kc = kc_ref[...]
    vcx = _ones_ext(vc_ref[...])
    for c in range(2 * s_len // NA_CTX_ROWS):
        rs = slice(c * NA_CTX_ROWS, (c + 1) * NA_CTX_ROWS)
        s = _dot_t(qs_ref[rs, :], kc)
        m = _row_max(_block_max(s))
        acc = _dot(jnp.exp2(s - _lanes(m, s.shape[1])).astype(BF16), vcx)
        mc_ref[rs, :] = m
        numc_ref[rs, :] = acc[:, :LANES]
        lc_ref[rs, :] = acc[:, LANES:]

    lane_q = lax.broadcasted_iota(jnp.int32, (GRID_W, LANES), 1)

    def both_heads(ref, q0):
        return jnp.concatenate([ref[pl.ds(q0, GRID_W), :], ref[pl.ds(s_len + q0, GRID_W), :]], axis=0)

    def one_row(r, carry):
        r0 = jnp.clip(r - NA_WIN_ROWS // 2, 0, rows - NA_WIN_ROWS)
        q0 = pl.multiple_of(r * GRID_W, GRID_W)
        k0 = pl.multiple_of(r0 * GRID_W, GRID_W)
        d0 = r0 - r + NA_WIN_ROWS - 1
        bias = jnp.concatenate(
            [jnp.concatenate([bias_ref[0, d0 + 2 * j], bias_ref[1, d0 + 2 * j]], axis=0)
             for j in range(NA_WIN_ROWS // 2)], axis=1)
        qs = both_heads(qs_ref, q0)
        half = band // 2
        k1 = pl.multiple_of(k0 + half, GRID_W)
        s = jnp.concatenate([_dot_t(qs, k_ref[pl.ds(k0, half), :]), _dot_t(qs, k_ref[pl.ds(k1, half), :])],
                            axis=1) + bias
        mc = both_heads(mc_ref, q0)
        m = jnp.maximum(_row_max(_block_max(s)), mc)
        acc = _dot(jnp.exp2(s - _lanes(m, band)).astype(BF16), vext_ref[pl.ds(k0, band), :])
        alpha = jnp.exp2(mc - m)
        num = acc[:, :LANES] + alpha * both_heads(numc_ref, q0)
        den = acc[:, LANES:] + alpha * both_heads(lc_ref, q0)
        o = num / den
        o_ref[pl.ds(q0, GRID_W), :] = jnp.where(lane_q < 64, o[:GRID_W], o[GRID_W:]).astype(BF16)
        return carry

    lax.fori_loop(0, rows, one_row, 0, unroll=NA_UNROLL)


def _na_bias_tables(rel_bias):
    n_h = rel_bias.shape[0]
    col = jnp.arange(GRID_W)
    col_start = jnp.clip(col - NA_WIN_COLS // 2, 0, GRID_W - NA_WIN_COLS)
    in_win = (col[None, :] >= col_start[:, None]) & (col[None, :] < col_start[:, None] + NA_WIN_COLS)
    col_idx = jnp.clip(col[None, :] - col[:, None] + NA_WIN_COLS - 1, 0, 2 * NA_WIN_COLS - 2)
    masked = jnp.where(in_win[None, None], rel_bias[:, :, col_idx] * LOG2E, NEG_BIG)
    pairs = jnp.concatenate([masked[:, :-1], masked[:, 1:]], axis=-1)
    return pairs.reshape((n_h // 2, 2) + pairs.shape[1:])


def _na_attention(q, k, v, rel_bias, dims):
    b, s_len, n_ctx = dims
    rows = s_len // GRID_W
    n_pairs = q.shape[1] // LANES
    bias = _na_bias_tables(rel_bias)
    ctx_blk = b * s_len // n_ctx
    lat = pl.BlockSpec((s_len, LANES), lambda p, i: (i, p))
    ctx = pl.BlockSpec((n_ctx, LANES), lambda p, i: (ctx_blk + i, p))
    return pl.pallas_call(
        functools.partial(_na_kernel, rows=rows),
        grid=(n_pairs, b),
        in_specs=[lat, lat, lat, ctx, ctx,
                  pl.BlockSpec((None,) + bias.shape[1:], lambda p, i: (p, 0, 0, 0, 0))],
        out_specs=lat,
        out_shape=jax.ShapeDtypeStruct((b * s_len, q.shape[1]), BF16),
        scratch_shapes=[pltpu.VMEM((2 * s_len, LANES), BF16),
                        pltpu.VMEM((s_len, 2 * LANES), BF16),
                        pltpu.VMEM((2 * s_len, LANES), F32),
                        pltpu.VMEM((2 * s_len, LANES), F32),
                        pltpu.VMEM((2 * s_len, LANES), F32)],
        compiler_params=_params("arbitrary", "arbitrary"),
        name="attn_na",
    )(q, k, v, k, v, bias)


def _swa_kernel(sink_ref, q_ref, k_ref, v_ref, kc_ref, vc_ref, o_ref, kwin_ref, vwin_ref, *, nh, n_units):
    tq = q_ref.shape[0]
    p_blk, t = pl.program_id(1), pl.program_id(2)
    n_t = pl.num_programs(2)

    @pl.when(t == 0)
    def _():
        kwin_ref[3 * tq:, :] = kc_ref[...]
        vwin_ref[3 * tq:, 0:LANES] = vc_ref[...]
        vwin_ref[:, LANES:] = jnp.ones((vwin_ref.shape[0], LANES), BF16)

    starts = (jnp.maximum(t - 1, 0) * tq, t * tq, jnp.minimum(t + 1, n_t - 1) * tq)
    for j, k0 in enumerate(starts):
        k0 = pl.multiple_of(k0, tq)
        kwin_ref[j * tq:(j + 1) * tq, :] = k_ref[pl.ds(k0, tq), :]
        vwin_ref[j * tq:(j + 1) * tq, 0:LANES] = v_ref[pl.ds(k0, tq), :]

    row = lax.broadcasted_iota(jnp.int32, (tq, tq), 0)
    col = lax.broadcasted_iota(jnp.int32, (tq, tq), 1)
    bias_prev = jnp.where(t > 0, jnp.where(col >= row, 0.0, NEG_BIG), NEG_BIG)
    bias_next = jnp.where(t < n_t - 1, jnp.where(col <= row, 0.0, NEG_BIG), NEG_BIG)

    qs_all = _stack_heads(q_ref[...], 64, nh)
    sink_all = _sink_rows(sink_ref, p_blk * nh, nh, tq)
    hu = nh // n_units
    kwin = kwin_ref[...]
    vwin = vwin_ref[...]
    outs = []
    for u in range(n_units):
        qs = qs_all[u * hu * tq:(u + 1) * hu * tq]
        sink = sink_all[u * hu * tq:(u + 1) * hu * tq]
        s = _dot_t(qs, kwin)

        def masked(blk, bias):
            return (blk.reshape(hu, tq, tq) + bias[None]).reshape(hu * tq, tq)

        s = jnp.concatenate([masked(s[:, 0:tq], bias_prev), s[:, tq:2 * tq],
                             masked(s[:, 2 * tq:3 * tq], bias_next), s[:, 3 * tq:]], axis=1)
        m = jnp.maximum(_row_max(_block_max(s)), sink)
        acc = _dot(jnp.exp2(s - _lanes(m, s.shape[1])).astype(BF16), vwin)
        outs.append(acc[:, :LANES] / (acc[:, LANES:] + jnp.exp2(sink - m)))
    o_ref[...] = _unstack_heads(jnp.concatenate(outs, axis=0), 64, nh).astype(BF16)


def _swa_attention(q, k, v, sink, dims):
    b, s_len, n_ctx = dims
    tq = SW_WINDOW
    n_t = s_len // tq
    n_kvblk = k.shape[1] // LANES
    nh = q.shape[1] // k.shape[1] * 2
    qw = nh * 64
    ctx_blk = b * s_len // n_ctx
    qspec = pl.BlockSpec((tq, qw), lambda i, p, t, *_: (i * n_t + t, p))
    lat = pl.BlockSpec((s_len, LANES), lambda i, p, t, *_: (i, p))
    ctx = pl.BlockSpec((n_ctx, LANES), lambda i, p, t, *_: (ctx_blk + i, p))
    return pl.pallas_call(
        functools.partial(_swa_kernel, nh=nh, n_units=2),
        grid_spec=pltpu.PrefetchScalarGridSpec(
            num_scalar_prefetch=1,
            grid=(b, n_kvblk, n_t),
            in_specs=[qspec, lat, lat, ctx, ctx],
            out_specs=qspec,
            scratch_shapes=[pltpu.VMEM((3 * tq + n_ctx, LANES), BF16),
                            pltpu.VMEM((3 * tq + n_ctx, 2 * LANES), BF16)]),
        out_shape=jax.ShapeDtypeStruct((b * s_len, q.shape[1]), BF16),
        compiler_params=_params("arbitrary", "arbitrary", "arbitrary"),
        name="attn_swa",
    )(sink, q, k, v, k, v)


def _global_kernel(q_ref, k_ref, v_ref, kc_ref, vc_ref, o_ref, kall_ref, vext_ref, s_ref, p_ref, *, nh, s_len):
    @pl.when(pl.program_id(2) == 0)
    def _():
        kall_ref[0:s_len, :] = k_ref[...]
        kall_ref[s_len:, :] = kc_ref[...]
        vext_ref[0:s_len, 0:LANES] = v_ref[...]
        vext_ref[s_len:, 0:LANES] = vc_ref[...]
        vext_ref[:, LANES:] = jnp.ones((vext_ref.shape[0], LANES), BF16)

    qs = _stack_heads(q_ref[...], LANES, nh)
    n_chunks = kall_ref.shape[0] // GLB_CHUNK
    mrun = None
    for c in range(n_chunks):
        ks = slice(c * GLB_CHUNK, (c + 1) * GLB_CHUNK)
        s = _dot_t(qs, kall_ref[ks, :])
        s_ref[:, ks] = s
        blk = _block_max(s)
        mrun = blk if mrun is None else jnp.maximum(mrun, blk)
    m = _lanes(_row_max(mrun), GLB_CHUNK)
    for c in range(n_chunks):
        ks = slice(c * GLB_CHUNK, (c + 1) * GLB_CHUNK)
        p_ref[:, ks] = jnp.exp2(s_ref[:, ks] - m).astype(BF16)
    acc = _dot_row_halves(p_ref, vext_ref[...])
    o_ref[...] = _unstack_heads(acc[:, :LANES] / acc[:, LANES:], LANES, nh).astype(BF16)


def _global_attention(q, k, v, dims):
    b, s_len, n_ctx = dims
    tq = GLB_TQ
    n_t = s_len // tq
    n_kvblk = k.shape[1] // LANES
    nh = q.shape[1] // k.shape[1]
    qw = nh * LANES
    ctx_blk = b * s_len // n_ctx
    n_keys = s_len + n_ctx
    qspec = pl.BlockSpec((tq, qw), lambda i, p, t: (i * n_t + t, p))
    lat = pl.BlockSpec((s_len, LANES), lambda i, p, t: (i, p))
    ctx = pl.BlockSpec((n_ctx, LANES), lambda i, p, t: (ctx_blk + i, p))
    return pl.pallas_call(
        functools.partial(_global_kernel, nh=nh, s_len=s_len),
        grid=(b, n_kvblk, n_t),
        in_specs=[qspec, lat, lat, ctx, ctx],
        out_specs=qspec,
        out_shape=jax.ShapeDtypeStruct((b * s_len, q.shape[1]), BF16),
        scratch_shapes=[pltpu.VMEM((n_keys, LANES), BF16),
                        pltpu.VMEM((n_keys, 2 * LANES), BF16),
                        pltpu.VMEM((nh * tq, n_keys), F32),
                        pltpu.VMEM((nh * tq, n_keys), BF16)],
        compiler_params=_params("arbitrary", "arbitrary", "arbitrary"),
        name="attn_global",
    )(q, k, v, k, v)


def _ctx_kernel(sink_ref, q_ref, kc_ref, vc_ref, o_ref, *, dh, nh, use_sink):
    tq = q_ref.shape[0]
    qs = _stack_heads(q_ref[...], dh, nh)
    s = _dot_t(qs, kc_ref[...])
    m = _row_max(_block_max(s))
    if use_sink:
        sink = _sink_rows(sink_ref, pl.program_id(1) * nh, nh, tq)
        m = jnp.maximum(m, sink)
    acc = _dot(jnp.exp2(s - _lanes(m, s.shape[1])).astype(BF16), _ones_ext(vc_ref[...]))
    den = acc[:, LANES:]
    if use_sink:
        den = den + jnp.exp2(sink - m)
    o_ref[...] = _unstack_heads(acc[:, :LANES] / den, dh, nh).astype(BF16)


def _ctx_attention(q, k, v, sink, dims, mixer):
    b, s_len, n_ctx = dims
    n_qh, n_kvh, dh = MIXER_HEADS[mixer]
    n_kvblk = k.shape[1] // LANES
    nh = n_qh // n_kvblk
    qw = nh * dh
    ctx_blk = b * s_len // n_ctx
    use_sink = sink is not None
    if not use_sink:
        sink = jnp.zeros((n_qh,), F32)
    return pl.pallas_call(
        functools.partial(_ctx_kernel, dh=dh, nh=nh, use_sink=use_sink),
        grid_spec=pltpu.PrefetchScalarGridSpec(
            num_scalar_prefetch=1,
            grid=(b, n_kvblk),
            in_specs=[pl.BlockSpec((n_ctx, qw), lambda i, p, *_: (ctx_blk + i, p)),
                      pl.BlockSpec((n_ctx, LANES), lambda i, p, *_: (ctx_blk + i, p)),
                      pl.BlockSpec((n_ctx, LANES), lambda i, p, *_: (ctx_blk + i, p))],
            out_specs=pl.BlockSpec((n_ctx, qw), lambda i, p, *_: (i, p))),
        out_shape=jax.ShapeDtypeStruct((b * n_ctx, q.shape[1]), BF16),
        compiler_params=_params("arbitrary", "arbitrary"),
        name="attn_ctx_m%d" % mixer,
    )(sink, q, k, v)


def _oproj_kernel(*refs, n_lat_tiles, with_ctx, moe):
    refs = list(refs)
    ylat_ref = refs.pop(0)
    yctx_ref = refs.pop(0) if with_ctx else None
    x_ref, mod_ref, n2_ref, wo_ref = refs[:4]
    refs = refs[4:]
    if moe:
        wr_ref, br_ref = refs[:2]
        refs = refs[2:]
    xo_ref, h_ref = refs[:2]
    y = ylat_ref[...]
    if with_ctx:
        y = jnp.where(pl.program_id(0) < n_lat_tiles, y, yctx_ref[...])
    x = x_ref[...] + mod_ref[2:3, :] * _dot(y, wo_ref[...])
    xo_ref[...] = x
    h = _rms_mod(x, n2_ref[...], mod_ref[4:5, :], mod_ref[3:4, :])
    h_ref[...] = h.astype(BF16)
    if moe:
        route_ref, counts_ref, base_ref = refs[2:5]
        h_hi, h_lo = _split_bf16(h)
        w_hi, w_lo = _split_bf16(wr_ref[...])
        logits = _dot(h_hi, w_hi) + _dot(h_hi, w_lo) + _dot(h_lo, w_hi) + br_ref[...]
        lane = lax.broadcasted_iota(jnp.int32, logits.shape, 1)
        v1 = jnp.max(logits, axis=-1, keepdims=True)
        i1 = jnp.min(jnp.where(logits == v1, lane, LANES), axis=-1, keepdims=True)
        rest = jnp.where(lane == i1, NEG_BIG, logits)
        v2 = jnp.max(rest, axis=-1, keepdims=True)
        i2 = jnp.min(jnp.where(rest == v2, lane, LANES), axis=-1, keepdims=True)
        e = jnp.exp(v2 - v1)
        w1 = 1.0 / (1.0 + e)
        w2 = e / (1.0 + e)
        @pl.when(pl.program_id(0) == 0)
        def _():
            base_ref[...] = jnp.zeros_like(base_ref)

        tm = logits.shape[0]
        pick1 = lane == i1
        pick2 = lane == i2
        cnt = jnp.where(pick1, 1.0, 0.0) + jnp.where(pick2, 1.0, 0.0)
        earlier = lax.broadcasted_iota(jnp.int32, (tm, tm), 0) > lax.broadcasted_iota(jnp.int32, (tm, tm), 1)
        before = _dot(jnp.where(earlier, 1.0, 0.0).astype(BF16), cnt.astype(BF16)) + base_ref[0:1, :]
        r1 = jnp.sum(jnp.where(pick1, before, 0.0), axis=-1, keepdims=True)
        r2 = jnp.sum(jnp.where(pick2, before, 0.0), axis=-1, keepdims=True)
        base_ref[...] = base_ref[...] + jnp.sum(cnt, axis=0, keepdims=True)
        counts_ref[...] = base_ref[...]
        route = jnp.where(lane == 0, i1.astype(F32), 0.0)
        route = jnp.where(lane == 1, i2.astype(F32), route)
        route = jnp.where(lane == 2, w1, route)
        route = jnp.where(lane == 3, w2, route)
        route = jnp.where(lane == 4, r1, route)
        route = jnp.where(lane == 5, r2, route)
        route_ref[...] = route


def _oproj(y_lat, y_ctx, xs, mod, norm2, w_o, router, dims, last):
    b, s_len, n_ctx = dims
    d = xs.shape[1]
    tm = _row_tile(s_len, b * n_ctx, 512)
    n_lat_tiles = b * s_len // tm
    tiles_per_sample = s_len // tm
    with_ctx = not last
    n_rows = b * s_len if last else xs.shape[0]
    n_tiles = n_rows // tm
    moe = router is not None

    def mod_idx(i):
        return jnp.where(i < n_lat_tiles, i // tiles_per_sample, b)

    row = pl.BlockSpec((tm, d), lambda i: (i, 0))
    in_specs = [pl.BlockSpec((tm, y_lat.shape[1]), lambda i: (jnp.minimum(i, n_lat_tiles - 1), 0))]
    args = [y_lat]
    if with_ctx:
        in_specs.append(pl.BlockSpec((tm, y_ctx.shape[1]), lambda i: (jnp.maximum(i - n_lat_tiles, 0), 0)))
        args.append(y_ctx)
    in_specs += [row,
                 pl.BlockSpec((None, 6, d), lambda i: (mod_idx(i), 0, 0)),
                 pl.BlockSpec((1, d), lambda i: (0, 0)),
                 pl.BlockSpec(w_o.shape, lambda i: (0, 0))]
    args += [xs, mod, norm2.reshape(1, d), w_o.astype(BF16)]
    out_specs = [row, row]
    out_shape = [jax.ShapeDtypeStruct((n_rows, d), F32), jax.ShapeDtypeStruct((n_rows, d), BF16)]
    if moe:
        w_router, b_router = router
        n_e = w_router.shape[1]
        wr = jnp.pad(w_router, ((0, 0), (0, LANES - n_e)))
        br = jnp.pad(b_router, (0, LANES - n_e), constant_values=NEG_BIG).reshape(1, LANES)
        in_specs += [pl.BlockSpec(wr.shape, lambda i: (0, 0)), pl.BlockSpec(br.shape, lambda i: (0, 0))]
        args += [wr, br]
        out_specs += [pl.BlockSpec((tm, LANES), lambda i: (i, 0)), pl.BlockSpec((8, LANES), lambda i: (0, 0))]
        out_shape += [jax.ShapeDtypeStruct((n_rows, LANES), F32), jax.ShapeDtypeStruct((8, LANES), F32)]
    return pl.pallas_call(
        functools.partial(_oproj_kernel, n_lat_tiles=n_lat_tiles, with_ctx=with_ctx, moe=moe),
        grid=(n_tiles,),
        in_specs=in_specs,
        out_specs=out_specs,
        out_shape=out_shape,
        scratch_shapes=[pltpu.VMEM((8, LANES), F32)] if moe else [],
        compiler_params=_params("arbitrary"),
        name="oproj_moe" if moe else "oproj",
    )(*args)


def _ffn_kernel(h_ref, wg_ref, wu_ref, wd_ref, x_ref, mod_ref, o_ref, acc_ref):
    f = pl.program_id(1)

    @pl.when(f == 0)
    def _():
        acc_ref[...] = jnp.zeros_like(acc_ref)

    h = h_ref[...]
    a = _silu(_dot(h, wg_ref[...])) * _dot(h, wu_ref[...])
    acc_ref[...] += _dot(a.astype(BF16), wd_ref[...])

    @pl.when(f == pl.num_programs(1) - 1)
    def _():
        o_ref[...] = x_ref[...] + mod_ref[5:6, :] * acc_ref[...]


def _ffn(h, xs, mod, w_in, w_out, dims):
    b, s_len, n_ctx = dims
    n_rows, d = h.shape
    d_ff = w_out.shape[0]
    tm = _row_tile(s_len, b * n_ctx, 1024)
    tf = FFN_TF
    n_f = d_ff // tf
    n_lat_tiles = b * s_len // tm
    tiles_per_sample = s_len // tm

    def mod_idx(i):
        return jnp.where(i < n_lat_tiles, i // tiles_per_sample, b)

    w_in = w_in.astype(BF16)
    return pl.pallas_call(
        _ffn_kernel,
        grid=(n_rows // tm, n_f),
        in_specs=[pl.BlockSpec((tm, d), lambda i, f: (i, 0)),
                  pl.BlockSpec((d, tf), lambda i, f: (0, f)),
                  pl.BlockSpec((d, tf), lambda i, f: (0, n_f + f)),
                  pl.BlockSpec((tf, d), lambda i, f: (f, 0)),
                  pl.BlockSpec((tm, d), lambda i, f: (i, 0)),
                  pl.BlockSpec((None, 6, d), lambda i, f: (mod_idx(i), 0, 0))],
        out_specs=pl.BlockSpec((tm, d), lambda i, f: (i, 0)),
        out_shape=jax.ShapeDtypeStruct((n_rows, d), F32),
        scratch_shapes=[pltpu.VMEM((tm, d), F32)],
        compiler_params=_params("arbitrary", "arbitrary"),
        name="ffn_dense",
    )(h, w_in, w_in, w_out.astype(BF16), xs, mod)


def _moe_ffn_kernel(te_ref, na_ref, h_ref, wg_ref, wu_ref, wd_ref, prev_ref, o_ref, acc_ref):
    del prev_ref
    i, f = pl.program_id(0), pl.program_id(1)

    @pl.when(i < na_ref[0])
    def _():
        @pl.when(f == 0)
        def _():
            acc_ref[...] = jnp.zeros_like(acc_ref)

        h = h_ref[...]
        a = _silu(_dot(h, wg_ref[...].astype(BF16))) * _dot(h, wu_ref[...].astype(BF16))
        acc_ref[...] += _dot(a.astype(BF16), wd_ref[...].astype(BF16))

        @pl.when(f == pl.num_programs(1) - 1)
        def _():
            o_ref[...] = acc_ref[...].astype(o_ref.dtype)


def _moe_ffn(hs, ys, first_tile, tile_expert, n_active, w_in, w_out):
    n_rows, d = hs.shape
    d_ff = w_out.shape[1]
    tm, tf = MOE_TM, FFN_TF
    n_f = d_ff // tf

    def tile(i, na):
        return jnp.minimum(i, jnp.maximum(na[0] - 1, 0))

    def expert(i, te, na):
        return te[tile(i, na)]

    return pl.pallas_call(
        _moe_ffn_kernel,
        grid_spec=pltpu.PrefetchScalarGridSpec(
            num_scalar_prefetch=2,
            grid=(n_rows // tm, n_f),
            in_specs=[pl.BlockSpec((tm, d), lambda i, f, te, na: (tile(i, na), 0)),
                      pl.BlockSpec((None, d, tf), lambda i, f, te, na: (expert(i, te, na), 0, f)),
                      pl.BlockSpec((None, d, tf), lambda i, f, te, na: (expert(i, te, na), 0, n_f + f)),
                      pl.BlockSpec((None, tf, d), lambda i, f, te, na: (expert(i, te, na), f, 0)),
                      pl.BlockSpec(memory_space=pl.ANY)],
            out_specs=pl.BlockSpec((tm, d), lambda i, f, te, na: (first_tile + tile(i, na), 0)),
            scratch_shapes=[pltpu.VMEM((tm, d), F32)]),
        out_shape=jax.ShapeDtypeStruct(ys.shape, ys.dtype),
        input_output_aliases={6: 0},
        compiler_params=_params("arbitrary", "arbitrary"),
        name="moe_ffn",
    )(tile_expert, n_active, hs, w_in, w_in, w_out, ys)


def _combine_kernel(x_ref, a_ref, b_ref, route_ref, mod_ref, o_ref):
    w1 = route_ref[:, 2:3]
    w2 = route_ref[:, 3:4]
    mix = w1 * a_ref[...].astype(F32) + w2 * b_ref[...].astype(F32)
    o_ref[...] = x_ref[...] + mod_ref[5:6, :] * mix


def _combine(xs, ya, yb, route, mod, dims):
    b, s_len, n_ctx = dims
    n_rows, d = ya.shape
    tm = _row_tile(s_len, b * n_ctx, 512)
    n_lat_tiles = b * s_len // tm
    tiles_per_sample = s_len // tm

    def mod_idx(i):
        return jnp.where(i < n_lat_tiles, i // tiles_per_sample, b)

    row = pl.BlockSpec((tm, d), lambda i: (i, 0))
    return pl.pallas_call(
        _combine_kernel,
        grid=(n_rows // tm,),
        in_specs=[row, row, row,
                  pl.BlockSpec((tm, LANES), lambda i: (i, 0)),
                  pl.BlockSpec((None, 6, d), lambda i: (mod_idx(i), 0, 0))],
        out_specs=row,
        out_shape=jax.ShapeDtypeStruct((n_rows, d), F32),
        compiler_params=_params("arbitrary"),
        name="moe_combine",
    )(xs, ya, yb, route, mod)


def _route_plan(idx, rank, counts, tm):
    n = idx.shape[0]
    tiles_per = (counts + tm - 1) // tm
    tile_end = jnp.cumsum(tiles_per)
    tile_start = tile_end - tiles_per
    experts = jnp.arange(N_EXPERTS, dtype=jnp.int32)
    start = jnp.sum(jnp.where(idx[:, :, None] == experts, tile_start, 0), axis=-1)
    slot = start * tm + rank
    n_tiles = (2 * n) // tm + N_EXPERTS
    tile_ids = jnp.arange(n_tiles, dtype=jnp.int32)
    tile_expert = jnp.minimum(jnp.sum((tile_end[None, :] <= tile_ids[:, None]).astype(jnp.int32), axis=1),
                              N_EXPERTS - 1)
    token = jnp.arange(2 * n, dtype=jnp.int32) // 2
    _, token_sorted = lax.sort((slot.reshape(-1), token), num_keys=1)
    first_pair = jnp.cumsum(counts) - counts
    row_in_expert = ((tile_ids - tile_start[tile_expert]) * tm)[:, None] + jnp.arange(tm, dtype=jnp.int32)[None, :]
    pair = jnp.clip(first_pair[tile_expert][:, None] + row_in_expert, 0, 2 * n - 1)
    src = jnp.where(row_in_expert < counts[tile_expert][:, None], token_sorted[pair], 0).reshape(-1)
    return slot, src, tile_expert, tile_end[-1:].astype(jnp.int32)


def _moe(h, xs, route, counts, mod, w_exp_in, w_exp_out, dims):
    idx = route[:, 0:2].astype(jnp.int32)
    rank = route[:, 4:6].astype(jnp.int32)
    slot, src, tile_expert, n_active = _route_plan(idx, rank, counts[0, :N_EXPERTS].astype(jnp.int32), MOE_TM)
    n_tiles = src.shape[0] // MOE_TM
    per_chunk = -(-n_tiles // MOE_CHUNKS)
    ys = jnp.zeros((src.shape[0], h.shape[1]), h.dtype)
    for t0 in range(0, n_tiles, per_chunk):
        t1 = min(t0 + per_chunk, n_tiles)
        hs = jnp.take(h, src[t0 * MOE_TM:t1 * MOE_TM], axis=0)
        ys = _moe_ffn(hs, ys, t0, tile_expert[t0:t1], jnp.clip(n_active - t0, 0, t1 - t0), w_exp_in, w_exp_out)
    ya = jnp.take(ys, slot[:, 0], axis=0)
    yb = jnp.take(ys, slot[:, 1], axis=0)
    return _combine(xs, ya, yb, route, mod, dims)


def _layer(xs, cond, p, dims, mixer, last):
    b, s_len, n_ctx = dims
    mod = _adaln(cond, p["w_mod"], p["b_mod"])
    q, k, v = _qkv(xs, mod, p["norm1"], p["w_qkv"], p["q_norm"], p["k_norm"], dims, mixer)
    if mixer == 0:
        y_lat = _na_attention(q, k, v, p["rel_bias"], dims)
    elif mixer == 1:
        y_lat = _swa_attention(q, k, v, p["sink"], dims)
    else:
        y_lat = _global_attention(q, k, v, dims)
    y_ctx = None if last else _ctx_attention(q, k, v, p.get("sink"), dims, mixer)
    router = (p["w_router"], p["b_router"]) if "w_router" in p else None
    outs = _oproj(y_lat, y_ctx, xs, mod, p["norm2"], p["w_o"], router, dims, last)
    if router is None:
        xs, h = outs
        return _ffn(h, xs, mod, p["w_ffn_in"], p["w_ffn_out"], dims)
    xs, h, route, counts = outs
    return _moe(h, xs, route, counts, mod, p["w_exp_in"], p["w_exp_out"], dims)


def kernel(x, c, ctx, c_ctx, l0_w_mod, l0_b_mod, l0_norm1, l0_norm2, l0_w_qkv, l0_q_norm, l0_k_norm, l0_rel_bias, l0_w_o, l0_w_ffn_in, l0_w_ffn_out, l1_w_mod, l1_b_mod, l1_norm1, l1_norm2, l1_w_qkv, l1_q_norm, l1_k_norm, l1_sink, l1_w_o, l1_w_router, l1_b_router, l1_w_exp_in, l1_w_exp_out, l2_w_mod, l2_b_mod, l2_norm1, l2_norm2, l2_w_qkv, l2_q_norm, l2_k_norm, l2_w_o, l2_w_ffn_in, l2_w_ffn_out, l3_w_mod, l3_b_mod, l3_norm1, l3_norm2, l3_w_qkv, l3_q_norm, l3_k_norm, l3_rel_bias, l3_w_o, l3_w_router, l3_b_router, l3_w_exp_in, l3_w_exp_out):
    b, s_len, d = x.shape
    n_ctx = ctx.shape[1]
    dims = (b, s_len, n_ctx)
    layers = (
        dict(w_mod=l0_w_mod, b_mod=l0_b_mod, norm1=l0_norm1, norm2=l0_norm2, w_qkv=l0_w_qkv, q_norm=l0_q_norm,
             k_norm=l0_k_norm, rel_bias=l0_rel_bias, w_o=l0_w_o, w_ffn_in=l0_w_ffn_in, w_ffn_out=l0_w_ffn_out),
        dict(w_mod=l1_w_mod, b_mod=l1_b_mod, norm1=l1_norm1, norm2=l1_norm2, w_qkv=l1_w_qkv, q_norm=l1_q_norm,
             k_norm=l1_k_norm, sink=l1_sink, w_o=l1_w_o, w_router=l1_w_router, b_router=l1_b_router,
             w_exp_in=l1_w_exp_in, w_exp_out=l1_w_exp_out),
        dict(w_mod=l2_w_mod, b_mod=l2_b_mod, norm1=l2_norm1, norm2=l2_norm2, w_qkv=l2_w_qkv, q_norm=l2_q_norm,
             k_norm=l2_k_norm, w_o=l2_w_o, w_ffn_in=l2_w_ffn_in, w_ffn_out=l2_w_ffn_out),
        dict(w_mod=l3_w_mod, b_mod=l3_b_mod, norm1=l3_norm1, norm2=l3_norm2, w_qkv=l3_w_qkv, q_norm=l3_q_norm,
             k_norm=l3_k_norm, rel_bias=l3_rel_bias, w_o=l3_w_o, w_router=l3_w_router, b_router=l3_b_router,
             w_exp_in=l3_w_exp_in, w_exp_out=l3_w_exp_out),
    )
    xs = jnp.concatenate([x.reshape(b * s_len, d), ctx.reshape(b * n_ctx, d)], axis=0)
    pad_rows = -(b + 1) % 8
    cond = jnp.concatenate([c, c_ctx[None, :], jnp.zeros((pad_rows, d), F32)], axis=0)
    n_layers = len(layers)
    for i, p in enumerate(layers):
        xs = _layer(xs, cond, p, dims, i % 3, i == n_layers - 1)
    return xs.reshape(b, s_len, d)
```

```python
import functools

import jax
import jax.numpy as jnp
from jax import lax
from jax.experimental import pallas as pl
from jax.experimental.pallas import tpu as pltpu

F32 = jnp.float32
BF16 = jnp.bfloat16

GRID_W = 64
NORM_EPS = 1e-6
ROPE_THETA = 10000.0
NA_WIN_ROWS = 8
NA_WIN_COLS = 16
SW_WINDOW = 128
N_EXPERTS = 8
MIXER_HEADS = ((16, 16, 64), (16, 4, 64), (8, 4, 128))

LANES = 128
VMEM_LIMIT_BYTES = 56 * 1024 * 1024
NEG_BIG = -1e30
LOG2E = 1.4426950408889634

QKV_CHUNK = 512
GLB_TQ = 256
GLB_CHUNK = 256
SWA_TQ = 256
NA_CTX_ROWS = 1024
NA_UNROLL = 16
MOE_TM = 1024
MOE_CHUNKS = 4
FFN_TF = 512


def _params(*sem):
    return pltpu.CompilerParams(dimension_semantics=sem, vmem_limit_bytes=VMEM_LIMIT_BYTES)


def _row_tile(n_lat_per_sample, n_ctx_rows, cap):
    for tm in (1024, 512, 256, 128):
        if tm <= cap and n_lat_per_sample % tm == 0 and n_ctx_rows % tm == 0:
            return tm
    raise ValueError("no row tile fits")


def _split_bf16(a):
    hi = a.astype(BF16)
    lo = (a - hi.astype(F32)).astype(BF16)
    return hi, lo


def _dot(a, b):
    return jnp.dot(a, b, preferred_element_type=F32)


def _dot_t(a, b):
    return lax.dot_general(a, b, (((1,), (1,)), ((), ())), preferred_element_type=F32)


def _silu(g):
    return g / (1.0 + jnp.exp(-g))


def _rms_mod(x, gain, scale, shift):
    ms = jnp.mean(x * x, axis=-1, keepdims=True)
    return x * lax.rsqrt(ms + NORM_EPS) * gain * (1.0 + scale) + shift


def _adaln_kernel(c_ref, w_ref, b_ref, o_ref):
    a_hi, a_lo = _split_bf16(_silu(c_ref[...]))
    w_hi, w_lo = _split_bf16(w_ref[...])
    o_ref[...] = _dot(a_hi, w_hi) + _dot(a_hi, w_lo) + _dot(a_lo, w_hi) + b_ref[...]


def _adaln(cond, w_mod, b_mod):
    r, d = cond.shape
    n = w_mod.shape[1]
    tn = 1536
    out = pl.pallas_call(
        _adaln_kernel,
        grid=(n // tn,),
        in_specs=[pl.BlockSpec((r, d), lambda j: (0, 0)),
                  pl.BlockSpec((d, tn), lambda j: (0, j)),
                  pl.BlockSpec((1, tn), lambda j: (0, j))],
        out_specs=pl.BlockSpec((r, tn), lambda j: (0, j)),
        out_shape=jax.ShapeDtypeStruct((r, n), F32),
        compiler_params=_params("arbitrary"),
        name="adaln",
    )(cond, w_mod, b_mod.reshape(1, n))
    return out.reshape(r, 6, d)


def _rot_half(z, dh):
    if dh == LANES:
        return pltpu.roll(z, LANES // 2, axis=1)
    lane = lax.broadcasted_iota(jnp.int32, z.shape, 1)
    from_right = pltpu.roll(z, LANES - dh // 2, axis=1)
    from_left = pltpu.roll(z, dh // 2, axis=1)
    return jnp.where((lane % dh) < dh // 2, from_right, from_left)


def _head_mean_sq(z, dh):
    z2 = z * z
    if dh == LANES:
        return jnp.broadcast_to(jnp.sum(z2, axis=-1, keepdims=True), z.shape) * (1.0 / dh)
    low = lax.broadcasted_iota(jnp.int32, z.shape, 1) < dh
    s_low = jnp.sum(jnp.where(low, z2, 0.0), axis=-1, keepdims=True)
    s_high = jnp.sum(jnp.where(low, 0.0, z2), axis=-1, keepdims=True)
    return jnp.where(low, s_low, s_high) * (1.0 / dh)


def _qkv_kernel(*refs, n_q, n_kv, dh, rope):
    if rope:
        x_ref, mod_ref, n1_ref, w_ref, g_ref, cos_ref, sin_ref, q_ref, k_ref, v_ref = refs
    else:
        x_ref, mod_ref, n1_ref, w_ref, g_ref, q_ref, k_ref, v_ref = refs
    h = _rms_mod(x_ref[...], n1_ref[...], mod_ref[1:2, :], mod_ref[0:1, :]).astype(BF16)
    n_qk = n_q + n_kv
    cw = QKV_CHUNK
    for c in range((n_qk + n_kv) // cw):
        y = _dot(h, w_ref[:, c * cw:(c + 1) * cw])
        for s in range(cw // LANES):
            col = c * cw + s * LANES
            z = y[:, s * LANES:(s + 1) * LANES]
            if col < n_qk:
                z = z * lax.rsqrt(_head_mean_sq(z, dh) + NORM_EPS) * g_ref[:, col:col + LANES]
                if rope:
                    z = z * cos_ref[...] + _rot_half(z, dh) * sin_ref[...]
            z = z.astype(BF16)
            if col < n_q:
                q_ref[:, col:col + LANES] = z
            elif col < n_qk:
                k_ref[:, col - n_q:col - n_q + LANES] = z
            else:
                v_ref[:, col - n_qk:col - n_qk + LANES] = z


def _rope_tables(s_len, dh, tm):
    n_freq = dh // 4
    inv_freq = ROPE_THETA ** (-jnp.arange(n_freq, dtype=F32) / n_freq)
    t = jnp.arange(s_len)
    row = (t // GRID_W).astype(F32)
    col = (t % GRID_W).astype(F32)
    ang = jnp.concatenate([row[:, None] * inv_freq, col[:, None] * inv_freq], axis=-1)
    cos, sin = jnp.cos(ang), jnp.sin(ang)
    reps = LANES // dh
    cos_t = jnp.tile(jnp.concatenate([cos, cos], axis=-1), (1, reps))
    sin_t = jnp.tile(jnp.concatenate([-sin, sin], axis=-1), (1, reps))
    cos_t = jnp.concatenate([cos_t, jnp.ones((tm, LANES), F32)], axis=0)
    sin_t = jnp.concatenate([sin_t, jnp.zeros((tm, LANES), F32)], axis=0)
    return cos_t, sin_t


def _qkv(xs, mod, norm1, w_qkv, q_gain, k_gain, dims, mixer):
    b, s_len, n_ctx = dims
    n_rows, d = xs.shape
    n_qh, n_kvh, dh = MIXER_HEADS[mixer]
    n_q, n_kv = n_qh * dh, n_kvh * dh
    rope = mixer != 0
    tm = _row_tile(s_len, b * n_ctx, 512)
    n_lat_tiles = b * s_len // tm
    tiles_per_sample = s_len // tm

    gains = jnp.concatenate([jnp.tile(q_gain * (dh ** -0.5 * LOG2E), n_qh),
                             jnp.tile(k_gain, n_kvh)]).reshape(1, n_q + n_kv)

    def mod_idx(i):
        return jnp.where(i < n_lat_tiles, i // tiles_per_sample, b)

    in_specs = [pl.BlockSpec((tm, d), lambda i: (i, 0)),
                pl.BlockSpec((None, 6, d), lambda i: (mod_idx(i), 0, 0)),
                pl.BlockSpec((1, d), lambda i: (0, 0)),
                pl.BlockSpec(w_qkv.shape, lambda i: (0, 0)),
                pl.BlockSpec(gains.shape, lambda i: (0, 0))]
    args = [xs, mod, norm1.reshape(1, d), w_qkv.astype(BF16), gains]
    if rope:
        cos_t, sin_t = _rope_tables(s_len, dh, tm)

        def pos_idx(i):
            return jnp.where(i < n_lat_tiles, i % tiles_per_sample, tiles_per_sample)

        in_specs += [pl.BlockSpec((tm, LANES), lambda i: (pos_idx(i), 0))] * 2
        args += [cos_t, sin_t]
    return pl.pallas_call(
        functools.partial(_qkv_kernel, n_q=n_q, n_kv=n_kv, dh=dh, rope=rope),
        grid=(n_rows // tm,),
        in_specs=in_specs,
        out_specs=[pl.BlockSpec((tm, n_q), lambda i: (i, 0)),
                   pl.BlockSpec((tm, n_kv), lambda i: (i, 0)),
                   pl.BlockSpec((tm, n_kv), lambda i: (i, 0))],
        out_shape=[jax.ShapeDtypeStruct((n_rows, n_q), BF16),
                   jax.ShapeDtypeStruct((n_rows, n_kv), BF16),
                   jax.ShapeDtypeStruct((n_rows, n_kv), BF16)],
        compiler_params=_params("arbitrary"),
        name="qkv_m%d" % mixer,
    )(*args)


def _stack_heads(q, dh, nh):
    if dh == LANES:
        return jnp.concatenate([q[:, h * LANES:(h + 1) * LANES] for h in range(nh)], axis=0)
    per_kv = nh // 2
    tq = q.shape[0]
    lane = lax.broadcasted_iota(jnp.int32, (tq, LANES), 1)
    ops = []
    for h in range(nh):
        slot = h // per_kv
        chunk = q[:, (h // 2) * LANES:(h // 2 + 1) * LANES].astype(F32)
        if h % 2 != slot:
            chunk = pltpu.roll(chunk, dh, axis=1)
        keep = lane < dh if slot == 0 else lane >= dh
        ops.append(jnp.where(keep, chunk, 0.0).astype(BF16))
    return jnp.concatenate(ops, axis=0)


def _unstack_heads(o, dh, nh):
    tq = o.shape[0] // nh
    if dh == LANES:
        return jnp.concatenate([o[h * tq:(h + 1) * tq] for h in range(nh)], axis=1)
    per_kv = nh // 2
    lane = lax.broadcasted_iota(jnp.int32, (tq, LANES), 1)
    chunks = []
    for c in range(nh // 2):
        parts = []
        for h in (2 * c, 2 * c + 1):
            oh = o[h * tq:(h + 1) * tq]
            if h % 2 != h // per_kv:
                oh = pltpu.roll(oh, dh, axis=1)
            parts.append(oh)
        chunks.append(jnp.where(lane < dh, parts[0], parts[1]))
    return jnp.concatenate(chunks, axis=1)


def _lanes(x, n):
    return x if n == LANES else jnp.concatenate([x] * (n // LANES), axis=1)


def _block_max(s):
    return functools.reduce(jnp.maximum, [s[:, j * LANES:(j + 1) * LANES] for j in range(s.shape[1] // LANES)])


def _row_max(blk):
    return jnp.broadcast_to(jnp.max(blk, axis=-1, keepdims=True), blk.shape)


def _dot_row_halves(p, v):
    half = p.shape[0] // 2
    return jnp.concatenate([_dot(p[0:half, :], v), _dot(p[half:, :], v)], axis=0)


def _ones_ext(v):
    return jnp.concatenate([v, jnp.ones_like(v)], axis=1)


def _sink_rows(sink_ref, first_head, nh, tq):
    return jnp.concatenate([jnp.full((tq, LANES), sink_ref[first_head + h] * LOG2E, F32) for h in range(nh)], axis=0)


def _na_kernel(q_ref, k_ref, v_ref, kc_ref, vc_ref, bias_ref, o_ref,
               qs_ref, vext_ref, mc_ref, numc_ref, lc_ref, *, rows):
    s_len = q_ref.shape[0]
    band = NA_WIN_ROWS * GRID_W
    lane = lax.broadcasted_iota(jnp.int32, (s_len, LANES), 1)
    q = q_ref[...]
    qs_ref[0:s_len, :] = jnp.where(lane < 64, q, jnp.zeros_like(q))
    qs_ref[s_len:, :] = jnp.where(lane >= 64, q, jnp.zeros_like(q))
    vext_ref[:, 0:LANES] = v_ref[...]
    vext_ref[:, LANES:] = jnp.ones((s_len, LANES), BF16)

    kc = kc_ref[...]
    vcx = _ones_ext(vc_ref[...])
    for c in range(2 * s_len // NA_CTX_ROWS):
        rs = slice(c * NA_CTX_ROWS, (c + 1) * NA_CTX_ROWS)
        s = _dot_t(qs_ref[rs, :], kc)
        m = _row_max(_block_max(s))
        acc = _dot(jnp.exp2(s - _lanes(m, s.shape[1])).astype(BF16), vcx)
        mc_ref[rs, :] = m
        numc_ref[rs, :] = acc[:, :LANES]
        lc_ref[rs, :] = acc[:, LANES:]

    lane_q = lax.broadcasted_iota(jnp.int32, (GRID_W, LANES), 1)

    def both_heads(ref, q0):
        return jnp.concatenate([ref[pl.ds(q0, GRID_W), :], ref[pl.ds(s_len + q0, GRID_W), :]], axis=0)

    def one_row(r, carry):
        r0 = jnp.clip(r - NA_WIN_ROWS // 2, 0, rows - NA_WIN_ROWS)
        q0 = pl.multiple_of(r * GRID_W, GRID_W)
        k0 = pl.multiple_of(r0 * GRID_W, GRID_W)
        d0 = r0 - r + NA_WIN_ROWS - 1
        bias = jnp.concatenate(
            [jnp.concatenate([bias_ref[0, d0 + 2 * j], bias_ref[1, d0 + 2 * j]], axis=0)
             for j in range(NA_WIN_ROWS // 2)], axis=1)
        qs = both_heads(qs_ref, q0)
        half = band // 2
        k1 = pl.multiple_of(k0 + half, GRID_W)
        s = jnp.concatenate([_dot_t(qs, k_ref[pl.ds(k0, half), :]), _dot_t(qs, k_ref[pl.ds(k1, half), :])],
                            axis=1) + bias
        mc = both_heads(mc_ref, q0)
        m = jnp.maximum(_row_max(_block_max(s)), mc)
        acc = _dot(jnp.exp2(s - _lanes(m, band)).astype(BF16), vext_ref[pl.ds(k0, band), :])
        alpha = jnp.exp2(mc - m)
        num = acc[:, :LANES] + alpha * both_heads(numc_ref, q0)
        den = acc[:, LANES:] + alpha * both_heads(lc_ref, q0)
        o = num / den
        o_ref[pl.ds(q0, GRID_W), :] = jnp.where(lane_q < 64, o[:GRID_W], o[GRID_W:]).astype(BF16)
        return carry

    lax.fori_loop(0, rows, one_row, 0, unroll=NA_UNROLL)


def _na_bias_tables(rel_bias):
    n_h = rel_bias.shape[0]
    col = jnp.arange(GRID_W)
    col_start = jnp.clip(col - NA_WIN_COLS // 2, 0, GRID_W - NA_WIN_COLS)
    in_win = (col[None, :] >= col_start[:, None]) & (col[None, :] < col_start[:, None] + NA_WIN_COLS)
    col_idx = jnp.clip(col[None, :] - col[:, None] + NA_WIN_COLS - 1, 0, 2 * NA_WIN_COLS - 2)
    masked = jnp.where(in_win[None, None], rel_bias[:, :, col_idx] * LOG2E, NEG_BIG)
    pairs = jnp.concatenate([masked[:, :-1], masked[:, 1:]], axis=-1)
    return pairs.reshape((n_h // 2, 2) + pairs.shape[1:])


def _na_attention(q, k, v, rel_bias, dims):
    b, s_len, n_ctx = dims
    rows = s_len // GRID_W
    n_pairs = q.shape[1] // LANES
    bias = _na_bias_tables(rel_bias)
    ctx_blk = b * s_len // n_ctx
    lat = pl.BlockSpec((s_len, LANES), lambda p, i: (i, p))
    ctx = pl.BlockSpec((n_ctx, LANES), lambda p, i: (ctx_blk + i, p))
    return pl.pallas_call(
        functools.partial(_na_kernel, rows=rows),
        grid=(n_pairs, b),
        in_specs=[lat, lat, lat, ctx, ctx,
                  pl.BlockSpec((None,) + bias.shape[1:], lambda p, i: (p, 0, 0, 0, 0))],
        out_specs=lat,
        out_shape=jax.ShapeDtypeStruct((b * s_len, q.shape[1]), BF16),
        scratch_shapes=[pltpu.VMEM((2 * s_len, LANES), BF16),
                        pltpu.VMEM((s_len, 2 * LANES), BF16),
                        pltpu.VMEM((2 * s_len, LANES), F32),
                        pltpu.VMEM((2 * s_len, LANES), F32),
                        pltpu.VMEM((2 * s_len, LANES), F32)],
        compiler_params=_params("arbitrary", "arbitrary"),
        name="attn_na",
    )(q, k, v, k, v, bias)


def _swa_kernel(sink_ref, q_ref, k_ref, v_ref, kc_ref, vc_ref, o_ref, kwin_ref, vwin_ref, *, nh, n_units, s_len):
    tq = q_ref.shape[0]
    span = tq + 2 * SW_WINDOW
    p_blk, t = pl.program_id(1), pl.program_id(2)

    @pl.when(t == 0)
    def _():
        kwin_ref[span:, :] = kc_ref[...]
        vwin_ref[span:, 0:LANES] = vc_ref[...]
        vwin_ref[:, LANES:] = jnp.ones((vwin_ref.shape[0], LANES), BF16)

    q0 = t * tq
    k0 = pl.multiple_of(jnp.clip(q0 - SW_WINDOW, 0, s_len - span), SW_WINDOW)
    kwin_ref[0:span, :] = k_ref[pl.ds(k0, span), :]
    vwin_ref[0:span, 0:LANES] = v_ref[pl.ds(k0, span), :]

    row = lax.broadcasted_iota(jnp.int32, (tq, span), 0)
    col = lax.broadcasted_iota(jnp.int32, (tq, span), 1)
    bias = jnp.where(jnp.abs(col - row + (k0 - q0)) <= SW_WINDOW, 0.0, NEG_BIG)

    qs_all = _stack_heads(q_ref[...], 64, nh)
    sink_all = _sink_rows(sink_ref, p_blk * nh, nh, tq)
    hu = nh // n_units
    vwin = vwin_ref[...]
    kd = 2 * LANES
    outs = []
    for u in range(n_units):
        qs = qs_all[u * hu * tq:(u + 1) * hu * tq]
        sink = sink_all[u * hu * tq:(u + 1) * hu * tq]
        s = jnp.concatenate([_dot_t(qs, kwin_ref[j * kd:(j + 1) * kd, :]) for j in range(kwin_ref.shape[0] // kd)],
                            axis=1)
        s = jnp.concatenate([(s[:, :span].reshape(hu, tq, span) + bias[None]).reshape(hu * tq, span),
                             s[:, span:]], axis=1)
        m = jnp.maximum(_row_max(_block_max(s)), sink)
        acc = _dot(jnp.exp2(s - _lanes(m, s.shape[1])).astype(BF16), vwin)
        outs.append(acc[:, :LANES] / (acc[:, LANES:] + jnp.exp2(sink - m)))
    o_ref[...] = _unstack_heads(jnp.concatenate(outs, axis=0), 64, nh).astype(BF16)


def _swa_attention(q, k, v, sink, dims):
    b, s_len, n_ctx = dims
    tq = SWA_TQ
    n_t = s_len // tq
    n_kvblk = k.shape[1] // LANES
    nh = q.shape[1] // k.shape[1] * 2
    qw = nh * 64
    ctx_blk = b * s_len // n_ctx
    n_win = tq + 2 * SW_WINDOW + n_ctx
    assert n_win % (2 * LANES) == 0
    qspec = pl.BlockSpec((tq, qw), lambda i, p, t, *_: (i * n_t + t, p))
    lat = pl.BlockSpec((s_len, LANES), lambda i, p, t, *_: (i, p))
    ctx = pl.BlockSpec((n_ctx, LANES), lambda i, p, t, *_: (ctx_blk + i, p))
    return pl.pallas_call(
        functools.partial(_swa_kernel, nh=nh, n_units=4, s_len=s_len),
        grid_spec=pltpu.PrefetchScalarGridSpec(
            num_scalar_prefetch=1,
            grid=(b, n_kvblk, n_t),
            in_specs=[qspec, lat, lat, ctx, ctx],
            out_specs=qspec,
            scratch_shapes=[pltpu.VMEM((n_win, LANES), BF16),
                            pltpu.VMEM((n_win, 2 * LANES), BF16)]),
        out_shape=jax.ShapeDtypeStruct((b * s_len, q.shape[1]), BF16),
        compiler_params=_params("arbitrary", "arbitrary", "arbitrary"),
        name="attn_swa",
    )(sink, q, k, v, k, v)


def _global_kernel(q_ref, k_ref, v_ref, kc_ref, vc_ref, o_ref, kall_ref, vext_ref, s_ref, p_ref, *, nh, s_len):
    @pl.when(pl.program_id(2) == 0)
    def _():
        kall_ref[0:s_len, :] = k_ref[...]
        kall_ref[s_len:, :] = kc_ref[...]
        vext_ref[0:s_len, 0:LANES] = v_ref[...]
        vext_ref[s_len:, 0:LANES] = vc_ref[...]
        vext_ref[:, LANES:] = jnp.ones((vext_ref.shape[0], LANES), BF16)

    qs = _stack_heads(q_ref[...], LANES, nh)
    n_chunks = kall_ref.shape[0] // GLB_CHUNK
    mrun = None
    for c in range(n_chunks):
        ks = slice(c * GLB_CHUNK, (c + 1) * GLB_CHUNK)
        s = _dot_t(qs, kall_ref[ks, :])
        s_ref[:, ks] = s
        blk = _block_max(s)
        mrun = blk if mrun is None else jnp.maximum(mrun, blk)
    m = _lanes(_row_max(mrun), GLB_CHUNK)
    for c in range(n_chunks):
        ks = slice(c * GLB_CHUNK, (c + 1) * GLB_CHUNK)
        p_ref[:, ks] = jnp.exp2(s_ref[:, ks] - m).astype(BF16)
    acc = _dot_row_halves(p_ref, vext_ref[...])
    o_ref[...] = _unstack_heads(acc[:, :LANES] / acc[:, LANES:], LANES, nh).astype(BF16)


def _global_attention(q, k, v, dims):
    b, s_len, n_ctx = dims
    tq = GLB_TQ
    n_t = s_len // tq
    n_kvblk = k.shape[1] // LANES
    nh = q.shape[1] // k.shape[1]
    qw = nh * LANES
    ctx_blk = b * s_len // n_ctx
    n_keys = s_len + n_ctx
    qspec = pl.BlockSpec((tq, qw), lambda i, p, t: (i * n_t + t, p))
    lat = pl.BlockSpec((s_len, LANES), lambda i, p, t: (i, p))
    ctx = pl.BlockSpec((n_ctx, LANES), lambda i, p, t: (ctx_blk + i, p))
    return pl.pallas_call(
        functools.partial(_global_kernel, nh=nh, s_len=s_len),
        grid=(b, n_kvblk, n_t),
        in_specs=[qspec, lat, lat, ctx, ctx],
        out_specs=qspec,
        out_shape=jax.ShapeDtypeStruct((b * s_len, q.shape[1]), BF16),
        scratch_shapes=[pltpu.VMEM((n_keys, LANES), BF16),
                        pltpu.VMEM((n_keys, 2 * LANES), BF16),
                        pltpu.VMEM((nh * tq, n_keys), F32),
                        pltpu.VMEM((nh * tq, n_keys), BF16)],
        compiler_params=_params("arbitrary", "arbitrary", "arbitrary"),
        name="attn_global",
    )(q, k, v, k, v)


def _ctx_kernel(sink_ref, q_ref, kc_ref, vc_ref, o_ref, *, dh, nh, use_sink):
    tq = q_ref.shape[0]
    qs = _stack_heads(q_ref[...], dh, nh)
    s = _dot_t(qs, kc_ref[...])
    m = _row_max(_block_max(s))
    if use_sink:
        sink = _sink_rows(sink_ref, pl.program_id(1) * nh, nh, tq)
        m = jnp.maximum(m, sink)
    acc = _dot(jnp.exp2(s - _lanes(m, s.shape[1])).astype(BF16), _ones_ext(vc_ref[...]))
    den = acc[:, LANES:]
    if use_sink:
        den = den + jnp.exp2(sink - m)
    o_ref[...] = _unstack_heads(acc[:, :LANES] / den, dh, nh).astype(BF16)


def _ctx_attention(q, k, v, sink, dims, mixer):
    b, s_len, n_ctx = dims
    n_qh, n_kvh, dh = MIXER_HEADS[mixer]
    n_kvblk = k.shape[1] // LANES
    nh = n_qh // n_kvblk
    qw = nh * dh
    ctx_blk = b * s_len // n_ctx
    use_sink = sink is not None
    if not use_sink:
        sink = jnp.zeros((n_qh,), F32)
    return pl.pallas_call(
        functools.partial(_ctx_kernel, dh=dh, nh=nh, use_sink=use_sink),
        grid_spec=pltpu.PrefetchScalarGridSpec(
            num_scalar_prefetch=1,
            grid=(b, n_kvblk),
            in_specs=[pl.BlockSpec((n_ctx, qw), lambda i, p, *_: (ctx_blk + i, p)),
                      pl.BlockSpec((n_ctx, LANES), lambda i, p, *_: (ctx_blk + i, p)),
                      pl.BlockSpec((n_ctx, LANES), lambda i, p, *_: (ctx_blk + i, p))],
            out_specs=pl.BlockSpec((n_ctx, qw), lambda i, p, *_: (i, p))),
        out_shape=jax.ShapeDtypeStruct((b * n_ctx, q.shape[1]), BF16),
        compiler_params=_params("arbitrary", "arbitrary"),
        name="attn_ctx_m%d" % mixer,
    )(sink, q, k, v)


def _oproj_kernel(*refs, n_lat_tiles, with_ctx, moe):
    refs = list(refs)
    ylat_ref = refs.pop(0)
    yctx_ref = refs.pop(0) if with_ctx else None
    x_ref, mod_ref, n2_ref, wo_ref = refs[:4]
    refs = refs[4:]
    if moe:
        wr_ref, br_ref = refs[:2]
        refs = refs[2:]
    xo_ref, h_ref = refs[:2]
    y = ylat_ref[...]
    if with_ctx:
        y = jnp.where(pl.program_id(0) < n_lat_tiles, y, yctx_ref[...])
    x = x_ref[...] + mod_ref[2:3, :] * _dot(y, wo_ref[...])
    xo_ref[...] = x
    h = _rms_mod(x, n2_ref[...], mod_ref[4:5, :], mod_ref[3:4, :])
    h_ref[...] = h.astype(BF16)
    if moe:
        route_ref, counts_ref, base_ref = refs[2:5]
        h_hi, h_lo = _split_bf16(h)
        w_hi, w_lo = _split_bf16(wr_ref[...])
        logits = _dot(h_hi, w_hi) + _dot(h_hi, w_lo) + _dot(h_lo, w_hi) + br_ref[...]
        lane = lax.broadcasted_iota(jnp.int32, logits.shape, 1)
        v1 = jnp.max(logits, axis=-1, keepdims=True)
        i1 = jnp.min(jnp.where(logits == v1, lane, LANES), axis=-1, keepdims=True)
        rest = jnp.where(lane == i1, NEG_BIG, logits)
        v2 = jnp.max(rest, axis=-1, keepdims=True)
        i2 = jnp.min(jnp.where(rest == v2, lane, LANES), axis=-1, keepdims=True)
        e = jnp.exp(v2 - v1)
        w1 = 1.0 / (1.0 + e)
        w2 = e / (1.0 + e)
        @pl.when(pl.program_id(0) == 0)
        def _():
            base_ref[...] = jnp.zeros_like(base_ref)

        tm = logits.shape[0]
        pick1 = lane == i1
        pick2 = lane == i2
        cnt = jnp.where(pick1, 1.0, 0.0) + jnp.where(pick2, 1.0, 0.0)
        earlier = lax.broadcasted_iota(jnp.int32, (tm, tm), 0) > lax.broadcasted_iota(jnp.int32, (tm, tm), 1)
        before = _dot(jnp.where(earlier, 1.0, 0.0).astype(BF16), cnt.astype(BF16)) + base_ref[0:1, :]
        r1 = jnp.sum(jnp.where(pick1, before, 0.0), axis=-1, keepdims=True)
        r2 = jnp.sum(jnp.where(pick2, before, 0.0), axis=-1, keepdims=True)
        base_ref[...] = base_ref[...] + jnp.sum(cnt, axis=0, keepdims=True)
        counts_ref[...] = base_ref[...]
        route = jnp.where(lane == 0, i1.astype(F32), 0.0)
        route = jnp.where(lane == 1, i2.astype(F32), route)
        route = jnp.where(lane == 2, w1, route)
        route = jnp.where(lane == 3, w2, route)
        route = jnp.where(lane == 4, r1, route)
        route = jnp.where(lane == 5, r2, route)
        route_ref[...] = route


def _oproj(y_lat, y_ctx, xs, mod, norm2, w_o, router, dims, last):
    b, s_len, n_ctx = dims
    d = xs.shape[1]
    tm = _row_tile(s_len, b * n_ctx, 512)
    n_lat_tiles = b * s_len // tm
    tiles_per_sample = s_len // tm
    with_ctx = not last
    n_rows = b * s_len if last else xs.shape[0]
    n_tiles = n_rows // tm
    moe = router is not None

    def mod_idx(i):
        return jnp.where(i < n_lat_tiles, i // tiles_per_sample, b)

    row = pl.BlockSpec((tm, d), lambda i: (i, 0))
    in_specs = [pl.BlockSpec((tm, y_lat.shape[1]), lambda i: (jnp.minimum(i, n_lat_tiles - 1), 0))]
    args = [y_lat]
    if with_ctx:
        in_specs.append(pl.BlockSpec((tm, y_ctx.shape[1]), lambda i: (jnp.maximum(i - n_lat_tiles, 0), 0)))
        args.append(y_ctx)
    in_specs += [row,
                 pl.BlockSpec((None, 6, d), lambda i: (mod_idx(i), 0, 0)),
                 pl.BlockSpec((1, d), lambda i: (0, 0)),
                 pl.BlockSpec(w_o.shape, lambda i: (0, 0))]
    args += [xs, mod, norm2.reshape(1, d), w_o.astype(BF16)]
    out_specs = [row, row]
    out_shape = [jax.ShapeDtypeStruct((n_rows, d), F32), jax.ShapeDtypeStruct((n_rows, d), BF16)]
    if moe:
        w_router, b_router = router
        n_e = w_router.shape[1]
        wr = jnp.pad(w_router, ((0, 0), (0, LANES - n_e)))
        br = jnp.pad(b_router, (0, LANES - n_e), constant_values=NEG_BIG).reshape(1, LANES)
        in_specs += [pl.BlockSpec(wr.shape, lambda i: (0, 0)), pl.BlockSpec(br.shape, lambda i: (0, 0))]
        args += [wr, br]
        out_specs += [pl.BlockSpec((tm, LANES), lambda i: (i, 0)), pl.BlockSpec((8, LANES), lambda i: (0, 0))]
        out_shape += [jax.ShapeDtypeStruct((n_rows, LANES), F32), jax.ShapeDtypeStruct((8, LANES), F32)]
    return pl.pallas_call(
        functools.partial(_oproj_kernel, n_lat_tiles=n_lat_tiles, with_ctx=with_ctx, moe=moe),
        grid=(n_tiles,),
        in_specs=in_specs,
        out_specs=out_specs,
        out_shape=out_shape,
        scratch_shapes=[pltpu.VMEM((8, LANES), F32)] if moe else [],
        compiler_params=_params("arbitrary"),
        name="oproj_moe" if moe else "oproj",
    )(*args)


def _ffn_kernel(h_ref, wg_ref, wu_ref, wd_ref, x_ref, mod_ref, o_ref, acc_ref):
    f = pl.program_id(1)

    @pl.when(f == 0)
    def _():
        acc_ref[...] = jnp.zeros_like(acc_ref)

    h = h_ref[...]
    a = _silu(_dot(h, wg_ref[...])) * _dot(h, wu_ref[...])
    acc_ref[...] += _dot(a.astype(BF16), wd_ref[...])

    @pl.when(f == pl.num_programs(1) - 1)
    def _():
        o_ref[...] = x_ref[...] + mod_ref[5:6, :] * acc_ref[...]


def _ffn(h, xs, mod, w_in, w_out, dims):
    b, s_len, n_ctx = dims
    n_rows, d = h.shape
    d_ff = w_out.shape[0]
    tm = _row_tile(s_len, b * n_ctx, 1024)
    tf = FFN_TF
    n_f = d_ff // tf
    n_lat_tiles = b * s_len // tm
    tiles_per_sample = s_len // tm

    def mod_idx(i):
        return jnp.where(i < n_lat_tiles, i // tiles_per_sample, b)

    w_in = w_in.astype(BF16)
    return pl.pallas_call(
        _ffn_kernel,
        grid=(n_rows // tm, n_f),
        in_specs=[pl.BlockSpec((tm, d), lambda i, f: (i, 0)),
                  pl.BlockSpec((d, tf), lambda i, f: (0, f)),
                  pl.BlockSpec((d, tf), lambda i, f: (0, n_f + f)),
                  pl.BlockSpec((tf, d), lambda i, f: (f, 0)),
                  pl.BlockSpec((tm, d), lambda i, f: (i, 0)),
                  pl.BlockSpec((None, 6, d), lambda i, f: (mod_idx(i), 0, 0))],
        out_specs=pl.BlockSpec((tm, d), lambda i, f: (i, 0)),
        out_shape=jax.ShapeDtypeStruct((n_rows, d), F32),
        scratch_shapes=[pltpu.VMEM((tm, d), F32)],
        compiler_params=_params("arbitrary", "arbitrary"),
        name="ffn_dense",
    )(h, w_in, w_in, w_out.astype(BF16), xs, mod)


def _moe_ffn_kernel(te_ref, na_ref, h_ref, wg_ref, wu_ref, wd_ref, *rest):
    o_ref, acc_ref = rest[-2:]
    i, f = pl.program_id(0), pl.program_id(1)

    @pl.when(i < na_ref[0])
    def _():
        @pl.when(f == 0)
        def _():
            acc_ref[...] = jnp.zeros_like(acc_ref)

        h = h_ref[...]
        a = _silu(_dot(h, wg_ref[...].astype(BF16))) * _dot(h, wu_ref[...].astype(BF16))
        acc_ref[...] += _dot(a.astype(BF16), wd_ref[...].astype(BF16))

        @pl.when(f == pl.num_programs(1) - 1)
        def _():
            o_ref[...] = acc_ref[...].astype(o_ref.dtype)


def _moe_ffn(hs, ys, n_rows_total, first_tile, tile_expert, n_active, w_in, w_out):
    n_rows, d = hs.shape
    d_ff = w_out.shape[1]
    tm, tf = MOE_TM, FFN_TF
    n_f = d_ff // tf

    def tile(i, na):
        return jnp.minimum(i, jnp.maximum(na[0] - 1, 0))

    def expert(i, te, na):
        return te[tile(i, na)]

    in_specs = [pl.BlockSpec((tm, d), lambda i, f, te, na: (tile(i, na), 0)),
                pl.BlockSpec((None, d, tf), lambda i, f, te, na: (expert(i, te, na), 0, f)),
                pl.BlockSpec((None, d, tf), lambda i, f, te, na: (expert(i, te, na), 0, n_f + f)),
                pl.BlockSpec((None, tf, d), lambda i, f, te, na: (expert(i, te, na), f, 0))]
    args = [tile_expert, n_active, hs, w_in, w_in, w_out]
    aliases = {}
    if ys is not None:
        in_specs.append(pl.BlockSpec(memory_space=pl.ANY))
        aliases = {len(args): 0}
        args.append(ys)
    return pl.pallas_call(
        _moe_ffn_kernel,
        grid_spec=pltpu.PrefetchScalarGridSpec(
            num_scalar_prefetch=2,
            grid=(n_rows // tm, n_f),
            in_specs=in_specs,
            out_specs=pl.BlockSpec((tm, d), lambda i, f, te, na: (first_tile + tile(i, na), 0)),
            scratch_shapes=[pltpu.VMEM((tm, d), F32)]),
        out_shape=jax.ShapeDtypeStruct((n_rows_total, d), hs.dtype),
        input_output_aliases=aliases,
        compiler_params=_params("arbitrary", "arbitrary"),
        name="moe_ffn",
    )(*args)


def _combine_kernel(x_ref, a_ref, b_ref, route_ref, mod_ref, o_ref):
    w1 = route_ref[:, 2:3]
    w2 = route_ref[:, 3:4]
    mix = w1 * a_ref[...].astype(F32) + w2 * b_ref[...].astype(F32)
    o_ref[...] = x_ref[...] + mod_ref[5:6, :] * mix


def _combine(xs, ya, yb, route, mod, dims):
    b, s_len, n_ctx = dims
    n_rows, d = ya.shape
    tm = _row_tile(s_len, b * n_ctx, 512)
    n_lat_tiles = b * s_len // tm
    tiles_per_sample = s_len // tm

    def mod_idx(i):
        return jnp.where(i < n_lat_tiles, i // tiles_per_sample, b)

    row = pl.BlockSpec((tm, d), lambda i: (i, 0))
    return pl.pallas_call(
        _combine_kernel,
        grid=(n_rows // tm,),
        in_specs=[row, row, row,
                  pl.BlockSpec((tm, LANES), lambda i: (i, 0)),
                  pl.BlockSpec((None, 6, d), lambda i: (mod_idx(i), 0, 0))],
        out_specs=row,
        out_shape=jax.ShapeDtypeStruct((n_rows, d), F32),
        compiler_params=_params("arbitrary"),
        name="moe_combine",
    )(xs, ya, yb, route, mod)


def _route_plan(idx, rank, counts, tm):
    n = idx.shape[0]
    tiles_per = (counts + tm - 1) // tm
    tile_end = jnp.cumsum(tiles_per)
    tile_start = tile_end - tiles_per
    experts = jnp.arange(N_EXPERTS, dtype=jnp.int32)
    start = jnp.sum(jnp.where(idx[:, :, None] == experts, tile_start, 0), axis=-1)
    slot = start * tm + rank
    n_tiles = (2 * n) // tm + N_EXPERTS
    tile_ids = jnp.arange(n_tiles, dtype=jnp.int32)
    tile_expert = jnp.minimum(jnp.sum((tile_end[None, :] <= tile_ids[:, None]).astype(jnp.int32), axis=1),
                              N_EXPERTS - 1)
    token = jnp.arange(2 * n, dtype=jnp.int32) // 2
    _, token_sorted = lax.sort((slot.reshape(-1), token), num_keys=1)
    first_pair = jnp.cumsum(counts) - counts
    row_in_expert = ((tile_ids - tile_start[tile_expert]) * tm)[:, None] + jnp.arange(tm, dtype=jnp.int32)[None, :]
    pair = jnp.clip(first_pair[tile_expert][:, None] + row_in_expert, 0, 2 * n - 1)
    src = jnp.where(row_in_expert < counts[tile_expert][:, None], token_sorted[pair], 0).reshape(-1)
    return slot, src, tile_expert, tile_end[-1:].astype(jnp.int32)


def _take_rows(a, rows):
    return a.at[rows].get(mode="promise_in_bounds")


def _moe(h, xs, route, counts, mod, w_exp_in, w_exp_out, dims):
    idx = route[:, 0:2].astype(jnp.int32)
    rank = route[:, 4:6].astype(jnp.int32)
    slot, src, tile_expert, n_active = _route_plan(idx, rank, counts[0, :N_EXPERTS].astype(jnp.int32), MOE_TM)
    n_tiles = src.shape[0] // MOE_TM
    per_chunk = -(-n_tiles // MOE_CHUNKS)
    ys = None
    for t0 in range(0, n_tiles, per_chunk):
        t1 = min(t0 + per_chunk, n_tiles)
        hs = _take_rows(h, src[t0 * MOE_TM:t1 * MOE_TM])
        ys = _moe_ffn(hs, ys, src.shape[0], t0, tile_expert[t0:t1], jnp.clip(n_active - t0, 0, t1 - t0),
                      w_exp_in, w_exp_out)
    ya = _take_rows(ys, slot[:, 0])
    yb = _take_rows(ys, slot[:, 1])
    return _combine(xs, ya, yb, route, mod, dims)


def _layer(xs, cond, p, dims, mixer, last):
    b, s_len, n_ctx = dims
    mod = _adaln(cond, p["w_mod"], p["b_mod"])
    q, k, v = _qkv(xs, mod, p["norm1"], p["w_qkv"], p["q_norm"], p["k_norm"], dims, mixer)
    if mixer == 0:
        y_lat = _na_attention(q, k, v, p["rel_bias"], dims)
    elif mixer == 1:
        y_lat = _swa_attention(q, k, v, p["sink"], dims)
    else:
        y_lat = _global_attention(q, k, v, dims)
    y_ctx = None if last else _ctx_attention(q, k, v, p.get("sink"), dims, mixer)
    router = (p["w_router"], p["b_router"]) if "w_router" in p else None
    outs = _oproj(y_lat, y_ctx, xs, mod, p["norm2"], p["w_o"], router, dims, last)
    if router is None:
        xs, h = outs
        return _ffn(h, xs, mod, p["w_ffn_in"], p["w_ffn_out"], dims)
    xs, h, route, counts = outs
    return _moe(h, xs, route, counts, mod, p["w_exp_in"], p["w_exp_out"], dims)


def kernel(x, c, ctx, c_ctx, l0_w_mod, l0_b_mod, l0_norm1, l0_norm2, l0_w_qkv, l0_q_norm, l0_k_norm, l0_rel_bias, l0_w_o, l0_w_ffn_in, l0_w_ffn_out, l1_w_mod, l1_b_mod, l1_norm1, l1_norm2, l1_w_qkv, l1_q_norm, l1_k_norm, l1_sink, l1_w_o, l1_w_router, l1_b_router, l1_w_exp_in, l1_w_exp_out, l2_w_mod, l2_b_mod, l2_norm1, l2_norm2, l2_w_qkv, l2_q_norm, l2_k_norm, l2_w_o, l2_w_ffn_in, l2_w_ffn_out, l3_w_mod, l3_b_mod, l3_norm1, l3_norm2, l3_w_qkv, l3_q_norm, l3_k_norm, l3_rel_bias, l3_w_o, l3_w_router, l3_b_router, l3_w_exp_in, l3_w_exp_out):
    b, s_len, d = x.shape
    n_ctx = ctx.shape[1]
    dims = (b, s_len, n_ctx)
    layers = (
        dict(w_mod=l0_w_mod, b_mod=l0_b_mod, norm1=l0_norm1, norm2=l0_norm2, w_qkv=l0_w_qkv, q_norm=l0_q_norm,
             k_norm=l0_k_norm, rel_bias=l0_rel_bias, w_o=l0_w_o, w_ffn_in=l0_w_ffn_in, w_ffn_out=l0_w_ffn_out),
        dict(w_mod=l1_w_mod, b_mod=l1_b_mod, norm1=l1_norm1, norm2=l1_norm2, w_qkv=l1_w_qkv, q_norm=l1_q_norm,
             k_norm=l1_k_norm, sink=l1_sink, w_o=l1_w_o, w_router=l1_w_router, b_router=l1_b_router,
             w_exp_in=l1_w_exp_in, w_exp_out=l1_w_exp_out),
        dict(w_mod=l2_w_mod, b_mod=l2_b_mod, norm1=l2_norm1, norm2=l2_norm2, w_qkv=l2_w_qkv, q_norm=l2_q_norm,
             k_norm=l2_k_norm, w_o=l2_w_o, w_ffn_in=l2_w_ffn_in, w_ffn_out=l2_w_ffn_out),
        dict(w_mod=l3_w_mod, b_mod=l3_b_mod, norm1=l3_norm1, norm2=l3_norm2, w_qkv=l3_w_qkv, q_norm=l3_q_norm,
             k_norm=l3_k_norm, rel_bias=l3_rel_bias, w_o=l3_w_o, w_router=l3_w_router, b_router=l3_b_router,
             w_exp_in=l3_w_exp_in, w_exp_out=l3_w_exp_out),
    )
    xs = jnp.concatenate([x.reshape(b * s_len, d), ctx.reshape(b * n_ctx, d)], axis=0)
    pad_rows = -(b + 1) % 8
    cond = jnp.concatenate([c, c_ctx[None, :], jnp.zeros((pad_rows, d), F32)], axis=0)
    n_layers = len(layers)
    for i, p in enumerate(layers):
        xs = _layer(xs, cond, p, dims, i % 3, i == n_layers - 1)
    return xs.reshape(b, s_len, d)
```

```python
import functools

import jax
import jax.numpy as jnp
from jax import lax
from jax.experimental import pallas as pl
from jax.experimental.pallas import tpu as pltpu

F32 = jnp.float32
BF16 = jnp.bfloat16

GRID_W = 64
NORM_EPS = 1e-6
ROPE_THETA = 10000.0
NA_WIN_ROWS = 8
NA_WIN_COLS = 16
SW_WINDOW = 128
N_EXPERTS = 8
MIXER_HEADS = ((16, 16, 64), (16, 4, 64), (8, 4, 128))

LANES = 128
VMEM_LIMIT_BYTES = 56 * 1024 * 1024
NEG_BIG = -1e30
LOG2E = 1.4426950408889634

QKV_CHUNK = 512
GLB_TQ = 256
GLB_CHUNK = 256
SWA_TQ = 256
NA_CTX_ROWS = 1024
NA_UNROLL = 16
MOE_TM = 1024
MOE_CHUNKS = 4
FFN_TF = 512


def _params(*sem):
    return pltpu.CompilerParams(dimension_semantics=sem, vmem_limit_bytes=VMEM_LIMIT_BYTES)


def _row_tile(n_lat_per_sample, n_ctx_rows, cap):
    for tm in (1024, 512, 256, 128):
        if tm <= cap and n_lat_per_sample % tm == 0 and n_ctx_rows % tm == 0:
            return tm
    raise ValueError("no row tile fits")


def _split_bf16(a):
    hi = a.astype(BF16)
    lo = (a - hi.astype(F32)).astype(BF16)
    return hi, lo


def _dot(a, b):
    return jnp.dot(a, b, preferred_element_type=F32)


def _dot_t(a, b):
    return lax.dot_general(a, b, (((1,), (1,)), ((), ())), preferred_element_type=F32)


def _silu(g):
    return g / (1.0 + jnp.exp(-g))


def _rms_mod(x, gain, scale, shift):
    ms = jnp.mean(x * x, axis=-1, keepdims=True)
    return x * lax.rsqrt(ms + NORM_EPS) * gain * (1.0 + scale) + shift


def _adaln_kernel(c_ref, w_ref, b_ref, o_ref):
    a_hi, a_lo = _split_bf16(_silu(c_ref[...]))
    w_hi, w_lo = _split_bf16(w_ref[...])
    o_ref[...] = _dot(a_hi, w_hi) + _dot(a_hi, w_lo) + _dot(a_lo, w_hi) + b_ref[...]


def _adaln(cond, w_mod, b_mod):
    r, d = cond.shape
    n = w_mod.shape[1]
    tn = 1536
    out = pl.pallas_call(
        _adaln_kernel,
        grid=(n // tn,),
        in_specs=[pl.BlockSpec((r, d), lambda j: (0, 0)),
                  pl.BlockSpec((d, tn), lambda j: (0, j)),
                  pl.BlockSpec((1, tn), lambda j: (0, j))],
        out_specs=pl.BlockSpec((r, tn), lambda j: (0, j)),
        out_shape=jax.ShapeDtypeStruct((r, n), F32),
        compiler_params=_params("arbitrary"),
        name="adaln",
    )(cond, w_mod, b_mod.reshape(1, n))
    return out.reshape(r, 6, d)


def _rot_half(z, dh):
    if dh == LANES:
        return pltpu.roll(z, LANES // 2, axis=1)
    lane = lax.broadcasted_iota(jnp.int32, z.shape, 1)
    from_right = pltpu.roll(z, LANES - dh // 2, axis=1)
    from_left = pltpu.roll(z, dh // 2, axis=1)
    return jnp.where((lane % dh) < dh // 2, from_right, from_left)


def _head_mean_sq(z, dh):
    z2 = z * z
    if dh == LANES:
        return jnp.broadcast_to(jnp.sum(z2, axis=-1, keepdims=True), z.shape) * (1.0 / dh)
    low = lax.broadcasted_iota(jnp.int32, z.shape, 1) < dh
    s_low = jnp.sum(jnp.where(low, z2, 0.0), axis=-1, keepdims=True)
    s_high = jnp.sum(jnp.where(low, 0.0, z2), axis=-1, keepdims=True)
    return jnp.where(low, s_low, s_high) * (1.0 / dh)


def _qkv_kernel(*refs, n_q, n_kv, dh, rope):
    if rope:
        x_ref, mod_ref, n1_ref, w_ref, g_ref, cos_ref, sin_ref, q_ref, k_ref, v_ref = refs
    else:
        x_ref, mod_ref, n1_ref, w_ref, g_ref, q_ref, k_ref, v_ref = refs
    h = _rms_mod(x_ref[...], n1_ref[...], mod_ref[1:2, :], mod_ref[0:1, :]).astype(BF16)
    n_qk = n_q + n_kv
    cw = QKV_CHUNK
    for c in range((n_qk + n_kv) // cw):
        y = _dot(h, w_ref[:, c * cw:(c + 1) * cw])
        for s in range(cw // LANES):
            col = c * cw + s * LANES
            z = y[:, s * LANES:(s + 1) * LANES]
            if col < n_qk:
                scale = lax.rsqrt(_head_mean_sq(z, dh) + NORM_EPS)
                z = z * g_ref[:, col:col + LANES]
                if rope:
                    z = z * cos_ref[...] + _rot_half(z, dh) * sin_ref[...]
                z = z * scale
            z = z.astype(BF16)
            if col < n_q:
                q_ref[:, col:col + LANES] = z
            elif col < n_qk:
                k_ref[:, col - n_q:col - n_q + LANES] = z
            else:
                v_ref[:, col - n_qk:col - n_qk + LANES] = z


def _rope_tables(s_len, dh, tm):
    n_freq = dh // 4
    inv_freq = ROPE_THETA ** (-jnp.arange(n_freq, dtype=F32) / n_freq)
    t = jnp.arange(s_len)
    row = (t // GRID_W).astype(F32)
    col = (t % GRID_W).astype(F32)
    ang = jnp.concatenate([row[:, None] * inv_freq, col[:, None] * inv_freq], axis=-1)
    cos, sin = jnp.cos(ang), jnp.sin(ang)
    reps = LANES // dh
    cos_t = jnp.tile(jnp.concatenate([cos, cos], axis=-1), (1, reps))
    sin_t = jnp.tile(jnp.concatenate([-sin, sin], axis=-1), (1, reps))
    cos_t = jnp.concatenate([cos_t, jnp.ones((tm, LANES), F32)], axis=0)
    sin_t = jnp.concatenate([sin_t, jnp.zeros((tm, LANES), F32)], axis=0)
    return cos_t, sin_t


def _qkv(xs, mod, norm1, w_qkv, q_gain, k_gain, dims, mixer):
    b, s_len, n_ctx = dims
    n_rows, d = xs.shape
    n_qh, n_kvh, dh = MIXER_HEADS[mixer]
    n_q, n_kv = n_qh * dh, n_kvh * dh
    rope = mixer != 0
    tm = _row_tile(s_len, b * n_ctx, 512)
    n_lat_tiles = b * s_len // tm
    tiles_per_sample = s_len // tm

    gains = jnp.concatenate([jnp.tile(q_gain * (dh ** -0.5 * LOG2E), n_qh),
                             jnp.tile(k_gain, n_kvh)]).reshape(1, n_q + n_kv)

    def mod_idx(i):
        return jnp.where(i < n_lat_tiles, i // tiles_per_sample, b)

    in_specs = [pl.BlockSpec((tm, d), lambda i: (i, 0)),
                pl.BlockSpec((None, 6, d), lambda i: (mod_idx(i), 0, 0)),
                pl.BlockSpec((1, d), lambda i: (0, 0)),
                pl.BlockSpec(w_qkv.shape, lambda i: (0, 0)),
                pl.BlockSpec(gains.shape, lambda i: (0, 0))]
    args = [xs, mod, norm1.reshape(1, d), w_qkv.astype(BF16), gains]
    if rope:
        cos_t, sin_t = _rope_tables(s_len, dh, tm)

        def pos_idx(i):
            return jnp.where(i < n_lat_tiles, i % tiles_per_sample, tiles_per_sample)

        in_specs += [pl.BlockSpec((tm, LANES), lambda i: (pos_idx(i), 0))] * 2
        args += [cos_t, sin_t]
    return pl.pallas_call(
        functools.partial(_qkv_kernel, n_q=n_q, n_kv=n_kv, dh=dh, rope=rope),
        grid=(n_rows // tm,),
        in_specs=in_specs,
        out_specs=[pl.BlockSpec((tm, n_q), lambda i: (i, 0)),
                   pl.BlockSpec((tm, n_kv), lambda i: (i, 0)),
                   pl.BlockSpec((tm, n_kv), lambda i: (i, 0))],
        out_shape=[jax.ShapeDtypeStruct((n_rows, n_q), BF16),
                   jax.ShapeDtypeStruct((n_rows, n_kv), BF16),
                   jax.ShapeDtypeStruct((n_rows, n_kv), BF16)],
        compiler_params=_params("arbitrary"),
        name="qkv_m%d" % mixer,
    )(*args)


def _stack_heads(q, dh, nh):
    if dh == LANES:
        return jnp.concatenate([q[:, h * LANES:(h + 1) * LANES] for h in range(nh)], axis=0)
    per_kv = nh // 2
    tq = q.shape[0]
    lane = lax.broadcasted_iota(jnp.int32, (tq, LANES), 1)
    ops = []
    for h in range(nh):
        slot = h // per_kv
        chunk = q[:, (h // 2) * LANES:(h // 2 + 1) * LANES].astype(F32)
        if h % 2 != slot:
            chunk = pltpu.roll(chunk, dh, axis=1)
        keep = lane < dh if slot == 0 else lane >= dh
        ops.append(jnp.where(keep, chunk, 0.0).astype(BF16))
    return jnp.concatenate(ops, axis=0)


def _unstack_heads(o, dh, nh):
    tq = o.shape[0] // nh
    if dh == LANES:
        return jnp.concatenate([o[h * tq:(h + 1) * tq] for h in range(nh)], axis=1)
    per_kv = nh // 2
    lane = lax.broadcasted_iota(jnp.int32, (tq, LANES), 1)
    chunks = []
    for c in range(nh // 2):
        parts = []
        for h in (2 * c, 2 * c + 1):
            oh = o[h * tq:(h + 1) * tq]
            if h % 2 != h // per_kv:
                oh = pltpu.roll(oh, dh, axis=1)
            parts.append(oh)
        chunks.append(jnp.where(lane < dh, parts[0], parts[1]))
    return jnp.concatenate(chunks, axis=1)


def _lanes(x, n):
    return x if n == LANES else jnp.concatenate([x] * (n // LANES), axis=1)


def _block_max(s):
    return functools.reduce(jnp.maximum, [s[:, j * LANES:(j + 1) * LANES] for j in range(s.shape[1] // LANES)])


def _row_max(blk):
    return jnp.broadcast_to(jnp.max(blk, axis=-1, keepdims=True), blk.shape)


def _dot_row_halves(p, v):
    half = p.shape[0] // 2
    return jnp.concatenate([_dot(p[0:half, :], v), _dot(p[half:, :], v)], axis=0)


def _ones_ext(v):
    return jnp.concatenate([v, jnp.ones_like(v)], axis=1)


def _sink_rows(sink_ref, first_head, nh, tq):
    return jnp.concatenate([jnp.full((tq, LANES), sink_ref[first_head + h] * LOG2E, F32) for h in range(nh)], axis=0)


def _na_kernel(q_ref, k_ref, v_ref, kc_ref, vc_ref, bias_ref, o_ref,
               qs_ref, vext_ref, mc_ref, numc_ref, lc_ref, *, rows):
    s_len = q_ref.shape[0]
    band = NA_WIN_ROWS * GRID_W
    lane = lax.broadcasted_iota(jnp.int32, (s_len, LANES), 1)
    q = q_ref[...]
    qs_ref[0:s_len, :] = jnp.where(lane < 64, q, jnp.zeros_like(q))
    qs_ref[s_len:, :] = jnp.where(lane >= 64, q, jnp.zeros_like(q))
    vext_ref[:, 0:LANES] = v_ref[...]
    vext_ref[:, LANES:] = jnp.ones((s_len, LANES), BF16)

    kc = kc_ref[...]
    vcx = _ones_ext(vc_ref[...])
    for c in range(2 * s_len // NA_CTX_ROWS):
        rs = slice(c * NA_CTX_ROWS, (c + 1) * NA_CTX_ROWS)
        s = _dot_t(qs_ref[rs, :], kc)
        m = _row_max(_block_max(s))
        acc = _dot(jnp.exp2(s - _lanes(m, s.shape[1])).astype(BF16), vcx)
        mc_ref[rs, :] = m
        numc_ref[rs, :] = acc[:, :LANES]
        lc_ref[rs, :] = acc[:, LANES:]

    lane_q = lax.broadcasted_iota(jnp.int32, (GRID_W, LANES), 1)

    def both_heads(ref, q0):
        return jnp.concatenate([ref[pl.ds(q0, GRID_W), :], ref[pl.ds(s_len + q0, GRID_W), :]], axis=0)

    def one_row(r, carry):
        r0 = jnp.clip(r - NA_WIN_ROWS // 2, 0, rows - NA_WIN_ROWS)
        q0 = pl.multiple_of(r * GRID_W, GRID_W)
        k0 = pl.multiple_of(r0 * GRID_W, GRID_W)
        d0 = r0 - r + NA_WIN_ROWS - 1
        bias = jnp.concatenate(
            [jnp.concatenate([bias_ref[0, d0 + 2 * j], bias_ref[1, d0 + 2 * j]], axis=0)
             for j in range(NA_WIN_ROWS // 2)], axis=1)
        qs = both_heads(qs_ref, q0)
        half = band // 2
        k1 = pl.multiple_of(k0 + half, GRID_W)
        s = jnp.concatenate([_dot_t(qs, k_ref[pl.ds(k0, half), :]), _dot_t(qs, k_ref[pl.ds(k1, half), :])],
                            axis=1) + bias
        mc = both_heads(mc_ref, q0)
        m = jnp.maximum(_row_max(_block_max(s)), mc)
        acc = _dot(jnp.exp2(s - _lanes(m, band)).astype(BF16), vext_ref[pl.ds(k0, band), :])
        alpha = jnp.exp2(mc - m)
        num = acc[:, :LANES] + alpha * both_heads(numc_ref, q0)
        den = acc[:, LANES:] + alpha * both_heads(lc_ref, q0)
        o = num / den
        o_ref[pl.ds(q0, GRID_W), :] = jnp.where(lane_q < 64, o[:GRID_W], o[GRID_W:]).astype(BF16)
        return carry

    lax.fori_loop(0, rows, one_row, 0, unroll=NA_UNROLL)


def _na_bias_tables(rel_bias):
    n_h = rel_bias.shape[0]
    col = jnp.arange(GRID_W)
    col_start = jnp.clip(col - NA_WIN_COLS // 2, 0, GRID_W - NA_WIN_COLS)
    in_win = (col[None, :] >= col_start[:, None]) & (col[None, :] < col_start[:, None] + NA_WIN_COLS)
    col_idx = jnp.clip(col[None, :] - col[:, None] + NA_WIN_COLS - 1, 0, 2 * NA_WIN_COLS - 2)
    masked = jnp.where(in_win[None, None], rel_bias[:, :, col_idx] * LOG2E, NEG_BIG)
    pairs = jnp.concatenate([masked[:, :-1], masked[:, 1:]], axis=-1)
    return pairs.reshape((n_h // 2, 2) + pairs.shape[1:])


def _na_attention(q, k, v, rel_bias, dims):
    b, s_len, n_ctx = dims
    rows = s_len // GRID_W
    n_pairs = q.shape[1] // LANES
    bias = _na_bias_tables(rel_bias)
    ctx_blk = b * s_len // n_ctx
    lat = pl.BlockSpec((s_len, LANES), lambda p, i: (i, p))
    ctx = pl.BlockSpec((n_ctx, LANES), lambda p, i: (ctx_blk + i, p))
    return pl.pallas_call(
        functools.partial(_na_kernel, rows=rows),
        grid=(n_pairs, b),
        in_specs=[lat, lat, lat, ctx, ctx,
                  pl.BlockSpec((None,) + bias.shape[1:], lambda p, i: (p, 0, 0, 0, 0))],
        out_specs=lat,
        out_shape=jax.ShapeDtypeStruct((b * s_len, q.shape[1]), BF16),
        scratch_shapes=[pltpu.VMEM((2 * s_len, LANES), BF16),
                        pltpu.VMEM((s_len, 2 * LANES), BF16),
                        pltpu.VMEM((2 * s_len, LANES), F32),
                        pltpu.VMEM((2 * s_len, LANES), F32),
                        pltpu.VMEM((2 * s_len, LANES), F32)],
        compiler_params=_params("arbitrary", "arbitrary"),
        name="attn_na",
    )(q, k, v, k, v, bias)


def _swa_kernel(sink_ref, q_ref, k_ref, v_ref, kc_ref, vc_ref, o_ref, kwin_ref, vwin_ref, *, nh, n_units, s_len):
    tq = q_ref.shape[0]
    span = tq + 2 * SW_WINDOW
    p_blk, t = pl.program_id(1), pl.program_id(2)

    @pl.when(t == 0)
    def _():
        kwin_ref[span:, :] = kc_ref[...]
        vwin_ref[span:, 0:LANES] = vc_ref[...]
        vwin_ref[:, LANES:] = jnp.ones((vwin_ref.shape[0], LANES), BF16)

    q0 = t * tq
    k0 = pl.multiple_of(jnp.clip(q0 - SW_WINDOW, 0, s_len - span), SW_WINDOW)
    kwin_ref[0:span, :] = k_ref[pl.ds(k0, span), :]
    vwin_ref[0:span, 0:LANES] = v_ref[pl.ds(k0, span), :]

    row = lax.broadcasted_iota(jnp.int32, (tq, span), 0)
    col = lax.broadcasted_iota(jnp.int32, (tq, span), 1)
    bias = jnp.where(jnp.abs(col - row + (k0 - q0)) <= SW_WINDOW, 0.0, NEG_BIG)

    qs_all = _stack_heads(q_ref[...], 64, nh)
    sink_all = _sink_rows(sink_ref, p_blk * nh, nh, tq)
    hu = nh // n_units
    vwin = vwin_ref[...]
    kd = 2 * LANES
    outs = []
    for u in range(n_units):
        qs = qs_all[u * hu * tq:(u + 1) * hu * tq]
        sink = sink_all[u * hu * tq:(u + 1) * hu * tq]
        s = jnp.concatenate([_dot_t(qs, kwin_ref[j * kd:(j + 1) * kd, :]) for j in range(kwin_ref.shape[0] // kd)],
                            axis=1)
        s = jnp.concatenate([(s[:, :span].reshape(hu, tq, span) + bias[None]).reshape(hu * tq, span),
                             s[:, span:]], axis=1)
        m = jnp.maximum(_row_max(_block_max(s)), sink)
        acc = _dot(jnp.exp2(s - _lanes(m, s.shape[1])).astype(BF16), vwin)
        outs.append(acc[:, :LANES] / (acc[:, LANES:] + jnp.exp2(sink - m)))
    o_ref[...] = _unstack_heads(jnp.concatenate(outs, axis=0), 64, nh).astype(BF16)


def _swa_attention(q, k, v, sink, dims):
    b, s_len, n_ctx = dims
    tq = SWA_TQ
    n_t = s_len // tq
    n_kvblk = k.shape[1] // LANES
    nh = q.shape[1] // k.shape[1] * 2
    qw = nh * 64
    ctx_blk = b * s_len // n_ctx
    n_win = tq + 2 * SW_WINDOW + n_ctx
    assert n_win % (2 * LANES) == 0
    qspec = pl.BlockSpec((tq, qw), lambda i, p, t, *_: (i * n_t + t, p))
    lat = pl.BlockSpec((s_len, LANES), lambda i, p, t, *_: (i, p))
    ctx = pl.BlockSpec((n_ctx, LANES), lambda i, p, t, *_: (ctx_blk + i, p))
    return pl.pallas_call(
        functools.partial(_swa_kernel, nh=nh, n_units=nh, s_len=s_len),
        grid_spec=pltpu.PrefetchScalarGridSpec(
            num_scalar_prefetch=1,
            grid=(b, n_kvblk, n_t),
            in_specs=[qspec, lat, lat, ctx, ctx],
            out_specs=qspec,
            scratch_shapes=[pltpu.VMEM((n_win, LANES), BF16),
                            pltpu.VMEM((n_win, 2 * LANES), BF16)]),
        out_shape=jax.ShapeDtypeStruct((b * s_len, q.shape[1]), BF16),
        compiler_params=_params("arbitrary", "arbitrary", "arbitrary"),
        name="attn_swa",
    )(sink, q, k, v, k, v)


def _global_kernel(q_ref, k_ref, v_ref, kc_ref, vc_ref, o_ref, kall_ref, vext_ref, s_ref, p_ref, *, nh, s_len):
    @pl.when(pl.program_id(2) == 0)
    def _():
        kall_ref[0:s_len, :] = k_ref[...]
        kall_ref[s_len:, :] = kc_ref[...]
        vext_ref[0:s_len, 0:LANES] = v_ref[...]
        vext_ref[s_len:, 0:LANES] = vc_ref[...]
        vext_ref[:, LANES:] = jnp.ones((vext_ref.shape[0], LANES), BF16)

    qs = _stack_heads(q_ref[...], LANES, nh)
    n_chunks = kall_ref.shape[0] // GLB_CHUNK
    mrun = None
    for c in range(n_chunks):
        ks = slice(c * GLB_CHUNK, (c + 1) * GLB_CHUNK)
        s = _dot_t(qs, kall_ref[ks, :])
        s_ref[:, ks] = s
        blk = _block_max(s)
        mrun = blk if mrun is None else jnp.maximum(mrun, blk)
    m = _lanes(_row_max(mrun), GLB_CHUNK)
    for c in range(n_chunks):
        ks = slice(c * GLB_CHUNK, (c + 1) * GLB_CHUNK)
        p_ref[:, ks] = jnp.exp2(s_ref[:, ks] - m).astype(BF16)
    acc = _dot_row_halves(p_ref, vext_ref[...])
    o_ref[...] = _unstack_heads(acc[:, :LANES] / acc[:, LANES:], LANES, nh).astype(BF16)


def _global_attention(q, k, v, dims):
    b, s_len, n_ctx = dims
    tq = GLB_TQ
    n_t = s_len // tq
    n_kvblk = k.shape[1] // LANES
    nh = q.shape[1] // k.shape[1]
    qw = nh * LANES
    ctx_blk = b * s_len // n_ctx
    n_keys = s_len + n_ctx
    qspec = pl.BlockSpec((tq, qw), lambda i, p, t: (i * n_t + t, p))
    lat = pl.BlockSpec((s_len, LANES), lambda i, p, t: (i, p))
    ctx = pl.BlockSpec((n_ctx, LANES), lambda i, p, t: (ctx_blk + i, p))
    return pl.pallas_call(
        functools.partial(_global_kernel, nh=nh, s_len=s_len),
        grid=(b, n_kvblk, n_t),
        in_specs=[qspec, lat, lat, ctx, ctx],
        out_specs=qspec,
        out_shape=jax.ShapeDtypeStruct((b * s_len, q.shape[1]), BF16),
        scratch_shapes=[pltpu.VMEM((n_keys, LANES), BF16),
                        pltpu.VMEM((n_keys, 2 * LANES), BF16),
                        pltpu.VMEM((nh * tq, n_keys), F32),
                        pltpu.VMEM((nh * tq, n_keys), BF16)],
        compiler_params=_params("arbitrary", "arbitrary", "arbitrary"),
        name="attn_global",
    )(q, k, v, k, v)


def _ctx_kernel(sink_ref, q_ref, kc_ref, vc_ref, o_ref, *, dh, nh, use_sink):
    tq = q_ref.shape[0]
    qs = _stack_heads(q_ref[...], dh, nh)
    s = _dot_t(qs, kc_ref[...])
    m = _row_max(_block_max(s))
    if use_sink:
        sink = _sink_rows(sink_ref, pl.program_id(1) * nh, nh, tq)
        m = jnp.maximum(m, sink)
    acc = _dot(jnp.exp2(s - _lanes(m, s.shape[1])).astype(BF16), _ones_ext(vc_ref[...]))
    den = acc[:, LANES:]
    if use_sink:
        den = den + jnp.exp2(sink - m)
    o_ref[...] = _unstack_heads(acc[:, :LANES] / den, dh, nh).astype(BF16)


def _ctx_attention(q, k, v, sink, dims, mixer):
    b, s_len, n_ctx = dims
    n_qh, n_kvh, dh = MIXER_HEADS[mixer]
    n_kvblk = k.shape[1] // LANES
    nh = n_qh // n_kvblk
    qw = nh * dh
    ctx_blk = b * s_len // n_ctx
    use_sink = sink is not None
    if not use_sink:
        sink = jnp.zeros((n_qh,), F32)
    return pl.pallas_call(
        functools.partial(_ctx_kernel, dh=dh, nh=nh, use_sink=use_sink),
        grid_spec=pltpu.PrefetchScalarGridSpec(
            num_scalar_prefetch=1,
            grid=(b, n_kvblk),
            in_specs=[pl.BlockSpec((n_ctx, qw), lambda i, p, *_: (ctx_blk + i, p)),
                      pl.BlockSpec((n_ctx, LANES), lambda i, p, *_: (ctx_blk + i, p)),
                      pl.BlockSpec((n_ctx, LANES), lambda i, p, *_: (ctx_blk + i, p))],
            out_specs=pl.BlockSpec((n_ctx, qw), lambda i, p, *_: (i, p))),
        out_shape=jax.ShapeDtypeStruct((b * n_ctx, q.shape[1]), BF16),
        compiler_params=_params("arbitrary", "arbitrary"),
        name="attn_ctx_m%d" % mixer,
    )(sink, q, k, v)


def _oproj_kernel(*refs, n_lat_tiles, with_ctx, moe):
    refs = list(refs)
    ylat_ref = refs.pop(0)
    yctx_ref = refs.pop(0) if with_ctx else None
    x_ref, mod_ref, n2_ref, wo_ref = refs[:4]
    refs = refs[4:]
    if moe:
        wr_ref, br_ref = refs[:2]
        refs = refs[2:]
    xo_ref, h_ref = refs[:2]
    y = ylat_ref[...]
    if with_ctx:
        y = jnp.where(pl.program_id(0) < n_lat_tiles, y, yctx_ref[...])
    x = x_ref[...] + mod_ref[2:3, :] * _dot(y, wo_ref[...])
    xo_ref[...] = x
    h = _rms_mod(x, n2_ref[...], mod_ref[4:5, :], mod_ref[3:4, :])
    h_ref[...] = h.astype(BF16)
    if moe:
        route_ref, counts_ref, base_ref = refs[2:5]
        h_hi, h_lo = _split_bf16(h)
        w_hi, w_lo = _split_bf16(wr_ref[...])
        logits = _dot(h_hi, w_hi) + _dot(h_hi, w_lo) + _dot(h_lo, w_hi) + br_ref[...]
        lane = lax.broadcasted_iota(jnp.int32, logits.shape, 1)
        v1 = jnp.max(logits, axis=-1, keepdims=True)
        i1 = jnp.min(jnp.where(logits == v1, lane, LANES), axis=-1, keepdims=True)
        rest = jnp.where(lane == i1, NEG_BIG, logits)
        v2 = jnp.max(rest, axis=-1, keepdims=True)
        i2 = jnp.min(jnp.where(rest == v2, lane, LANES), axis=-1, keepdims=True)
        e = jnp.exp(v2 - v1)
        w1 = 1.0 / (1.0 + e)
        w2 = e / (1.0 + e)
        @pl.when(pl.program_id(0) == 0)
        def _():
            base_ref[...] = jnp.zeros_like(base_ref)

        tm = logits.shape[0]
        pick1 = lane == i1
        pick2 = lane == i2
        cnt = jnp.where(pick1, 1.0, 0.0) + jnp.where(pick2, 1.0, 0.0)
        earlier = lax.broadcasted_iota(jnp.int32, (tm, tm), 0) > lax.broadcasted_iota(jnp.int32, (tm, tm), 1)
        before = _dot(jnp.where(earlier, 1.0, 0.0).astype(BF16), cnt.astype(BF16)) + base_ref[0:1, :]
        r1 = jnp.sum(jnp.where(pick1, before, 0.0), axis=-1, keepdims=True)
        r2 = jnp.sum(jnp.where(pick2, before, 0.0), axis=-1, keepdims=True)
        base_ref[...] = base_ref[...] + jnp.sum(cnt, axis=0, keepdims=True)
        counts_ref[...] = base_ref[...]
        route = jnp.where(lane == 0, i1.astype(F32), 0.0)
        route = jnp.where(lane == 1, i2.astype(F32), route)
        route = jnp.where(lane == 2, w1, route)
        route = jnp.where(lane == 3, w2, route)
        route = jnp.where(lane == 4, r1, route)
        route = jnp.where(lane == 5, r2, route)
        route_ref[...] = route


def _oproj(y_lat, y_ctx, xs, mod, norm2, w_o, router, dims, last):
    b, s_len, n_ctx = dims
    d = xs.shape[1]
    tm = _row_tile(s_len, b * n_ctx, 512)
    n_lat_tiles = b * s_len // tm
    tiles_per_sample = s_len // tm
    with_ctx = not last
    n_rows = b * s_len if last else xs.shape[0]
    n_tiles = n_rows // tm
    moe = router is not None

    def mod_idx(i):
        return jnp.where(i < n_lat_tiles, i // tiles_per_sample, b)

    row = pl.BlockSpec((tm, d), lambda i: (i, 0))
    in_specs = [pl.BlockSpec((tm, y_lat.shape[1]), lambda i: (jnp.minimum(i, n_lat_tiles - 1), 0))]
    args = [y_lat]
    if with_ctx:
        in_specs.append(pl.BlockSpec((tm, y_ctx.shape[1]), lambda i: (jnp.maximum(i - n_lat_tiles, 0), 0)))
        args.append(y_ctx)
    in_specs += [row,
                 pl.BlockSpec((None, 6, d), lambda i: (mod_idx(i), 0, 0)),
                 pl.BlockSpec((1, d), lambda i: (0, 0)),
                 pl.BlockSpec(w_o.shape, lambda i: (0, 0))]
    args += [xs, mod, norm2.reshape(1, d), w_o.astype(BF16)]
    out_specs = [row, row]
    out_shape = [jax.ShapeDtypeStruct((n_rows, d), F32), jax.ShapeDtypeStruct((n_rows, d), BF16)]
    if moe:
        w_router, b_router = router
        n_e = w_router.shape[1]
        wr = jnp.pad(w_router, ((0, 0), (0, LANES - n_e)))
        br = jnp.pad(b_router, (0, LANES - n_e), constant_values=NEG_BIG).reshape(1, LANES)
        in_specs += [pl.BlockSpec(wr.shape, lambda i: (0, 0)), pl.BlockSpec(br.shape, lambda i: (0, 0))]
        args += [wr, br]
        out_specs += [pl.BlockSpec((tm, LANES), lambda i: (i, 0)), pl.BlockSpec((8, LANES), lambda i: (0, 0))]
        out_shape += [jax.ShapeDtypeStruct((n_rows, LANES), F32), jax.ShapeDtypeStruct((8, LANES), F32)]
    return pl.pallas_call(
        functools.partial(_oproj_kernel, n_lat_tiles=n_lat_tiles, with_ctx=with_ctx, moe=moe),
        grid=(n_tiles,),
        in_specs=in_specs,
        out_specs=out_specs,
        out_shape=out_shape,
        scratch_shapes=[pltpu.VMEM((8, LANES), F32)] if moe else [],
        compiler_params=_params("arbitrary"),
        name="oproj_moe" if moe else "oproj",
    )(*args)


def _ffn_kernel(h_ref, wg_ref, wu_ref, wd_ref, x_ref, mod_ref, o_ref, acc_ref):
    f = pl.program_id(1)

    @pl.when(f == 0)
    def _():
        acc_ref[...] = jnp.zeros_like(acc_ref)

    h = h_ref[...]
    a = _silu(_dot(h, wg_ref[...])) * _dot(h, wu_ref[...])
    acc_ref[...] += _dot(a.astype(BF16), wd_ref[...])

    @pl.when(f == pl.num_programs(1) - 1)
    def _():
        o_ref[...] = x_ref[...] + mod_ref[5:6, :] * acc_ref[...]


def _ffn(h, xs, mod, w_in, w_out, dims):
    b, s_len, n_ctx = dims
    n_rows, d = h.shape
    d_ff = w_out.shape[0]
    tm = _row_tile(s_len, b * n_ctx, 1024)
    tf = FFN_TF
    n_f = d_ff // tf
    n_lat_tiles = b * s_len // tm
    tiles_per_sample = s_len // tm

    def mod_idx(i):
        return jnp.where(i < n_lat_tiles, i // tiles_per_sample, b)

    w_in = w_in.astype(BF16)
    return pl.pallas_call(
        _ffn_kernel,
        grid=(n_rows // tm, n_f),
        in_specs=[pl.BlockSpec((tm, d), lambda i, f: (i, 0)),
                  pl.BlockSpec((d, tf), lambda i, f: (0, f)),
                  pl.BlockSpec((d, tf), lambda i, f: (0, n_f + f)),
                  pl.BlockSpec((tf, d), lambda i, f: (f, 0)),
                  pl.BlockSpec((tm, d), lambda i, f: (i, 0)),
                  pl.BlockSpec((None, 6, d), lambda i, f: (mod_idx(i), 0, 0))],
        out_specs=pl.BlockSpec((tm, d), lambda i, f: (i, 0)),
        out_shape=jax.ShapeDtypeStruct((n_rows, d), F32),
        scratch_shapes=[pltpu.VMEM((tm, d), F32)],
        compiler_params=_params("arbitrary", "arbitrary"),
        name="ffn_dense",
    )(h, w_in, w_in, w_out.astype(BF16), xs, mod)


def _moe_ffn_kernel(te_ref, na_ref, h_ref, wg_ref, wu_ref, wd_ref, *rest):
    o_ref, acc_ref = rest[-2:]
    i, f = pl.program_id(0), pl.program_id(1)

    @pl.when(i < na_ref[0])
    def _():
        @pl.when(f == 0)
        def _():
            acc_ref[...] = jnp.zeros_like(acc_ref)

        h = h_ref[...]
        a = _silu(_dot(h, wg_ref[...].astype(BF16))) * _dot(h, wu_ref[...].astype(BF16))
        acc_ref[...] += _dot(a.astype(BF16), wd_ref[...].astype(BF16))

        @pl.when(f == pl.num_programs(1) - 1)
        def _():
            o_ref[...] = acc_ref[...].astype(o_ref.dtype)


def _moe_ffn(hs, ys, n_rows_total, first_tile, tile_expert, n_active, w_in, w_out):
    n_rows, d = hs.shape
    d_ff = w_out.shape[1]
    tm, tf = MOE_TM, FFN_TF
    n_f = d_ff // tf

    def tile(i, na):
        return jnp.minimum(i, jnp.maximum(na[0] - 1, 0))

    def expert(i, te, na):
        return te[tile(i, na)]

    in_specs = [pl.BlockSpec((tm, d), lambda i, f, te, na: (tile(i, na), 0)),
                pl.BlockSpec((None, d, tf), lambda i, f, te, na: (expert(i, te, na), 0, f)),
                pl.BlockSpec((None, d, tf), lambda i, f, te, na: (expert(i, te, na), 0, n_f + f)),
                pl.BlockSpec((None, tf, d), lambda i, f, te, na: (expert(i, te, na), f, 0))]
    args = [tile_expert, n_active, hs, w_in, w_in, w_out]
    aliases = {}
    if ys is not None:
        in_specs.append(pl.BlockSpec(memory_space=pl.ANY))
        aliases = {len(args): 0}
        args.append(ys)
    return pl.pallas_call(
        _moe_ffn_kernel,
        grid_spec=pltpu.PrefetchScalarGridSpec(
            num_scalar_prefetch=2,
            grid=(n_rows // tm, n_f),
            in_specs=in_specs,
            out_specs=pl.BlockSpec((tm, d), lambda i, f, te, na: (first_tile + tile(i, na), 0)),
            scratch_shapes=[pltpu.VMEM((tm, d), F32)]),
        out_shape=jax.ShapeDtypeStruct((n_rows_total, d), hs.dtype),
        input_output_aliases=aliases,
        compiler_params=_params("arbitrary", "arbitrary"),
        name="moe_ffn",
    )(*args)


def _combine_kernel(x_ref, a_ref, b_ref, route_ref, mod_ref, o_ref):
    w1 = route_ref[:, 2:3]
    w2 = route_ref[:, 3:4]
    mix = w1 * a_ref[...].astype(F32) + w2 * b_ref[...].astype(F32)
    o_ref[...] = x_ref[...] + mod_ref[5:6, :] * mix


def _combine(xs, ya, yb, route, mod, dims):
    b, s_len, n_ctx = dims
    n_rows, d = ya.shape
    tm = _row_tile(s_len, b * n_ctx, 512)
    n_lat_tiles = b * s_len // tm
    tiles_per_sample = s_len // tm

    def mod_idx(i):
        return jnp.where(i < n_lat_tiles, i // tiles_per_sample, b)

    row = pl.BlockSpec((tm, d), lambda i: (i, 0))
    return pl.pallas_call(
        _combine_kernel,
        grid=(n_rows // tm,),
        in_specs=[row, row, row,
                  pl.BlockSpec((tm, LANES), lambda i: (i, 0)),
                  pl.BlockSpec((None, 6, d), lambda i: (mod_idx(i), 0, 0))],
        out_specs=row,
        out_shape=jax.ShapeDtypeStruct((n_rows, d), F32),
        compiler_params=_params("arbitrary"),
        name="moe_combine",
    )(xs, ya, yb, route, mod)


def _route_plan(idx, rank, counts, tm):
    n = idx.shape[0]
    tiles_per = (counts + tm - 1) // tm
    tile_end = jnp.cumsum(tiles_per)
    tile_start = tile_end - tiles_per
    experts = jnp.arange(N_EXPERTS, dtype=jnp.int32)
    start = jnp.sum(jnp.where(idx[:, :, None] == experts, tile_start, 0), axis=-1)
    slot = start * tm + rank
    n_tiles = (2 * n) // tm + N_EXPERTS
    tile_ids = jnp.arange(n_tiles, dtype=jnp.int32)
    tile_expert = jnp.minimum(jnp.sum((tile_end[None, :] <= tile_ids[:, None]).astype(jnp.int32), axis=1),
                              N_EXPERTS - 1)
    token = jnp.arange(2 * n, dtype=jnp.int32) // 2
    _, token_sorted = lax.sort((slot.reshape(-1), token), num_keys=1)
    first_pair = jnp.cumsum(counts) - counts
    row_in_expert = ((tile_ids - tile_start[tile_expert]) * tm)[:, None] + jnp.arange(tm, dtype=jnp.int32)[None, :]
    pair = jnp.clip(first_pair[tile_expert][:, None] + row_in_expert, 0, 2 * n - 1)
    src = jnp.where(row_in_expert < counts[tile_expert][:, None], token_sorted[pair], 0).reshape(-1)
    return slot, src, tile_expert, tile_end[-1:].astype(jnp.int32)


def _take_rows(a, rows):
    return a.at[rows].get(mode="promise_in_bounds")


def _moe(h, xs, route, counts, mod, w_exp_in, w_exp_out, dims):
    idx = route[:, 0:2].astype(jnp.int32)
    rank = route[:, 4:6].astype(jnp.int32)
    slot, src, tile_expert, n_active = _route_plan(idx, rank, counts[0, :N_EXPERTS].astype(jnp.int32), MOE_TM)
    n_tiles = src.shape[0] // MOE_TM
    per_chunk = -(-n_tiles // MOE_CHUNKS)
    ys = None
    for t0 in range(0, n_tiles, per_chunk):
        t1 = min(t0 + per_chunk, n_tiles)
        hs = _take_rows(h, src[t0 * MOE_TM:t1 * MOE_TM])
        ys = _moe_ffn(hs, ys, src.shape[0], t0, tile_expert[t0:t1], jnp.clip(n_active - t0, 0, t1 - t0),
                      w_exp_in, w_exp_out)
    ya = _take_rows(ys, slot[:, 0])
    yb = _take_rows(ys, slot[:, 1])
    return _combine(xs, ya, yb, route, mod, dims)


def _layer(xs, cond, p, dims, mixer, last):
    b, s_len, n_ctx = dims
    mod = _adaln(cond, p["w_mod"], p["b_mod"])
    q, k, v = _qkv(xs, mod, p["norm1"], p["w_qkv"], p["q_norm"], p["k_norm"], dims, mixer)
    if mixer == 0:
        y_lat = _na_attention(q, k, v, p["rel_bias"], dims)
    elif mixer == 1:
        y_lat = _swa_attention(q, k, v, p["sink"], dims)
    else:
        y_lat = _global_attention(q, k, v, dims)
    y_ctx = None if last else _ctx_attention(q, k, v, p.get("sink"), dims, mixer)
    router = (p["w_router"], p["b_router"]) if "w_router" in p else None
    outs = _oproj(y_lat, y_ctx, xs, mod, p["norm2"], p["w_o"], router, dims, last)
    if router is None:
        xs, h = outs
        return _ffn(h, xs, mod, p["w_ffn_in"], p["w_ffn_out"], dims)
    xs, h, route, counts = outs
    return _moe(h, xs, route, counts, mod, p["w_exp_in"], p["w_exp_out"], dims)


def kernel(x, c, ctx, c_ctx, l0_w_mod, l0_b_mod, l0_norm1, l0_norm2, l0_w_qkv, l0_q_norm, l0_k_norm, l0_rel_bias, l0_w_o, l0_w_ffn_in, l0_w_ffn_out, l1_w_mod, l1_b_mod, l1_norm1, l1_norm2, l1_w_qkv, l1_q_norm, l1_k_norm, l1_sink, l1_w_o, l1_w_router, l1_b_router, l1_w_exp_in, l1_w_exp_out, l2_w_mod, l2_b_mod, l2_norm1, l2_norm2, l2_w_qkv, l2_q_norm, l2_k_norm, l2_w_o, l2_w_ffn_in, l2_w_ffn_out, l3_w_mod, l3_b_mod, l3_norm1, l3_norm2, l3_w_qkv, l3_q_norm, l3_k_norm, l3_rel_bias, l3_w_o, l3_w_router, l3_b_router, l3_w_exp_in, l3_w_exp_out):
    b, s_len, d = x.shape
    n_ctx = ctx.shape[1]
    dims = (b, s_len, n_ctx)
    layers = (
        dict(w_mod=l0_w_mod, b_mod=l0_b_mod, norm1=l0_norm1, norm2=l0_norm2, w_qkv=l0_w_qkv, q_norm=l0_q_norm,
             k_norm=l0_k_norm, rel_bias=l0_rel_bias, w_o=l0_w_o, w_ffn_in=l0_w_ffn_in, w_ffn_out=l0_w_ffn_out),
        dict(w_mod=l1_w_mod, b_mod=l1_b_mod, norm1=l1_norm1, norm2=l1_norm2, w_qkv=l1_w_qkv, q_norm=l1_q_norm,
             k_norm=l1_k_norm, sink=l1_sink, w_o=l1_w_o, w_router=l1_w_router, b_router=l1_b_router,
             w_exp_in=l1_w_exp_in, w_exp_out=l1_w_exp_out),
        dict(w_mod=l2_w_mod, b_mod=l2_b_mod, norm1=l2_norm1, norm2=l2_norm2, w_qkv=l2_w_qkv, q_norm=l2_q_norm,
             k_norm=l2_k_norm, w_o=l2_w_o, w_ffn_in=l2_w_ffn_in, w_ffn_out=l2_w_ffn_out),
        dict(w_mod=l3_w_mod, b_mod=l3_b_mod, norm1=l3_norm1, norm2=l3_norm2, w_qkv=l3_w_qkv, q_norm=l3_q_norm,
             k_norm=l3_k_norm, rel_bias=l3_rel_bias, w_o=l3_w_o, w_router=l3_w_router, b_router=l3_b_router,
             w_exp_in=l3_w_exp_in, w_exp_out=l3_w_exp_out),
    )
    xs = jnp.concatenate([x.reshape(b * s_len, d), ctx.reshape(b * n_ctx, d)], axis=0)
    pad_rows = -(b + 1) % 8
    cond = jnp.concatenate([c, c_ctx[None, :], jnp.zeros((pad_rows, d), F32)], axis=0)
    n_layers = len(layers)
    for i, p in enumerate(layers):
        xs = _layer(xs, cond, p, dims, i % 3, i == n_layers - 1)
    return xs.reshape(b, s_len, d)
```

```python
import functools

import jax
import jax.numpy as jnp
from jax import lax
from jax.experimental import pallas as pl
from jax.experimental.pallas import tpu as pltpu

F32 = jnp.float32
BF16 = jnp.bfloat16

GRID_W = 64
NORM_EPS = 1e-6
ROPE_THETA = 10000.0
NA_WIN_ROWS = 8
NA_WIN_COLS = 16
SW_WINDOW = 128
N_EXPERTS = 8
MIXER_HEADS = ((16, 16, 64), (16, 4, 64), (8, 4, 128))

LANES = 128
VMEM_LIMIT_BYTES = 56 * 1024 * 1024
NEG_BIG = -1e30
LOG2E = 1.4426950408889634

QKV_CHUNK = 512
GLB_TQ = 512
GLB_CHUNK = 256
SWA_TQ = 256
NA_CTX_ROWS = 1024
NA_UNROLL = 16
MOE_TM = 1024
MOE_CHUNKS = 4
FFN_TF = 512


def _params(*sem):
    return pltpu.CompilerParams(dimension_semantics=sem, vmem_limit_bytes=VMEM_LIMIT_BYTES)


def _row_tile(n_lat_per_sample, n_ctx_rows, cap):
    for tm in (1024, 512, 256, 128):
        if tm <= cap and n_lat_per_sample % tm == 0 and n_ctx_rows % tm == 0:
            return tm
    raise ValueError("no row tile fits")


def _split_bf16(a):
    hi = a.astype(BF16)
    lo = (a - hi.astype(F32)).astype(BF16)
    return hi, lo


def _dot(a, b):
    return jnp.dot(a, b, preferred_element_type=F32)


def _dot_t(a, b):
    return lax.dot_general(a, b, (((1,), (1,)), ((), ())), preferred_element_type=F32)


def _silu(g):
    return g / (1.0 + jnp.exp(-g))


def _rms_mod(x, gain, scale, shift):
    ms = jnp.mean(x * x, axis=-1, keepdims=True)
    return x * lax.rsqrt(ms + NORM_EPS) * gain * (1.0 + scale) + shift


def _adaln_kernel(c_ref, w_ref, b_ref, o_ref):
    a_hi, a_lo = _split_bf16(_silu(c_ref[...]))
    w_hi, w_lo = _split_bf16(w_ref[...])
    o_ref[...] = _dot(a_hi, w_hi) + _dot(a_hi, w_lo) + _dot(a_lo, w_hi) + b_ref[...]


def _adaln(cond, w_mod, b_mod):
    r, d = cond.shape
    n = w_mod.shape[1]
    tn = 1536
    out = pl.pallas_call(
        _adaln_kernel,
        grid=(n // tn,),
        in_specs=[pl.BlockSpec((r, d), lambda j: (0, 0)),
                  pl.BlockSpec((d, tn), lambda j: (0, j)),
                  pl.BlockSpec((1, tn), lambda j: (0, j))],
        out_specs=pl.BlockSpec((r, tn), lambda j: (0, j)),
        out_shape=jax.ShapeDtypeStruct((r, n), F32),
        compiler_params=_params("arbitrary"),
        name="adaln",
    )(cond, w_mod, b_mod.reshape(1, n))
    return out.reshape(r, 6, d)


def _rot_half(z, dh):
    if dh == LANES:
        return pltpu.roll(z, LANES // 2, axis=1)
    lane = lax.broadcasted_iota(jnp.int32, z.shape, 1)
    from_right = pltpu.roll(z, LANES - dh // 2, axis=1)
    from_left = pltpu.roll(z, dh // 2, axis=1)
    return jnp.where((lane % dh) < dh // 2, from_right, from_left)


def _head_mean_sq(z, dh):
    z2 = z * z
    if dh == LANES:
        return jnp.broadcast_to(jnp.sum(z2, axis=-1, keepdims=True), z.shape) * (1.0 / dh)
    low = lax.broadcasted_iota(jnp.int32, z.shape, 1) < dh
    s_low = jnp.sum(jnp.where(low, z2, 0.0), axis=-1, keepdims=True)
    s_high = jnp.sum(jnp.where(low, 0.0, z2), axis=-1, keepdims=True)
    return jnp.where(low, s_low, s_high) * (1.0 / dh)


def _row_sources(arrays, tm, n_lat_tiles):
    lat, ctx = arrays
    return [pl.BlockSpec((tm, lat.shape[1]), lambda i: (jnp.minimum(i, n_lat_tiles - 1), 0)),
            pl.BlockSpec((tm, ctx.shape[1]), lambda i: (jnp.maximum(i - n_lat_tiles, 0), 0))]


def _pick_rows(refs, n_lat_tiles):
    if len(refs) == 1:
        return refs[0][...]
    return jnp.where(pl.program_id(0) < n_lat_tiles, refs[0][...], refs[1][...])


def _qkv_kernel(*refs, n_q, n_kv, dh, rope, n_x, n_lat_tiles):
    x = _pick_rows(refs[:n_x], n_lat_tiles)
    refs = refs[n_x:]
    if rope:
        mod_ref, n1_ref, w_ref, g_ref, cos_ref, sin_ref, q_ref, k_ref, v_ref = refs
    else:
        mod_ref, n1_ref, w_ref, g_ref, q_ref, k_ref, v_ref = refs
    h = _rms_mod(x, n1_ref[...], mod_ref[1:2, :], mod_ref[0:1, :]).astype(BF16)
    n_qk = n_q + n_kv
    cw = QKV_CHUNK
    for c in range((n_qk + n_kv) // cw):
        y = _dot(h, w_ref[:, c * cw:(c + 1) * cw])
        for s in range(cw // LANES):
            col = c * cw + s * LANES
            z = y[:, s * LANES:(s + 1) * LANES]
            if col < n_qk:
                z = z * lax.rsqrt(_head_mean_sq(z, dh) + NORM_EPS) * g_ref[:, col:col + LANES]
                if rope:
                    z = z * cos_ref[...] + _rot_half(z, dh) * sin_ref[...]
            z = z.astype(BF16)
            if col < n_q:
                q_ref[:, col:col + LANES] = z
            elif col < n_qk:
                k_ref[:, col - n_q:col - n_q + LANES] = z
            else:
                v_ref[:, col - n_qk:col - n_qk + LANES] = z


def _rope_tables(s_len, dh, tm):
    n_freq = dh // 4
    inv_freq = ROPE_THETA ** (-jnp.arange(n_freq, dtype=F32) / n_freq)
    t = jnp.arange(s_len)
    row = (t // GRID_W).astype(F32)
    col = (t % GRID_W).astype(F32)
    ang = jnp.concatenate([row[:, None] * inv_freq, col[:, None] * inv_freq], axis=-1)
    cos, sin = jnp.cos(ang), jnp.sin(ang)
    reps = LANES // dh
    cos_t = jnp.tile(jnp.concatenate([cos, cos], axis=-1), (1, reps))
    sin_t = jnp.tile(jnp.concatenate([-sin, sin], axis=-1), (1, reps))
    cos_t = jnp.concatenate([cos_t, jnp.ones((tm, LANES), F32)], axis=0)
    sin_t = jnp.concatenate([sin_t, jnp.zeros((tm, LANES), F32)], axis=0)
    return cos_t, sin_t


def _qkv(xs, mod, norm1, w_qkv, q_gain, k_gain, dims, mixer):
    b, s_len, n_ctx = dims
    x_parts = list(xs) if isinstance(xs, tuple) else [xs]
    n_rows, d = sum(a.shape[0] for a in x_parts), x_parts[0].shape[1]
    n_qh, n_kvh, dh = MIXER_HEADS[mixer]
    n_q, n_kv = n_qh * dh, n_kvh * dh
    rope = mixer != 0
    tm = _row_tile(s_len, b * n_ctx, 512)
    n_lat_tiles = b * s_len // tm
    tiles_per_sample = s_len // tm

    gains = jnp.concatenate([jnp.tile(q_gain * (dh ** -0.5 * LOG2E), n_qh),
                             jnp.tile(k_gain, n_kvh)]).reshape(1, n_q + n_kv)

    def mod_idx(i):
        return jnp.where(i < n_lat_tiles, i // tiles_per_sample, b)

    x_specs = [pl.BlockSpec((tm, d), lambda i: (i, 0))] if len(x_parts) == 1 else _row_sources(x_parts, tm, n_lat_tiles)
    in_specs = x_specs + [pl.BlockSpec((None, 6, d), lambda i: (mod_idx(i), 0, 0)),
                          pl.BlockSpec((1, d), lambda i: (0, 0)),
                          pl.BlockSpec(w_qkv.shape, lambda i: (0, 0)),
                          pl.BlockSpec(gains.shape, lambda i: (0, 0))]
    args = x_parts + [mod, norm1.reshape(1, d), w_qkv.astype(BF16), gains]
    if rope:
        cos_t, sin_t = _rope_tables(s_len, dh, tm)

        def pos_idx(i):
            return jnp.where(i < n_lat_tiles, i % tiles_per_sample, tiles_per_sample)

        in_specs += [pl.BlockSpec((tm, LANES), lambda i: (pos_idx(i), 0))] * 2
        args += [cos_t, sin_t]
    return pl.pallas_call(
        functools.partial(_qkv_kernel, n_q=n_q, n_kv=n_kv, dh=dh, rope=rope, n_x=len(x_parts),
                          n_lat_tiles=n_lat_tiles),
        grid=(n_rows // tm,),
        in_specs=in_specs,
        out_specs=[pl.BlockSpec((tm, n_q), lambda i: (i, 0)),
                   pl.BlockSpec((tm, n_kv), lambda i: (i, 0)),
                   pl.BlockSpec((tm, n_kv), lambda i: (i, 0))],
        out_shape=[jax.ShapeDtypeStruct((n_rows, n_q), BF16),
                   jax.ShapeDtypeStruct((n_rows, n_kv), BF16),
                   jax.ShapeDtypeStruct((n_rows, n_kv), BF16)],
        compiler_params=_params("arbitrary"),
        name="qkv_m%d" % mixer,
    )(*args)


def _stack_heads(q, dh, nh):
    if dh == LANES:
        return jnp.concatenate([q[:, h * LANES:(h + 1) * LANES] for h in range(nh)], axis=0)
    per_kv = nh // 2
    tq = q.shape[0]
    lane = lax.broadcasted_iota(jnp.int32, (tq, LANES), 1)
    ops = []
    for h in range(nh):
        slot = h // per_kv
        chunk = q[:, (h // 2) * LANES:(h // 2 + 1) * LANES].astype(F32)
        if h % 2 != slot:
            chunk = pltpu.roll(chunk, dh, axis=1)
        keep = lane < dh if slot == 0 else lane >= dh
        ops.append(jnp.where(keep, chunk, 0.0).astype(BF16))
    return jnp.concatenate(ops, axis=0)


def _unstack_heads(o, dh, nh):
    tq = o.shape[0] // nh
    if dh == LANES:
        return jnp.concatenate([o[h * tq:(h + 1) * tq] for h in range(nh)], axis=1)
    per_kv = nh // 2
    lane = lax.broadcasted_iota(jnp.int32, (tq, LANES), 1)
    chunks = []
    for c in range(nh // 2):
        parts = []
        for h in (2 * c, 2 * c + 1):
            oh = o[h * tq:(h + 1) * tq]
            if h % 2 != h // per_kv:
                oh = pltpu.roll(oh, dh, axis=1)
            parts.append(oh)
        chunks.append(jnp.where(lane < dh, parts[0], parts[1]))
    return jnp.concatenate(chunks, axis=1)


def _lanes(x, n):
    return x if n == LANES else jnp.concatenate([x] * (n // LANES), axis=1)


def _block_max(s):
    return functools.reduce(jnp.maximum, [s[:, j * LANES:(j + 1) * LANES] for j in range(s.shape[1] // LANES)])


def _row_max(blk):
    return jnp.broadcast_to(jnp.max(blk, axis=-1, keepdims=True), blk.shape)


def _dot_row_halves(p, v):
    half = p.shape[0] // 2
    return jnp.concatenate([_dot(p[0:half, :], v), _dot(p[half:, :], v)], axis=0)


def _ones_ext(v):
    return jnp.concatenate([v, jnp.ones_like(v)], axis=1)


def _sink_rows(sink_ref, first_head, nh, tq):
    return jnp.concatenate([jnp.full((tq, LANES), sink_ref[first_head + h] * LOG2E, F32) for h in range(nh)], axis=0)


def _na_kernel(q_ref, k_ref, v_ref, kc_ref, vc_ref, bias_ref, o_ref,
               qs_ref, vext_ref, mc_ref, numc_ref, lc_ref, *, rows):
    s_len = q_ref.shape[0]
    band = NA_WIN_ROWS * GRID_W
    lane = lax.broadcasted_iota(jnp.int32, (s_len, LANES), 1)
    q = q_ref[...]
    qs_ref[0:s_len, :] = jnp.where(lane < 64, q, jnp.zeros_like(q))
    qs_ref[s_len:, :] = jnp.where(lane >= 64, q, jnp.zeros_like(q))
    vext_ref[:, 0:LANES] = v_ref[...]
    vext_ref[:, LANES:] = jnp.ones((s_len, LANES), BF16)

    kc = kc_ref[...]
    vcx = _ones_ext(vc_ref[...])
    for c in range(2 * s_len // NA_CTX_ROWS):
        rs = slice(c * NA_CTX_ROWS, (c + 1) * NA_CTX_ROWS)
        s = _dot_t(qs_ref[rs, :], kc)
        m = _row_max(_block_max(s))
        acc = _dot(jnp.exp2(s - _lanes(m, s.shape[1])).astype(BF16), vcx)
        mc_ref[rs, :] = m
        numc_ref[rs, :] = acc[:, :LANES]
        lc_ref[rs, :] = acc[:, LANES:]

    lane_q = lax.broadcasted_iota(jnp.int32, (GRID_W, LANES), 1)

    def both_heads(ref, q0):
        return jnp.concatenate([ref[pl.ds(q0, GRID_W), :], ref[pl.ds(s_len + q0, GRID_W), :]], axis=0)

    def one_row(r, carry):
        r0 = jnp.clip(r - NA_WIN_ROWS // 2, 0, rows - NA_WIN_ROWS)
        q0 = pl.multiple_of(r * GRID_W, GRID_W)
        k0 = pl.multiple_of(r0 * GRID_W, GRID_W)
        d0 = r0 - r + NA_WIN_ROWS - 1
        bias = jnp.concatenate(
            [jnp.concatenate([bias_ref[0, d0 + 2 * j], bias_ref[1, d0 + 2 * j]], axis=0)
             for j in range(NA_WIN_ROWS // 2)], axis=1)
        qs = both_heads(qs_ref, q0)
        half = band // 2
        k1 = pl.multiple_of(k0 + half, GRID_W)
        s = jnp.concatenate([_dot_t(qs, k_ref[pl.ds(k0, half), :]), _dot_t(qs, k_ref[pl.ds(k1, half), :])],
                            axis=1) + bias
        mc = both_heads(mc_ref, q0)
        m = jnp.maximum(_row_max(_block_max(s)), mc)
        acc = _dot(jnp.exp2(s - _lanes(m, band)).astype(BF16), vext_ref[pl.ds(k0, band), :])
        alpha = jnp.exp2(mc - m)
        num = acc[:, :LANES] + alpha * both_heads(numc_ref, q0)
        den = acc[:, LANES:] + alpha * both_heads(lc_ref, q0)
        o = num / den
        o_ref[pl.ds(q0, GRID_W), :] = jnp.where(lane_q < 64, o[:GRID_W], o[GRID_W:]).astype(BF16)
        return carry

    lax.fori_loop(0, rows, one_row, 0, unroll=NA_UNROLL)


def _na_bias_tables(rel_bias):
    n_h = rel_bias.shape[0]
    col = jnp.arange(GRID_W)
    col_start = jnp.clip(col - NA_WIN_COLS // 2, 0, GRID_W - NA_WIN_COLS)
    in_win = (col[None, :] >= col_start[:, None]) & (col[None, :] < col_start[:, None] + NA_WIN_COLS)
    col_idx = jnp.clip(col[None, :] - col[:, None] + NA_WIN_COLS - 1, 0, 2 * NA_WIN_COLS - 2)
    masked = jnp.where(in_win[None, None], rel_bias[:, :, col_idx] * LOG2E, NEG_BIG)
    pairs = jnp.concatenate([masked[:, :-1], masked[:, 1:]], axis=-1)
    return pairs.reshape((n_h // 2, 2) + pairs.shape[1:])


def _na_attention(q, k, v, rel_bias, dims):
    b, s_len, n_ctx = dims
    rows = s_len // GRID_W
    n_pairs = q.shape[1] // LANES
    bias = _na_bias_tables(rel_bias)
    ctx_blk = b * s_len // n_ctx
    lat = pl.BlockSpec((s_len, LANES), lambda p, i: (i, p))
    ctx = pl.BlockSpec((n_ctx, LANES), lambda p, i: (ctx_blk + i, p))
    return pl.pallas_call(
        functools.partial(_na_kernel, rows=rows),
        grid=(n_pairs, b),
        in_specs=[lat, lat, lat, ctx, ctx,
                  pl.BlockSpec((None,) + bias.shape[1:], lambda p, i: (p, 0, 0, 0, 0))],
        out_specs=lat,
        out_shape=jax.ShapeDtypeStruct((b * s_len, q.shape[1]), BF16),
        scratch_shapes=[pltpu.VMEM((2 * s_len, LANES), BF16),
                        pltpu.VMEM((s_len, 2 * LANES), BF16),
                        pltpu.VMEM((2 * s_len, LANES), F32),
                        pltpu.VMEM((2 * s_len, LANES), F32),
                        pltpu.VMEM((2 * s_len, LANES), F32)],
        compiler_params=_params("arbitrary", "arbitrary"),
        name="attn_na",
    )(q, k, v, k, v, bias)


def _swa_kernel(sink_ref, q_ref, k_ref, v_ref, kc_ref, vc_ref, o_ref, kwin_ref, vwin_ref, *, nh, n_units, s_len):
    tq = q_ref.shape[0]
    span = tq + 2 * SW_WINDOW
    p_blk, t = pl.program_id(1), pl.program_id(2)

    @pl.when(t == 0)
    def _():
        kwin_ref[span:, :] = kc_ref[...]
        vwin_ref[span:, 0:LANES] = vc_ref[...]
        vwin_ref[:, LANES:] = jnp.ones((vwin_ref.shape[0], LANES), BF16)

    q0 = t * tq
    k0 = pl.multiple_of(jnp.clip(q0 - SW_WINDOW, 0, s_len - span), SW_WINDOW)
    kwin_ref[0:span, :] = k_ref[pl.ds(k0, span), :]
    vwin_ref[0:span, 0:LANES] = v_ref[pl.ds(k0, span), :]

    row = lax.broadcasted_iota(jnp.int32, (tq, span), 0)
    col = lax.broadcasted_iota(jnp.int32, (tq, span), 1)
    bias = jnp.where(jnp.abs(col - row + (k0 - q0)) <= SW_WINDOW, 0.0, NEG_BIG)

    qs_all = _stack_heads(q_ref[...], 64, nh)
    sink_all = _sink_rows(sink_ref, p_blk * nh, nh, tq)
    hu = nh // n_units
    vwin = vwin_ref[...]
    kd = 2 * LANES
    outs = []
    for u in range(n_units):
        qs = qs_all[u * hu * tq:(u + 1) * hu * tq]
        sink = sink_all[u * hu * tq:(u + 1) * hu * tq]
        s = jnp.concatenate([_dot_t(qs, kwin_ref[j * kd:(j + 1) * kd, :]) for j in range(kwin_ref.shape[0] // kd)],
                            axis=1)
        s = jnp.concatenate([(s[:, :span].reshape(hu, tq, span) + bias[None]).reshape(hu * tq, span),
                             s[:, span:]], axis=1)
        m = jnp.maximum(_row_max(_block_max(s)), sink)
        acc = _dot(jnp.exp2(s - _lanes(m, s.shape[1])).astype(BF16), vwin)
        outs.append(acc[:, :LANES] / (acc[:, LANES:] + jnp.exp2(sink - m)))
    o_ref[...] = _unstack_heads(jnp.concatenate(outs, axis=0), 64, nh).astype(BF16)


def _swa_attention(q, k, v, sink, dims):
    b, s_len, n_ctx = dims
    tq = SWA_TQ
    n_t = s_len // tq
    n_kvblk = k.shape[1] // LANES
    nh = q.shape[1] // k.shape[1] * 2
    qw = nh * 64
    ctx_blk = b * s_len // n_ctx
    n_win = tq + 2 * SW_WINDOW + n_ctx
    assert n_win % (2 * LANES) == 0
    qspec = pl.BlockSpec((tq, qw), lambda i, p, t, *_: (i * n_t + t, p))
    lat = pl.BlockSpec((s_len, LANES), lambda i, p, t, *_: (i, p))
    ctx = pl.BlockSpec((n_ctx, LANES), lambda i, p, t, *_: (ctx_blk + i, p))
    return pl.pallas_call(
        functools.partial(_swa_kernel, nh=nh, n_units=nh, s_len=s_len),
        grid_spec=pltpu.PrefetchScalarGridSpec(
            num_scalar_prefetch=1,
            grid=(b, n_kvblk, n_t),
            in_specs=[qspec, lat, lat, ctx, ctx],
            out_specs=qspec,
            scratch_shapes=[pltpu.VMEM((n_win, LANES), BF16),
                            pltpu.VMEM((n_win, 2 * LANES), BF16)]),
        out_shape=jax.ShapeDtypeStruct((b * s_len, q.shape[1]), BF16),
        compiler_params=_params("arbitrary", "arbitrary", "arbitrary"),
        name="attn_swa",
    )(sink, q, k, v, k, v)


def _global_kernel(q_ref, k_ref, v_ref, kc_ref, vc_ref, o_ref, kall_ref, vext_ref, s_ref, p_ref, *, nh, s_len):
    @pl.when(pl.program_id(2) == 0)
    def _():
        kall_ref[0:s_len, :] = k_ref[...]
        kall_ref[s_len:, :] = kc_ref[...]
        vext_ref[0:s_len, 0:LANES] = v_ref[...]
        vext_ref[s_len:, 0:LANES] = vc_ref[...]
        vext_ref[:, LANES:] = jnp.ones((vext_ref.shape[0], LANES), BF16)

    qs = _stack_heads(q_ref[...], LANES, nh)
    n_chunks = kall_ref.shape[0] // GLB_CHUNK
    mrun = None
    for c in range(n_chunks):
        ks = slice(c * GLB_CHUNK, (c + 1) * GLB_CHUNK)
        s = _dot_t(qs, kall_ref[ks, :])
        s_ref[:, ks] = s
        blk = _block_max(s)
        mrun = blk if mrun is None else jnp.maximum(mrun, blk)
    m = _lanes(_row_max(mrun), GLB_CHUNK)
    for c in range(n_chunks):
        ks = slice(c * GLB_CHUNK, (c + 1) * GLB_CHUNK)
        p_ref[:, ks] = jnp.exp2(s_ref[:, ks] - m).astype(BF16)
    acc = _dot_row_halves(p_ref, vext_ref[...])
    o_ref[...] = _unstack_heads(acc[:, :LANES] / acc[:, LANES:], LANES, nh).astype(BF16)


def _global_attention(q, k, v, dims):
    b, s_len, n_ctx = dims
    tq = GLB_TQ
    n_t = s_len // tq
    n_kvblk = k.shape[1] // LANES
    nh = q.shape[1] // k.shape[1]
    qw = nh * LANES
    ctx_blk = b * s_len // n_ctx
    n_keys = s_len + n_ctx
    qspec = pl.BlockSpec((tq, qw), lambda i, p, t: (i * n_t + t, p))
    lat = pl.BlockSpec((s_len, LANES), lambda i, p, t: (i, p))
    ctx = pl.BlockSpec((n_ctx, LANES), lambda i, p, t: (ctx_blk + i, p))
    return pl.pallas_call(
        functools.partial(_global_kernel, nh=nh, s_len=s_len),
        grid=(b, n_kvblk, n_t),
        in_specs=[qspec, lat, lat, ctx, ctx],
        out_specs=qspec,
        out_shape=jax.ShapeDtypeStruct((b * s_len, q.shape[1]), BF16),
        scratch_shapes=[pltpu.VMEM((n_keys, LANES), BF16),
                        pltpu.VMEM((n_keys, 2 * LANES), BF16),
                        pltpu.VMEM((nh * tq, n_keys), F32),
                        pltpu.VMEM((nh * tq, n_keys), BF16)],
        compiler_params=_params("arbitrary", "arbitrary", "arbitrary"),
        name="attn_global",
    )(q, k, v, k, v)


def _ctx_kernel(sink_ref, q_ref, kc_ref, vc_ref, o_ref, *, dh, nh, use_sink):
    tq = q_ref.shape[0]
    qs = _stack_heads(q_ref[...], dh, nh)
    s = _dot_t(qs, kc_ref[...])
    m = _row_max(_block_max(s))
    if use_sink:
        sink = _sink_rows(sink_ref, pl.program_id(1) * nh, nh, tq)
        m = jnp.maximum(m, sink)
    acc = _dot(jnp.exp2(s - _lanes(m, s.shape[1])).astype(BF16), _ones_ext(vc_ref[...]))
    den = acc[:, LANES:]
    if use_sink:
        den = den + jnp.exp2(sink - m)
    o_ref[...] = _unstack_heads(acc[:, :LANES] / den, dh, nh).astype(BF16)


def _ctx_attention(q, k, v, sink, dims, mixer):
    b, s_len, n_ctx = dims
    n_qh, n_kvh, dh = MIXER_HEADS[mixer]
    n_kvblk = k.shape[1] // LANES
    nh = n_qh // n_kvblk
    qw = nh * dh
    ctx_blk = b * s_len // n_ctx
    use_sink = sink is not None
    if not use_sink:
        sink = jnp.zeros((n_qh,), F32)
    return pl.pallas_call(
        functools.partial(_ctx_kernel, dh=dh, nh=nh, use_sink=use_sink),
        grid_spec=pltpu.PrefetchScalarGridSpec(
            num_scalar_prefetch=1,
            grid=(b, n_kvblk),
            in_specs=[pl.BlockSpec((n_ctx, qw), lambda i, p, *_: (ctx_blk + i, p)),
                      pl.BlockSpec((n_ctx, LANES), lambda i, p, *_: (ctx_blk + i, p)),
                      pl.BlockSpec((n_ctx, LANES), lambda i, p, *_: (ctx_blk + i, p))],
            out_specs=pl.BlockSpec((n_ctx, qw), lambda i, p, *_: (i, p))),
        out_shape=jax.ShapeDtypeStruct((b * n_ctx, q.shape[1]), BF16),
        compiler_params=_params("arbitrary", "arbitrary"),
        name="attn_ctx_m%d" % mixer,
    )(sink, q, k, v)


def _oproj_kernel(*refs, n_lat_tiles, n_y, n_x, moe):
    y = _pick_rows(refs[:n_y], n_lat_tiles)
    x_in = _pick_rows(refs[n_y:n_y + n_x], n_lat_tiles)
    refs = refs[n_y + n_x:]
    mod_ref, n2_ref, wo_ref = refs[:3]
    refs = refs[3:]
    if moe:
        wr_ref, br_ref = refs[:2]
        refs = refs[2:]
    xo_ref, h_ref = refs[:2]
    x = x_in + mod_ref[2:3, :] * _dot(y, wo_ref[...])
    xo_ref[...] = x
    h = _rms_mod(x, n2_ref[...], mod_ref[4:5, :], mod_ref[3:4, :])
    h_ref[...] = h.astype(BF16)
    if moe:
        route_ref, counts_ref, base_ref = refs[2:5]
        h_hi, h_lo = _split_bf16(h)
        w_hi, w_lo = _split_bf16(wr_ref[...])
        logits = _dot(h_hi, w_hi) + _dot(h_hi, w_lo) + _dot(h_lo, w_hi) + br_ref[...]
        lane = lax.broadcasted_iota(jnp.int32, logits.shape, 1)
        v1 = jnp.max(logits, axis=-1, keepdims=True)
        i1 = jnp.min(jnp.where(logits == v1, lane, LANES), axis=-1, keepdims=True)
        rest = jnp.where(lane == i1, NEG_BIG, logits)
        v2 = jnp.max(rest, axis=-1, keepdims=True)
        i2 = jnp.min(jnp.where(rest == v2, lane, LANES), axis=-1, keepdims=True)
        e = jnp.exp(v2 - v1)
        w1 = 1.0 / (1.0 + e)
        w2 = e / (1.0 + e)
        @pl.when(pl.program_id(0) == 0)
        def _():
            base_ref[...] = jnp.zeros_like(base_ref)

        tm = logits.shape[0]
        pick1 = lane == i1
        pick2 = lane == i2
        cnt = jnp.where(pick1, 1.0, 0.0) + jnp.where(pick2, 1.0, 0.0)
        earlier = lax.broadcasted_iota(jnp.int32, (tm, tm), 0) > lax.broadcasted_iota(jnp.int32, (tm, tm), 1)
        before = _dot(jnp.where(earlier, 1.0, 0.0).astype(BF16), cnt.astype(BF16)) + base_ref[0:1, :]
        r1 = jnp.sum(jnp.where(pick1, before, 0.0), axis=-1, keepdims=True)
        r2 = jnp.sum(jnp.where(pick2, before, 0.0), axis=-1, keepdims=True)
        base_ref[...] = base_ref[...] + jnp.sum(cnt, axis=0, keepdims=True)
        counts_ref[...] = base_ref[...]
        route = jnp.where(lane == 0, i1.astype(F32), 0.0)
        route = jnp.where(lane == 1, i2.astype(F32), route)
        route = jnp.where(lane == 2, w1, route)
        route = jnp.where(lane == 3, w2, route)
        route = jnp.where(lane == 4, r1, route)
        route = jnp.where(lane == 5, r2, route)
        route_ref[...] = route


def _oproj(y_lat, y_ctx, xs, mod, norm2, w_o, router, dims, last):
    b, s_len, n_ctx = dims
    x_parts = list(xs) if isinstance(xs, tuple) else [xs]
    d = x_parts[0].shape[1]
    tm = _row_tile(s_len, b * n_ctx, 512)
    n_lat_tiles = b * s_len // tm
    tiles_per_sample = s_len // tm
    n_rows = b * s_len if last else b * (s_len + n_ctx)
    n_tiles = n_rows // tm
    moe = router is not None

    def mod_idx(i):
        return jnp.where(i < n_lat_tiles, i // tiles_per_sample, b)

    row = pl.BlockSpec((tm, d), lambda i: (i, 0))
    y_parts = [y_lat] if last else [y_lat, y_ctx]
    in_specs = ([pl.BlockSpec((tm, y_lat.shape[1]), lambda i: (i, 0))] if last
                else _row_sources(y_parts, tm, n_lat_tiles))
    in_specs += [row] if len(x_parts) == 1 else _row_sources(x_parts, tm, n_lat_tiles)
    in_specs += [pl.BlockSpec((None, 6, d), lambda i: (mod_idx(i), 0, 0)),
                 pl.BlockSpec((1, d), lambda i: (0, 0)),
                 pl.BlockSpec(w_o.shape, lambda i: (0, 0))]
    args = y_parts + x_parts + [mod, norm2.reshape(1, d), w_o.astype(BF16)]
    out_specs = [row, row]
    out_shape = [jax.ShapeDtypeStruct((n_rows, d), F32), jax.ShapeDtypeStruct((n_rows, d), BF16)]
    if moe:
        w_router, b_router = router
        n_e = w_router.shape[1]
        wr = jnp.pad(w_router, ((0, 0), (0, LANES - n_e)))
        br = jnp.pad(b_router, (0, LANES - n_e), constant_values=NEG_BIG).reshape(1, LANES)
        in_specs += [pl.BlockSpec(wr.shape, lambda i: (0, 0)), pl.BlockSpec(br.shape, lambda i: (0, 0))]
        args += [wr, br]
        out_specs += [pl.BlockSpec((tm, LANES), lambda i: (i, 0)), pl.BlockSpec((8, LANES), lambda i: (0, 0))]
        out_shape += [jax.ShapeDtypeStruct((n_rows, LANES), F32), jax.ShapeDtypeStruct((8, LANES), F32)]
    return pl.pallas_call(
        functools.partial(_oproj_kernel, n_lat_tiles=n_lat_tiles, n_y=len(y_parts), n_x=len(x_parts), moe=moe),
        grid=(n_tiles,),
        in_specs=in_specs,
        out_specs=out_specs,
        out_shape=out_shape,
        scratch_shapes=[pltpu.VMEM((8, LANES), F32)] if moe else [],
        compiler_params=_params("arbitrary"),
        name="oproj_moe" if moe else "oproj",
    )(*args)


def _ffn_kernel(h_ref, wg_ref, wu_ref, wd_ref, x_ref, mod_ref, o_ref, acc_ref):
    f = pl.program_id(1)

    @pl.when(f == 0)
    def _():
        acc_ref[...] = jnp.zeros_like(acc_ref)

    h = h_ref[...]
    a = _silu(_dot(h, wg_ref[...])) * _dot(h, wu_ref[...])
    acc_ref[...] += _dot(a.astype(BF16), wd_ref[...])

    @pl.when(f == pl.num_programs(1) - 1)
    def _():
        o_ref[...] = x_ref[...] + mod_ref[5:6, :] * acc_ref[...]


def _ffn(h, xs, mod, w_in, w_out, dims):
    b, s_len, n_ctx = dims
    n_rows, d = h.shape
    d_ff = w_out.shape[0]
    tm = _row_tile(s_len, b * n_ctx, 1024)
    tf = FFN_TF
    n_f = d_ff // tf
    n_lat_tiles = b * s_len // tm
    tiles_per_sample = s_len // tm

    def mod_idx(i):
        return jnp.where(i < n_lat_tiles, i // tiles_per_sample, b)

    w_in = w_in.astype(BF16)
    return pl.pallas_call(
        _ffn_kernel,
        grid=(n_rows // tm, n_f),
        in_specs=[pl.BlockSpec((tm, d), lambda i, f: (i, 0)),
                  pl.BlockSpec((d, tf), lambda i, f: (0, f)),
                  pl.BlockSpec((d, tf), lambda i, f: (0, n_f + f)),
                  pl.BlockSpec((tf, d), lambda i, f: (f, 0)),
                  pl.BlockSpec((tm, d), lambda i, f: (i, 0)),
                  pl.BlockSpec((None, 6, d), lambda i, f: (mod_idx(i), 0, 0))],
        out_specs=pl.BlockSpec((tm, d), lambda i, f: (i, 0)),
        out_shape=jax.ShapeDtypeStruct((n_rows, d), F32),
        scratch_shapes=[pltpu.VMEM((tm, d), F32)],
        compiler_params=_params("arbitrary", "arbitrary"),
        name="ffn_dense",
    )(h, w_in, w_in, w_out.astype(BF16), xs, mod)


def _moe_ffn_kernel(te_ref, na_ref, h_ref, wg_ref, wu_ref, wd_ref, *rest):
    o_ref, acc_ref = rest[-2:]
    i, f = pl.program_id(0), pl.program_id(1)

    @pl.when(i < na_ref[0])
    def _():
        @pl.when(f == 0)
        def _():
            acc_ref[...] = jnp.zeros_like(acc_ref)

        h = h_ref[...]
        a = _silu(_dot(h, wg_ref[...].astype(BF16))) * _dot(h, wu_ref[...].astype(BF16))
        acc_ref[...] += _dot(a.astype(BF16), wd_ref[...].astype(BF16))

        @pl.when(f == pl.num_programs(1) - 1)
        def _():
            o_ref[...] = acc_ref[...].astype(o_ref.dtype)


def _moe_ffn(hs, ys, n_rows_total, first_tile, tile_expert, n_active, w_in, w_out):
    n_rows, d = hs.shape
    d_ff = w_out.shape[1]
    tm, tf = MOE_TM, FFN_TF
    n_f = d_ff // tf

    def tile(i, na):
        return jnp.minimum(i, jnp.maximum(na[0] - 1, 0))

    def expert(i, te, na):
        return te[tile(i, na)]

    in_specs = [pl.BlockSpec((tm, d), lambda i, f, te, na: (tile(i, na), 0)),
                pl.BlockSpec((None, d, tf), lambda i, f, te, na: (expert(i, te, na), 0, f)),
                pl.BlockSpec((None, d, tf), lambda i, f, te, na: (expert(i, te, na), 0, n_f + f)),
                pl.BlockSpec((None, tf, d), lambda i, f, te, na: (expert(i, te, na), f, 0))]
    args = [tile_expert, n_active, hs, w_in, w_in, w_out]
    aliases = {}
    if ys is not None:
        in_specs.append(pl.BlockSpec(memory_space=pl.ANY))
        aliases = {len(args): 0}
        args.append(ys)
    return pl.pallas_call(
        _moe_ffn_kernel,
        grid_spec=pltpu.PrefetchScalarGridSpec(
            num_scalar_prefetch=2,
            grid=(n_rows // tm, n_f),
            in_specs=in_specs,
            out_specs=pl.BlockSpec((tm, d), lambda i, f, te, na: (first_tile + tile(i, na), 0)),
            scratch_shapes=[pltpu.VMEM((tm, d), F32)]),
        out_shape=jax.ShapeDtypeStruct((n_rows_total, d), hs.dtype),
        input_output_aliases=aliases,
        compiler_params=_params("arbitrary", "arbitrary"),
        name="moe_ffn",
    )(*args)


def _combine_kernel(x_ref, a_ref, b_ref, route_ref, mod_ref, o_ref):
    w1 = route_ref[:, 2:3]
    w2 = route_ref[:, 3:4]
    mix = w1 * a_ref[...].astype(F32) + w2 * b_ref[...].astype(F32)
    o_ref[...] = x_ref[...] + mod_ref[5:6, :] * mix


def _combine(xs, ya, yb, route, mod, dims):
    b, s_len, n_ctx = dims
    n_rows, d = ya.shape
    tm = _row_tile(s_len, b * n_ctx, 512)
    n_lat_tiles = b * s_len // tm
    tiles_per_sample = s_len // tm

    def mod_idx(i):
        return jnp.where(i < n_lat_tiles, i // tiles_per_sample, b)

    row = pl.BlockSpec((tm, d), lambda i: (i, 0))
    return pl.pallas_call(
        _combine_kernel,
        grid=(n_rows // tm,),
        in_specs=[row, row, row,
                  pl.BlockSpec((tm, LANES), lambda i: (i, 0)),
                  pl.BlockSpec((None, 6, d), lambda i: (mod_idx(i), 0, 0))],
        out_specs=row,
        out_shape=jax.ShapeDtypeStruct((n_rows, d), F32),
        compiler_params=_params("arbitrary"),
        name="moe_combine",
    )(xs, ya, yb, route, mod)


def _route_plan(idx, rank, counts, tm):
    n = idx.shape[0]
    tiles_per = (counts + tm - 1) // tm
    tile_end = jnp.cumsum(tiles_per)
    tile_start = tile_end - tiles_per
    experts = jnp.arange(N_EXPERTS, dtype=jnp.int32)
    start = jnp.sum(jnp.where(idx[:, :, None] == experts, tile_start, 0), axis=-1)
    slot = start * tm + rank
    n_tiles = (2 * n) // tm + N_EXPERTS
    tile_ids = jnp.arange(n_tiles, dtype=jnp.int32)
    tile_expert = jnp.minimum(jnp.sum((tile_end[None, :] <= tile_ids[:, None]).astype(jnp.int32), axis=1),
                              N_EXPERTS - 1)
    token = jnp.arange(2 * n, dtype=jnp.int32) // 2
    _, token_sorted = lax.sort((slot.reshape(-1), token), num_keys=1)
    first_pair = jnp.cumsum(counts) - counts
    row_in_expert = ((tile_ids - tile_start[tile_expert]) * tm)[:, None] + jnp.arange(tm, dtype=jnp.int32)[None, :]
    pair = jnp.clip(first_pair[tile_expert][:, None] + row_in_expert, 0, 2 * n - 1)
    src = jnp.where(row_in_expert < counts[tile_expert][:, None], token_sorted[pair], 0).reshape(-1)
    return slot, src, tile_expert, tile_end[-1:].astype(jnp.int32)


def _take_rows(a, rows):
    return a.at[rows].get(mode="promise_in_bounds")


def _moe(h, xs, route, counts, mod, w_exp_in, w_exp_out, dims):
    idx = route[:, 0:2].astype(jnp.int32)
    rank = route[:, 4:6].astype(jnp.int32)
    slot, src, tile_expert, n_active = _route_plan(idx, rank, counts[0, :N_EXPERTS].astype(jnp.int32), MOE_TM)
    n_tiles = src.shape[0] // MOE_TM
    per_chunk = -(-n_tiles // MOE_CHUNKS)
    ys = None
    for t0 in range(0, n_tiles, per_chunk):
        t1 = min(t0 + per_chunk, n_tiles)
        hs = _take_rows(h, src[t0 * MOE_TM:t1 * MOE_TM])
        ys = _moe_ffn(hs, ys, src.shape[0], t0, tile_expert[t0:t1], jnp.clip(n_active - t0, 0, t1 - t0),
                      w_exp_in, w_exp_out)
    ya = _take_rows(ys, slot[:, 0])
    yb = _take_rows(ys, slot[:, 1])
    return _combine(xs, ya, yb, route, mod, dims)


def _layer(xs, cond, p, dims, mixer, last):
    b, s_len, n_ctx = dims
    mod = _adaln(cond, p["w_mod"], p["b_mod"])
    q, k, v = _qkv(xs, mod, p["norm1"], p["w_qkv"], p["q_norm"], p["k_norm"], dims, mixer)
    if mixer == 0:
        y_lat = _na_attention(q, k, v, p["rel_bias"], dims)
    elif mixer == 1:
        y_lat = _swa_attention(q, k, v, p["sink"], dims)
    else:
        y_lat = _global_attention(q, k, v, dims)
    y_ctx = None if last else _ctx_attention(q, k, v, p.get("sink"), dims, mixer)
    router = (p["w_router"], p["b_router"]) if "w_router" in p else None
    outs = _oproj(y_lat, y_ctx, xs, mod, p["norm2"], p["w_o"], router, dims, last)
    if router is None:
        xs, h = outs
        return _ffn(h, xs, mod, p["w_ffn_in"], p["w_ffn_out"], dims)
    xs, h, route, counts = outs
    return _moe(h, xs, route, counts, mod, p["w_exp_in"], p["w_exp_out"], dims)


def kernel(x, c, ctx, c_ctx, l0_w_mod, l0_b_mod, l0_norm1, l0_norm2, l0_w_qkv, l0_q_norm, l0_k_norm, l0_rel_bias, l0_w_o, l0_w_ffn_in, l0_w_ffn_out, l1_w_mod, l1_b_mod, l1_norm1, l1_norm2, l1_w_qkv, l1_q_norm, l1_k_norm, l1_sink, l1_w_o, l1_w_router, l1_b_router, l1_w_exp_in, l1_w_exp_out, l2_w_mod, l2_b_mod, l2_norm1, l2_norm2, l2_w_qkv, l2_q_norm, l2_k_norm, l2_w_o, l2_w_ffn_in, l2_w_ffn_out, l3_w_mod, l3_b_mod, l3_norm1, l3_norm2, l3_w_qkv, l3_q_norm, l3_k_norm, l3_rel_bias, l3_w_o, l3_w_router, l3_b_router, l3_w_exp_in, l3_w_exp_out):
    b, s_len, d = x.shape
    n_ctx = ctx.shape[1]
    dims = (b, s_len, n_ctx)
    layers = (
        dict(w_mod=l0_w_mod, b_mod=l0_b_mod, norm1=l0_norm1, norm2=l0_norm2, w_qkv=l0_w_qkv, q_norm=l0_q_norm,
             k_norm=l0_k_norm, rel_bias=l0_rel_bias, w_o=l0_w_o, w_ffn_in=l0_w_ffn_in, w_ffn_out=l0_w_ffn_out),
        dict(w_mod=l1_w_mod, b_mod=l1_b_mod, norm1=l1_norm1, norm2=l1_norm2, w_qkv=l1_w_qkv, q_norm=l1_q_norm,
             k_norm=l1_k_norm, sink=l1_sink, w_o=l1_w_o, w_router=l1_w_router, b_router=l1_b_router,
             w_exp_in=l1_w_exp_in, w_exp_out=l1_w_exp_out),
        dict(w_mod=l2_w_mod, b_mod=l2_b_mod, norm1=l2_norm1, norm2=l2_norm2, w_qkv=l2_w_qkv, q_norm=l2_q_norm,
             k_norm=l2_k_norm, w_o=l2_w_o, w_ffn_in=l2_w_ffn_in, w_ffn_out=l2_w_ffn_out),
        dict(w_mod=l3_w_mod, b_mod=l3_b_mod, norm1=l3_norm1, norm2=l3_norm2, w_qkv=l3_w_qkv, q_norm=l3_q_norm,
             k_norm=l3_k_norm, rel_bias=l3_rel_bias, w_o=l3_w_o, w_router=l3_w_router, b_router=l3_b_router,
             w_exp_in=l3_w_exp_in, w_exp_out=l3_w_exp_out),
    )
    xs = (x.reshape(b * s_len, d), ctx.reshape(b * n_ctx, d))
    pad_rows = -(b + 1) % 8
    cond = jnp.concatenate([c, c_ctx[None, :], jnp.zeros((pad_rows, d), F32)], axis=0)
    n_layers = len(layers)
    for i, p in enumerate(layers):
        xs = _layer(xs, cond, p, dims, i % 3, i == n_layers - 1)
    return xs.reshape(b, s_len, d)
```

```python
import functools

import jax
import jax.numpy as jnp
from jax import lax
from jax.experimental import pallas as pl
from jax.experimental.pallas import tpu as pltpu

F32 = jnp.float32
BF16 = jnp.bfloat16

GRID_W = 64
NORM_EPS = 1e-6
ROPE_THETA = 10000.0
NA_WIN_ROWS = 8
NA_WIN_COLS = 16
SW_WINDOW = 128
N_EXPERTS = 8
MIXER_HEADS = ((16, 16, 64), (16, 4, 64), (8, 4, 128))

LANES = 128
VMEM_LIMIT_BYTES = 56 * 1024 * 1024
NEG_BIG = -1e30
LOG2E = 1.4426950408889634

QKV_CHUNK = 512
GLB_TQ = 512
GLB_CHUNK = 256
SWA_TQ = 256
NA_CTX_ROWS = 1024
NA_UNROLL = 32
MOE_TM = 1024
MOE_CHUNKS = 4
FFN_TF = 512


def _params(*sem):
    return pltpu.CompilerParams(dimension_semantics=sem, vmem_limit_bytes=VMEM_LIMIT_BYTES)


def _row_tile(n_lat_per_sample, n_ctx_rows, cap):
    for tm in (1024, 512, 256, 128):
        if tm <= cap and n_lat_per_sample % tm == 0 and n_ctx_rows % tm == 0:
            return tm
    raise ValueError("no row tile fits")


def _split_bf16(a):
    hi = a.astype(BF16)
    lo = (a - hi.astype(F32)).astype(BF16)
    return hi, lo


def _dot(a, b):
    return jnp.dot(a, b, preferred_element_type=F32)


def _dot_t(a, b):
    return lax.dot_general(a, b, (((1,), (1,)), ((), ())), preferred_element_type=F32)


def _silu(g):
    return g / (1.0 + jnp.exp(-g))


def _rms_mod(x, gain, scale, shift):
    ms = jnp.mean(x * x, axis=-1, keepdims=True)
    return x * lax.rsqrt(ms + NORM_EPS) * gain * (1.0 + scale) + shift


def _adaln_kernel(c_ref, w_ref, b_ref, o_ref):
    a_hi, a_lo = _split_bf16(_silu(c_ref[...]))
    w_hi, w_lo = _split_bf16(w_ref[...])
    o_ref[...] = _dot(a_hi, w_hi) + _dot(a_hi, w_lo) + _dot(a_lo, w_hi) + b_ref[...]


def _adaln(cond, w_mod, b_mod):
    r, d = cond.shape
    n = w_mod.shape[1]
    tn = 1536
    out = pl.pallas_call(
        _adaln_kernel,
        grid=(n // tn,),
        in_specs=[pl.BlockSpec((r, d), lambda j: (0, 0)),
                  pl.BlockSpec((d, tn), lambda j: (0, j)),
                  pl.BlockSpec((1, tn), lambda j: (0, j))],
        out_specs=pl.BlockSpec((r, tn), lambda j: (0, j)),
        out_shape=jax.ShapeDtypeStruct((r, n), F32),
        compiler_params=_params("arbitrary"),
        name="adaln",
    )(cond, w_mod, b_mod.reshape(1, n))
    return out.reshape(r, 6, d)


def _rot_half(z, dh):
    if dh == LANES:
        return pltpu.roll(z, LANES // 2, axis=1)
    lane = lax.broadcasted_iota(jnp.int32, z.shape, 1)
    from_right = pltpu.roll(z, LANES - dh // 2, axis=1)
    from_left = pltpu.roll(z, dh // 2, axis=1)
    return jnp.where((lane % dh) < dh // 2, from_right, from_left)


def _head_mean_sq(z, dh):
    z2 = z * z
    if dh == LANES:
        return jnp.broadcast_to(jnp.sum(z2, axis=-1, keepdims=True), z.shape) * (1.0 / dh)
    low = lax.broadcasted_iota(jnp.int32, z.shape, 1) < dh
    s_low = jnp.sum(jnp.where(low, z2, 0.0), axis=-1, keepdims=True)
    s_high = jnp.sum(jnp.where(low, 0.0, z2), axis=-1, keepdims=True)
    return jnp.where(low, s_low, s_high) * (1.0 / dh)


def _row_sources(arrays, tm, n_lat_tiles):
    lat, ctx = arrays
    return [pl.BlockSpec((tm, lat.shape[1]), lambda i: (jnp.minimum(i, n_lat_tiles - 1), 0)),
            pl.BlockSpec((tm, ctx.shape[1]), lambda i: (jnp.maximum(i - n_lat_tiles, 0), 0))]


def _pick_rows(refs, n_lat_tiles):
    if len(refs) == 1:
        return refs[0][...]
    return jnp.where(pl.program_id(0) < n_lat_tiles, refs[0][...], refs[1][...])


def _qkv_kernel(*refs, n_q, n_kv, dh, rope, n_x, n_lat_tiles):
    x = _pick_rows(refs[:n_x], n_lat_tiles)
    refs = refs[n_x:]
    if rope:
        mod_ref, n1_ref, w_ref, g_ref, cos_ref, sin_ref, q_ref, k_ref, v_ref = refs
    else:
        mod_ref, n1_ref, w_ref, g_ref, q_ref, k_ref, v_ref = refs
    h = _rms_mod(x, n1_ref[...], mod_ref[1:2, :], mod_ref[0:1, :]).astype(BF16)
    n_qk = n_q + n_kv
    cw = QKV_CHUNK
    for c in range((n_qk + n_kv) // cw):
        y = _dot(h, w_ref[:, c * cw:(c + 1) * cw])
        for s in range(cw // LANES):
            col = c * cw + s * LANES
            z = y[:, s * LANES:(s + 1) * LANES]
            if col < n_qk:
                z = z * lax.rsqrt(_head_mean_sq(z, dh) + NORM_EPS) * g_ref[:, col:col + LANES]
                if rope:
                    z = z * cos_ref[...] + _rot_half(z, dh) * sin_ref[...]
            z = z.astype(BF16)
            if col < n_q:
                q_ref[:, col:col + LANES] = z
            elif col < n_qk:
                k_ref[:, col - n_q:col - n_q + LANES] = z
            else:
                v_ref[:, col - n_qk:col - n_qk + LANES] = z


def _rope_tables(s_len, dh, tm):
    n_freq = dh // 4
    inv_freq = ROPE_THETA ** (-jnp.arange(n_freq, dtype=F32) / n_freq)
    t = jnp.arange(s_len)
    row = (t // GRID_W).astype(F32)
    col = (t % GRID_W).astype(F32)
    ang = jnp.concatenate([row[:, None] * inv_freq, col[:, None] * inv_freq], axis=-1)
    cos, sin = jnp.cos(ang), jnp.sin(ang)
    reps = LANES // dh
    cos_t = jnp.tile(jnp.concatenate([cos, cos], axis=-1), (1, reps))
    sin_t = jnp.tile(jnp.concatenate([-sin, sin], axis=-1), (1, reps))
    cos_t = jnp.concatenate([cos_t, jnp.ones((tm, LANES), F32)], axis=0)
    sin_t = jnp.concatenate([sin_t, jnp.zeros((tm, LANES), F32)], axis=0)
    return cos_t, sin_t


def _qkv(xs, mod, norm1, w_qkv, q_gain, k_gain, dims, mixer):
    b, s_len, n_ctx = dims
    x_parts = list(xs) if isinstance(xs, tuple) else [xs]
    n_rows, d = sum(a.shape[0] for a in x_parts), x_parts[0].shape[1]
    n_qh, n_kvh, dh = MIXER_HEADS[mixer]
    n_q, n_kv = n_qh * dh, n_kvh * dh
    rope = mixer != 0
    tm = _row_tile(s_len, b * n_ctx, 512)
    n_lat_tiles = b * s_len // tm
    tiles_per_sample = s_len // tm

    gains = jnp.concatenate([jnp.tile(q_gain * (dh ** -0.5 * LOG2E), n_qh),
                             jnp.tile(k_gain, n_kvh)]).reshape(1, n_q + n_kv)

    def mod_idx(i):
        return jnp.where(i < n_lat_tiles, i // tiles_per_sample, b)

    x_specs = [pl.BlockSpec((tm, d), lambda i: (i, 0))] if len(x_parts) == 1 else _row_sources(x_parts, tm, n_lat_tiles)
    in_specs = x_specs + [pl.BlockSpec((None, 6, d), lambda i: (mod_idx(i), 0, 0)),
                          pl.BlockSpec((1, d), lambda i: (0, 0)),
                          pl.BlockSpec(w_qkv.shape, lambda i: (0, 0)),
                          pl.BlockSpec(gains.shape, lambda i: (0, 0))]
    args = x_parts + [mod, norm1.reshape(1, d), w_qkv.astype(BF16), gains]
    if rope:
        cos_t, sin_t = _rope_tables(s_len, dh, tm)

        def pos_idx(i):
            return jnp.where(i < n_lat_tiles, i % tiles_per_sample, tiles_per_sample)

        in_specs += [pl.BlockSpec((tm, LANES), lambda i: (pos_idx(i), 0))] * 2
        args += [cos_t, sin_t]
    return pl.pallas_call(
        functools.partial(_qkv_kernel, n_q=n_q, n_kv=n_kv, dh=dh, rope=rope, n_x=len(x_parts),
                          n_lat_tiles=n_lat_tiles),
        grid=(n_rows // tm,),
        in_specs=in_specs,
        out_specs=[pl.BlockSpec((tm, n_q), lambda i: (i, 0)),
                   pl.BlockSpec((tm, n_kv), lambda i: (i, 0)),
                   pl.BlockSpec((tm, n_kv), lambda i: (i, 0))],
        out_shape=[jax.ShapeDtypeStruct((n_rows, n_q), BF16),
                   jax.ShapeDtypeStruct((n_rows, n_kv), BF16),
                   jax.ShapeDtypeStruct((n_rows, n_kv), BF16)],
        compiler_params=_params("arbitrary"),
        name="qkv_m%d" % mixer,
    )(*args)


def _stack_heads(q, dh, nh):
    if dh == LANES:
        return jnp.concatenate([q[:, h * LANES:(h + 1) * LANES] for h in range(nh)], axis=0)
    per_kv = nh // 2
    tq = q.shape[0]
    lane = lax.broadcasted_iota(jnp.int32, (tq, LANES), 1)
    ops = []
    for h in range(nh):
        slot = h // per_kv
        chunk = q[:, (h // 2) * LANES:(h // 2 + 1) * LANES].astype(F32)
        if h % 2 != slot:
            chunk = pltpu.roll(chunk, dh, axis=1)
        keep = lane < dh if slot == 0 else lane >= dh
        ops.append(jnp.where(keep, chunk, 0.0).astype(BF16))
    return jnp.concatenate(ops, axis=0)


def _unstack_heads(o, dh, nh):
    tq = o.shape[0] // nh
    if dh == LANES:
        return jnp.concatenate([o[h * tq:(h + 1) * tq] for h in range(nh)], axis=1)
    per_kv = nh // 2
    lane = lax.broadcasted_iota(jnp.int32, (tq, LANES), 1)
    chunks = []
    for c in range(nh // 2):
        parts = []
        for h in (2 * c, 2 * c + 1):
            oh = o[h * tq:(h + 1) * tq]
            if h % 2 != h // per_kv:
                oh = pltpu.roll(oh, dh, axis=1)
            parts.append(oh)
        chunks.append(jnp.where(lane < dh, parts[0], parts[1]))
    return jnp.concatenate(chunks, axis=1)


def _lanes(x, n):
    return x if n == LANES else jnp.concatenate([x] * (n // LANES), axis=1)


def _block_max(s):
    return functools.reduce(jnp.maximum, [s[:, j * LANES:(j + 1) * LANES] for j in range(s.shape[1] // LANES)])


def _row_max(blk):
    return jnp.broadcast_to(jnp.max(blk, axis=-1, keepdims=True), blk.shape)


def _dot_row_halves(p, v):
    half = p.shape[0] // 2
    return jnp.concatenate([_dot(p[0:half, :], v), _dot(p[half:, :], v)], axis=0)


def _ones_ext(v):
    return jnp.concatenate([v, jnp.ones_like(v)], axis=1)


def _sink_rows(sink_ref, first_head, nh, tq):
    return jnp.concatenate([jnp.full((tq, LANES), sink_ref[first_head + h] * LOG2E, F32) for h in range(nh)], axis=0)


def _na_kernel(q_ref, k_ref, v_ref, kc_ref, vc_ref, bias_ref, o_ref,
               qs_ref, vext_ref, mc_ref, numc_ref, lc_ref, *, rows):
    s_len = q_ref.shape[0]
    band = NA_WIN_ROWS * GRID_W
    lane = lax.broadcasted_iota(jnp.int32, (s_len, LANES), 1)
    q = q_ref[...]
    qs_ref[0:s_len, :] = jnp.where(lane < 64, q, jnp.zeros_like(q))
    qs_ref[s_len:, :] = jnp.where(lane >= 64, q, jnp.zeros_like(q))
    vext_ref[:, 0:LANES] = v_ref[...]
    vext_ref[:, LANES:] = jnp.ones((s_len, LANES), BF16)

    kc = kc_ref[...]
    vcx = _ones_ext(vc_ref[...])
    for c in range(2 * s_len // NA_CTX_ROWS):
        rs = slice(c * NA_CTX_ROWS, (c + 1) * NA_CTX_ROWS)
        s = _dot_t(qs_ref[rs, :], kc)
        m = _row_max(_block_max(s))
        acc = _dot(jnp.exp2(s - _lanes(m, s.shape[1])).astype(BF16), vcx)
        mc_ref[rs, :] = m
        numc_ref[rs, :] = acc[:, :LANES]
        lc_ref[rs, :] = acc[:, LANES:]

    lane_q = lax.broadcasted_iota(jnp.int32, (GRID_W, LANES), 1)

    def both_heads(ref, q0):
        return jnp.concatenate([ref[pl.ds(q0, GRID_W), :], ref[pl.ds(s_len + q0, GRID_W), :]], axis=0)

    def one_row(r, carry):
        r0 = jnp.clip(r - NA_WIN_ROWS // 2, 0, rows - NA_WIN_ROWS)
        q0 = pl.multiple_of(r * GRID_W, GRID_W)
        k0 = pl.multiple_of(r0 * GRID_W, GRID_W)
        d0 = r0 - r + NA_WIN_ROWS - 1
        bias = jnp.concatenate(
            [jnp.concatenate([bias_ref[0, d0 + 2 * j], bias_ref[1, d0 + 2 * j]], axis=0)
             for j in range(NA_WIN_ROWS // 2)], axis=1)
        qs = both_heads(qs_ref, q0)
        half = band // 2
        k1 = pl.multiple_of(k0 + half, GRID_W)
        s = jnp.concatenate([_dot_t(qs, k_ref[pl.ds(k0, half), :]), _dot_t(qs, k_ref[pl.ds(k1, half), :])],
                            axis=1) + bias
        mc = both_heads(mc_ref, q0)
        m = jnp.maximum(_row_max(_block_max(s)), mc)
        acc = _dot(jnp.exp2(s - _lanes(m, band)).astype(BF16), vext_ref[pl.ds(k0, band), :])
        alpha = jnp.exp2(mc - m)
        num = acc[:, :LANES] + alpha * both_heads(numc_ref, q0)
        den = acc[:, LANES:] + alpha * both_heads(lc_ref, q0)
        o = num / den
        o_ref[pl.ds(q0, GRID_W), :] = jnp.where(lane_q < 64, o[:GRID_W], o[GRID_W:]).astype(BF16)
        return carry

    lax.fori_loop(0, rows, one_row, 0, unroll=NA_UNROLL)


def _na_bias_tables(rel_bias):
    n_h = rel_bias.shape[0]
    col = jnp.arange(GRID_W)
    col_start = jnp.clip(col - NA_WIN_COLS // 2, 0, GRID_W - NA_WIN_COLS)
    in_win = (col[None, :] >= col_start[:, None]) & (col[None, :] < col_start[:, None] + NA_WIN_COLS)
    col_idx = jnp.clip(col[None, :] - col[:, None] + NA_WIN_COLS - 1, 0, 2 * NA_WIN_COLS - 2)
    masked = jnp.where(in_win[None, None], rel_bias[:, :, col_idx] * LOG2E, NEG_BIG)
    pairs = jnp.concatenate([masked[:, :-1], masked[:, 1:]], axis=-1)
    return pairs.reshape((n_h // 2, 2) + pairs.shape[1:])


def _na_attention(q, k, v, rel_bias, dims):
    b, s_len, n_ctx = dims
    rows = s_len // GRID_W
    n_pairs = q.shape[1] // LANES
    bias = _na_bias_tables(rel_bias)
    ctx_blk = b * s_len // n_ctx
    lat = pl.BlockSpec((s_len, LANES), lambda p, i: (i, p))
    ctx = pl.BlockSpec((n_ctx, LANES), lambda p, i: (ctx_blk + i, p))
    return pl.pallas_call(
        functools.partial(_na_kernel, rows=rows),
        grid=(n_pairs, b),
        in_specs=[lat, lat, lat, ctx, ctx,
                  pl.BlockSpec((None,) + bias.shape[1:], lambda p, i: (p, 0, 0, 0, 0))],
        out_specs=lat,
        out_shape=jax.ShapeDtypeStruct((b * s_len, q.shape[1]), BF16),
        scratch_shapes=[pltpu.VMEM((2 * s_len, LANES), BF16),
                        pltpu.VMEM((s_len, 2 * LANES), BF16),
                        pltpu.VMEM((2 * s_len, LANES), F32),
                        pltpu.VMEM((2 * s_len, LANES), F32),
                        pltpu.VMEM((2 * s_len, LANES), F32)],
        compiler_params=_params("arbitrary", "arbitrary"),
        name="attn_na",
    )(q, k, v, k, v, bias)


def _swa_kernel(sink_ref, q_ref, k_ref, v_ref, kc_ref, vc_ref, o_ref, kwin_ref, vwin_ref, *, nh, n_units, s_len):
    tq = q_ref.shape[0]
    span = tq + 2 * SW_WINDOW
    p_blk, t = pl.program_id(1), pl.program_id(2)

    @pl.when(t == 0)
    def _():
        kwin_ref[span:, :] = kc_ref[...]
        vwin_ref[span:, 0:LANES] = vc_ref[...]
        vwin_ref[:, LANES:] = jnp.ones((vwin_ref.shape[0], LANES), BF16)

    q0 = t * tq
    k0 = pl.multiple_of(jnp.clip(q0 - SW_WINDOW, 0, s_len - span), SW_WINDOW)
    kwin_ref[0:span, :] = k_ref[pl.ds(k0, span), :]
    vwin_ref[0:span, 0:LANES] = v_ref[pl.ds(k0, span), :]

    row = lax.broadcasted_iota(jnp.int32, (tq, span), 0)
    col = lax.broadcasted_iota(jnp.int32, (tq, span), 1)
    bias = jnp.where(jnp.abs(col - row + (k0 - q0)) <= SW_WINDOW, 0.0, NEG_BIG)

    qs_all = _stack_heads(q_ref[...], 64, nh)
    sink_all = _sink_rows(sink_ref, p_blk * nh, nh, tq)
    hu = nh // n_units
    vwin = vwin_ref[...]
    kd = 2 * LANES
    outs = []
    for u in range(n_units):
        qs = qs_all[u * hu * tq:(u + 1) * hu * tq]
        sink = sink_all[u * hu * tq:(u + 1) * hu * tq]
        s = jnp.concatenate([_dot_t(qs, kwin_ref[j * kd:(j + 1) * kd, :]) for j in range(kwin_ref.shape[0] // kd)],
                            axis=1)
        s = jnp.concatenate([(s[:, :span].reshape(hu, tq, span) + bias[None]).reshape(hu * tq, span),
                             s[:, span:]], axis=1)
        m = jnp.maximum(_row_max(_block_max(s)), sink)
        acc = _dot(jnp.exp2(s - _lanes(m, s.shape[1])).astype(BF16), vwin)
        outs.append(acc[:, :LANES] / (acc[:, LANES:] + jnp.exp2(sink - m)))
    o_ref[...] = _unstack_heads(jnp.concatenate(outs, axis=0), 64, nh).astype(BF16)


def _swa_attention(q, k, v, sink, dims):
    b, s_len, n_ctx = dims
    tq = SWA_TQ
    n_t = s_len // tq
    n_kvblk = k.shape[1] // LANES
    nh = q.shape[1] // k.shape[1] * 2
    qw = nh * 64
    ctx_blk = b * s_len // n_ctx
    n_win = tq + 2 * SW_WINDOW + n_ctx
    assert n_win % (2 * LANES) == 0
    qspec = pl.BlockSpec((tq, qw), lambda i, p, t, *_: (i * n_t + t, p))
    lat = pl.BlockSpec((s_len, LANES), lambda i, p, t, *_: (i, p))
    ctx = pl.BlockSpec((n_ctx, LANES), lambda i, p, t, *_: (ctx_blk + i, p))
    return pl.pallas_call(
        functools.partial(_swa_kernel, nh=nh, n_units=nh, s_len=s_len),
        grid_spec=pltpu.PrefetchScalarGridSpec(
            num_scalar_prefetch=1,
            grid=(b, n_kvblk, n_t),
            in_specs=[qspec, lat, lat, ctx, ctx],
            out_specs=qspec,
            scratch_shapes=[pltpu.VMEM((n_win, LANES), BF16),
                            pltpu.VMEM((n_win, 2 * LANES), BF16)]),
        out_shape=jax.ShapeDtypeStruct((b * s_len, q.shape[1]), BF16),
        compiler_params=_params("arbitrary", "arbitrary", "arbitrary"),
        name="attn_swa",
    )(sink, q, k, v, k, v)


def _global_kernel(q_ref, k_ref, v_ref, kc_ref, vc_ref, o_ref, kall_ref, vext_ref, s_ref, p_ref, *, nh, s_len):
    @pl.when(pl.program_id(2) == 0)
    def _():
        kall_ref[0:s_len, :] = k_ref[...]
        kall_ref[s_len:, :] = kc_ref[...]
        vext_ref[0:s_len, 0:LANES] = v_ref[...]
        vext_ref[s_len:, 0:LANES] = vc_ref[...]
        vext_ref[:, LANES:] = jnp.ones((vext_ref.shape[0], LANES), BF16)

    qs = _stack_heads(q_ref[...], LANES, nh)
    n_chunks = kall_ref.shape[0] // GLB_CHUNK
    mrun = None
    for c in range(n_chunks):
        ks = slice(c * GLB_CHUNK, (c + 1) * GLB_CHUNK)
        s = _dot_t(qs, kall_ref[ks, :])
        s_ref[:, ks] = s
        blk = _block_max(s)
        mrun = blk if mrun is None else jnp.maximum(mrun, blk)
    m = _lanes(_row_max(mrun), GLB_CHUNK)
    for c in range(n_chunks):
        ks = slice(c * GLB_CHUNK, (c + 1) * GLB_CHUNK)
        p_ref[:, ks] = jnp.exp2(s_ref[:, ks] - m).astype(BF16)
    acc = _dot_row_halves(p_ref, vext_ref[...])
    o_ref[...] = _unstack_heads(acc[:, :LANES] / acc[:, LANES:], LANES, nh).astype(BF16)


def _global_attention(q, k, v, dims):
    b, s_len, n_ctx = dims
    tq = GLB_TQ
    n_t = s_len // tq
    n_kvblk = k.shape[1] // LANES
    nh = q.shape[1] // k.shape[1]
    qw = nh * LANES
    ctx_blk = b * s_len // n_ctx
    n_keys = s_len + n_ctx
    qspec = pl.BlockSpec((tq, qw), lambda i, p, t: (i * n_t + t, p))
    lat = pl.BlockSpec((s_len, LANES), lambda i, p, t: (i, p))
    ctx = pl.BlockSpec((n_ctx, LANES), lambda i, p, t: (ctx_blk + i, p))
    return pl.pallas_call(
        functools.partial(_global_kernel, nh=nh, s_len=s_len),
        grid=(b, n_kvblk, n_t),
        in_specs=[qspec, lat, lat, ctx, ctx],
        out_specs=qspec,
        out_shape=jax.ShapeDtypeStruct((b * s_len, q.shape[1]), BF16),
        scratch_shapes=[pltpu.VMEM((n_keys, LANES), BF16),
                        pltpu.VMEM((n_keys, 2 * LANES), BF16),
                        pltpu.VMEM((nh * tq, n_keys), F32),
                        pltpu.VMEM((nh * tq, n_keys), BF16)],
        compiler_params=_params("arbitrary", "arbitrary", "arbitrary"),
        name="attn_global",
    )(q, k, v, k, v)


def _ctx_kernel(sink_ref, q_ref, kc_ref, vc_ref, o_ref, *, dh, nh, use_sink):
    tq = q_ref.shape[0]
    qs = _stack_heads(q_ref[...], dh, nh)
    s = _dot_t(qs, kc_ref[...])
    m = _row_max(_block_max(s))
    if use_sink:
        sink = _sink_rows(sink_ref, pl.program_id(1) * nh, nh, tq)
        m = jnp.maximum(m, sink)
    acc = _dot(jnp.exp2(s - _lanes(m, s.shape[1])).astype(BF16), _ones_ext(vc_ref[...]))
    den = acc[:, LANES:]
    if use_sink:
        den = den + jnp.exp2(sink - m)
    o_ref[...] = _unstack_heads(acc[:, :LANES] / den, dh, nh).astype(BF16)


def _ctx_attention(q, k, v, sink, dims, mixer):
    b, s_len, n_ctx = dims
    n_qh, n_kvh, dh = MIXER_HEADS[mixer]
    n_kvblk = k.shape[1] // LANES
    nh = n_qh // n_kvblk
    qw = nh * dh
    ctx_blk = b * s_len // n_ctx
    use_sink = sink is not None
    if not use_sink:
        sink = jnp.zeros((n_qh,), F32)
    return pl.pallas_call(
        functools.partial(_ctx_kernel, dh=dh, nh=nh, use_sink=use_sink),
        grid_spec=pltpu.PrefetchScalarGridSpec(
            num_scalar_prefetch=1,
            grid=(b, n_kvblk),
            in_specs=[pl.BlockSpec((n_ctx, qw), lambda i, p, *_: (ctx_blk + i, p)),
                      pl.BlockSpec((n_ctx, LANES), lambda i, p, *_: (ctx_blk + i, p)),
                      pl.BlockSpec((n_ctx, LANES), lambda i, p, *_: (ctx_blk + i, p))],
            out_specs=pl.BlockSpec((n_ctx, qw), lambda i, p, *_: (i, p))),
        out_shape=jax.ShapeDtypeStruct((b * n_ctx, q.shape[1]), BF16),
        compiler_params=_params("arbitrary", "arbitrary"),
        name="attn_ctx_m%d" % mixer,
    )(sink, q, k, v)


def _oproj_kernel(*refs, n_lat_tiles, n_y, n_x, moe):
    y = _pick_rows(refs[:n_y], n_lat_tiles)
    x_in = _pick_rows(refs[n_y:n_y + n_x], n_lat_tiles)
    refs = refs[n_y + n_x:]
    mod_ref, n2_ref, wo_ref = refs[:3]
    refs = refs[3:]
    if moe:
        wr_ref, br_ref = refs[:2]
        refs = refs[2:]
    xo_ref, h_ref = refs[:2]
    x = x_in + mod_ref[2:3, :] * _dot(y, wo_ref[...])
    xo_ref[...] = x
    h = _rms_mod(x, n2_ref[...], mod_ref[4:5, :], mod_ref[3:4, :])
    h_ref[...] = h.astype(BF16)
    if moe:
        route_ref, counts_ref, base_ref = refs[2:5]
        h_hi, h_lo = _split_bf16(h)
        w_hi, w_lo = _split_bf16(wr_ref[...])
        logits = _dot(h_hi, w_hi) + _dot(h_hi, w_lo) + _dot(h_lo, w_hi) + br_ref[...]
        lane = lax.broadcasted_iota(jnp.int32, logits.shape, 1)
        v1 = jnp.max(logits, axis=-1, keepdims=True)
        i1 = jnp.min(jnp.where(logits == v1, lane, LANES), axis=-1, keepdims=True)
        rest = jnp.where(lane == i1, NEG_BIG, logits)
        v2 = jnp.max(rest, axis=-1, keepdims=True)
        i2 = jnp.min(jnp.where(rest == v2, lane, LANES), axis=-1, keepdims=True)
        e = jnp.exp(v2 - v1)
        w1 = 1.0 / (1.0 + e)
        w2 = e / (1.0 + e)
        @pl.when(pl.program_id(0) == 0)
        def _():
            base_ref[...] = jnp.zeros_like(base_ref)

        tm = logits.shape[0]
        pick1 = lane == i1
        pick2 = lane == i2
        cnt = jnp.where(pick1, 1.0, 0.0) + jnp.where(pick2, 1.0, 0.0)
        earlier = lax.broadcasted_iota(jnp.int32, (tm, tm), 0) > lax.broadcasted_iota(jnp.int32, (tm, tm), 1)
        before = _dot(jnp.where(earlier, 1.0, 0.0).astype(BF16), cnt.astype(BF16)) + base_ref[0:1, :]
        r1 = jnp.sum(jnp.where(pick1, before, 0.0), axis=-1, keepdims=True)
        r2 = jnp.sum(jnp.where(pick2, before, 0.0), axis=-1, keepdims=True)
        base_ref[...] = base_ref[...] + jnp.sum(cnt, axis=0, keepdims=True)
        counts_ref[...] = base_ref[...]
        route = jnp.where(lane == 0, i1.astype(F32), 0.0)
        route = jnp.where(lane == 1, i2.astype(F32), route)
        route = jnp.where(lane == 2, w1, route)
        route = jnp.where(lane == 3, w2, route)
        route = jnp.where(lane == 4, r1, route)
        route = jnp.where(lane == 5, r2, route)
        route_ref[...] = route


def _oproj(y_lat, y_ctx, xs, mod, norm2, w_o, router, dims, last):
    b, s_len, n_ctx = dims
    x_parts = list(xs) if isinstance(xs, tuple) else [xs]
    d = x_parts[0].shape[1]
    tm = _row_tile(s_len, b * n_ctx, 512)
    n_lat_tiles = b * s_len // tm
    tiles_per_sample = s_len // tm
    n_rows = b * s_len if last else b * (s_len + n_ctx)
    n_tiles = n_rows // tm
    moe = router is not None

    def mod_idx(i):
        return jnp.where(i < n_lat_tiles, i // tiles_per_sample, b)

    row = pl.BlockSpec((tm, d), lambda i: (i, 0))
    y_parts = [y_lat] if last else [y_lat, y_ctx]
    in_specs = ([pl.BlockSpec((tm, y_lat.shape[1]), lambda i: (i, 0))] if last
                else _row_sources(y_parts, tm, n_lat_tiles))
    in_specs += [row] if len(x_parts) == 1 else _row_sources(x_parts, tm, n_lat_tiles)
    in_specs += [pl.BlockSpec((None, 6, d), lambda i: (mod_idx(i), 0, 0)),
                 pl.BlockSpec((1, d), lambda i: (0, 0)),
                 pl.BlockSpec(w_o.shape, lambda i: (0, 0))]
    args = y_parts + x_parts + [mod, norm2.reshape(1, d), w_o.astype(BF16)]
    out_specs = [row, row]
    out_shape = [jax.ShapeDtypeStruct((n_rows, d), F32), jax.ShapeDtypeStruct((n_rows, d), BF16)]
    if moe:
        w_router, b_router = router
        n_e = w_router.shape[1]
        wr = jnp.pad(w_router, ((0, 0), (0, LANES - n_e)))
        br = jnp.pad(b_router, (0, LANES - n_e), constant_values=NEG_BIG).reshape(1, LANES)
        in_specs += [pl.BlockSpec(wr.shape, lambda i: (0, 0)), pl.BlockSpec(br.shape, lambda i: (0, 0))]
        args += [wr, br]
        out_specs += [pl.BlockSpec((tm, LANES), lambda i: (i, 0)), pl.BlockSpec((8, LANES), lambda i: (0, 0))]
        out_shape += [jax.ShapeDtypeStruct((n_rows, LANES), F32), jax.ShapeDtypeStruct((8, LANES), F32)]
    return pl.pallas_call(
        functools.partial(_oproj_kernel, n_lat_tiles=n_lat_tiles, n_y=len(y_parts), n_x=len(x_parts), moe=moe),
        grid=(n_tiles,),
        in_specs=in_specs,
        out_specs=out_specs,
        out_shape=out_shape,
        scratch_shapes=[pltpu.VMEM((8, LANES), F32)] if moe else [],
        compiler_params=_params("arbitrary"),
        name="oproj_moe" if moe else "oproj",
    )(*args)


def _swiglu_partial(h_ref, wg_ref, wu_ref, wd_ref):
    h = h_ref[...]
    a = _silu(_dot(h, wg_ref[...].astype(BF16))) * _dot(h, wu_ref[...].astype(BF16))
    return _dot(a.astype(BF16), wd_ref[...].astype(BF16))


def _accumulate_over_hidden(f, n_f, acc_ref, partial, finish):
    @pl.when(f == 0)
    def _():
        acc_ref[...] = partial()

    @pl.when(jnp.logical_and(f > 0, f < n_f - 1))
    def _():
        acc_ref[...] += partial()

    @pl.when(f == n_f - 1)
    def _():
        finish(acc_ref[...] + partial())


def _ffn_kernel(h_ref, wg_ref, wu_ref, wd_ref, x_ref, mod_ref, o_ref, acc_ref):
    def finish(total):
        o_ref[...] = x_ref[...] + mod_ref[5:6, :] * total

    _accumulate_over_hidden(pl.program_id(1), pl.num_programs(1), acc_ref,
                            functools.partial(_swiglu_partial, h_ref, wg_ref, wu_ref, wd_ref), finish)


def _ffn(h, xs, mod, w_in, w_out, dims):
    b, s_len, n_ctx = dims
    n_rows, d = h.shape
    d_ff = w_out.shape[0]
    tm = _row_tile(s_len, b * n_ctx, 1024)
    tf = FFN_TF
    n_f = d_ff // tf
    n_lat_tiles = b * s_len // tm
    tiles_per_sample = s_len // tm

    def mod_idx(i):
        return jnp.where(i < n_lat_tiles, i // tiles_per_sample, b)

    w_in = w_in.astype(BF16)
    return pl.pallas_call(
        _ffn_kernel,
        grid=(n_rows // tm, n_f),
        in_specs=[pl.BlockSpec((tm, d), lambda i, f: (i, 0)),
                  pl.BlockSpec((d, tf), lambda i, f: (0, f)),
                  pl.BlockSpec((d, tf), lambda i, f: (0, n_f + f)),
                  pl.BlockSpec((tf, d), lambda i, f: (f, 0)),
                  pl.BlockSpec((tm, d), lambda i, f: (i, 0)),
                  pl.BlockSpec((None, 6, d), lambda i, f: (mod_idx(i), 0, 0))],
        out_specs=pl.BlockSpec((tm, d), lambda i, f: (i, 0)),
        out_shape=jax.ShapeDtypeStruct((n_rows, d), F32),
        scratch_shapes=[pltpu.VMEM((tm, d), F32)],
        compiler_params=_params("arbitrary", "arbitrary"),
        name="ffn_dense",
    )(h, w_in, w_in, w_out.astype(BF16), xs, mod)


def _moe_ffn_kernel(te_ref, na_ref, h_ref, wg_ref, wu_ref, wd_ref, *rest):
    o_ref, acc_ref = rest[-2:]
    i, f = pl.program_id(0), pl.program_id(1)

    def finish(total):
        o_ref[...] = total.astype(o_ref.dtype)

    @pl.when(i < na_ref[0])
    def _():
        _accumulate_over_hidden(f, pl.num_programs(1), acc_ref,
                                functools.partial(_swiglu_partial, h_ref, wg_ref, wu_ref, wd_ref), finish)


def _moe_ffn(hs, ys, n_rows_total, first_tile, tile_expert, n_active, w_in, w_out):
    n_rows, d = hs.shape
    d_ff = w_out.shape[1]
    tm, tf = MOE_TM, FFN_TF
    n_f = d_ff // tf

    def tile(i, na):
        return jnp.minimum(i, jnp.maximum(na[0] - 1, 0))

    def expert(i, te, na):
        return te[tile(i, na)]

    in_specs = [pl.BlockSpec((tm, d), lambda i, f, te, na: (tile(i, na), 0)),
                pl.BlockSpec((None, d, tf), lambda i, f, te, na: (expert(i, te, na), 0, f)),
                pl.BlockSpec((None, d, tf), lambda i, f, te, na: (expert(i, te, na), 0, n_f + f)),
                pl.BlockSpec((None, tf, d), lambda i, f, te, na: (expert(i, te, na), f, 0))]
    args = [tile_expert, n_active, hs, w_in, w_in, w_out]
    aliases = {}
    if ys is not None:
        in_specs.append(pl.BlockSpec(memory_space=pl.ANY))
        aliases = {len(args): 0}
        args.append(ys)
    return pl.pallas_call(
        _moe_ffn_kernel,
        grid_spec=pltpu.PrefetchScalarGridSpec(
            num_scalar_prefetch=2,
            grid=(n_rows // tm, n_f),
            in_specs=in_specs,
            out_specs=pl.BlockSpec((tm, d), lambda i, f, te, na: (first_tile + tile(i, na), 0)),
            scratch_shapes=[pltpu.VMEM((tm, d), F32)]),
        out_shape=jax.ShapeDtypeStruct((n_rows_total, d), hs.dtype),
        input_output_aliases=aliases,
        compiler_params=_params("arbitrary", "arbitrary"),
        name="moe_ffn",
    )(*args)


def _combine_kernel(x_ref, a_ref, b_ref, route_ref, mod_ref, o_ref):
    w1 = route_ref[:, 2:3]
    w2 = route_ref[:, 3:4]
    mix = w1 * a_ref[...].astype(F32) + w2 * b_ref[...].astype(F32)
    o_ref[...] = x_ref[...] + mod_ref[5:6, :] * mix


def _combine(xs, ya, yb, route, mod, dims):
    b, s_len, n_ctx = dims
    n_rows, d = ya.shape
    tm = _row_tile(s_len, b * n_ctx, 512)
    n_lat_tiles = b * s_len // tm
    tiles_per_sample = s_len // tm

    def mod_idx(i):
        return jnp.where(i < n_lat_tiles, i // tiles_per_sample, b)

    row = pl.BlockSpec((tm, d), lambda i: (i, 0))
    return pl.pallas_call(
        _combine_kernel,
        grid=(n_rows // tm,),
        in_specs=[row, row, row,
                  pl.BlockSpec((tm, LANES), lambda i: (i, 0)),
                  pl.BlockSpec((None, 6, d), lambda i: (mod_idx(i), 0, 0))],
        out_specs=row,
        out_shape=jax.ShapeDtypeStruct((n_rows, d), F32),
        compiler_params=_params("arbitrary"),
        name="moe_combine",
    )(xs, ya, yb, route, mod)


def _route_plan(idx, rank, counts, tm):
    n = idx.shape[0]
    tiles_per = (counts + tm - 1) // tm
    tile_end = jnp.cumsum(tiles_per)
    tile_start = tile_end - tiles_per
    experts = jnp.arange(N_EXPERTS, dtype=jnp.int32)
    start = jnp.sum(jnp.where(idx[:, :, None] == experts, tile_start, 0), axis=-1)
    slot = start * tm + rank
    n_tiles = (2 * n) // tm + N_EXPERTS
    tile_ids = jnp.arange(n_tiles, dtype=jnp.int32)
    tile_expert = jnp.minimum(jnp.sum((tile_end[None, :] <= tile_ids[:, None]).astype(jnp.int32), axis=1),
                              N_EXPERTS - 1)
    token = jnp.arange(2 * n, dtype=jnp.int32) // 2
    _, token_sorted = lax.sort((slot.reshape(-1), token), num_keys=1)
    first_pair = jnp.cumsum(counts) - counts
    row_in_expert = ((tile_ids - tile_start[tile_expert]) * tm)[:, None] + jnp.arange(tm, dtype=jnp.int32)[None, :]
    pair = jnp.clip(first_pair[tile_expert][:, None] + row_in_expert, 0, 2 * n - 1)
    src = jnp.where(row_in_expert < counts[tile_expert][:, None], token_sorted[pair], 0).reshape(-1)
    return slot, src, tile_expert, tile_end[-1:].astype(jnp.int32)


def _take_rows(a, rows):
    return a.at[rows].get(mode="promise_in_bounds")


def _moe(h, xs, route, counts, mod, w_exp_in, w_exp_out, dims):
    idx = route[:, 0:2].astype(jnp.int32)
    rank = route[:, 4:6].astype(jnp.int32)
    slot, src, tile_expert, n_active = _route_plan(idx, rank, counts[0, :N_EXPERTS].astype(jnp.int32), MOE_TM)
    n_tiles = src.shape[0] // MOE_TM
    per_chunk = -(-n_tiles // MOE_CHUNKS)
    ys = None
    for t0 in range(0, n_tiles, per_chunk):
        t1 = min(t0 + per_chunk, n_tiles)
        hs = _take_rows(h, src[t0 * MOE_TM:t1 * MOE_TM])
        ys = _moe_ffn(hs, ys, src.shape[0], t0, tile_expert[t0:t1], jnp.clip(n_active - t0, 0, t1 - t0),
                      w_exp_in, w_exp_out)
    ya = _take_rows(ys, slot[:, 0])
    yb = _take_rows(ys, slot[:, 1])
    return _combine(xs, ya, yb, route, mod, dims)


def _layer(xs, cond, p, dims, mixer, last):
    b, s_len, n_ctx = dims
    mod = _adaln(cond, p["w_mod"], p["b_mod"])
    q, k, v = _qkv(xs, mod, p["norm1"], p["w_qkv"], p["q_norm"], p["k_norm"], dims, mixer)
    if mixer == 0:
        y_lat = _na_attention(q, k, v, p["rel_bias"], dims)
    elif mixer == 1:
        y_lat = _swa_attention(q, k, v, p["sink"], dims)
    else:
        y_lat = _global_attention(q, k, v, dims)
    y_ctx = None if last else _ctx_attention(q, k, v, p.get("sink"), dims, mixer)
    router = (p["w_router"], p["b_router"]) if "w_router" in p else None
    outs = _oproj(y_lat, y_ctx, xs, mod, p["norm2"], p["w_o"], router, dims, last)
    if router is None:
        xs, h = outs
        return _ffn(h, xs, mod, p["w_ffn_in"], p["w_ffn_out"], dims)
    xs, h, route, counts = outs
    return _moe(h, xs, route, counts, mod, p["w_exp_in"], p["w_exp_out"], dims)


def kernel(x, c, ctx, c_ctx, l0_w_mod, l0_b_mod, l0_norm1, l0_norm2, l0_w_qkv, l0_q_norm, l0_k_norm, l0_rel_bias, l0_w_o, l0_w_ffn_in, l0_w_ffn_out, l1_w_mod, l1_b_mod, l1_norm1, l1_norm2, l1_w_qkv, l1_q_norm, l1_k_norm, l1_sink, l1_w_o, l1_w_router, l1_b_router, l1_w_exp_in, l1_w_exp_out, l2_w_mod, l2_b_mod, l2_norm1, l2_norm2, l2_w_qkv, l2_q_norm, l2_k_norm, l2_w_o, l2_w_ffn_in, l2_w_ffn_out, l3_w_mod, l3_b_mod, l3_norm1, l3_norm2, l3_w_qkv, l3_q_norm, l3_k_norm, l3_rel_bias, l3_w_o, l3_w_router, l3_b_router, l3_w_exp_in, l3_w_exp_out):
    b, s_len, d = x.shape
    n_ctx = ctx.shape[1]
    dims = (b, s_len, n_ctx)
    layers = (
        dict(w_mod=l0_w_mod, b_mod=l0_b_mod, norm1=l0_norm1, norm2=l0_norm2, w_qkv=l0_w_qkv, q_norm=l0_q_norm,
             k_norm=l0_k_norm, rel_bias=l0_rel_bias, w_o=l0_w_o, w_ffn_in=l0_w_ffn_in, w_ffn_out=l0_w_ffn_out),
        dict(w_mod=l1_w_mod, b_mod=l1_b_mod, norm1=l1_norm1, norm2=l1_norm2, w_qkv=l1_w_qkv, q_norm=l1_q_norm,
             k_norm=l1_k_norm, sink=l1_sink, w_o=l1_w_o, w_router=l1_w_router, b_router=l1_b_router,
             w_exp_in=l1_w_exp_in, w_exp_out=l1_w_exp_out),
        dict(w_mod=l2_w_mod, b_mod=l2_b_mod, norm1=l2_norm1, norm2=l2_norm2, w_qkv=l2_w_qkv, q_norm=l2_q_norm,
             k_norm=l2_k_norm, w_o=l2_w_o, w_ffn_in=l2_w_ffn_in, w_ffn_out=l2_w_ffn_out),
        dict(w_mod=l3_w_mod, b_mod=l3_b_mod, norm1=l3_norm1, norm2=l3_norm2, w_qkv=l3_w_qkv, q_norm=l3_q_norm,
             k_norm=l3_k_norm, rel_bias=l3_rel_bias, w_o=l3_w_o, w_router=l3_w_router, b_router=l3_b_router,
             w_exp_in=l3_w_exp_in, w_exp_out=l3_w_exp_out),
    )
    xs = (x.reshape(b * s_len, d), ctx.reshape(b * n_ctx, d))
    pad_rows = -(b + 1) % 8
    cond = jnp.concatenate([c, c_ctx[None, :], jnp.zeros((pad_rows, d), F32)], axis=0)
    n_layers = len(layers)
    for i, p in enumerate(layers):
        xs = _layer(xs, cond, p, dims, i % 3, i == n_layers - 1)
    return xs.reshape(b, s_len, d)
```

```python
import functools

import jax
import jax.numpy as jnp
from jax import lax
from jax.experimental import pallas as pl
from jax.experimental.pallas import tpu as pltpu

F32 = jnp.float32
BF16 = jnp.bfloat16

GRID_W = 64
NORM_EPS = 1e-6
ROPE_THETA = 10000.0
NA_WIN_ROWS = 8
NA_WIN_COLS = 16
SW_WINDOW = 128
N_EXPERTS = 8
MIXER_HEADS = ((16, 16, 64), (16, 4, 64), (8, 4, 128))

LANES = 128
VMEM_LIMIT_BYTES = 56 * 1024 * 1024
NEG_BIG = -1e30
LOG2E = 1.4426950408889634

QKV_CHUNK = 512
GLB_TQ = 512
GLB_CHUNK = 256
SWA_TQ = 256
NA_CTX_ROWS = 1024
NA_UNROLL = 32
MOE_TM = 1024
MOE_CHUNKS = 4
MOE_FIRST_CHUNK_DIV = 16
FFN_TF = 512


def _params(*sem):
    return pltpu.CompilerParams(dimension_semantics=sem, vmem_limit_bytes=VMEM_LIMIT_BYTES)


def _row_tile(n_lat_per_sample, n_ctx_rows, cap):
    for tm in (1024, 512, 256, 128):
        if tm <= cap and n_lat_per_sample % tm == 0 and n_ctx_rows % tm == 0:
            return tm
    raise ValueError("no row tile fits")


def _split_bf16(a):
    hi = a.astype(BF16)
    lo = (a - hi.astype(F32)).astype(BF16)
    return hi, lo


def _dot(a, b):
    return jnp.dot(a, b, preferred_element_type=F32)


def _dot_t(a, b):
    return lax.dot_general(a, b, (((1,), (1,)), ((), ())), preferred_element_type=F32)


def _silu(g):
    return g / (1.0 + jnp.exp(-g))


def _rms_mod(x, gain, scale, shift):
    ms = jnp.mean(x * x, axis=-1, keepdims=True)
    return x * lax.rsqrt(ms + NORM_EPS) * gain * (1.0 + scale) + shift


def _adaln_kernel(c_ref, w_ref, b_ref, o_ref):
    a_hi, a_lo = _split_bf16(_silu(c_ref[...]))
    w_hi, w_lo = _split_bf16(w_ref[...])
    o_ref[...] = _dot(a_hi, w_hi) + _dot(a_hi, w_lo) + _dot(a_lo, w_hi) + b_ref[...]


def _adaln(cond, w_mod, b_mod):
    r, d = cond.shape
    n = w_mod.shape[1]
    tn = 1536
    out = pl.pallas_call(
        _adaln_kernel,
        grid=(n // tn,),
        in_specs=[pl.BlockSpec((r, d), lambda j: (0, 0)),
                  pl.BlockSpec((d, tn), lambda j: (0, j)),
                  pl.BlockSpec((1, tn), lambda j: (0, j))],
        out_specs=pl.BlockSpec((r, tn), lambda j: (0, j)),
        out_shape=jax.ShapeDtypeStruct((r, n), F32),
        compiler_params=_params("arbitrary"),
        name="adaln",
    )(cond, w_mod, b_mod.reshape(1, n))
    return out.reshape(r, 6, d)


def _rot_half(z, dh):
    if dh == LANES:
        return pltpu.roll(z, LANES // 2, axis=1)
    lane = lax.broadcasted_iota(jnp.int32, z.shape, 1)
    from_right = pltpu.roll(z, LANES - dh // 2, axis=1)
    from_left = pltpu.roll(z, dh // 2, axis=1)
    return jnp.where((lane % dh) < dh // 2, from_right, from_left)


def _head_mean_sq(z, dh):
    z2 = z * z
    if dh == LANES:
        return jnp.broadcast_to(jnp.sum(z2, axis=-1, keepdims=True), z.shape) * (1.0 / dh)
    low = lax.broadcasted_iota(jnp.int32, z.shape, 1) < dh
    s_low = jnp.sum(jnp.where(low, z2, 0.0), axis=-1, keepdims=True)
    s_high = jnp.sum(jnp.where(low, 0.0, z2), axis=-1, keepdims=True)
    return jnp.where(low, s_low, s_high) * (1.0 / dh)


def _row_sources(arrays, tm, n_lat_tiles):
    lat, ctx = arrays
    return [pl.BlockSpec((tm, lat.shape[1]), lambda i: (jnp.minimum(i, n_lat_tiles - 1), 0)),
            pl.BlockSpec((tm, ctx.shape[1]), lambda i: (jnp.maximum(i - n_lat_tiles, 0), 0))]


def _pick_rows(refs, n_lat_tiles):
    if len(refs) == 1:
        return refs[0][...]
    return jnp.where(pl.program_id(0) < n_lat_tiles, refs[0][...], refs[1][...])


def _qkv_kernel(*refs, n_q, n_kv, dh, rope, n_x, n_lat_tiles):
    x = _pick_rows(refs[:n_x], n_lat_tiles)
    refs = refs[n_x:]
    if rope:
        mod_ref, n1_ref, w_ref, g_ref, cos_ref, sin_ref, q_ref, k_ref, v_ref = refs
    else:
        mod_ref, n1_ref, w_ref, g_ref, q_ref, k_ref, v_ref = refs
    h = _rms_mod(x, n1_ref[...], mod_ref[1:2, :], mod_ref[0:1, :]).astype(BF16)
    n_qk = n_q + n_kv
    cw = QKV_CHUNK
    for c in range((n_qk + n_kv) // cw):
        y = _dot(h, w_ref[:, c * cw:(c + 1) * cw])
        for s in range(cw // LANES):
            col = c * cw + s * LANES
            z = y[:, s * LANES:(s + 1) * LANES]
            if col < n_qk:
                z = z * lax.rsqrt(_head_mean_sq(z, dh) + NORM_EPS) * g_ref[:, col:col + LANES]
                if rope:
                    z = z * cos_ref[...] + _rot_half(z, dh) * sin_ref[...]
            z = z.astype(BF16)
            if col < n_q:
                q_ref[:, col:col + LANES] = z
            elif col < n_qk:
                k_ref[:, col - n_q:col - n_q + LANES] = z
            else:
                v_ref[:, col - n_qk:col - n_qk + LANES] = z


def _rope_tables(s_len, dh, tm):
    n_freq = dh // 4
    inv_freq = ROPE_THETA ** (-jnp.arange(n_freq, dtype=F32) / n_freq)
    t = jnp.arange(s_len)
    row = (t // GRID_W).astype(F32)
    col = (t % GRID_W).astype(F32)
    ang = jnp.concatenate([row[:, None] * inv_freq, col[:, None] * inv_freq], axis=-1)
    cos, sin = jnp.cos(ang), jnp.sin(ang)
    reps = LANES // dh
    cos_t = jnp.tile(jnp.concatenate([cos, cos], axis=-1), (1, reps))
    sin_t = jnp.tile(jnp.concatenate([-sin, sin], axis=-1), (1, reps))
    cos_t = jnp.concatenate([cos_t, jnp.ones((tm, LANES), F32)], axis=0)
    sin_t = jnp.concatenate([sin_t, jnp.zeros((tm, LANES), F32)], axis=0)
    return cos_t, sin_t


def _qkv(xs, mod, norm1, w_qkv, q_gain, k_gain, dims, mixer):
    b, s_len, n_ctx = dims
    x_parts = list(xs) if isinstance(xs, tuple) else [xs]
    n_rows, d = sum(a.shape[0] for a in x_parts), x_parts[0].shape[1]
    n_qh, n_kvh, dh = MIXER_HEADS[mixer]
    n_q, n_kv = n_qh * dh, n_kvh * dh
    rope = mixer != 0
    tm = _row_tile(s_len, b * n_ctx, 512)
    n_lat_tiles = b * s_len // tm
    tiles_per_sample = s_len // tm

    gains = jnp.concatenate([jnp.tile(q_gain * (dh ** -0.5 * LOG2E), n_qh),
                             jnp.tile(k_gain, n_kvh)]).reshape(1, n_q + n_kv)

    def mod_idx(i):
        return jnp.where(i < n_lat_tiles, i // tiles_per_sample, b)

    x_specs = [pl.BlockSpec((tm, d), lambda i: (i, 0))] if len(x_parts) == 1 else _row_sources(x_parts, tm, n_lat_tiles)
    in_specs = x_specs + [pl.BlockSpec((None, 6, d), lambda i: (mod_idx(i), 0, 0)),
                          pl.BlockSpec((1, d), lambda i: (0, 0)),
                          pl.BlockSpec(w_qkv.shape, lambda i: (0, 0)),
                          pl.BlockSpec(gains.shape, lambda i: (0, 0))]
    args = x_parts + [mod, norm1.reshape(1, d), w_qkv.astype(BF16), gains]
    if rope:
        cos_t, sin_t = _rope_tables(s_len, dh, tm)

        def pos_idx(i):
            return jnp.where(i < n_lat_tiles, i % tiles_per_sample, tiles_per_sample)

        in_specs += [pl.BlockSpec((tm, LANES), lambda i: (pos_idx(i), 0))] * 2
        args += [cos_t, sin_t]
    return pl.pallas_call(
        functools.partial(_qkv_kernel, n_q=n_q, n_kv=n_kv, dh=dh, rope=rope, n_x=len(x_parts),
                          n_lat_tiles=n_lat_tiles),
        grid=(n_rows // tm,),
        in_specs=in_specs,
        out_specs=[pl.BlockSpec((tm, n_q), lambda i: (i, 0)),
                   pl.BlockSpec((tm, n_kv), lambda i: (i, 0)),
                   pl.BlockSpec((tm, n_kv), lambda i: (i, 0))],
        out_shape=[jax.ShapeDtypeStruct((n_rows, n_q), BF16),
                   jax.ShapeDtypeStruct((n_rows, n_kv), BF16),
                   jax.ShapeDtypeStruct((n_rows, n_kv), BF16)],
        compiler_params=_params("arbitrary"),
        name="qkv_m%d" % mixer,
    )(*args)


def _stack_heads(q, dh, nh):
    if dh == LANES:
        return jnp.concatenate([q[:, h * LANES:(h + 1) * LANES] for h in range(nh)], axis=0)
    per_kv = nh // 2
    tq = q.shape[0]
    lane = lax.broadcasted_iota(jnp.int32, (tq, LANES), 1)
    ops = []
    for h in range(nh):
        slot = h // per_kv
        chunk = q[:, (h // 2) * LANES:(h // 2 + 1) * LANES].astype(F32)
        if h % 2 != slot:
            chunk = pltpu.roll(chunk, dh, axis=1)
        keep = lane < dh if slot == 0 else lane >= dh
        ops.append(jnp.where(keep, chunk, 0.0).astype(BF16))
    return jnp.concatenate(ops, axis=0)


def _unstack_heads(o, dh, nh):
    tq = o.shape[0] // nh
    if dh == LANES:
        return jnp.concatenate([o[h * tq:(h + 1) * tq] for h in range(nh)], axis=1)
    per_kv = nh // 2
    lane = lax.broadcasted_iota(jnp.int32, (tq, LANES), 1)
    chunks = []
    for c in range(nh // 2):
        parts = []
        for h in (2 * c, 2 * c + 1):
            oh = o[h * tq:(h + 1) * tq]
            if h % 2 != h // per_kv:
                oh = pltpu.roll(oh, dh, axis=1)
            parts.append(oh)
        chunks.append(jnp.where(lane < dh, parts[0], parts[1]))
    return jnp.concatenate(chunks, axis=1)


def _lanes(x, n):
    return x if n == LANES else jnp.concatenate([x] * (n // LANES), axis=1)


def _block_max(s):
    return functools.reduce(jnp.maximum, [s[:, j * LANES:(j + 1) * LANES] for j in range(s.shape[1] // LANES)])


def _row_max(blk):
    return jnp.broadcast_to(jnp.max(blk, axis=-1, keepdims=True), blk.shape)


def _dot_row_halves(p, v):
    half = p.shape[0] // 2
    return jnp.concatenate([_dot(p[0:half, :], v), _dot(p[half:, :], v)], axis=0)


def _ones_ext(v):
    return jnp.concatenate([v, jnp.ones_like(v)], axis=1)


def _sink_rows(sink_ref, first_head, nh, tq):
    return jnp.concatenate([jnp.full((tq, LANES), sink_ref[first_head + h] * LOG2E, F32) for h in range(nh)], axis=0)


def _na_kernel(q_ref, k_ref, v_ref, kc_ref, vc_ref, bias_ref, o_ref,
               qs_ref, vext_ref, mc_ref, numc_ref, lc_ref, *, rows):
    s_len = q_ref.shape[0]
    band = NA_WIN_ROWS * GRID_W
    lane = lax.broadcasted_iota(jnp.int32, (s_len, LANES), 1)
    q = q_ref[...]
    qs_ref[0:s_len, :] = jnp.where(lane < 64, q, jnp.zeros_like(q))
    qs_ref[s_len:, :] = jnp.where(lane >= 64, q, jnp.zeros_like(q))
    vext_ref[:, 0:LANES] = v_ref[...]
    vext_ref[:, LANES:] = jnp.ones((s_len, LANES), BF16)

    kc = kc_ref[...]
    vcx = _ones_ext(vc_ref[...])
    for c in range(2 * s_len // NA_CTX_ROWS):
        rs = slice(c * NA_CTX_ROWS, (c + 1) * NA_CTX_ROWS)
        s = _dot_t(qs_ref[rs, :], kc)
        m = _row_max(_block_max(s))
        acc = _dot(jnp.exp2(s - _lanes(m, s.shape[1])).astype(BF16), vcx)
        mc_ref[rs, :] = m
        numc_ref[rs, :] = acc[:, :LANES]
        lc_ref[rs, :] = acc[:, LANES:]

    lane_q = lax.broadcasted_iota(jnp.int32, (GRID_W, LANES), 1)

    def both_heads(ref, q0):
        return jnp.concatenate([ref[pl.ds(q0, GRID_W), :], ref[pl.ds(s_len + q0, GRID_W), :]], axis=0)

    def one_row(r, carry):
        r0 = jnp.clip(r - NA_WIN_ROWS // 2, 0, rows - NA_WIN_ROWS)
        q0 = pl.multiple_of(r * GRID_W, GRID_W)
        k0 = pl.multiple_of(r0 * GRID_W, GRID_W)
        d0 = r0 - r + NA_WIN_ROWS - 1
        bias = jnp.concatenate(
            [jnp.concatenate([bias_ref[0, d0 + 2 * j], bias_ref[1, d0 + 2 * j]], axis=0)
             for j in range(NA_WIN_ROWS // 2)], axis=1)
        qs = both_heads(qs_ref, q0)
        half = band // 2
        k1 = pl.multiple_of(k0 + half, GRID_W)
        s = jnp.concatenate([_dot_t(qs, k_ref[pl.ds(k0, half), :]), _dot_t(qs, k_ref[pl.ds(k1, half), :])],
                            axis=1) + bias
        mc = both_heads(mc_ref, q0)
        m = jnp.maximum(_row_max(_block_max(s)), mc)
        acc = _dot(jnp.exp2(s - _lanes(m, band)).astype(BF16), vext_ref[pl.ds(k0, band), :])
        alpha = jnp.exp2(mc - m)
        num = acc[:, :LANES] + alpha * both_heads(numc_ref, q0)
        den = acc[:, LANES:] + alpha * both_heads(lc_ref, q0)
        o = num / den
        o_ref[pl.ds(q0, GRID_W), :] = jnp.where(lane_q < 64, o[:GRID_W], o[GRID_W:]).astype(BF16)
        return carry

    lax.fori_loop(0, rows, one_row, 0, unroll=NA_UNROLL)


def _na_bias_tables(rel_bias):
    n_h = rel_bias.shape[0]
    col = jnp.arange(GRID_W)
    col_start = jnp.clip(col - NA_WIN_COLS // 2, 0, GRID_W - NA_WIN_COLS)
    in_win = (col[None, :] >= col_start[:, None]) & (col[None, :] < col_start[:, None] + NA_WIN_COLS)
    col_idx = jnp.clip(col[None, :] - col[:, None] + NA_WIN_COLS - 1, 0, 2 * NA_WIN_COLS - 2)
    masked = jnp.where(in_win[None, None], rel_bias[:, :, col_idx] * LOG2E, NEG_BIG)
    pairs = jnp.concatenate([masked[:, :-1], masked[:, 1:]], axis=-1)
    return pairs.reshape((n_h // 2, 2) + pairs.shape[1:])


def _na_attention(q, k, v, rel_bias, dims):
    b, s_len, n_ctx = dims
    rows = s_len // GRID_W
    n_pairs = q.shape[1] // LANES
    bias = _na_bias_tables(rel_bias)
    ctx_blk = b * s_len // n_ctx
    lat = pl.BlockSpec((s_len, LANES), lambda p, i: (i, p))
    ctx = pl.BlockSpec((n_ctx, LANES), lambda p, i: (ctx_blk + i, p))
    return pl.pallas_call(
        functools.partial(_na_kernel, rows=rows),
        grid=(n_pairs, b),
        in_specs=[lat, lat, lat, ctx, ctx,
                  pl.BlockSpec((None,) + bias.shape[1:], lambda p, i: (p, 0, 0, 0, 0))],
        out_specs=lat,
        out_shape=jax.ShapeDtypeStruct((b * s_len, q.shape[1]), BF16),
        scratch_shapes=[pltpu.VMEM((2 * s_len, LANES), BF16),
                        pltpu.VMEM((s_len, 2 * LANES), BF16),
                        pltpu.VMEM((2 * s_len, LANES), F32),
                        pltpu.VMEM((2 * s_len, LANES), F32),
                        pltpu.VMEM((2 * s_len, LANES), F32)],
        compiler_params=_params("arbitrary", "arbitrary"),
        name="attn_na",
    )(q, k, v, k, v, bias)


def _swa_kernel(sink_ref, q_ref, k_ref, v_ref, kc_ref, vc_ref, o_ref, kwin_ref, vwin_ref, *, nh, n_units, s_len):
    tq = q_ref.shape[0]
    span = tq + 2 * SW_WINDOW
    p_blk, t = pl.program_id(1), pl.program_id(2)

    @pl.when(t == 0)
    def _():
        kwin_ref[span:, :] = kc_ref[...]
        vwin_ref[span:, 0:LANES] = vc_ref[...]
        vwin_ref[:, LANES:] = jnp.ones((vwin_ref.shape[0], LANES), BF16)

    q0 = t * tq
    k0 = pl.multiple_of(jnp.clip(q0 - SW_WINDOW, 0, s_len - span), SW_WINDOW)
    kwin_ref[0:span, :] = k_ref[pl.ds(k0, span), :]
    vwin_ref[0:span, 0:LANES] = v_ref[pl.ds(k0, span), :]

    row = lax.broadcasted_iota(jnp.int32, (tq, span), 0)
    col = lax.broadcasted_iota(jnp.int32, (tq, span), 1)
    bias = jnp.where(jnp.abs(col - row + (k0 - q0)) <= SW_WINDOW, 0.0, NEG_BIG)

    qs_all = _stack_heads(q_ref[...], 64, nh)
    sink_all = _sink_rows(sink_ref, p_blk * nh, nh, tq)
    hu = nh // n_units
    vwin = vwin_ref[...]
    kd = 2 * LANES
    outs = []
    for u in range(n_units):
        qs = qs_all[u * hu * tq:(u + 1) * hu * tq]
        sink = sink_all[u * hu * tq:(u + 1) * hu * tq]
        s = jnp.concatenate([_dot_t(qs, kwin_ref[j * kd:(j + 1) * kd, :]) for j in range(kwin_ref.shape[0] // kd)],
                            axis=1)
        s = jnp.concatenate([(s[:, :span].reshape(hu, tq, span) + bias[None]).reshape(hu * tq, span),
                             s[:, span:]], axis=1)
        m = jnp.maximum(_row_max(_block_max(s)), sink)
        acc = _dot(jnp.exp2(s - _lanes(m, s.shape[1])).astype(BF16), vwin)
        outs.append(acc[:, :LANES] / (acc[:, LANES:] + jnp.exp2(sink - m)))
    o_ref[...] = _unstack_heads(jnp.concatenate(outs, axis=0), 64, nh).astype(BF16)


def _swa_attention(q, k, v, sink, dims):
    b, s_len, n_ctx = dims
    tq = SWA_TQ
    n_t = s_len // tq
    n_kvblk = k.shape[1] // LANES
    nh = q.shape[1] // k.shape[1] * 2
    qw = nh * 64
    ctx_blk = b * s_len // n_ctx
    n_win = tq + 2 * SW_WINDOW + n_ctx
    assert n_win % (2 * LANES) == 0
    qspec = pl.BlockSpec((tq, qw), lambda i, p, t, *_: (i * n_t + t, p))
    lat = pl.BlockSpec((s_len, LANES), lambda i, p, t, *_: (i, p))
    ctx = pl.BlockSpec((n_ctx, LANES), lambda i, p, t, *_: (ctx_blk + i, p))
    return pl.pallas_call(
        functools.partial(_swa_kernel, nh=nh, n_units=nh, s_len=s_len),
        grid_spec=pltpu.PrefetchScalarGridSpec(
            num_scalar_prefetch=1,
            grid=(b, n_kvblk, n_t),
            in_specs=[qspec, lat, lat, ctx, ctx],
            out_specs=qspec,
            scratch_shapes=[pltpu.VMEM((n_win, LANES), BF16),
                            pltpu.VMEM((n_win, 2 * LANES), BF16)]),
        out_shape=jax.ShapeDtypeStruct((b * s_len, q.shape[1]), BF16),
        compiler_params=_params("arbitrary", "arbitrary", "arbitrary"),
        name="attn_swa",
    )(sink, q, k, v, k, v)


def _global_kernel(q_ref, k_ref, v_ref, kc_ref, vc_ref, o_ref, kall_ref, vext_ref, s_ref, p_ref, *, nh, s_len):
    @pl.when(pl.program_id(2) == 0)
    def _():
        kall_ref[0:s_len, :] = k_ref[...]
        kall_ref[s_len:, :] = kc_ref[...]
        vext_ref[0:s_len, 0:LANES] = v_ref[...]
        vext_ref[s_len:, 0:LANES] = vc_ref[...]
        vext_ref[:, LANES:] = jnp.ones((vext_ref.shape[0], LANES), BF16)

    qs = _stack_heads(q_ref[...], LANES, nh)
    n_chunks = kall_ref.shape[0] // GLB_CHUNK
    mrun = None
    for c in range(n_chunks):
        ks = slice(c * GLB_CHUNK, (c + 1) * GLB_CHUNK)
        s = _dot_t(qs, kall_ref[ks, :])
        s_ref[:, ks] = s
        blk = _block_max(s)
        mrun = blk if mrun is None else jnp.maximum(mrun, blk)
    m = _lanes(_row_max(mrun), GLB_CHUNK)
    for c in range(n_chunks):
        ks = slice(c * GLB_CHUNK, (c + 1) * GLB_CHUNK)
        p_ref[:, ks] = jnp.exp2(s_ref[:, ks] - m).astype(BF16)
    acc = _dot_row_halves(p_ref, vext_ref[...])
    o_ref[...] = _unstack_heads(acc[:, :LANES] / acc[:, LANES:], LANES, nh).astype(BF16)


def _global_attention(q, k, v, dims):
    b, s_len, n_ctx = dims
    tq = GLB_TQ
    n_t = s_len // tq
    n_kvblk = k.shape[1] // LANES
    nh = q.shape[1] // k.shape[1]
    qw = nh * LANES
    ctx_blk = b * s_len // n_ctx
    n_keys = s_len + n_ctx
    qspec = pl.BlockSpec((tq, qw), lambda i, p, t: (i * n_t + t, p))
    lat = pl.BlockSpec((s_len, LANES), lambda i, p, t: (i, p))
    ctx = pl.BlockSpec((n_ctx, LANES), lambda i, p, t: (ctx_blk + i, p))
    return pl.pallas_call(
        functools.partial(_global_kernel, nh=nh, s_len=s_len),
        grid=(b, n_kvblk, n_t),
        in_specs=[qspec, lat, lat, ctx, ctx],
        out_specs=qspec,
        out_shape=jax.ShapeDtypeStruct((b * s_len, q.shape[1]), BF16),
        scratch_shapes=[pltpu.VMEM((n_keys, LANES), BF16),
                        pltpu.VMEM((n_keys, 2 * LANES), BF16),
                        pltpu.VMEM((nh * tq, n_keys), F32),
                        pltpu.VMEM((nh * tq, n_keys), BF16)],
        compiler_params=_params("arbitrary", "arbitrary", "arbitrary"),
        name="attn_global",
    )(q, k, v, k, v)


def _ctx_kernel(sink_ref, q_ref, kc_ref, vc_ref, o_ref, *, dh, nh, use_sink):
    tq = q_ref.shape[0]
    qw = nh * dh
    for p in range(kc_ref.shape[1] // LANES):
        kv = slice(p * LANES, (p + 1) * LANES)
        qs = _stack_heads(q_ref[:, p * qw:(p + 1) * qw], dh, nh)
        s = _dot_t(qs, kc_ref[:, kv])
        m = _row_max(_block_max(s))
        if use_sink:
            sink = _sink_rows(sink_ref, p * nh, nh, tq)
            m = jnp.maximum(m, sink)
        acc = _dot(jnp.exp2(s - _lanes(m, s.shape[1])).astype(BF16), _ones_ext(vc_ref[:, kv]))
        den = acc[:, LANES:]
        if use_sink:
            den = den + jnp.exp2(sink - m)
        o_ref[:, p * qw:(p + 1) * qw] = _unstack_heads(acc[:, :LANES] / den, dh, nh).astype(BF16)


def _ctx_attention(q, k, v, sink, dims, mixer):
    b, s_len, n_ctx = dims
    n_qh, n_kvh, dh = MIXER_HEADS[mixer]
    n_kvblk = k.shape[1] // LANES
    nh = n_qh // n_kvblk
    ctx_blk = b * s_len // n_ctx
    use_sink = sink is not None
    if not use_sink:
        sink = jnp.zeros((n_qh,), F32)
    return pl.pallas_call(
        functools.partial(_ctx_kernel, dh=dh, nh=nh, use_sink=use_sink),
        grid_spec=pltpu.PrefetchScalarGridSpec(
            num_scalar_prefetch=1,
            grid=(b,),
            in_specs=[pl.BlockSpec((n_ctx, q.shape[1]), lambda i, *_: (ctx_blk + i, 0)),
                      pl.BlockSpec((n_ctx, k.shape[1]), lambda i, *_: (ctx_blk + i, 0)),
                      pl.BlockSpec((n_ctx, v.shape[1]), lambda i, *_: (ctx_blk + i, 0))],
            out_specs=pl.BlockSpec((n_ctx, q.shape[1]), lambda i, *_: (i, 0))),
        out_shape=jax.ShapeDtypeStruct((b * n_ctx, q.shape[1]), BF16),
        compiler_params=_params("arbitrary"),
        name="attn_ctx_m%d" % mixer,
    )(sink, q, k, v)


def _oproj_kernel(*refs, n_lat_tiles, n_y, n_x, moe):
    y = _pick_rows(refs[:n_y], n_lat_tiles)
    x_in = _pick_rows(refs[n_y:n_y + n_x], n_lat_tiles)
    refs = refs[n_y + n_x:]
    mod_ref, n2_ref, wo_ref = refs[:3]
    refs = refs[3:]
    if moe:
        wr_ref, br_ref = refs[:2]
        refs = refs[2:]
    xo_ref, h_ref = refs[:2]
    x = x_in + mod_ref[2:3, :] * _dot(y, wo_ref[...])
    xo_ref[...] = x
    h = _rms_mod(x, n2_ref[...], mod_ref[4:5, :], mod_ref[3:4, :])
    h_ref[...] = h.astype(BF16)
    if moe:
        route_ref, counts_ref, base_ref = refs[2:5]
        h_hi, h_lo = _split_bf16(h)
        w_hi, w_lo = _split_bf16(wr_ref[...])
        hi_terms = _dot(h_hi, jnp.concatenate([w_hi, w_lo], axis=1))
        lo_terms = _dot(h_lo, jnp.concatenate([w_hi, jnp.zeros_like(w_hi)], axis=1))
        logits = hi_terms[:, :LANES] + hi_terms[:, LANES:] + lo_terms[:, :LANES] + br_ref[...]
        lane = lax.broadcasted_iota(jnp.int32, logits.shape, 1)
        v1 = jnp.max(logits, axis=-1, keepdims=True)
        i1 = jnp.min(jnp.where(logits == v1, lane, LANES), axis=-1, keepdims=True)
        rest = jnp.where(lane == i1, NEG_BIG, logits)
        v2 = jnp.max(rest, axis=-1, keepdims=True)
        i2 = jnp.min(jnp.where(rest == v2, lane, LANES), axis=-1, keepdims=True)
        e = jnp.exp(v2 - v1)
        w1 = 1.0 / (1.0 + e)
        w2 = e / (1.0 + e)
        @pl.when(pl.program_id(0) == 0)
        def _():
            base_ref[...] = jnp.zeros_like(base_ref)

        tm = logits.shape[0]
        pick1 = lane == i1
        pick2 = lane == i2
        cnt = jnp.where(pick1, 1.0, 0.0) + jnp.where(pick2, 1.0, 0.0)
        earlier = lax.broadcasted_iota(jnp.int32, (tm, tm), 0) > lax.broadcasted_iota(jnp.int32, (tm, tm), 1)
        before = _dot(jnp.where(earlier, 1.0, 0.0).astype(BF16), cnt.astype(BF16)) + base_ref[0:1, :]
        r1 = jnp.sum(jnp.where(pick1, before, 0.0), axis=-1, keepdims=True)
        r2 = jnp.sum(jnp.where(pick2, before, 0.0), axis=-1, keepdims=True)
        base_ref[...] = base_ref[...] + jnp.sum(cnt, axis=0, keepdims=True)
        counts_ref[...] = base_ref[...]
        route = jnp.where(lane == 0, i1.astype(F32), 0.0)
        route = jnp.where(lane == 1, i2.astype(F32), route)
        route = jnp.where(lane == 2, w1, route)
        route = jnp.where(lane == 3, w2, route)
        route = jnp.where(lane == 4, r1, route)
        route = jnp.where(lane == 5, r2, route)
        route_ref[...] = route


def _oproj(y_lat, y_ctx, xs, mod, norm2, w_o, router, dims, last):
    b, s_len, n_ctx = dims
    x_parts = list(xs) if isinstance(xs, tuple) else [xs]
    d = x_parts[0].shape[1]
    tm = _row_tile(s_len, b * n_ctx, 512)
    n_lat_tiles = b * s_len // tm
    tiles_per_sample = s_len // tm
    n_rows = b * s_len if last else b * (s_len + n_ctx)
    n_tiles = n_rows // tm
    moe = router is not None

    def mod_idx(i):
        return jnp.where(i < n_lat_tiles, i // tiles_per_sample, b)

    row = pl.BlockSpec((tm, d), lambda i: (i, 0))
    y_parts = [y_lat] if last else [y_lat, y_ctx]
    in_specs = ([pl.BlockSpec((tm, y_lat.shape[1]), lambda i: (i, 0))] if last
                else _row_sources(y_parts, tm, n_lat_tiles))
    in_specs += [row] if len(x_parts) == 1 else _row_sources(x_parts, tm, n_lat_tiles)
    in_specs += [pl.BlockSpec((None, 6, d), lambda i: (mod_idx(i), 0, 0)),
                 pl.BlockSpec((1, d), lambda i: (0, 0)),
                 pl.BlockSpec(w_o.shape, lambda i: (0, 0))]
    args = y_parts + x_parts + [mod, norm2.reshape(1, d), w_o.astype(BF16)]
    out_specs = [row, row]
    out_shape = [jax.ShapeDtypeStruct((n_rows, d), F32), jax.ShapeDtypeStruct((n_rows, d), BF16)]
    if moe:
        w_router, b_router = router
        n_e = w_router.shape[1]
        wr = jnp.pad(w_router, ((0, 0), (0, LANES - n_e)))
        br = jnp.pad(b_router, (0, LANES - n_e), constant_values=NEG_BIG).reshape(1, LANES)
        in_specs += [pl.BlockSpec(wr.shape, lambda i: (0, 0)), pl.BlockSpec(br.shape, lambda i: (0, 0))]
        args += [wr, br]
        out_specs += [pl.BlockSpec((tm, LANES), lambda i: (i, 0)), pl.BlockSpec((8, LANES), lambda i: (0, 0))]
        out_shape += [jax.ShapeDtypeStruct((n_rows, LANES), F32), jax.ShapeDtypeStruct((8, LANES), F32)]
    return pl.pallas_call(
        functools.partial(_oproj_kernel, n_lat_tiles=n_lat_tiles, n_y=len(y_parts), n_x=len(x_parts), moe=moe),
        grid=(n_tiles,),
        in_specs=in_specs,
        out_specs=out_specs,
        out_shape=out_shape,
        scratch_shapes=[pltpu.VMEM((8, LANES), F32)] if moe else [],
        compiler_params=_params("arbitrary"),
        name="oproj_moe" if moe else "oproj",
    )(*args)


def _swiglu_partial(h_ref, wg_ref, wu_ref, wd_ref):
    h = h_ref[...]
    a = _silu(_dot(h, wg_ref[...].astype(BF16))) * _dot(h, wu_ref[...].astype(BF16))
    return _dot(a.astype(BF16), wd_ref[...].astype(BF16))


def _accumulate_over_hidden(f, n_f, acc_ref, partial, finish):
    @pl.when(f == 0)
    def _():
        acc_ref[...] = partial()

    @pl.when(jnp.logical_and(f > 0, f < n_f - 1))
    def _():
        acc_ref[...] += partial()

    @pl.when(f == n_f - 1)
    def _():
        finish(acc_ref[...] + partial())


def _ffn_kernel(h_ref, wg_ref, wu_ref, wd_ref, x_ref, mod_ref, o_ref, acc_ref):
    def finish(total):
        o_ref[...] = x_ref[...] + mod_ref[5:6, :] * total

    _accumulate_over_hidden(pl.program_id(1), pl.num_programs(1), acc_ref,
                            functools.partial(_swiglu_partial, h_ref, wg_ref, wu_ref, wd_ref), finish)


def _ffn(h, xs, mod, w_in, w_out, dims):
    b, s_len, n_ctx = dims
    n_rows, d = h.shape
    d_ff = w_out.shape[0]
    tm = _row_tile(s_len, b * n_ctx, 1024)
    tf = FFN_TF
    n_f = d_ff // tf
    n_lat_tiles = b * s_len // tm
    tiles_per_sample = s_len // tm

    def mod_idx(i):
        return jnp.where(i < n_lat_tiles, i // tiles_per_sample, b)

    w_in = w_in.astype(BF16)
    return pl.pallas_call(
        _ffn_kernel,
        grid=(n_rows // tm, n_f),
        in_specs=[pl.BlockSpec((tm, d), lambda i, f: (i, 0)),
                  pl.BlockSpec((d, tf), lambda i, f: (0, f)),
                  pl.BlockSpec((d, tf), lambda i, f: (0, n_f + f)),
                  pl.BlockSpec((tf, d), lambda i, f: (f, 0)),
                  pl.BlockSpec((tm, d), lambda i, f: (i, 0)),
                  pl.BlockSpec((None, 6, d), lambda i, f: (mod_idx(i), 0, 0))],
        out_specs=pl.BlockSpec((tm, d), lambda i, f: (i, 0)),
        out_shape=jax.ShapeDtypeStruct((n_rows, d), F32),
        scratch_shapes=[pltpu.VMEM((tm, d), F32)],
        compiler_params=_params("arbitrary", "arbitrary"),
        name="ffn_dense",
    )(h, w_in, w_in, w_out.astype(BF16), xs, mod)


def _moe_ffn_kernel(te_ref, na_ref, h_ref, wg_ref, wu_ref, wd_ref, *rest):
    o_ref, acc_ref = rest[-2:]
    i, f = pl.program_id(0), pl.program_id(1)

    def finish(total):
        o_ref[...] = total.astype(o_ref.dtype)

    @pl.when(i < na_ref[0])
    def _():
        _accumulate_over_hidden(f, pl.num_programs(1), acc_ref,
                                functools.partial(_swiglu_partial, h_ref, wg_ref, wu_ref, wd_ref), finish)


def _moe_ffn(hs, ys, n_rows_total, first_tile, tile_expert, n_active, w_in, w_out):
    n_rows, d = hs.shape
    d_ff = w_out.shape[1]
    tm, tf = MOE_TM, FFN_TF
    n_f = d_ff // tf

    def tile(i, na):
        return jnp.minimum(i, jnp.maximum(na[0] - 1, 0))

    def expert(i, te, na):
        return te[tile(i, na)]

    in_specs = [pl.BlockSpec((tm, d), lambda i, f, te, na: (tile(i, na), 0)),
                pl.BlockSpec((None, d, tf), lambda i, f, te, na: (expert(i, te, na), 0, f)),
                pl.BlockSpec((None, d, tf), lambda i, f, te, na: (expert(i, te, na), 0, n_f + f)),
                pl.BlockSpec((None, tf, d), lambda i, f, te, na: (expert(i, te, na), f, 0))]
    args = [tile_expert, n_active, hs, w_in, w_in, w_out]
    aliases = {}
    if ys is not None:
        in_specs.append(pl.BlockSpec(memory_space=pl.ANY))
        aliases = {len(args): 0}
        args.append(ys)
    return pl.pallas_call(
        _moe_ffn_kernel,
        grid_spec=pltpu.PrefetchScalarGridSpec(
            num_scalar_prefetch=2,
            grid=(n_rows // tm, n_f),
            in_specs=in_specs,
            out_specs=pl.BlockSpec((tm, d), lambda i, f, te, na: (first_tile + tile(i, na), 0)),
            scratch_shapes=[pltpu.VMEM((tm, d), F32)]),
        out_shape=jax.ShapeDtypeStruct((n_rows_total, d), hs.dtype),
        input_output_aliases=aliases,
        compiler_params=_params("arbitrary", "arbitrary"),
        name="moe_ffn",
    )(*args)


def _combine_kernel(x_ref, a_ref, b_ref, route_ref, mod_ref, o_ref):
    w1 = route_ref[:, 2:3]
    w2 = route_ref[:, 3:4]
    mix = w1 * a_ref[...].astype(F32) + w2 * b_ref[...].astype(F32)
    o_ref[...] = x_ref[...] + mod_ref[5:6, :] * mix


def _combine(xs, ya, yb, route, mod, dims):
    b, s_len, n_ctx = dims
    n_rows, d = ya.shape
    tm = _row_tile(s_len, b * n_ctx, 512)
    n_lat_tiles = b * s_len // tm
    tiles_per_sample = s_len // tm

    def mod_idx(i):
        return jnp.where(i < n_lat_tiles, i // tiles_per_sample, b)

    row = pl.BlockSpec((tm, d), lambda i: (i, 0))
    return pl.pallas_call(
        _combine_kernel,
        grid=(n_rows // tm,),
        in_specs=[row, row, row,
                  pl.BlockSpec((tm, LANES), lambda i: (i, 0)),
                  pl.BlockSpec((None, 6, d), lambda i: (mod_idx(i), 0, 0))],
        out_specs=row,
        out_shape=jax.ShapeDtypeStruct((n_rows, d), F32),
        compiler_params=_params("arbitrary"),
        name="moe_combine",
    )(xs, ya, yb, route, mod)


def _route_plan(idx, rank, counts, tm):
    n = idx.shape[0]
    tiles_per = (counts + tm - 1) // tm
    tile_end = jnp.cumsum(tiles_per)
    tile_start = tile_end - tiles_per
    experts = jnp.arange(N_EXPERTS, dtype=jnp.int32)
    start = jnp.sum(jnp.where(idx[:, :, None] == experts, tile_start, 0), axis=-1)
    slot = start * tm + rank
    n_tiles = (2 * n) // tm + N_EXPERTS
    tile_ids = jnp.arange(n_tiles, dtype=jnp.int32)
    tile_expert = jnp.minimum(jnp.sum((tile_end[None, :] <= tile_ids[:, None]).astype(jnp.int32), axis=1),
                              N_EXPERTS - 1)
    token = jnp.arange(2 * n, dtype=jnp.int32) // 2
    _, token_sorted = lax.sort((slot.reshape(-1), token), num_keys=1)
    first_pair = jnp.cumsum(counts) - counts
    row_in_expert = ((tile_ids - tile_start[tile_expert]) * tm)[:, None] + jnp.arange(tm, dtype=jnp.int32)[None, :]
    pair = jnp.clip(first_pair[tile_expert][:, None] + row_in_expert, 0, 2 * n - 1)
    src = jnp.where(row_in_expert < counts[tile_expert][:, None], token_sorted[pair], 0).reshape(-1)
    return slot, src, tile_expert, tile_end[-1:].astype(jnp.int32)


def _take_rows(a, rows):
    return a.at[rows].get(mode="promise_in_bounds")


def _moe(h, xs, route, counts, mod, w_exp_in, w_exp_out, dims):
    idx = route[:, 0:2].astype(jnp.int32)
    rank = route[:, 4:6].astype(jnp.int32)
    slot, src, tile_expert, n_active = _route_plan(idx, rank, counts[0, :N_EXPERTS].astype(jnp.int32), MOE_TM)
    n_tiles = src.shape[0] // MOE_TM
    first = max(1, n_tiles // MOE_FIRST_CHUNK_DIV)
    rest = -(-(n_tiles - first) // (MOE_CHUNKS - 1))
    bounds = [0] + [min(first + j * rest, n_tiles) for j in range(MOE_CHUNKS)]
    ys = None
    for t0, t1 in zip(bounds[:-1], bounds[1:]):
        if t1 == t0:
            continue
        hs = _take_rows(h, src[t0 * MOE_TM:t1 * MOE_TM])
        ys = _moe_ffn(hs, ys, src.shape[0], t0, tile_expert[t0:t1], jnp.clip(n_active - t0, 0, t1 - t0),
                      w_exp_in, w_exp_out)
    ya = _take_rows(ys, slot[:, 0])
    yb = _take_rows(ys, slot[:, 1])
    return _combine(xs, ya, yb, route, mod, dims)


def _layer(xs, cond, p, dims, mixer, last):
    b, s_len, n_ctx = dims
    mod = _adaln(cond, p["w_mod"], p["b_mod"])
    q, k, v = _qkv(xs, mod, p["norm1"], p["w_qkv"], p["q_norm"], p["k_norm"], dims, mixer)
    if mixer == 0:
        y_lat = _na_attention(q, k, v, p["rel_bias"], dims)
    elif mixer == 1:
        y_lat = _swa_attention(q, k, v, p["sink"], dims)
    else:
        y_lat = _global_attention(q, k, v, dims)
    y_ctx = None if last else _ctx_attention(q, k, v, p.get("sink"), dims, mixer)
    router = (p["w_router"], p["b_router"]) if "w_router" in p else None
    outs = _oproj(y_lat, y_ctx, xs, mod, p["norm2"], p["w_o"], router, dims, last)
    if router is None:
        xs, h = outs
        return _ffn(h, xs, mod, p["w_ffn_in"], p["w_ffn_out"], dims)
    xs, h, route, counts = outs
    return _moe(h, xs, route, counts, mod, p["w_exp_in"], p["w_exp_out"], dims)


def kernel(x, c, ctx, c_ctx, l0_w_mod, l0_b_mod, l0_norm1, l0_norm2, l0_w_qkv, l0_q_norm, l0_k_norm, l0_rel_bias, l0_w_o, l0_w_ffn_in, l0_w_ffn_out, l1_w_mod, l1_b_mod, l1_norm1, l1_norm2, l1_w_qkv, l1_q_norm, l1_k_norm, l1_sink, l1_w_o, l1_w_router, l1_b_router, l1_w_exp_in, l1_w_exp_out, l2_w_mod, l2_b_mod, l2_norm1, l2_norm2, l2_w_qkv, l2_q_norm, l2_k_norm, l2_w_o, l2_w_ffn_in, l2_w_ffn_out, l3_w_mod, l3_b_mod, l3_norm1, l3_norm2, l3_w_qkv, l3_q_norm, l3_k_norm, l3_rel_bias, l3_w_o, l3_w_router, l3_b_router, l3_w_exp_in, l3_w_exp_out):
    b, s_len, d = x.shape
    n_ctx = ctx.shape[1]
    dims = (b, s_len, n_ctx)
    layers = (
        dict(w_mod=l0_w_mod, b_mod=l0_b_mod, norm1=l0_norm1, norm2=l0_norm2, w_qkv=l0_w_qkv, q_norm=l0_q_norm,
             k_norm=l0_k_norm, rel_bias=l0_rel_bias, w_o=l0_w_o, w_ffn_in=l0_w_ffn_in, w_ffn_out=l0_w_ffn_out),
        dict(w_mod=l1_w_mod, b_mod=l1_b_mod, norm1=l1_norm1, norm2=l1_norm2, w_qkv=l1_w_qkv, q_norm=l1_q_norm,
             k_norm=l1_k_norm, sink=l1_sink, w_o=l1_w_o, w_router=l1_w_router, b_router=l1_b_router,
             w_exp_in=l1_w_exp_in, w_exp_out=l1_w_exp_out),
        dict(w_mod=l2_w_mod, b_mod=l2_b_mod, norm1=l2_norm1, norm2=l2_norm2, w_qkv=l2_w_qkv, q_norm=l2_q_norm,
             k_norm=l2_k_norm, w_o=l2_w_o, w_ffn_in=l2_w_ffn_in, w_ffn_out=l2_w_ffn_out),
        dict(w_mod=l3_w_mod, b_mod=l3_b_mod, norm1=l3_norm1, norm2=l3_norm2, w_qkv=l3_w_qkv, q_norm=l3_q_norm,
             k_norm=l3_k_norm, rel_bias=l3_rel_bias, w_o=l3_w_o, w_router=l3_w_router, b_router=l3_b_router,
             w_exp_in=l3_w_exp_in, w_exp_out=l3_w_exp_out),
    )
    xs = (x.reshape(b * s_len, d), ctx.reshape(b * n_ctx, d))
    pad_rows = -(b + 1) % 8
    cond = jnp.concatenate([c, c_ctx[None, :], jnp.zeros((pad_rows, d), F32)], axis=0)
    n_layers = len(layers)
    for i, p in enumerate(layers):
        xs = _layer(xs, cond, p, dims, i % 3, i == n_layers - 1)
    return xs.reshape(b, s_len, d)
```

```python
import functools

import jax
import jax.numpy as jnp
from jax import lax
from jax.experimental import pallas as pl
from jax.experimental.pallas import tpu as pltpu

F32 = jnp.float32
BF16 = jnp.bfloat16

GRID_W = 64
NORM_EPS = 1e-6
ROPE_THETA = 10000.0
NA_WIN_ROWS = 8
NA_WIN_COLS = 16
SW_WINDOW = 128
N_EXPERTS = 8
MIXER_HEADS = ((16, 16, 64), (16, 4, 64), (8, 4, 128))

LANES = 128
VMEM_LIMIT_BYTES = 56 * 1024 * 1024
NEG_BIG = -1e30
LOG2E = 1.4426950408889634

QKV_CHUNK = 512
GLB_TQ = 512
GLB_CHUNK = 256
SWA_TQ = 256
NA_CTX_ROWS = 1024
NA_UNROLL = 32
MOE_TM = 1024
MOE_CHUNKS = 4
MOE_FIRST_CHUNK_DIV = 16
CAST_BLOCK_BYTES = 8 * 1024 * 1024
FFN_TF = 512


def _params(*sem):
    return pltpu.CompilerParams(dimension_semantics=sem, vmem_limit_bytes=VMEM_LIMIT_BYTES)


def _row_tile(n_lat_per_sample, n_ctx_rows, cap):
    for tm in (1024, 512, 256, 128):
        if tm <= cap and n_lat_per_sample % tm == 0 and n_ctx_rows % tm == 0:
            return tm
    raise ValueError("no row tile fits")


def _split_bf16(a):
    hi = a.astype(BF16)
    lo = (a - hi.astype(F32)).astype(BF16)
    return hi, lo


def _dot(a, b):
    return jnp.dot(a, b, preferred_element_type=F32)


def _dot_t(a, b):
    return lax.dot_general(a, b, (((1,), (1,)), ((), ())), preferred_element_type=F32)


def _silu(g):
    return g / (1.0 + jnp.exp(-g))


def _rms_mod(x, gain, scale, shift):
    ms = jnp.mean(x * x, axis=-1, keepdims=True)
    return x * lax.rsqrt(ms + NORM_EPS) * gain * (1.0 + scale) + shift


def _adaln_kernel(c_ref, w_ref, b_ref, o_ref):
    a_hi, a_lo = _split_bf16(_silu(c_ref[...]))
    w_hi, w_lo = _split_bf16(w_ref[...])
    o_ref[...] = _dot(a_hi, w_hi) + _dot(a_hi, w_lo) + _dot(a_lo, w_hi) + b_ref[...]


def _adaln(cond, w_mod, b_mod):
    r, d = cond.shape
    n = w_mod.shape[1]
    tn = 1536
    out = pl.pallas_call(
        _adaln_kernel,
        grid=(n // tn,),
        in_specs=[pl.BlockSpec((r, d), lambda j: (0, 0)),
                  pl.BlockSpec((d, tn), lambda j: (0, j)),
                  pl.BlockSpec((1, tn), lambda j: (0, j))],
        out_specs=pl.BlockSpec((r, tn), lambda j: (0, j)),
        out_shape=jax.ShapeDtypeStruct((r, n), F32),
        compiler_params=_params("arbitrary"),
        name="adaln",
    )(cond, w_mod, b_mod.reshape(1, n))
    return out.reshape(r, 6, d)


def _rot_half(z, dh):
    if dh == LANES:
        return pltpu.roll(z, LANES // 2, axis=1)
    lane = lax.broadcasted_iota(jnp.int32, z.shape, 1)
    from_right = pltpu.roll(z, LANES - dh // 2, axis=1)
    from_left = pltpu.roll(z, dh // 2, axis=1)
    return jnp.where((lane % dh) < dh // 2, from_right, from_left)


def _head_mean_sq(z, dh):
    z2 = z * z
    if dh == LANES:
        return jnp.broadcast_to(jnp.sum(z2, axis=-1, keepdims=True), z.shape) * (1.0 / dh)
    low = lax.broadcasted_iota(jnp.int32, z.shape, 1) < dh
    s_low = jnp.sum(jnp.where(low, z2, 0.0), axis=-1, keepdims=True)
    s_high = jnp.sum(jnp.where(low, 0.0, z2), axis=-1, keepdims=True)
    return jnp.where(low, s_low, s_high) * (1.0 / dh)


def _row_sources(arrays, tm, n_lat_tiles):
    lat, ctx = arrays
    return [pl.BlockSpec((tm, lat.shape[1]), lambda i: (jnp.minimum(i, n_lat_tiles - 1), 0)),
            pl.BlockSpec((tm, ctx.shape[1]), lambda i: (jnp.maximum(i - n_lat_tiles, 0), 0))]


def _pick_rows(refs, n_lat_tiles):
    if len(refs) == 1:
        return refs[0][...]
    return jnp.where(pl.program_id(0) < n_lat_tiles, refs[0][...], refs[1][...])


def _qkv_kernel(*refs, n_q, n_kv, dh, rope, n_x, n_lat_tiles):
    x = _pick_rows(refs[:n_x], n_lat_tiles)
    refs = refs[n_x:]
    if rope:
        mod_ref, n1_ref, w_ref, g_ref, cos_ref, sin_ref, q_ref, k_ref, v_ref = refs
    else:
        mod_ref, n1_ref, w_ref, g_ref, q_ref, k_ref, v_ref = refs
    h = _rms_mod(x, n1_ref[...], mod_ref[1:2, :], mod_ref[0:1, :]).astype(BF16)
    n_qk = n_q + n_kv
    cw = QKV_CHUNK
    for c in range((n_qk + n_kv) // cw):
        y = _dot(h, w_ref[:, c * cw:(c + 1) * cw])
        for s in range(cw // LANES):
            col = c * cw + s * LANES
            z = y[:, s * LANES:(s + 1) * LANES]
            if col < n_qk:
                z = z * lax.rsqrt(_head_mean_sq(z, dh) + NORM_EPS) * g_ref[:, col:col + LANES]
                if rope:
                    z = z * cos_ref[...] + _rot_half(z, dh) * sin_ref[...]
            z = z.astype(BF16)
            if col < n_q:
                q_ref[:, col:col + LANES] = z
            elif col < n_qk:
                k_ref[:, col - n_q:col - n_q + LANES] = z
            else:
                v_ref[:, col - n_qk:col - n_qk + LANES] = z


def _rope_tables(s_len, dh, tm):
    n_freq = dh // 4
    inv_freq = ROPE_THETA ** (-jnp.arange(n_freq, dtype=F32) / n_freq)
    t = jnp.arange(s_len)
    row = (t // GRID_W).astype(F32)
    col = (t % GRID_W).astype(F32)
    ang = jnp.concatenate([row[:, None] * inv_freq, col[:, None] * inv_freq], axis=-1)
    cos, sin = jnp.cos(ang), jnp.sin(ang)
    reps = LANES // dh
    cos_t = jnp.tile(jnp.concatenate([cos, cos], axis=-1), (1, reps))
    sin_t = jnp.tile(jnp.concatenate([-sin, sin], axis=-1), (1, reps))
    cos_t = jnp.concatenate([cos_t, jnp.ones((tm, LANES), F32)], axis=0)
    sin_t = jnp.concatenate([sin_t, jnp.zeros((tm, LANES), F32)], axis=0)
    return cos_t, sin_t


def _qkv(xs, mod, norm1, w_qkv, q_gain, k_gain, dims, mixer):
    b, s_len, n_ctx = dims
    x_parts = list(xs) if isinstance(xs, tuple) else [xs]
    n_rows, d = sum(a.shape[0] for a in x_parts), x_parts[0].shape[1]
    n_qh, n_kvh, dh = MIXER_HEADS[mixer]
    n_q, n_kv = n_qh * dh, n_kvh * dh
    rope = mixer != 0
    tm = _row_tile(s_len, b * n_ctx, 512)
    n_lat_tiles = b * s_len // tm
    tiles_per_sample = s_len // tm

    gains = jnp.concatenate([jnp.tile(q_gain * (dh ** -0.5 * LOG2E), n_qh),
                             jnp.tile(k_gain, n_kvh)]).reshape(1, n_q + n_kv)

    def mod_idx(i):
        return jnp.where(i < n_lat_tiles, i // tiles_per_sample, b)

    x_specs = [pl.BlockSpec((tm, d), lambda i: (i, 0))] if len(x_parts) == 1 else _row_sources(x_parts, tm, n_lat_tiles)
    in_specs = x_specs + [pl.BlockSpec((None, 6, d), lambda i: (mod_idx(i), 0, 0)),
                          pl.BlockSpec((1, d), lambda i: (0, 0)),
                          pl.BlockSpec(w_qkv.shape, lambda i: (0, 0)),
                          pl.BlockSpec(gains.shape, lambda i: (0, 0))]
    args = x_parts + [mod, norm1.reshape(1, d), w_qkv.astype(BF16), gains]
    if rope:
        cos_t, sin_t = _rope_tables(s_len, dh, tm)

        def pos_idx(i):
            return jnp.where(i < n_lat_tiles, i % tiles_per_sample, tiles_per_sample)

        in_specs += [pl.BlockSpec((tm, LANES), lambda i: (pos_idx(i), 0))] * 2
        args += [cos_t, sin_t]
    return pl.pallas_call(
        functools.partial(_qkv_kernel, n_q=n_q, n_kv=n_kv, dh=dh, rope=rope, n_x=len(x_parts),
                          n_lat_tiles=n_lat_tiles),
        grid=(n_rows // tm,),
        in_specs=in_specs,
        out_specs=[pl.BlockSpec((tm, n_q), lambda i: (i, 0)),
                   pl.BlockSpec((tm, n_kv), lambda i: (i, 0)),
                   pl.BlockSpec((tm, n_kv), lambda i: (i, 0))],
        out_shape=[jax.ShapeDtypeStruct((n_rows, n_q), BF16),
                   jax.ShapeDtypeStruct((n_rows, n_kv), BF16),
                   jax.ShapeDtypeStruct((n_rows, n_kv), BF16)],
        compiler_params=_params("arbitrary"),
        name="qkv_m%d" % mixer,
    )(*args)


def _stack_heads(q, dh, nh):
    if dh == LANES:
        return jnp.concatenate([q[:, h * LANES:(h + 1) * LANES] for h in range(nh)], axis=0)
    per_kv = nh // 2
    tq = q.shape[0]
    lane = lax.broadcasted_iota(jnp.int32, (tq, LANES), 1)
    ops = []
    for h in range(nh):
        slot = h // per_kv
        chunk = q[:, (h // 2) * LANES:(h // 2 + 1) * LANES].astype(F32)
        if h % 2 != slot:
            chunk = pltpu.roll(chunk, dh, axis=1)
        keep = lane < dh if slot == 0 else lane >= dh
        ops.append(jnp.where(keep, chunk, 0.0).astype(BF16))
    return jnp.concatenate(ops, axis=0)


def _unstack_heads(o, dh, nh):
    tq = o.shape[0] // nh
    if dh == LANES:
        return jnp.concatenate([o[h * tq:(h + 1) * tq] for h in range(nh)], axis=1)
    per_kv = nh // 2
    lane = lax.broadcasted_iota(jnp.int32, (tq, LANES), 1)
    chunks = []
    for c in range(nh // 2):
        parts = []
        for h in (2 * c, 2 * c + 1):
            oh = o[h * tq:(h + 1) * tq]
            if h % 2 != h // per_kv:
                oh = pltpu.roll(oh, dh, axis=1)
            parts.append(oh)
        chunks.append(jnp.where(lane < dh, parts[0], parts[1]))
    return jnp.concatenate(chunks, axis=1)


def _lanes(x, n):
    return x if n == LANES else jnp.concatenate([x] * (n // LANES), axis=1)


def _block_max(s):
    return functools.reduce(jnp.maximum, [s[:, j * LANES:(j + 1) * LANES] for j in range(s.shape[1] // LANES)])


def _row_max(blk):
    return jnp.broadcast_to(jnp.max(blk, axis=-1, keepdims=True), blk.shape)


def _dot_row_halves(p, v):
    half = p.shape[0] // 2
    return jnp.concatenate([_dot(p[0:half, :], v), _dot(p[half:, :], v)], axis=0)


def _ones_ext(v):
    return jnp.concatenate([v, jnp.ones_like(v)], axis=1)


def _sink_rows(sink_ref, first_head, nh, tq):
    return jnp.concatenate([jnp.full((tq, LANES), sink_ref[first_head + h] * LOG2E, F32) for h in range(nh)], axis=0)


def _na_kernel(q_ref, k_ref, v_ref, kc_ref, vc_ref, bias_ref, o_ref,
               qs_ref, vext_ref, mc_ref, numc_ref, lc_ref, *, rows):
    s_len = q_ref.shape[0]
    band = NA_WIN_ROWS * GRID_W
    lane = lax.broadcasted_iota(jnp.int32, (s_len, LANES), 1)
    q = q_ref[...]
    qs_ref[0:s_len, :] = jnp.where(lane < 64, q, jnp.zeros_like(q))
    qs_ref[s_len:, :] = jnp.where(lane >= 64, q, jnp.zeros_like(q))
    vext_ref[:, 0:LANES] = v_ref[...]
    vext_ref[:, LANES:] = jnp.ones((s_len, LANES), BF16)

    kc = kc_ref[...]
    vcx = _ones_ext(vc_ref[...])
    for c in range(2 * s_len // NA_CTX_ROWS):
        rs = slice(c * NA_CTX_ROWS, (c + 1) * NA_CTX_ROWS)
        s = _dot_t(qs_ref[rs, :], kc)
        m = _row_max(_block_max(s))
        acc = _dot(jnp.exp2(s - _lanes(m, s.shape[1])).astype(BF16), vcx)
        mc_ref[rs, :] = m
        numc_ref[rs, :] = acc[:, :LANES]
        lc_ref[rs, :] = acc[:, LANES:]

    lane_q = lax.broadcasted_iota(jnp.int32, (GRID_W, LANES), 1)

    def both_heads(ref, q0):
        return jnp.concatenate([ref[pl.ds(q0, GRID_W), :], ref[pl.ds(s_len + q0, GRID_W), :]], axis=0)

    def one_row(r, carry):
        r0 = jnp.clip(r - NA_WIN_ROWS // 2, 0, rows - NA_WIN_ROWS)
        q0 = pl.multiple_of(r * GRID_W, GRID_W)
        k0 = pl.multiple_of(r0 * GRID_W, GRID_W)
        d0 = r0 - r + NA_WIN_ROWS - 1
        bias = jnp.concatenate(
            [jnp.concatenate([bias_ref[0, d0 + 2 * j], bias_ref[1, d0 + 2 * j]], axis=0)
             for j in range(NA_WIN_ROWS // 2)], axis=1)
        qs = both_heads(qs_ref, q0)
        half = band // 2
        k1 = pl.multiple_of(k0 + half, GRID_W)
        s = jnp.concatenate([_dot_t(qs, k_ref[pl.ds(k0, half), :]), _dot_t(qs, k_ref[pl.ds(k1, half), :])],
                            axis=1) + bias
        mc = both_heads(mc_ref, q0)
        m = jnp.maximum(_row_max(_block_max(s)), mc)
        acc = _dot(jnp.exp2(s - _lanes(m, band)).astype(BF16), vext_ref[pl.ds(k0, band), :])
        alpha = jnp.exp2(mc - m)
        num = acc[:, :LANES] + alpha * both_heads(numc_ref, q0)
        den = acc[:, LANES:] + alpha * both_heads(lc_ref, q0)
        o = num / den
        o_ref[pl.ds(q0, GRID_W), :] = jnp.where(lane_q < 64, o[:GRID_W], o[GRID_W:]).astype(BF16)
        return carry

    lax.fori_loop(0, rows, one_row, 0, unroll=NA_UNROLL)


def _na_bias_tables(rel_bias):
    n_h = rel_bias.shape[0]
    col = jnp.arange(GRID_W)
    col_start = jnp.clip(col - NA_WIN_COLS // 2, 0, GRID_W - NA_WIN_COLS)
    in_win = (col[None, :] >= col_start[:, None]) & (col[None, :] < col_start[:, None] + NA_WIN_COLS)
    col_idx = jnp.clip(col[None, :] - col[:, None] + NA_WIN_COLS - 1, 0, 2 * NA_WIN_COLS - 2)
    masked = jnp.where(in_win[None, None], rel_bias[:, :, col_idx] * LOG2E, NEG_BIG)
    pairs = jnp.concatenate([masked[:, :-1], masked[:, 1:]], axis=-1)
    return pairs.reshape((n_h // 2, 2) + pairs.shape[1:])


def _na_attention(q, k, v, rel_bias, dims):
    b, s_len, n_ctx = dims
    rows = s_len // GRID_W
    n_pairs = q.shape[1] // LANES
    bias = _na_bias_tables(rel_bias)
    ctx_blk = b * s_len // n_ctx
    lat = pl.BlockSpec((s_len, LANES), lambda p, i: (i, p))
    ctx = pl.BlockSpec((n_ctx, LANES), lambda p, i: (ctx_blk + i, p))
    return pl.pallas_call(
        functools.partial(_na_kernel, rows=rows),
        grid=(n_pairs, b),
        in_specs=[lat, lat, lat, ctx, ctx,
                  pl.BlockSpec((None,) + bias.shape[1:], lambda p, i: (p, 0, 0, 0, 0))],
        out_specs=lat,
        out_shape=jax.ShapeDtypeStruct((b * s_len, q.shape[1]), BF16),
        scratch_shapes=[pltpu.VMEM((2 * s_len, LANES), BF16),
                        pltpu.VMEM((s_len, 2 * LANES), BF16),
                        pltpu.VMEM((2 * s_len, LANES), F32),
                        pltpu.VMEM((2 * s_len, LANES), F32),
                        pltpu.VMEM((2 * s_len, LANES), F32)],
        compiler_params=_params("arbitrary", "arbitrary"),
        name="attn_na",
    )(q, k, v, k, v, bias)


def _swa_kernel(sink_ref, q_ref, k_ref, v_ref, kc_ref, vc_ref, o_ref, kwin_ref, vwin_ref, *, nh, n_units, s_len):
    tq = q_ref.shape[0]
    span = tq + 2 * SW_WINDOW
    p_blk, t = pl.program_id(1), pl.program_id(2)

    @pl.when(t == 0)
    def _():
        kwin_ref[span:, :] = kc_ref[...]
        vwin_ref[span:, 0:LANES] = vc_ref[...]
        vwin_ref[:, LANES:] = jnp.ones((vwin_ref.shape[0], LANES), BF16)

    q0 = t * tq
    k0 = pl.multiple_of(jnp.clip(q0 - SW_WINDOW, 0, s_len - span), SW_WINDOW)
    kwin_ref[0:span, :] = k_ref[pl.ds(k0, span), :]
    vwin_ref[0:span, 0:LANES] = v_ref[pl.ds(k0, span), :]

    row = lax.broadcasted_iota(jnp.int32, (tq, span), 0)
    col = lax.broadcasted_iota(jnp.int32, (tq, span), 1)
    bias = jnp.where(jnp.abs(col - row + (k0 - q0)) <= SW_WINDOW, 0.0, NEG_BIG)

    qs_all = _stack_heads(q_ref[...], 64, nh)
    sink_all = _sink_rows(sink_ref, p_blk * nh, nh, tq)
    hu = nh // n_units
    vwin = vwin_ref[...]
    kd = 2 * LANES
    outs = []
    for u in range(n_units):
        qs = qs_all[u * hu * tq:(u + 1) * hu * tq]
        sink = sink_all[u * hu * tq:(u + 1) * hu * tq]
        s = jnp.concatenate([_dot_t(qs, kwin_ref[j * kd:(j + 1) * kd, :]) for j in range(kwin_ref.shape[0] // kd)],
                            axis=1)
        s = jnp.concatenate([(s[:, :span].reshape(hu, tq, span) + bias[None]).reshape(hu * tq, span),
                             s[:, span:]], axis=1)
        m = jnp.maximum(_row_max(_block_max(s)), sink)
        acc = _dot(jnp.exp2(s - _lanes(m, s.shape[1])).astype(BF16), vwin)
        outs.append(acc[:, :LANES] / (acc[:, LANES:] + jnp.exp2(sink - m)))
    o_ref[...] = _unstack_heads(jnp.concatenate(outs, axis=0), 64, nh).astype(BF16)


def _swa_attention(q, k, v, sink, dims):
    b, s_len, n_ctx = dims
    tq = SWA_TQ
    n_t = s_len // tq
    n_kvblk = k.shape[1] // LANES
    nh = q.shape[1] // k.shape[1] * 2
    qw = nh * 64
    ctx_blk = b * s_len // n_ctx
    n_win = tq + 2 * SW_WINDOW + n_ctx
    assert n_win % (2 * LANES) == 0
    qspec = pl.BlockSpec((tq, qw), lambda i, p, t, *_: (i * n_t + t, p))
    lat = pl.BlockSpec((s_len, LANES), lambda i, p, t, *_: (i, p))
    ctx = pl.BlockSpec((n_ctx, LANES), lambda i, p, t, *_: (ctx_blk + i, p))
    return pl.pallas_call(
        functools.partial(_swa_kernel, nh=nh, n_units=nh, s_len=s_len),
        grid_spec=pltpu.PrefetchScalarGridSpec(
            num_scalar_prefetch=1,
            grid=(b, n_kvblk, n_t),
            in_specs=[qspec, lat, lat, ctx, ctx],
            out_specs=qspec,
            scratch_shapes=[pltpu.VMEM((n_win, LANES), BF16),
                            pltpu.VMEM((n_win, 2 * LANES), BF16)]),
        out_shape=jax.ShapeDtypeStruct((b * s_len, q.shape[1]), BF16),
        compiler_params=_params("arbitrary", "arbitrary", "arbitrary"),
        name="attn_swa",
    )(sink, q, k, v, k, v)


def _global_kernel(q_ref, k_ref, v_ref, kc_ref, vc_ref, o_ref, kall_ref, vext_ref, s_ref, p_ref, *, nh, s_len):
    @pl.when(pl.program_id(2) == 0)
    def _():
        kall_ref[0:s_len, :] = k_ref[...]
        kall_ref[s_len:, :] = kc_ref[...]
        vext_ref[0:s_len, 0:LANES] = v_ref[...]
        vext_ref[s_len:, 0:LANES] = vc_ref[...]
        vext_ref[:, LANES:] = jnp.ones((vext_ref.shape[0], LANES), BF16)

    qs = _stack_heads(q_ref[...], LANES, nh)
    n_chunks = kall_ref.shape[0] // GLB_CHUNK
    mrun = None
    for c in range(n_chunks):
        ks = slice(c * GLB_CHUNK, (c + 1) * GLB_CHUNK)
        s = _dot_t(qs, kall_ref[ks, :])
        s_ref[:, ks] = s
        blk = _block_max(s)
        mrun = blk if mrun is None else jnp.maximum(mrun, blk)
    m = _lanes(_row_max(mrun), GLB_CHUNK)
    for c in range(n_chunks):
        ks = slice(c * GLB_CHUNK, (c + 1) * GLB_CHUNK)
        p_ref[:, ks] = jnp.exp2(s_ref[:, ks] - m).astype(BF16)
    acc = _dot_row_halves(p_ref, vext_ref[...])
    o_ref[...] = _unstack_heads(acc[:, :LANES] / acc[:, LANES:], LANES, nh).astype(BF16)


def _global_attention(q, k, v, dims):
    b, s_len, n_ctx = dims
    tq = GLB_TQ
    n_t = s_len // tq
    n_kvblk = k.shape[1] // LANES
    nh = q.shape[1] // k.shape[1]
    qw = nh * LANES
    ctx_blk = b * s_len // n_ctx
    n_keys = s_len + n_ctx
    qspec = pl.BlockSpec((tq, qw), lambda i, p, t: (i * n_t + t, p))
    lat = pl.BlockSpec((s_len, LANES), lambda i, p, t: (i, p))
    ctx = pl.BlockSpec((n_ctx, LANES), lambda i, p, t: (ctx_blk + i, p))
    return pl.pallas_call(
        functools.partial(_global_kernel, nh=nh, s_len=s_len),
        grid=(b, n_kvblk, n_t),
        in_specs=[qspec, lat, lat, ctx, ctx],
        out_specs=qspec,
        out_shape=jax.ShapeDtypeStruct((b * s_len, q.shape[1]), BF16),
        scratch_shapes=[pltpu.VMEM((n_keys, LANES), BF16),
                        pltpu.VMEM((n_keys, 2 * LANES), BF16),
                        pltpu.VMEM((nh * tq, n_keys), F32),
                        pltpu.VMEM((nh * tq, n_keys), BF16)],
        compiler_params=_params("arbitrary", "arbitrary", "arbitrary"),
        name="attn_global",
    )(q, k, v, k, v)


def _ctx_kernel(sink_ref, q_ref, kc_ref, vc_ref, o_ref, *, dh, nh, use_sink):
    tq = q_ref.shape[0]
    qw = nh * dh
    for p in range(kc_ref.shape[1] // LANES):
        kv = slice(p * LANES, (p + 1) * LANES)
        qs = _stack_heads(q_ref[:, p * qw:(p + 1) * qw], dh, nh)
        s = _dot_t(qs, kc_ref[:, kv])
        m = _row_max(_block_max(s))
        if use_sink:
            sink = _sink_rows(sink_ref, p * nh, nh, tq)
            m = jnp.maximum(m, sink)
        acc = _dot(jnp.exp2(s - _lanes(m, s.shape[1])).astype(BF16), _ones_ext(vc_ref[:, kv]))
        den = acc[:, LANES:]
        if use_sink:
            den = den + jnp.exp2(sink - m)
        o_ref[:, p * qw:(p + 1) * qw] = _unstack_heads(acc[:, :LANES] / den, dh, nh).astype(BF16)


def _ctx_attention(q, k, v, sink, dims, mixer):
    b, s_len, n_ctx = dims
    n_qh, n_kvh, dh = MIXER_HEADS[mixer]
    n_kvblk = k.shape[1] // LANES
    nh = n_qh // n_kvblk
    ctx_blk = b * s_len // n_ctx
    use_sink = sink is not None
    if not use_sink:
        sink = jnp.zeros((n_qh,), F32)
    return pl.pallas_call(
        functools.partial(_ctx_kernel, dh=dh, nh=nh, use_sink=use_sink),
        grid_spec=pltpu.PrefetchScalarGridSpec(
            num_scalar_prefetch=1,
            grid=(b,),
            in_specs=[pl.BlockSpec((n_ctx, q.shape[1]), lambda i, *_: (ctx_blk + i, 0)),
                      pl.BlockSpec((n_ctx, k.shape[1]), lambda i, *_: (ctx_blk + i, 0)),
                      pl.BlockSpec((n_ctx, v.shape[1]), lambda i, *_: (ctx_blk + i, 0))],
            out_specs=pl.BlockSpec((n_ctx, q.shape[1]), lambda i, *_: (i, 0))),
        out_shape=jax.ShapeDtypeStruct((b * n_ctx, q.shape[1]), BF16),
        compiler_params=_params("arbitrary"),
        name="attn_ctx_m%d" % mixer,
    )(sink, q, k, v)


def _oproj_kernel(*refs, n_lat_tiles, n_y, n_x, moe):
    y = _pick_rows(refs[:n_y], n_lat_tiles)
    x_in = _pick_rows(refs[n_y:n_y + n_x], n_lat_tiles)
    refs = refs[n_y + n_x:]
    mod_ref, n2_ref, wo_ref = refs[:3]
    refs = refs[3:]
    if moe:
        wr_ref, br_ref = refs[:2]
        refs = refs[2:]
    xo_ref, h_ref = refs[:2]
    x = x_in + mod_ref[2:3, :] * _dot(y, wo_ref[...])
    xo_ref[...] = x
    h = _rms_mod(x, n2_ref[...], mod_ref[4:5, :], mod_ref[3:4, :])
    h_ref[...] = h.astype(BF16)
    if moe:
        route_ref, counts_ref, base_ref = refs[2:5]
        h_hi, h_lo = _split_bf16(h)
        w_hi, w_lo = _split_bf16(wr_ref[...])
        hi_terms = _dot(h_hi, jnp.concatenate([w_hi, w_lo], axis=1))
        lo_terms = _dot(h_lo, jnp.concatenate([w_hi, jnp.zeros_like(w_hi)], axis=1))
        logits = hi_terms[:, :LANES] + hi_terms[:, LANES:] + lo_terms[:, :LANES] + br_ref[...]
        lane = lax.broadcasted_iota(jnp.int32, logits.shape, 1)
        v1 = jnp.max(logits, axis=-1, keepdims=True)
        i1 = jnp.min(jnp.where(logits == v1, lane, LANES), axis=-1, keepdims=True)
        rest = jnp.where(lane == i1, NEG_BIG, logits)
        v2 = jnp.max(rest, axis=-1, keepdims=True)
        i2 = jnp.min(jnp.where(rest == v2, lane, LANES), axis=-1, keepdims=True)
        e = jnp.exp(v2 - v1)
        w1 = 1.0 / (1.0 + e)
        w2 = e / (1.0 + e)
        @pl.when(pl.program_id(0) == 0)
        def _():
            base_ref[...] = jnp.zeros_like(base_ref)

        tm = logits.shape[0]
        pick1 = lane == i1
        pick2 = lane == i2
        cnt = jnp.where(pick1, 1.0, 0.0) + jnp.where(pick2, 1.0, 0.0)
        earlier = lax.broadcasted_iota(jnp.int32, (tm, tm), 0) > lax.broadcasted_iota(jnp.int32, (tm, tm), 1)
        before = _dot(jnp.where(earlier, 1.0, 0.0).astype(BF16), cnt.astype(BF16)) + base_ref[0:1, :]
        r1 = jnp.sum(jnp.where(pick1, before, 0.0), axis=-1, keepdims=True)
        r2 = jnp.sum(jnp.where(pick2, before, 0.0), axis=-1, keepdims=True)
        base_ref[...] = base_ref[...] + jnp.sum(cnt, axis=0, keepdims=True)
        counts_ref[...] = base_ref[...]
        route = jnp.where(lane == 0, i1.astype(F32), 0.0)
        route = jnp.where(lane == 1, i2.astype(F32), route)
        route = jnp.where(lane == 2, w1, route)
        route = jnp.where(lane == 3, w2, route)
        route = jnp.where(lane == 4, r1, route)
        route = jnp.where(lane == 5, r2, route)
        route_ref[...] = route


def _oproj(y_lat, y_ctx, xs, mod, norm2, w_o, router, dims, last):
    b, s_len, n_ctx = dims
    x_parts = list(xs) if isinstance(xs, tuple) else [xs]
    d = x_parts[0].shape[1]
    tm = _row_tile(s_len, b * n_ctx, 512)
    n_lat_tiles = b * s_len // tm
    tiles_per_sample = s_len // tm
    n_rows = b * s_len if last else b * (s_len + n_ctx)
    n_tiles = n_rows // tm
    moe = router is not None

    def mod_idx(i):
        return jnp.where(i < n_lat_tiles, i // tiles_per_sample, b)

    row = pl.BlockSpec((tm, d), lambda i: (i, 0))
    y_parts = [y_lat] if last else [y_lat, y_ctx]
    in_specs = ([pl.BlockSpec((tm, y_lat.shape[1]), lambda i: (i, 0))] if last
                else _row_sources(y_parts, tm, n_lat_tiles))
    in_specs += [row] if len(x_parts) == 1 else _row_sources(x_parts, tm, n_lat_tiles)
    in_specs += [pl.BlockSpec((None, 6, d), lambda i: (mod_idx(i), 0, 0)),
                 pl.BlockSpec((1, d), lambda i: (0, 0)),
                 pl.BlockSpec(w_o.shape, lambda i: (0, 0))]
    args = y_parts + x_parts + [mod, norm2.reshape(1, d), w_o.astype(BF16)]
    out_specs = [row, row]
    out_shape = [jax.ShapeDtypeStruct((n_rows, d), F32), jax.ShapeDtypeStruct((n_rows, d), BF16)]
    if moe:
        w_router, b_router = router
        n_e = w_router.shape[1]
        wr = jnp.pad(w_router, ((0, 0), (0, LANES - n_e)))
        br = jnp.pad(b_router, (0, LANES - n_e), constant_values=NEG_BIG).reshape(1, LANES)
        in_specs += [pl.BlockSpec(wr.shape, lambda i: (0, 0)), pl.BlockSpec(br.shape, lambda i: (0, 0))]
        args += [wr, br]
        out_specs += [pl.BlockSpec((tm, LANES), lambda i: (i, 0)), pl.BlockSpec((8, LANES), lambda i: (0, 0))]
        out_shape += [jax.ShapeDtypeStruct((n_rows, LANES), F32), jax.ShapeDtypeStruct((8, LANES), F32)]
    return pl.pallas_call(
        functools.partial(_oproj_kernel, n_lat_tiles=n_lat_tiles, n_y=len(y_parts), n_x=len(x_parts), moe=moe),
        grid=(n_tiles,),
        in_specs=in_specs,
        out_specs=out_specs,
        out_shape=out_shape,
        scratch_shapes=[pltpu.VMEM((8, LANES), F32)] if moe else [],
        compiler_params=_params("arbitrary"),
        name="oproj_moe" if moe else "oproj",
    )(*args)


def _swiglu_partial(h_ref, wg_ref, wu_ref, wd_ref):
    h = h_ref[...]
    a = _silu(_dot(h, wg_ref[...].astype(BF16))) * _dot(h, wu_ref[...].astype(BF16))
    return _dot(a.astype(BF16), wd_ref[...].astype(BF16))


def _accumulate_over_hidden(f, n_f, acc_ref, partial, finish):
    @pl.when(f == 0)
    def _():
        acc_ref[...] = partial()

    @pl.when(jnp.logical_and(f > 0, f < n_f - 1))
    def _():
        acc_ref[...] += partial()

    @pl.when(f == n_f - 1)
    def _():
        finish(acc_ref[...] + partial())


def _ffn_kernel(h_ref, wg_ref, wu_ref, wd_ref, x_ref, mod_ref, o_ref, acc_ref):
    def finish(total):
        o_ref[...] = x_ref[...] + mod_ref[5:6, :] * total

    _accumulate_over_hidden(pl.program_id(1), pl.num_programs(1), acc_ref,
                            functools.partial(_swiglu_partial, h_ref, wg_ref, wu_ref, wd_ref), finish)


def _ffn(h, xs, mod, w_in, w_out, dims):
    b, s_len, n_ctx = dims
    n_rows, d = h.shape
    d_ff = w_out.shape[0]
    tm = _row_tile(s_len, b * n_ctx, 1024)
    tf = FFN_TF
    n_f = d_ff // tf
    n_lat_tiles = b * s_len // tm
    tiles_per_sample = s_len // tm

    def mod_idx(i):
        return jnp.where(i < n_lat_tiles, i // tiles_per_sample, b)

    w_in = _to_bf16(w_in[None])[0]
    w_out = _to_bf16(w_out[None])[0]
    return pl.pallas_call(
        _ffn_kernel,
        grid=(n_rows // tm, n_f),
        in_specs=[pl.BlockSpec((tm, d), lambda i, f: (i, 0)),
                  pl.BlockSpec((d, tf), lambda i, f: (0, f)),
                  pl.BlockSpec((d, tf), lambda i, f: (0, n_f + f)),
                  pl.BlockSpec((tf, d), lambda i, f: (f, 0)),
                  pl.BlockSpec((tm, d), lambda i, f: (i, 0)),
                  pl.BlockSpec((None, 6, d), lambda i, f: (mod_idx(i), 0, 0))],
        out_specs=pl.BlockSpec((tm, d), lambda i, f: (i, 0)),
        out_shape=jax.ShapeDtypeStruct((n_rows, d), F32),
        scratch_shapes=[pltpu.VMEM((tm, d), F32)],
        compiler_params=_params("arbitrary", "arbitrary"),
        name="ffn_dense",
    )(h, w_in, w_in, w_out, xs, mod)


def _moe_ffn_kernel(te_ref, na_ref, h_ref, wg_ref, wu_ref, wd_ref, *rest):
    o_ref, acc_ref = rest[-2:]
    i, f = pl.program_id(0), pl.program_id(1)

    def finish(total):
        o_ref[...] = total.astype(o_ref.dtype)

    @pl.when(i < na_ref[0])
    def _():
        _accumulate_over_hidden(f, pl.num_programs(1), acc_ref,
                                functools.partial(_swiglu_partial, h_ref, wg_ref, wu_ref, wd_ref), finish)


def _moe_ffn(hs, ys, n_rows_total, first_tile, tile_expert, n_active, w_in, w_out):
    n_rows, d = hs.shape
    d_ff = w_out.shape[1]
    tm, tf = MOE_TM, FFN_TF
    n_f = d_ff // tf

    def tile(i, na):
        return jnp.minimum(i, jnp.maximum(na[0] - 1, 0))

    def expert(i, te, na):
        return te[tile(i, na)]

    in_specs = [pl.BlockSpec((tm, d), lambda i, f, te, na: (tile(i, na), 0)),
                pl.BlockSpec((None, d, tf), lambda i, f, te, na: (expert(i, te, na), 0, f)),
                pl.BlockSpec((None, d, tf), lambda i, f, te, na: (expert(i, te, na), 0, n_f + f)),
                pl.BlockSpec((None, tf, d), lambda i, f, te, na: (expert(i, te, na), f, 0))]
    args = [tile_expert, n_active, hs, w_in, w_in, w_out]
    aliases = {}
    if ys is not None:
        in_specs.append(pl.BlockSpec(memory_space=pl.ANY))
        aliases = {len(args): 0}
        args.append(ys)
    return pl.pallas_call(
        _moe_ffn_kernel,
        grid_spec=pltpu.PrefetchScalarGridSpec(
            num_scalar_prefetch=2,
            grid=(n_rows // tm, n_f),
            in_specs=in_specs,
            out_specs=pl.BlockSpec((tm, d), lambda i, f, te, na: (first_tile + tile(i, na), 0)),
            scratch_shapes=[pltpu.VMEM((tm, d), F32)]),
        out_shape=jax.ShapeDtypeStruct((n_rows_total, d), hs.dtype),
        input_output_aliases=aliases,
        compiler_params=_params("arbitrary", "arbitrary"),
        name="moe_ffn",
    )(*args)


def _combine_kernel(x_ref, a_ref, b_ref, route_ref, mod_ref, o_ref):
    w1 = route_ref[:, 2:3]
    w2 = route_ref[:, 3:4]
    mix = w1 * a_ref[...].astype(F32) + w2 * b_ref[...].astype(F32)
    o_ref[...] = x_ref[...] + mod_ref[5:6, :] * mix


def _combine(xs, ya, yb, route, mod, dims):
    b, s_len, n_ctx = dims
    n_rows, d = ya.shape
    tm = _row_tile(s_len, b * n_ctx, 512)
    n_lat_tiles = b * s_len // tm
    tiles_per_sample = s_len // tm

    def mod_idx(i):
        return jnp.where(i < n_lat_tiles, i // tiles_per_sample, b)

    row = pl.BlockSpec((tm, d), lambda i: (i, 0))
    return pl.pallas_call(
        _combine_kernel,
        grid=(n_rows // tm,),
        in_specs=[row, row, row,
                  pl.BlockSpec((tm, LANES), lambda i: (i, 0)),
                  pl.BlockSpec((None, 6, d), lambda i: (mod_idx(i), 0, 0))],
        out_specs=row,
        out_shape=jax.ShapeDtypeStruct((n_rows, d), F32),
        compiler_params=_params("arbitrary"),
        name="moe_combine",
    )(xs, ya, yb, route, mod)


def _route_plan(idx, rank, counts, tm):
    n = idx.shape[0]
    tiles_per = (counts + tm - 1) // tm
    tile_end = jnp.cumsum(tiles_per)
    tile_start = tile_end - tiles_per
    experts = jnp.arange(N_EXPERTS, dtype=jnp.int32)
    start = jnp.sum(jnp.where(idx[:, :, None] == experts, tile_start, 0), axis=-1)
    slot = start * tm + rank
    n_tiles = (2 * n) // tm + N_EXPERTS
    tile_ids = jnp.arange(n_tiles, dtype=jnp.int32)
    tile_expert = jnp.minimum(jnp.sum((tile_end[None, :] <= tile_ids[:, None]).astype(jnp.int32), axis=1),
                              N_EXPERTS - 1)
    token = jnp.arange(2 * n, dtype=jnp.int32) // 2
    _, token_sorted = lax.sort((slot.reshape(-1), token), num_keys=1)
    first_pair = jnp.cumsum(counts) - counts
    row_in_expert = ((tile_ids - tile_start[tile_expert]) * tm)[:, None] + jnp.arange(tm, dtype=jnp.int32)[None, :]
    pair = jnp.clip(first_pair[tile_expert][:, None] + row_in_expert, 0, 2 * n - 1)
    src = jnp.where(row_in_expert < counts[tile_expert][:, None], token_sorted[pair], 0).reshape(-1)
    return slot, src, tile_expert, tile_end[-1:].astype(jnp.int32)


def _cast_kernel(w_ref, o_ref):
    o_ref[...] = w_ref[...].astype(o_ref.dtype)


def _to_bf16(w):
    n_e, n_r, n_c = w.shape
    rows = n_r
    while rows * n_c * 4 > CAST_BLOCK_BYTES and rows % 2 == 0 and (rows // 2) % 16 == 0:
        rows //= 2
    return pl.pallas_call(
        _cast_kernel,
        grid=(n_e, n_r // rows),
        in_specs=[pl.BlockSpec((None, rows, n_c), lambda e, r: (e, r, 0))],
        out_specs=pl.BlockSpec((None, rows, n_c), lambda e, r: (e, r, 0)),
        out_shape=jax.ShapeDtypeStruct(w.shape, BF16),
        compiler_params=_params("arbitrary", "arbitrary"),
        name="cast_bf16",
    )(w)


def _take_rows(a, rows):
    return a.at[rows].get(mode="promise_in_bounds")


def _moe(h, xs, route, counts, mod, w_exp_in, w_exp_out, dims):
    idx = route[:, 0:2].astype(jnp.int32)
    rank = route[:, 4:6].astype(jnp.int32)
    slot, src, tile_expert, n_active = _route_plan(idx, rank, counts[0, :N_EXPERTS].astype(jnp.int32), MOE_TM)
    n_tiles = src.shape[0] // MOE_TM
    first = max(1, n_tiles // MOE_FIRST_CHUNK_DIV)
    rest = -(-(n_tiles - first) // (MOE_CHUNKS - 1))
    bounds = [0] + [min(first + j * rest, n_tiles) for j in range(MOE_CHUNKS)]
    w_exp_in, w_exp_out = _to_bf16(w_exp_in), _to_bf16(w_exp_out)
    ys = None
    for t0, t1 in zip(bounds[:-1], bounds[1:]):
        if t1 == t0:
            continue
        hs = _take_rows(h, src[t0 * MOE_TM:t1 * MOE_TM])
        ys = _moe_ffn(hs, ys, src.shape[0], t0, tile_expert[t0:t1], jnp.clip(n_active - t0, 0, t1 - t0),
                      w_exp_in, w_exp_out)
    ya = _take_rows(ys, slot[:, 0])
    yb = _take_rows(ys, slot[:, 1])
    return _combine(xs, ya, yb, route, mod, dims)


def _layer(xs, cond, p, dims, mixer, last):
    b, s_len, n_ctx = dims
    mod = _adaln(cond, p["w_mod"], p["b_mod"])
    q, k, v = _qkv(xs, mod, p["norm1"], p["w_qkv"], p["q_norm"], p["k_norm"], dims, mixer)
    if mixer == 0:
        y_lat = _na_attention(q, k, v, p["rel_bias"], dims)
    elif mixer == 1:
        y_lat = _swa_attention(q, k, v, p["sink"], dims)
    else:
        y_lat = _global_attention(q, k, v, dims)
    y_ctx = None if last else _ctx_attention(q, k, v, p.get("sink"), dims, mixer)
    router = (p["w_router"], p["b_router"]) if "w_router" in p else None
    outs = _oproj(y_lat, y_ctx, xs, mod, p["norm2"], p["w_o"], router, dims, last)
    if router is None:
        xs, h = outs
        return _ffn(h, xs, mod, p["w_ffn_in"], p["w_ffn_out"], dims)
    xs, h, route, counts = outs
    return _moe(h, xs, route, counts, mod, p["w_exp_in"], p["w_exp_out"], dims)


def kernel(x, c, ctx, c_ctx, l0_w_mod, l0_b_mod, l0_norm1, l0_norm2, l0_w_qkv, l0_q_norm, l0_k_norm, l0_rel_bias, l0_w_o, l0_w_ffn_in, l0_w_ffn_out, l1_w_mod, l1_b_mod, l1_norm1, l1_norm2, l1_w_qkv, l1_q_norm, l1_k_norm, l1_sink, l1_w_o, l1_w_router, l1_b_router, l1_w_exp_in, l1_w_exp_out, l2_w_mod, l2_b_mod, l2_norm1, l2_norm2, l2_w_qkv, l2_q_norm, l2_k_norm, l2_w_o, l2_w_ffn_in, l2_w_ffn_out, l3_w_mod, l3_b_mod, l3_norm1, l3_norm2, l3_w_qkv, l3_q_norm, l3_k_norm, l3_rel_bias, l3_w_o, l3_w_router, l3_b_router, l3_w_exp_in, l3_w_exp_out):
    b, s_len, d = x.shape
    n_ctx = ctx.shape[1]
    dims = (b, s_len, n_ctx)
    layers = (
        dict(w_mod=l0_w_mod, b_mod=l0_b_mod, norm1=l0_norm1, norm2=l0_norm2, w_qkv=l0_w_qkv, q_norm=l0_q_norm,
             k_norm=l0_k_norm, rel_bias=l0_rel_bias, w_o=l0_w_o, w_ffn_in=l0_w_ffn_in, w_ffn_out=l0_w_ffn_out),
        dict(w_mod=l1_w_mod, b_mod=l1_b_mod, norm1=l1_norm1, norm2=l1_norm2, w_qkv=l1_w_qkv, q_norm=l1_q_norm,
             k_norm=l1_k_norm, sink=l1_sink, w_o=l1_w_o, w_router=l1_w_router, b_router=l1_b_router,
             w_exp_in=l1_w_exp_in, w_exp_out=l1_w_exp_out),
        dict(w_mod=l2_w_mod, b_mod=l2_b_mod, norm1=l2_norm1, norm2=l2_norm2, w_qkv=l2_w_qkv, q_norm=l2_q_norm,
             k_norm=l2_k_norm, w_o=l2_w_o, w_ffn_in=l2_w_ffn_in, w_ffn_out=l2_w_ffn_out),
        dict(w_mod=l3_w_mod, b_mod=l3_b_mod, norm1=l3_norm1, norm2=l3_norm2, w_qkv=l3_w_qkv, q_norm=l3_q_norm,
             k_norm=l3_k_norm, rel_bias=l3_rel_bias, w_o=l3_w_o, w_router=l3_w_router, b_router=l3_b_router,
             w_exp_in=l3_w_exp_in, w_exp_out=l3_w_exp_out),
    )
    xs = (x.reshape(b * s_len, d), ctx.reshape(b * n_ctx, d))
    pad_rows = -(b + 1) % 8
    cond = jnp.concatenate([c, c_ctx[None, :], jnp.zeros((pad_rows, d), F32)], axis=0)
    n_layers = len(layers)
    for i, p in enumerate(layers):
        xs = _layer(xs, cond, p, dims, i % 3, i == n_layers - 1)
    return xs.reshape(b, s_len, d)
```

```python
import functools

import jax
import jax.numpy as jnp
from jax import lax
from jax.experimental import pallas as pl
from jax.experimental.pallas import tpu as pltpu

F32 = jnp.float32
BF16 = jnp.bfloat16

GRID_W = 64
NORM_EPS = 1e-6
ROPE_THETA = 10000.0
NA_WIN_ROWS = 8
NA_WIN_COLS = 16
SW_WINDOW = 128
N_EXPERTS = 8
MIXER_HEADS = ((16, 16, 64), (16, 4, 64), (8, 4, 128))

LANES = 128
VMEM_LIMIT_BYTES = 56 * 1024 * 1024
NEG_BIG = -1e30
LOG2E = 1.4426950408889634

QKV_CHUNK = 512
GLB_TQ = 512
GLB_CHUNK = 256
SWA_TQ = 256
NA_CTX_ROWS = 1024
NA_UNROLL = 32
MOE_TM = 1024
MOE_CHUNKS = 4
MOE_FIRST_CHUNK_DIV = 16
CAST_BLOCK_BYTES = 8 * 1024 * 1024
FFN_TF = 512


def _params(*sem):
    return pltpu.CompilerParams(dimension_semantics=sem, vmem_limit_bytes=VMEM_LIMIT_BYTES)


def _row_tile(n_lat_per_sample, n_ctx_rows, cap):
    for tm in (1024, 512, 256, 128):
        if tm <= cap and n_lat_per_sample % tm == 0 and n_ctx_rows % tm == 0:
            return tm
    raise ValueError("no row tile fits")


def _split_bf16(a):
    hi = a.astype(BF16)
    lo = (a - hi.astype(F32)).astype(BF16)
    return hi, lo


def _dot(a, b):
    return jnp.dot(a, b, preferred_element_type=F32)


def _dot_t(a, b):
    return lax.dot_general(a, b, (((1,), (1,)), ((), ())), preferred_element_type=F32)


def _silu(g):
    return g / (1.0 + jnp.exp(-g))


def _rms_mod(x, gain, scale, shift):
    ms = jnp.mean(x * x, axis=-1, keepdims=True)
    return x * lax.rsqrt(ms + NORM_EPS) * gain * (1.0 + scale) + shift


def _adaln_kernel(c_ref, w_ref, b_ref, o_ref):
    a_hi, a_lo = _split_bf16(_silu(c_ref[...]))
    w_hi, w_lo = _split_bf16(w_ref[...])
    o_ref[...] = _dot(a_hi, w_hi) + _dot(a_hi, w_lo) + _dot(a_lo, w_hi) + b_ref[...]


def _adaln(cond, w_mod, b_mod):
    r, d = cond.shape
    n = w_mod.shape[1]
    tn = 1536
    out = pl.pallas_call(
        _adaln_kernel,
        grid=(n // tn,),
        in_specs=[pl.BlockSpec((r, d), lambda j: (0, 0)),
                  pl.BlockSpec((d, tn), lambda j: (0, j)),
                  pl.BlockSpec((1, tn), lambda j: (0, j))],
        out_specs=pl.BlockSpec((r, tn), lambda j: (0, j)),
        out_shape=jax.ShapeDtypeStruct((r, n), F32),
        compiler_params=_params("arbitrary"),
        name="adaln",
    )(cond, w_mod, b_mod.reshape(1, n))
    return out.reshape(r, 6, d)


def _rot_half(z, dh):
    if dh == LANES:
        return pltpu.roll(z, LANES // 2, axis=1)
    lane = lax.broadcasted_iota(jnp.int32, z.shape, 1)
    from_right = pltpu.roll(z, LANES - dh // 2, axis=1)
    from_left = pltpu.roll(z, dh // 2, axis=1)
    return jnp.where((lane % dh) < dh // 2, from_right, from_left)


def _head_mean_sq(z, dh):
    z2 = z * z
    if dh == LANES:
        return jnp.broadcast_to(jnp.sum(z2, axis=-1, keepdims=True), z.shape) * (1.0 / dh)
    low = lax.broadcasted_iota(jnp.int32, z.shape, 1) < dh
    s_low = jnp.sum(jnp.where(low, z2, 0.0), axis=-1, keepdims=True)
    s_high = jnp.sum(jnp.where(low, 0.0, z2), axis=-1, keepdims=True)
    return jnp.where(low, s_low, s_high) * (1.0 / dh)


def _row_sources(arrays, tm, n_lat_tiles):
    lat, ctx = arrays
    return [pl.BlockSpec((tm, lat.shape[1]), lambda i: (jnp.minimum(i, n_lat_tiles - 1), 0)),
            pl.BlockSpec((tm, ctx.shape[1]), lambda i: (jnp.maximum(i - n_lat_tiles, 0), 0))]


def _pick_rows(refs, n_lat_tiles):
    if len(refs) == 1:
        return refs[0][...]
    return jnp.where(pl.program_id(0) < n_lat_tiles, refs[0][...], refs[1][...])


def _qkv_kernel(*refs, n_q, n_kv, dh, rope, n_x, n_lat_tiles):
    x = _pick_rows(refs[:n_x], n_lat_tiles)
    refs = refs[n_x:]
    if rope:
        mod_ref, n1_ref, w_ref, g_ref, cos_ref, sin_ref, q_ref, k_ref, v_ref = refs
    else:
        mod_ref, n1_ref, w_ref, g_ref, q_ref, k_ref, v_ref = refs
    h = _rms_mod(x, n1_ref[...], mod_ref[1:2, :], mod_ref[0:1, :]).astype(BF16)
    n_qk = n_q + n_kv
    cw = QKV_CHUNK
    for c in range((n_qk + n_kv) // cw):
        y = _dot(h, w_ref[:, c * cw:(c + 1) * cw])
        for s in range(cw // LANES):
            col = c * cw + s * LANES
            z = y[:, s * LANES:(s + 1) * LANES]
            if col < n_qk:
                z = z * lax.rsqrt(_head_mean_sq(z, dh) + NORM_EPS) * g_ref[:, col:col + LANES]
                if rope:
                    z = z * cos_ref[...] + _rot_half(z, dh) * sin_ref[...]
            z = z.astype(BF16)
            if col < n_q:
                q_ref[:, col:col + LANES] = z
            elif col < n_qk:
                k_ref[:, col - n_q:col - n_q + LANES] = z
            else:
                v_ref[:, col - n_qk:col - n_qk + LANES] = z


def _rope_tables(s_len, dh, tm):
    n_freq = dh // 4
    inv_freq = ROPE_THETA ** (-jnp.arange(n_freq, dtype=F32) / n_freq)
    t = jnp.arange(s_len)
    row = (t // GRID_W).astype(F32)
    col = (t % GRID_W).astype(F32)
    ang = jnp.concatenate([row[:, None] * inv_freq, col[:, None] * inv_freq], axis=-1)
    cos, sin = jnp.cos(ang), jnp.sin(ang)
    reps = LANES // dh
    cos_t = jnp.tile(jnp.concatenate([cos, cos], axis=-1), (1, reps))
    sin_t = jnp.tile(jnp.concatenate([-sin, sin], axis=-1), (1, reps))
    cos_t = jnp.concatenate([cos_t, jnp.ones((tm, LANES), F32)], axis=0)
    sin_t = jnp.concatenate([sin_t, jnp.zeros((tm, LANES), F32)], axis=0)
    return cos_t, sin_t


def _qkv(xs, mod, norm1, w_qkv, q_gain, k_gain, dims, mixer):
    b, s_len, n_ctx = dims
    x_parts = list(xs) if isinstance(xs, tuple) else [xs]
    n_rows, d = sum(a.shape[0] for a in x_parts), x_parts[0].shape[1]
    n_qh, n_kvh, dh = MIXER_HEADS[mixer]
    n_q, n_kv = n_qh * dh, n_kvh * dh
    rope = mixer != 0
    tm = _row_tile(s_len, b * n_ctx, 512)
    n_lat_tiles = b * s_len // tm
    tiles_per_sample = s_len // tm

    gains = jnp.concatenate([jnp.tile(q_gain * (dh ** -0.5 * LOG2E), n_qh),
                             jnp.tile(k_gain, n_kvh)]).reshape(1, n_q + n_kv)

    def mod_idx(i):
        return jnp.where(i < n_lat_tiles, i // tiles_per_sample, b)

    x_specs = [pl.BlockSpec((tm, d), lambda i: (i, 0))] if len(x_parts) == 1 else _row_sources(x_parts, tm, n_lat_tiles)
    in_specs = x_specs + [pl.BlockSpec((None, 6, d), lambda i: (mod_idx(i), 0, 0)),
                          pl.BlockSpec((1, d), lambda i: (0, 0)),
                          pl.BlockSpec(w_qkv.shape, lambda i: (0, 0)),
                          pl.BlockSpec(gains.shape, lambda i: (0, 0))]
    args = x_parts + [mod, norm1.reshape(1, d), w_qkv.astype(BF16), gains]
    if rope:
        cos_t, sin_t = _rope_tables(s_len, dh, tm)

        def pos_idx(i):
            return jnp.where(i < n_lat_tiles, i % tiles_per_sample, tiles_per_sample)

        in_specs += [pl.BlockSpec((tm, LANES), lambda i: (pos_idx(i), 0))] * 2
        args += [cos_t, sin_t]
    return pl.pallas_call(
        functools.partial(_qkv_kernel, n_q=n_q, n_kv=n_kv, dh=dh, rope=rope, n_x=len(x_parts),
                          n_lat_tiles=n_lat_tiles),
        grid=(n_rows // tm,),
        in_specs=in_specs,
        out_specs=[pl.BlockSpec((tm, n_q), lambda i: (i, 0)),
                   pl.BlockSpec((tm, n_kv), lambda i: (i, 0)),
                   pl.BlockSpec((tm, n_kv), lambda i: (i, 0))],
        out_shape=[jax.ShapeDtypeStruct((n_rows, n_q), BF16),
                   jax.ShapeDtypeStruct((n_rows, n_kv), BF16),
                   jax.ShapeDtypeStruct((n_rows, n_kv), BF16)],
        compiler_params=_params("arbitrary"),
        name="qkv_m%d" % mixer,
    )(*args)


def _stack_heads(q, dh, nh):
    if dh == LANES:
        return jnp.concatenate([q[:, h * LANES:(h + 1) * LANES] for h in range(nh)], axis=0)
    per_kv = nh // 2
    tq = q.shape[0]
    lane = lax.broadcasted_iota(jnp.int32, (tq, LANES), 1)
    ops = []
    for h in range(nh):
        slot = h // per_kv
        chunk = q[:, (h // 2) * LANES:(h // 2 + 1) * LANES].astype(F32)
        if h % 2 != slot:
            chunk = pltpu.roll(chunk, dh, axis=1)
        keep = lane < dh if slot == 0 else lane >= dh
        ops.append(jnp.where(keep, chunk, 0.0).astype(BF16))
    return jnp.concatenate(ops, axis=0)


def _unstack_heads(o, dh, nh):
    tq = o.shape[0] // nh
    if dh == LANES:
        return jnp.concatenate([o[h * tq:(h + 1) * tq] for h in range(nh)], axis=1)
    per_kv = nh // 2
    lane = lax.broadcasted_iota(jnp.int32, (tq, LANES), 1)
    chunks = []
    for c in range(nh // 2):
        parts = []
        for h in (2 * c, 2 * c + 1):
            oh = o[h * tq:(h + 1) * tq]
            if h % 2 != h // per_kv:
                oh = pltpu.roll(oh, dh, axis=1)
            parts.append(oh)
        chunks.append(jnp.where(lane < dh, parts[0], parts[1]))
    return jnp.concatenate(chunks, axis=1)


def _lanes(x, n):
    return x if n == LANES else jnp.concatenate([x] * (n // LANES), axis=1)


def _block_max(s):
    return functools.reduce(jnp.maximum, [s[:, j * LANES:(j + 1) * LANES] for j in range(s.shape[1] // LANES)])


def _row_max(blk):
    return jnp.broadcast_to(jnp.max(blk, axis=-1, keepdims=True), blk.shape)


def _dot_row_halves(p, v):
    half = p.shape[0] // 2
    return jnp.concatenate([_dot(p[0:half, :], v), _dot(p[half:, :], v)], axis=0)


def _ones_ext(v):
    return jnp.concatenate([v, jnp.ones_like(v)], axis=1)


def _sink_rows(sink_ref, first_head, nh, tq):
    return jnp.concatenate([jnp.full((tq, LANES), sink_ref[first_head + h] * LOG2E, F32) for h in range(nh)], axis=0)


def _na_kernel(q_ref, k_ref, v_ref, kc_ref, vc_ref, bias_ref, o_ref,
               qs_ref, vext_ref, mc_ref, numc_ref, lc_ref, *, rows):
    s_len = q_ref.shape[0]
    band = NA_WIN_ROWS * GRID_W
    lane = lax.broadcasted_iota(jnp.int32, (s_len, LANES), 1)
    q = q_ref[...]
    qs_ref[0:s_len, :] = jnp.where(lane < 64, q, jnp.zeros_like(q))
    qs_ref[s_len:, :] = jnp.where(lane >= 64, q, jnp.zeros_like(q))
    vext_ref[:, 0:LANES] = v_ref[...]
    vext_ref[:, LANES:] = jnp.ones((s_len, LANES), BF16)

    kc = kc_ref[...]
    vcx = _ones_ext(vc_ref[...])
    for c in range(2 * s_len // NA_CTX_ROWS):
        rs = slice(c * NA_CTX_ROWS, (c + 1) * NA_CTX_ROWS)
        s = _dot_t(qs_ref[rs, :], kc)
        m = _row_max(_block_max(s))
        acc = _dot(jnp.exp2(s - _lanes(m, s.shape[1])).astype(BF16), vcx)
        mc_ref[rs, :] = m
        numc_ref[rs, :] = acc[:, :LANES]
        lc_ref[rs, :] = acc[:, LANES:]

    lane_q = lax.broadcasted_iota(jnp.int32, (GRID_W, LANES), 1)

    def both_heads(ref, q0):
        return jnp.concatenate([ref[pl.ds(q0, GRID_W), :], ref[pl.ds(s_len + q0, GRID_W), :]], axis=0)

    def one_row(r, carry):
        r0 = jnp.clip(r - NA_WIN_ROWS // 2, 0, rows - NA_WIN_ROWS)
        q0 = pl.multiple_of(r * GRID_W, GRID_W)
        k0 = pl.multiple_of(r0 * GRID_W, GRID_W)
        d0 = r0 - r + NA_WIN_ROWS - 1
        bias = jnp.concatenate(
            [jnp.concatenate([bias_ref[0, d0 + 2 * j], bias_ref[1, d0 + 2 * j]], axis=0)
             for j in range(NA_WIN_ROWS // 2)], axis=1)
        qs = both_heads(qs_ref, q0)
        half = band // 2
        k1 = pl.multiple_of(k0 + half, GRID_W)
        s = jnp.concatenate([_dot_t(qs, k_ref[pl.ds(k0, half), :]), _dot_t(qs, k_ref[pl.ds(k1, half), :])],
                            axis=1) + bias
        mc = both_heads(mc_ref, q0)
        m = jnp.maximum(_row_max(_block_max(s)), mc)
        acc = _dot(jnp.exp2(s - _lanes(m, band)).astype(BF16), vext_ref[pl.ds(k0, band), :])
        alpha = jnp.exp2(mc - m)
        num = acc[:, :LANES] + alpha * both_heads(numc_ref, q0)
        den = acc[:, LANES:] + alpha * both_heads(lc_ref, q0)
        o = num / den
        o_ref[pl.ds(q0, GRID_W), :] = jnp.where(lane_q < 64, o[:GRID_W], o[GRID_W:]).astype(BF16)
        return carry

    lax.fori_loop(0, rows, one_row, 0, unroll=NA_UNROLL)


def _na_bias_tables(rel_bias):
    n_h = rel_bias.shape[0]
    col = jnp.arange(GRID_W)
    col_start = jnp.clip(col - NA_WIN_COLS // 2, 0, GRID_W - NA_WIN_COLS)
    in_win = (col[None, :] >= col_start[:, None]) & (col[None, :] < col_start[:, None] + NA_WIN_COLS)
    col_idx = jnp.clip(col[None, :] - col[:, None] + NA_WIN_COLS - 1, 0, 2 * NA_WIN_COLS - 2)
    masked = jnp.where(in_win[None, None], rel_bias[:, :, col_idx] * LOG2E, NEG_BIG)
    pairs = jnp.concatenate([masked[:, :-1], masked[:, 1:]], axis=-1)
    return pairs.reshape((n_h // 2, 2) + pairs.shape[1:])


def _na_attention(q, k, v, rel_bias, dims):
    b, s_len, n_ctx = dims
    rows = s_len // GRID_W
    n_pairs = q.shape[1] // LANES
    bias = _na_bias_tables(rel_bias)
    ctx_blk = b * s_len // n_ctx
    lat = pl.BlockSpec((s_len, LANES), lambda p, i: (i, p))
    ctx = pl.BlockSpec((n_ctx, LANES), lambda p, i: (ctx_blk + i, p))
    return pl.pallas_call(
        functools.partial(_na_kernel, rows=rows),
        grid=(n_pairs, b),
        in_specs=[lat, lat, lat, ctx, ctx,
                  pl.BlockSpec((None,) + bias.shape[1:], lambda p, i: (p, 0, 0, 0, 0))],
        out_specs=lat,
        out_shape=jax.ShapeDtypeStruct((b * s_len, q.shape[1]), BF16),
        scratch_shapes=[pltpu.VMEM((2 * s_len, LANES), BF16),
                        pltpu.VMEM((s_len, 2 * LANES), BF16),
                        pltpu.VMEM((2 * s_len, LANES), F32),
                        pltpu.VMEM((2 * s_len, LANES), F32),
                        pltpu.VMEM((2 * s_len, LANES), F32)],
        compiler_params=_params("arbitrary", "arbitrary"),
        name="attn_na",
    )(q, k, v, k, v, bias)


def _swa_kernel(sink_ref, q_ref, k_ref, v_ref, kc_ref, vc_ref, o_ref, kwin_ref, vwin_ref, *, nh, n_units, s_len):
    tq = q_ref.shape[0]
    span = tq + 2 * SW_WINDOW
    p_blk, t = pl.program_id(1), pl.program_id(2)

    @pl.when(t == 0)
    def _():
        kwin_ref[span:, :] = kc_ref[...]
        vwin_ref[span:, 0:LANES] = vc_ref[...]
        vwin_ref[:, LANES:] = jnp.ones((vwin_ref.shape[0], LANES), BF16)

    q0 = t * tq
    k0 = pl.multiple_of(jnp.clip(q0 - SW_WINDOW, 0, s_len - span), SW_WINDOW)
    kwin_ref[0:span, :] = k_ref[pl.ds(k0, span), :]
    vwin_ref[0:span, 0:LANES] = v_ref[pl.ds(k0, span), :]

    row = lax.broadcasted_iota(jnp.int32, (tq, span), 0)
    col = lax.broadcasted_iota(jnp.int32, (tq, span), 1)
    bias = jnp.where(jnp.abs(col - row + (k0 - q0)) <= SW_WINDOW, 0.0, NEG_BIG)

    qs_all = _stack_heads(q_ref[...], 64, nh)
    sink_all = _sink_rows(sink_ref, p_blk * nh, nh, tq)
    hu = nh // n_units
    vwin = vwin_ref[...]
    kd = 2 * LANES
    outs = []
    for u in range(n_units):
        qs = qs_all[u * hu * tq:(u + 1) * hu * tq]
        sink = sink_all[u * hu * tq:(u + 1) * hu * tq]
        s = jnp.concatenate([_dot_t(qs, kwin_ref[j * kd:(j + 1) * kd, :]) for j in range(kwin_ref.shape[0] // kd)],
                            axis=1)
        s = jnp.concatenate([(s[:, :span].reshape(hu, tq, span) + bias[None]).reshape(hu * tq, span),
                             s[:, span:]], axis=1)
        m = jnp.maximum(_row_max(_block_max(s)), sink)
        acc = _dot(jnp.exp2(s - _lanes(m, s.shape[1])).astype(BF16), vwin)
        outs.append(acc[:, :LANES] / (acc[:, LANES:] + jnp.exp2(sink - m)))
    o_ref[...] = _unstack_heads(jnp.concatenate(outs, axis=0), 64, nh).astype(BF16)


def _swa_attention(q, k, v, sink, dims):
    b, s_len, n_ctx = dims
    tq = SWA_TQ
    n_t = s_len // tq
    n_kvblk = k.shape[1] // LANES
    nh = q.shape[1] // k.shape[1] * 2
    qw = nh * 64
    ctx_blk = b * s_len // n_ctx
    n_win = tq + 2 * SW_WINDOW + n_ctx
    assert n_win % (2 * LANES) == 0
    qspec = pl.BlockSpec((tq, qw), lambda i, p, t, *_: (i * n_t + t, p))
    lat = pl.BlockSpec((s_len, LANES), lambda i, p, t, *_: (i, p))
    ctx = pl.BlockSpec((n_ctx, LANES), lambda i, p, t, *_: (ctx_blk + i, p))
    return pl.pallas_call(
        functools.partial(_swa_kernel, nh=nh, n_units=nh, s_len=s_len),
        grid_spec=pltpu.PrefetchScalarGridSpec(
            num_scalar_prefetch=1,
            grid=(b, n_kvblk, n_t),
            in_specs=[qspec, lat, lat, ctx, ctx],
            out_specs=qspec,
            scratch_shapes=[pltpu.VMEM((n_win, LANES), BF16),
                            pltpu.VMEM((n_win, 2 * LANES), BF16)]),
        out_shape=jax.ShapeDtypeStruct((b * s_len, q.shape[1]), BF16),
        compiler_params=_params("arbitrary", "arbitrary", "arbitrary"),
        name="attn_swa",
    )(sink, q, k, v, k, v)


def _global_kernel(q_ref, k_ref, v_ref, kc_ref, vc_ref, o_ref, kall_ref, vext_ref, s_ref, p_ref, *, nh, s_len):
    @pl.when(pl.program_id(2) == 0)
    def _():
        kall_ref[0:s_len, :] = k_ref[...]
        kall_ref[s_len:, :] = kc_ref[...]
        vext_ref[0:s_len, 0:LANES] = v_ref[...]
        vext_ref[s_len:, 0:LANES] = vc_ref[...]
        vext_ref[:, LANES:] = jnp.ones((vext_ref.shape[0], LANES), BF16)

    qs = _stack_heads(q_ref[...], LANES, nh)
    n_chunks = kall_ref.shape[0] // GLB_CHUNK
    mrun = None
    for c in range(n_chunks):
        ks = slice(c * GLB_CHUNK, (c + 1) * GLB_CHUNK)
        s = _dot_t(qs, kall_ref[ks, :])
        s_ref[:, ks] = s
        blk = _block_max(s)
        mrun = blk if mrun is None else jnp.maximum(mrun, blk)
    m = _lanes(_row_max(mrun), GLB_CHUNK)
    for c in range(n_chunks):
        ks = slice(c * GLB_CHUNK, (c + 1) * GLB_CHUNK)
        p_ref[:, ks] = jnp.exp2(s_ref[:, ks] - m).astype(BF16)
    acc = _dot_row_halves(p_ref, vext_ref[...])
    o_ref[...] = _unstack_heads(acc[:, :LANES] / acc[:, LANES:], LANES, nh).astype(BF16)


def _global_attention(q, k, v, dims):
    b, s_len, n_ctx = dims
    tq = GLB_TQ
    n_t = s_len // tq
    n_kvblk = k.shape[1] // LANES
    nh = q.shape[1] // k.shape[1]
    qw = nh * LANES
    ctx_blk = b * s_len // n_ctx
    n_keys = s_len + n_ctx
    qspec = pl.BlockSpec((tq, qw), lambda i, p, t: (i * n_t + t, p))
    lat = pl.BlockSpec((s_len, LANES), lambda i, p, t: (i, p))
    ctx = pl.BlockSpec((n_ctx, LANES), lambda i, p, t: (ctx_blk + i, p))
    return pl.pallas_call(
        functools.partial(_global_kernel, nh=nh, s_len=s_len),
        grid=(b, n_kvblk, n_t),
        in_specs=[qspec, lat, lat, ctx, ctx],
        out_specs=qspec,
        out_shape=jax.ShapeDtypeStruct((b * s_len, q.shape[1]), BF16),
        scratch_shapes=[pltpu.VMEM((n_keys, LANES), BF16),
                        pltpu.VMEM((n_keys, 2 * LANES), BF16),
                        pltpu.VMEM((nh * tq, n_keys), F32),
                        pltpu.VMEM((nh * tq, n_keys), BF16)],
        compiler_params=_params("arbitrary", "arbitrary", "arbitrary"),
        name="attn_global",
    )(q, k, v, k, v)


def _ctx_kernel(sink_ref, q_ref, kc_ref, vc_ref, o_ref, *, dh, nh, use_sink):
    tq = q_ref.shape[0]
    qw = nh * dh
    for p in range(kc_ref.shape[1] // LANES):
        kv = slice(p * LANES, (p + 1) * LANES)
        qs = _stack_heads(q_ref[:, p * qw:(p + 1) * qw], dh, nh)
        s = _dot_t(qs, kc_ref[:, kv])
        m = _row_max(_block_max(s))
        if use_sink:
            sink = _sink_rows(sink_ref, p * nh, nh, tq)
            m = jnp.maximum(m, sink)
        acc = _dot(jnp.exp2(s - _lanes(m, s.shape[1])).astype(BF16), _ones_ext(vc_ref[:, kv]))
        den = acc[:, LANES:]
        if use_sink:
            den = den + jnp.exp2(sink - m)
        o_ref[:, p * qw:(p + 1) * qw] = _unstack_heads(acc[:, :LANES] / den, dh, nh).astype(BF16)


def _ctx_attention(q, k, v, sink, dims, mixer):
    b, s_len, n_ctx = dims
    n_qh, n_kvh, dh = MIXER_HEADS[mixer]
    n_kvblk = k.shape[1] // LANES
    nh = n_qh // n_kvblk
    ctx_blk = b * s_len // n_ctx
    use_sink = sink is not None
    if not use_sink:
        sink = jnp.zeros((n_qh,), F32)
    return pl.pallas_call(
        functools.partial(_ctx_kernel, dh=dh, nh=nh, use_sink=use_sink),
        grid_spec=pltpu.PrefetchScalarGridSpec(
            num_scalar_prefetch=1,
            grid=(b,),
            in_specs=[pl.BlockSpec((n_ctx, q.shape[1]), lambda i, *_: (ctx_blk + i, 0)),
                      pl.BlockSpec((n_ctx, k.shape[1]), lambda i, *_: (ctx_blk + i, 0)),
                      pl.BlockSpec((n_ctx, v.shape[1]), lambda i, *_: (ctx_blk + i, 0))],
            out_specs=pl.BlockSpec((n_ctx, q.shape[1]), lambda i, *_: (i, 0))),
        out_shape=jax.ShapeDtypeStruct((b * n_ctx, q.shape[1]), BF16),
        compiler_params=_params("arbitrary"),
        name="attn_ctx_m%d" % mixer,
    )(sink, q, k, v)


def _oproj_kernel(*refs, n_lat_tiles, n_y, n_x, moe):
    y = _pick_rows(refs[:n_y], n_lat_tiles)
    x_in = _pick_rows(refs[n_y:n_y + n_x], n_lat_tiles)
    refs = refs[n_y + n_x:]
    mod_ref, n2_ref, wo_ref = refs[:3]
    refs = refs[3:]
    if moe:
        wr_ref, br_ref = refs[:2]
        refs = refs[2:]
    xo_ref, h_ref = refs[:2]
    x = x_in + mod_ref[2:3, :] * _dot(y, wo_ref[...])
    xo_ref[...] = x
    h = _rms_mod(x, n2_ref[...], mod_ref[4:5, :], mod_ref[3:4, :])
    h_ref[...] = h.astype(BF16)
    if moe:
        route_ref, counts_ref, base_ref = refs[2:5]
        h_hi, h_lo = _split_bf16(h)
        w_hi, w_lo = _split_bf16(wr_ref[...])
        hi_terms = _dot(h_hi, jnp.concatenate([w_hi, w_lo], axis=1))
        lo_terms = _dot(h_lo, jnp.concatenate([w_hi, jnp.zeros_like(w_hi)], axis=1))
        logits = hi_terms[:, :LANES] + hi_terms[:, LANES:] + lo_terms[:, :LANES] + br_ref[...]
        lane = lax.broadcasted_iota(jnp.int32, logits.shape, 1)
        v1 = jnp.max(logits, axis=-1, keepdims=True)
        i1 = jnp.min(jnp.where(logits == v1, lane, LANES), axis=-1, keepdims=True)
        rest = jnp.where(lane == i1, NEG_BIG, logits)
        v2 = jnp.max(rest, axis=-1, keepdims=True)
        i2 = jnp.min(jnp.where(rest == v2, lane, LANES), axis=-1, keepdims=True)
        e = jnp.exp(v2 - v1)
        w1 = 1.0 / (1.0 + e)
        w2 = e / (1.0 + e)
        @pl.when(pl.program_id(0) == 0)
        def _():
            base_ref[...] = jnp.zeros_like(base_ref)

        tm = logits.shape[0]
        pick1 = lane == i1
        pick2 = lane == i2
        cnt = jnp.where(pick1, 1.0, 0.0) + jnp.where(pick2, 1.0, 0.0)
        earlier = lax.broadcasted_iota(jnp.int32, (tm, tm), 0) > lax.broadcasted_iota(jnp.int32, (tm, tm), 1)
        before = _dot(jnp.where(earlier, 1.0, 0.0).astype(BF16), cnt.astype(BF16)) + base_ref[0:1, :]
        r1 = jnp.sum(jnp.where(pick1, before, 0.0), axis=-1, keepdims=True)
        r2 = jnp.sum(jnp.where(pick2, before, 0.0), axis=-1, keepdims=True)
        base_ref[...] = base_ref[...] + jnp.sum(cnt, axis=0, keepdims=True)
        counts_ref[...] = base_ref[...]
        route = jnp.where(lane == 0, i1.astype(F32), 0.0)
        route = jnp.where(lane == 1, i2.astype(F32), route)
        route = jnp.where(lane == 2, w1, route)
        route = jnp.where(lane == 3, w2, route)
        route = jnp.where(lane == 4, r1, route)
        route = jnp.where(lane == 5, r2, route)
        route_ref[...] = route


def _oproj(y_lat, y_ctx, xs, mod, norm2, w_o, router, dims, last):
    b, s_len, n_ctx = dims
    x_parts = list(xs) if isinstance(xs, tuple) else [xs]
    d = x_parts[0].shape[1]
    tm = _row_tile(s_len, b * n_ctx, 512)
    n_lat_tiles = b * s_len // tm
    tiles_per_sample = s_len // tm
    n_rows = b * s_len if last else b * (s_len + n_ctx)
    n_tiles = n_rows // tm
    moe = router is not None

    def mod_idx(i):
        return jnp.where(i < n_lat_tiles, i // tiles_per_sample, b)

    row = pl.BlockSpec((tm, d), lambda i: (i, 0))
    y_parts = [y_lat] if last else [y_lat, y_ctx]
    in_specs = ([pl.BlockSpec((tm, y_lat.shape[1]), lambda i: (i, 0))] if last
                else _row_sources(y_parts, tm, n_lat_tiles))
    in_specs += [row] if len(x_parts) == 1 else _row_sources(x_parts, tm, n_lat_tiles)
    in_specs += [pl.BlockSpec((None, 6, d), lambda i: (mod_idx(i), 0, 0)),
                 pl.BlockSpec((1, d), lambda i: (0, 0)),
                 pl.BlockSpec(w_o.shape, lambda i: (0, 0))]
    args = y_parts + x_parts + [mod, norm2.reshape(1, d), w_o.astype(BF16)]
    out_specs = [row, row]
    out_shape = [jax.ShapeDtypeStruct((n_rows, d), F32), jax.ShapeDtypeStruct((n_rows, d), BF16)]
    if moe:
        w_router, b_router = router
        n_e = w_router.shape[1]
        wr = jnp.pad(w_router, ((0, 0), (0, LANES - n_e)))
        br = jnp.pad(b_router, (0, LANES - n_e), constant_values=NEG_BIG).reshape(1, LANES)
        in_specs += [pl.BlockSpec(wr.shape, lambda i: (0, 0)), pl.BlockSpec(br.shape, lambda i: (0, 0))]
        args += [wr, br]
        out_specs += [pl.BlockSpec((tm, LANES), lambda i: (i, 0)), pl.BlockSpec((8, LANES), lambda i: (0, 0))]
        out_shape += [jax.ShapeDtypeStruct((n_rows, LANES), F32), jax.ShapeDtypeStruct((8, LANES), F32)]
    return pl.pallas_call(
        functools.partial(_oproj_kernel, n_lat_tiles=n_lat_tiles, n_y=len(y_parts), n_x=len(x_parts), moe=moe),
        grid=(n_tiles,),
        in_specs=in_specs,
        out_specs=out_specs,
        out_shape=out_shape,
        scratch_shapes=[pltpu.VMEM((8, LANES), F32)] if moe else [],
        compiler_params=_params("arbitrary"),
        name="oproj_moe" if moe else "oproj",
    )(*args)


def _swiglu_partial(h_ref, wg_ref, wu_ref, wd_ref):
    h = h_ref[...]
    a = _silu(_dot(h, wg_ref[...].astype(BF16))) * _dot(h, wu_ref[...].astype(BF16))
    return _dot(a.astype(BF16), wd_ref[...].astype(BF16))


def _accumulate_over_hidden(f, n_f, acc_ref, partial, finish):
    @pl.when(f == 0)
    def _():
        acc_ref[...] = partial()

    @pl.when(jnp.logical_and(f > 0, f < n_f - 1))
    def _():
        acc_ref[...] += partial()

    @pl.when(f == n_f - 1)
    def _():
        finish(acc_ref[...] + partial())


def _ffn_kernel(h_ref, wg_ref, wu_ref, wd_ref, x_ref, mod_ref, o_ref, acc_ref):
    def finish(total):
        o_ref[...] = x_ref[...] + mod_ref[5:6, :] * total

    _accumulate_over_hidden(pl.program_id(1), pl.num_programs(1), acc_ref,
                            functools.partial(_swiglu_partial, h_ref, wg_ref, wu_ref, wd_ref), finish)


def _ffn(h, xs, mod, w_in, w_out, dims):
    b, s_len, n_ctx = dims
    n_rows, d = h.shape
    d_ff = w_out.shape[0]
    tm = _row_tile(s_len, b * n_ctx, 1024)
    tf = FFN_TF
    n_f = d_ff // tf
    n_lat_tiles = b * s_len // tm
    tiles_per_sample = s_len // tm

    def mod_idx(i):
        return jnp.where(i < n_lat_tiles, i // tiles_per_sample, b)

    w_in = _to_bf16(w_in[None])[0]
    w_out = _to_bf16(w_out[None])[0]
    return pl.pallas_call(
        _ffn_kernel,
        grid=(n_rows // tm, n_f),
        in_specs=[pl.BlockSpec((tm, d), lambda i, f: (i, 0)),
                  pl.BlockSpec((d, tf), lambda i, f: (0, f)),
                  pl.BlockSpec((d, tf), lambda i, f: (0, n_f + f)),
                  pl.BlockSpec((tf, d), lambda i, f: (f, 0)),
                  pl.BlockSpec((tm, d), lambda i, f: (i, 0)),
                  pl.BlockSpec((None, 6, d), lambda i, f: (mod_idx(i), 0, 0))],
        out_specs=pl.BlockSpec((tm, d), lambda i, f: (i, 0)),
        out_shape=jax.ShapeDtypeStruct((n_rows, d), F32),
        scratch_shapes=[pltpu.VMEM((tm, d), F32)],
        compiler_params=_params("arbitrary", "arbitrary"),
        name="ffn_dense",
    )(h, w_in, w_in, w_out, xs, mod)


def _moe_ffn_kernel(te_ref, na_ref, h_ref, wg_ref, wu_ref, wd_ref, *rest):
    o_ref, acc_ref = rest[-2:]
    i, f = pl.program_id(0), pl.program_id(1)

    def finish(total):
        o_ref[...] = total.astype(o_ref.dtype)

    @pl.when(i < na_ref[0])
    def _():
        _accumulate_over_hidden(f, pl.num_programs(1), acc_ref,
                                functools.partial(_swiglu_partial, h_ref, wg_ref, wu_ref, wd_ref), finish)


def _moe_ffn(hs, ys, n_rows_total, first_tile, tile_expert, n_active, w_in, w_out):
    n_rows, d = hs.shape
    d_ff = w_out.shape[1]
    tm, tf = MOE_TM, FFN_TF
    n_f = d_ff // tf

    def tile(i, na):
        return jnp.minimum(i, jnp.maximum(na[0] - 1, 0))

    def expert(i, te, na):
        return te[tile(i, na)]

    in_specs = [pl.BlockSpec((tm, d), lambda i, f, te, na: (tile(i, na), 0)),
                pl.BlockSpec((None, d, tf), lambda i, f, te, na: (expert(i, te, na), 0, f)),
                pl.BlockSpec((None, d, tf), lambda i, f, te, na: (expert(i, te, na), 0, n_f + f)),
                pl.BlockSpec((None, tf, d), lambda i, f, te, na: (expert(i, te, na), f, 0))]
    args = [tile_expert, n_active, hs, w_in, w_in, w_out]
    aliases = {}
    if ys is not None:
        in_specs.append(pl.BlockSpec(memory_space=pl.ANY))
        aliases = {len(args): 0}
        args.append(ys)
    return pl.pallas_call(
        _moe_ffn_kernel,
        grid_spec=pltpu.PrefetchScalarGridSpec(
            num_scalar_prefetch=2,
            grid=(n_rows // tm, n_f),
            in_specs=in_specs,
            out_specs=pl.BlockSpec((tm, d), lambda i, f, te, na: (first_tile + tile(i, na), 0)),
            scratch_shapes=[pltpu.VMEM((tm, d), F32)]),
        out_shape=jax.ShapeDtypeStruct((n_rows_total, d), hs.dtype),
        input_output_aliases=aliases,
        compiler_params=_params("arbitrary", "arbitrary"),
        name="moe_ffn",
    )(*args)


def _combine_kernel(x_ref, a_ref, b_ref, route_ref, mod_ref, o_ref):
    w1 = route_ref[:, 2:3]
    w2 = route_ref[:, 3:4]
    mix = w1 * a_ref[...].astype(F32) + w2 * b_ref[...].astype(F32)
    o_ref[...] = x_ref[...] + mod_ref[5:6, :] * mix


def _combine(xs, ya, yb, route, mod, dims):
    b, s_len, n_ctx = dims
    n_rows, d = ya.shape
    tm = _row_tile(s_len, b * n_ctx, 512)
    n_lat_tiles = b * s_len // tm
    tiles_per_sample = s_len // tm

    def mod_idx(i):
        return jnp.where(i < n_lat_tiles, i // tiles_per_sample, b)

    row = pl.BlockSpec((tm, d), lambda i: (i, 0))
    return pl.pallas_call(
        _combine_kernel,
        grid=(n_rows // tm,),
        in_specs=[row, row, row,
                  pl.BlockSpec((tm, LANES), lambda i: (i, 0)),
                  pl.BlockSpec((None, 6, d), lambda i: (mod_idx(i), 0, 0))],
        out_specs=row,
        out_shape=jax.ShapeDtypeStruct((n_rows, d), F32),
        compiler_params=_params("arbitrary"),
        name="moe_combine",
    )(xs, ya, yb, route, mod)


def _route_plan(idx, rank, counts, tm):
    n = idx.shape[0]
    tiles_per = (counts + tm - 1) // tm
    tile_end = jnp.cumsum(tiles_per)
    tile_start = tile_end - tiles_per
    experts = jnp.arange(N_EXPERTS, dtype=jnp.int32)
    start = jnp.sum(jnp.where(idx[:, :, None] == experts, tile_start, 0), axis=-1)
    slot = start * tm + rank
    n_tiles = (2 * n) // tm + N_EXPERTS
    tile_ids = jnp.arange(n_tiles, dtype=jnp.int32)
    tile_expert = jnp.minimum(jnp.sum((tile_end[None, :] <= tile_ids[:, None]).astype(jnp.int32), axis=1),
                              N_EXPERTS - 1)
    token = jnp.arange(2 * n, dtype=jnp.int32) // 2
    _, token_sorted = lax.sort((slot.reshape(-1), token), num_keys=1)
    first_pair = jnp.cumsum(counts) - counts
    row_in_expert = ((tile_ids - tile_start[tile_expert]) * tm)[:, None] + jnp.arange(tm, dtype=jnp.int32)[None, :]
    pair = jnp.clip(first_pair[tile_expert][:, None] + row_in_expert, 0, 2 * n - 1)
    filler = (tile_ids[:, None] * tm + jnp.arange(tm, dtype=jnp.int32)[None, :]) % n
    src = jnp.where(row_in_expert < counts[tile_expert][:, None], token_sorted[pair], filler).reshape(-1)
    return slot, src, tile_expert, tile_end[-1:].astype(jnp.int32)


def _cast_kernel(w_ref, o_ref):
    o_ref[...] = w_ref[...].astype(o_ref.dtype)


def _to_bf16(w):
    n_e, n_r, n_c = w.shape
    rows = n_r
    while rows * n_c * 4 > CAST_BLOCK_BYTES and rows % 2 == 0 and (rows // 2) % 16 == 0:
        rows //= 2
    return pl.pallas_call(
        _cast_kernel,
        grid=(n_e, n_r // rows),
        in_specs=[pl.BlockSpec((None, rows, n_c), lambda e, r: (e, r, 0))],
        out_specs=pl.BlockSpec((None, rows, n_c), lambda e, r: (e, r, 0)),
        out_shape=jax.ShapeDtypeStruct(w.shape, BF16),
        compiler_params=_params("arbitrary", "arbitrary"),
        name="cast_bf16",
    )(w)


def _take_rows(a, rows):
    return a.at[rows].get(mode="promise_in_bounds")


def _moe(h, xs, route, counts, mod, w_exp_in, w_exp_out, dims):
    idx = route[:, 0:2].astype(jnp.int32)
    rank = route[:, 4:6].astype(jnp.int32)
    slot, src, tile_expert, n_active = _route_plan(idx, rank, counts[0, :N_EXPERTS].astype(jnp.int32), MOE_TM)
    n_tiles = src.shape[0] // MOE_TM
    first = max(1, n_tiles // MOE_FIRST_CHUNK_DIV)
    rest = -(-(n_tiles - first) // (MOE_CHUNKS - 1))
    bounds = [0] + [min(first + j * rest, n_tiles) for j in range(MOE_CHUNKS)]
    ys = None
    for t0, t1 in zip(bounds[:-1], bounds[1:]):
        if t1 == t0:
            continue
        hs = _take_rows(h, src[t0 * MOE_TM:t1 * MOE_TM])
        ys = _moe_ffn(hs, ys, src.shape[0], t0, tile_expert[t0:t1], jnp.clip(n_active - t0, 0, t1 - t0),
                      w_exp_in, w_exp_out)
    ya = _take_rows(ys, slot[:, 0])
    yb = _take_rows(ys, slot[:, 1])
    return _combine(xs, ya, yb, route, mod, dims)


def _layer(xs, cond, p, dims, mixer, last):
    b, s_len, n_ctx = dims
    mod = _adaln(cond, p["w_mod"], p["b_mod"])
    q, k, v = _qkv(xs, mod, p["norm1"], p["w_qkv"], p["q_norm"], p["k_norm"], dims, mixer)
    if mixer == 0:
        y_lat = _na_attention(q, k, v, p["rel_bias"], dims)
    elif mixer == 1:
        y_lat = _swa_attention(q, k, v, p["sink"], dims)
    else:
        y_lat = _global_attention(q, k, v, dims)
    y_ctx = None if last else _ctx_attention(q, k, v, p.get("sink"), dims, mixer)
    router = (p["w_router"], p["b_router"]) if "w_router" in p else None
    outs = _oproj(y_lat, y_ctx, xs, mod, p["norm2"], p["w_o"], router, dims, last)
    if router is None:
        xs, h = outs
        return _ffn(h, xs, mod, p["w_ffn_in"], p["w_ffn_out"], dims)
    xs, h, route, counts = outs
    return _moe(h, xs, route, counts, mod, p["w_exp_in"], p["w_exp_out"], dims)


def kernel(x, c, ctx, c_ctx, l0_w_mod, l0_b_mod, l0_norm1, l0_norm2, l0_w_qkv, l0_q_norm, l0_k_norm, l0_rel_bias, l0_w_o, l0_w_ffn_in, l0_w_ffn_out, l1_w_mod, l1_b_mod, l1_norm1, l1_norm2, l1_w_qkv, l1_q_norm, l1_k_norm, l1_sink, l1_w_o, l1_w_router, l1_b_router, l1_w_exp_in, l1_w_exp_out, l2_w_mod, l2_b_mod, l2_norm1, l2_norm2, l2_w_qkv, l2_q_norm, l2_k_norm, l2_w_o, l2_w_ffn_in, l2_w_ffn_out, l3_w_mod, l3_b_mod, l3_norm1, l3_norm2, l3_w_qkv, l3_q_norm, l3_k_norm, l3_rel_bias, l3_w_o, l3_w_router, l3_b_router, l3_w_exp_in, l3_w_exp_out):
    b, s_len, d = x.shape
    n_ctx = ctx.shape[1]
    dims = (b, s_len, n_ctx)
    layers = (
        dict(w_mod=l0_w_mod, b_mod=l0_b_mod, norm1=l0_norm1, norm2=l0_norm2, w_qkv=l0_w_qkv, q_norm=l0_q_norm,
             k_norm=l0_k_norm, rel_bias=l0_rel_bias, w_o=l0_w_o, w_ffn_in=l0_w_ffn_in, w_ffn_out=l0_w_ffn_out),
        dict(w_mod=l1_w_mod, b_mod=l1_b_mod, norm1=l1_norm1, norm2=l1_norm2, w_qkv=l1_w_qkv, q_norm=l1_q_norm,
             k_norm=l1_k_norm, sink=l1_sink, w_o=l1_w_o, w_router=l1_w_router, b_router=l1_b_router,
             w_exp_in=l1_w_exp_in, w_exp_out=l1_w_exp_out),
        dict(w_mod=l2_w_mod, b_mod=l2_b_mod, norm1=l2_norm1, norm2=l2_norm2, w_qkv=l2_w_qkv, q_norm=l2_q_norm,
             k_norm=l2_k_norm, w_o=l2_w_o, w_ffn_in=l2_w_ffn_in, w_ffn_out=l2_w_ffn_out),
        dict(w_mod=l3_w_mod, b_mod=l3_b_mod, norm1=l3_norm1, norm2=l3_norm2, w_qkv=l3_w_qkv, q_norm=l3_q_norm,
             k_norm=l3_k_norm, rel_bias=l3_rel_bias, w_o=l3_w_o, w_router=l3_w_router, b_router=l3_b_router,
             w_exp_in=l3_w_exp_in, w_exp_out=l3_w_exp_out),
    )
    xs = (x.reshape(b * s_len, d), ctx.reshape(b * n_ctx, d))
    pad_rows = -(b + 1) % 8
    cond = jnp.concatenate([c, c_ctx[None, :], jnp.zeros((pad_rows, d), F32)], axis=0)
    n_layers = len(layers)
    for i, p in enumerate(layers):
        xs = _layer(xs, cond, p, dims, i % 3, i == n_layers - 1)
    return xs.reshape(b, s_len, d)
```

```python
import functools

import jax
import jax.numpy as jnp
from jax import lax
from jax.experimental import pallas as pl
from jax.experimental.pallas import tpu as pltpu

F32 = jnp.float32
BF16 = jnp.bfloat16

GRID_W = 64
NORM_EPS = 1e-6
ROPE_THETA = 10000.0
NA_WIN_ROWS = 8
NA_WIN_COLS = 16
SW_WINDOW = 128
N_EXPERTS = 8
MIXER_HEADS = ((16, 16, 64), (16, 4, 64), (8, 4, 128))

LANES = 128
VMEM_LIMIT_BYTES = 56 * 1024 * 1024
NEG_BIG = -1e30
LOG2E = 1.4426950408889634

QKV_CHUNK = 512
GLB_TQ = 512
GLB_CHUNK = 256
SWA_TQ = 256
SWA_TILES_PER_STEP = 2
NA_CTX_ROWS = 1024
NA_UNROLL = 32
MOE_TM = 1024
MOE_CHUNKS = 4
MOE_FIRST_CHUNK_DIV = 16
CAST_BLOCK_BYTES = 8 * 1024 * 1024
FFN_TF = 512


def _params(*sem):
    return pltpu.CompilerParams(dimension_semantics=sem, vmem_limit_bytes=VMEM_LIMIT_BYTES)


def _row_tile(n_lat_per_sample, n_ctx_rows, cap):
    for tm in (1024, 512, 256, 128):
        if tm <= cap and n_lat_per_sample % tm == 0 and n_ctx_rows % tm == 0:
            return tm
    raise ValueError("no row tile fits")


def _split_bf16(a):
    hi = a.astype(BF16)
    lo = (a - hi.astype(F32)).astype(BF16)
    return hi, lo


def _dot(a, b):
    return jnp.dot(a, b, preferred_element_type=F32)


def _dot_t(a, b):
    return lax.dot_general(a, b, (((1,), (1,)), ((), ())), preferred_element_type=F32)


def _silu(g):
    return g / (1.0 + jnp.exp(-g))


def _rms_mod(x, gain, scale, shift):
    ms = jnp.mean(x * x, axis=-1, keepdims=True)
    return x * lax.rsqrt(ms + NORM_EPS) * gain * (1.0 + scale) + shift


def _adaln_kernel(c_ref, w_ref, b_ref, o_ref):
    a_hi, a_lo = _split_bf16(_silu(c_ref[...]))
    w_hi, w_lo = _split_bf16(w_ref[...])
    o_ref[...] = _dot(a_hi, w_hi) + _dot(a_hi, w_lo) + _dot(a_lo, w_hi) + b_ref[...]


def _adaln(cond, w_mod, b_mod):
    r, d = cond.shape
    n = w_mod.shape[1]
    tn = 1536
    out = pl.pallas_call(
        _adaln_kernel,
        grid=(n // tn,),
        in_specs=[pl.BlockSpec((r, d), lambda j: (0, 0)),
                  pl.BlockSpec((d, tn), lambda j: (0, j)),
                  pl.BlockSpec((1, tn), lambda j: (0, j))],
        out_specs=pl.BlockSpec((r, tn), lambda j: (0, j)),
        out_shape=jax.ShapeDtypeStruct((r, n), F32),
        compiler_params=_params("arbitrary"),
        name="adaln",
    )(cond, w_mod, b_mod.reshape(1, n))
    return out.reshape(r, 6, d)


def _rot_half(z, dh):
    if dh == LANES:
        return pltpu.roll(z, LANES // 2, axis=1)
    lane = lax.broadcasted_iota(jnp.int32, z.shape, 1)
    from_right = pltpu.roll(z, LANES - dh // 2, axis=1)
    from_left = pltpu.roll(z, dh // 2, axis=1)
    return jnp.where((lane % dh) < dh // 2, from_right, from_left)


def _head_mean_sq(z, dh):
    z2 = z * z
    if dh == LANES:
        return jnp.broadcast_to(jnp.sum(z2, axis=-1, keepdims=True), z.shape) * (1.0 / dh)
    low = lax.broadcasted_iota(jnp.int32, z.shape, 1) < dh
    s_low = jnp.sum(jnp.where(low, z2, 0.0), axis=-1, keepdims=True)
    s_high = jnp.sum(jnp.where(low, 0.0, z2), axis=-1, keepdims=True)
    return jnp.where(low, s_low, s_high) * (1.0 / dh)


def _row_sources(arrays, tm, n_lat_tiles):
    lat, ctx = arrays
    return [pl.BlockSpec((tm, lat.shape[1]), lambda i: (jnp.minimum(i, n_lat_tiles - 1), 0)),
            pl.BlockSpec((tm, ctx.shape[1]), lambda i: (jnp.maximum(i - n_lat_tiles, 0), 0))]


def _pick_rows(refs, n_lat_tiles):
    if len(refs) == 1:
        return refs[0][...]
    return jnp.where(pl.program_id(0) < n_lat_tiles, refs[0][...], refs[1][...])


def _qkv_kernel(*refs, n_q, n_kv, dh, rope, n_x, n_lat_tiles):
    x = _pick_rows(refs[:n_x], n_lat_tiles)
    refs = refs[n_x:]
    if rope:
        mod_ref, n1_ref, w_ref, g_ref, cos_ref, sin_ref, q_ref, k_ref, v_ref = refs
    else:
        mod_ref, n1_ref, w_ref, g_ref, q_ref, k_ref, v_ref = refs
    h = _rms_mod(x, n1_ref[...], mod_ref[1:2, :], mod_ref[0:1, :]).astype(BF16)
    n_qk = n_q + n_kv
    cw = QKV_CHUNK
    for c in range((n_qk + n_kv) // cw):
        y = _dot(h, w_ref[:, c * cw:(c + 1) * cw])
        for s in range(cw // LANES):
            col = c * cw + s * LANES
            z = y[:, s * LANES:(s + 1) * LANES]
            if col < n_qk:
                z = z * lax.rsqrt(_head_mean_sq(z, dh) + NORM_EPS) * g_ref[:, col:col + LANES]
                if rope:
                    z = z * cos_ref[...] + _rot_half(z, dh) * sin_ref[...]
            z = z.astype(BF16)
            if col < n_q:
                q_ref[:, col:col + LANES] = z
            elif col < n_qk:
                k_ref[:, col - n_q:col - n_q + LANES] = z
            else:
                v_ref[:, col - n_qk:col - n_qk + LANES] = z


def _rope_tables(s_len, dh, tm):
    n_freq = dh // 4
    inv_freq = ROPE_THETA ** (-jnp.arange(n_freq, dtype=F32) / n_freq)
    t = jnp.arange(s_len)
    row = (t // GRID_W).astype(F32)
    col = (t % GRID_W).astype(F32)
    ang = jnp.concatenate([row[:, None] * inv_freq, col[:, None] * inv_freq], axis=-1)
    cos, sin = jnp.cos(ang), jnp.sin(ang)
    reps = LANES // dh
    cos_t = jnp.tile(jnp.concatenate([cos, cos], axis=-1), (1, reps))
    sin_t = jnp.tile(jnp.concatenate([-sin, sin], axis=-1), (1, reps))
    cos_t = jnp.concatenate([cos_t, jnp.ones((tm, LANES), F32)], axis=0)
    sin_t = jnp.concatenate([sin_t, jnp.zeros((tm, LANES), F32)], axis=0)
    return cos_t, sin_t


def _qkv(xs, mod, norm1, w_qkv, q_gain, k_gain, dims, mixer):
    b, s_len, n_ctx = dims
    x_parts = list(xs) if isinstance(xs, tuple) else [xs]
    n_rows, d = sum(a.shape[0] for a in x_parts), x_parts[0].shape[1]
    n_qh, n_kvh, dh = MIXER_HEADS[mixer]
    n_q, n_kv = n_qh * dh, n_kvh * dh
    rope = mixer != 0
    tm = _row_tile(s_len, b * n_ctx, 512)
    n_lat_tiles = b * s_len // tm
    tiles_per_sample = s_len // tm

    gains = jnp.concatenate([jnp.tile(q_gain * (dh ** -0.5 * LOG2E), n_qh),
                             jnp.tile(k_gain, n_kvh)]).reshape(1, n_q + n_kv)

    def mod_idx(i):
        return jnp.where(i < n_lat_tiles, i // tiles_per_sample, b)

    x_specs = [pl.BlockSpec((tm, d), lambda i: (i, 0))] if len(x_parts) == 1 else _row_sources(x_parts, tm, n_lat_tiles)
    in_specs = x_specs + [pl.BlockSpec((None, 6, d), lambda i: (mod_idx(i), 0, 0)),
                          pl.BlockSpec((1, d), lambda i: (0, 0)),
                          pl.BlockSpec(w_qkv.shape, lambda i: (0, 0)),
                          pl.BlockSpec(gains.shape, lambda i: (0, 0))]
    args = x_parts + [mod, norm1.reshape(1, d), w_qkv.astype(BF16), gains]
    if rope:
        cos_t, sin_t = _rope_tables(s_len, dh, tm)

        def pos_idx(i):
            return jnp.where(i < n_lat_tiles, i % tiles_per_sample, tiles_per_sample)

        in_specs += [pl.BlockSpec((tm, LANES), lambda i: (pos_idx(i), 0))] * 2
        args += [cos_t, sin_t]
    return pl.pallas_call(
        functools.partial(_qkv_kernel, n_q=n_q, n_kv=n_kv, dh=dh, rope=rope, n_x=len(x_parts),
                          n_lat_tiles=n_lat_tiles),
        grid=(n_rows // tm,),
        in_specs=in_specs,
        out_specs=[pl.BlockSpec((tm, n_q), lambda i: (i, 0)),
                   pl.BlockSpec((tm, n_kv), lambda i: (i, 0)),
                   pl.BlockSpec((tm, n_kv), lambda i: (i, 0))],
        out_shape=[jax.ShapeDtypeStruct((n_rows, n_q), BF16),
                   jax.ShapeDtypeStruct((n_rows, n_kv), BF16),
                   jax.ShapeDtypeStruct((n_rows, n_kv), BF16)],
        compiler_params=_params("arbitrary"),
        name="qkv_m%d" % mixer,
    )(*args)


def _stack_heads(q, dh, nh):
    if dh == LANES:
        return jnp.concatenate([q[:, h * LANES:(h + 1) * LANES] for h in range(nh)], axis=0)
    per_kv = nh // 2
    tq = q.shape[0]
    lane = lax.broadcasted_iota(jnp.int32, (tq, LANES), 1)
    ops = []
    for h in range(nh):
        slot = h // per_kv
        chunk = q[:, (h // 2) * LANES:(h // 2 + 1) * LANES].astype(F32)
        if h % 2 != slot:
            chunk = pltpu.roll(chunk, dh, axis=1)
        keep = lane < dh if slot == 0 else lane >= dh
        ops.append(jnp.where(keep, chunk, 0.0).astype(BF16))
    return jnp.concatenate(ops, axis=0)


def _unstack_heads(o, dh, nh):
    tq = o.shape[0] // nh
    if dh == LANES:
        return jnp.concatenate([o[h * tq:(h + 1) * tq] for h in range(nh)], axis=1)
    per_kv = nh // 2
    lane = lax.broadcasted_iota(jnp.int32, (tq, LANES), 1)
    chunks = []
    for c in range(nh // 2):
        parts = []
        for h in (2 * c, 2 * c + 1):
            oh = o[h * tq:(h + 1) * tq]
            if h % 2 != h // per_kv:
                oh = pltpu.roll(oh, dh, axis=1)
            parts.append(oh)
        chunks.append(jnp.where(lane < dh, parts[0], parts[1]))
    return jnp.concatenate(chunks, axis=1)


def _lanes(x, n):
    return x if n == LANES else jnp.concatenate([x] * (n // LANES), axis=1)


def _block_max(s):
    return functools.reduce(jnp.maximum, [s[:, j * LANES:(j + 1) * LANES] for j in range(s.shape[1] // LANES)])


def _row_max(blk):
    return jnp.broadcast_to(jnp.max(blk, axis=-1, keepdims=True), blk.shape)


def _dot_row_halves(p, v):
    half = p.shape[0] // 2
    return jnp.concatenate([_dot(p[0:half, :], v), _dot(p[half:, :], v)], axis=0)


def _ones_ext(v):
    return jnp.concatenate([v, jnp.ones_like(v)], axis=1)


def _sink_rows(sink_ref, first_head, nh, tq):
    return jnp.concatenate([jnp.full((tq, LANES), sink_ref[first_head + h] * LOG2E, F32) for h in range(nh)], axis=0)


def _na_kernel(q_ref, k_ref, v_ref, kc_ref, vc_ref, bias_ref, o_ref,
               qs_ref, vext_ref, mc_ref, numc_ref, lc_ref, *, rows):
    s_len = q_ref.shape[0]
    band = NA_WIN_ROWS * GRID_W
    lane = lax.broadcasted_iota(jnp.int32, (s_len, LANES), 1)
    q = q_ref[...]
    qs_ref[0:s_len, :] = jnp.where(lane < 64, q, jnp.zeros_like(q))
    qs_ref[s_len:, :] = jnp.where(lane >= 64, q, jnp.zeros_like(q))
    vext_ref[:, 0:LANES] = v_ref[...]
    vext_ref[:, LANES:] = jnp.ones((s_len, LANES), BF16)

    kc = kc_ref[...]
    vcx = _ones_ext(vc_ref[...])
    for c in range(2 * s_len // NA_CTX_ROWS):
        rs = slice(c * NA_CTX_ROWS, (c + 1) * NA_CTX_ROWS)
        s = _dot_t(qs_ref[rs, :], kc)
        m = _row_max(_block_max(s))
        acc = _dot(jnp.exp2(s - _lanes(m, s.shape[1])).astype(BF16), vcx)
        mc_ref[rs, :] = m
        numc_ref[rs, :] = acc[:, :LANES]
        lc_ref[rs, :] = acc[:, LANES:]

    lane_q = lax.broadcasted_iota(jnp.int32, (GRID_W, LANES), 1)

    def both_heads(ref, q0):
        return jnp.concatenate([ref[pl.ds(q0, GRID_W), :], ref[pl.ds(s_len + q0, GRID_W), :]], axis=0)

    def one_row(r, carry):
        r0 = jnp.clip(r - NA_WIN_ROWS // 2, 0, rows - NA_WIN_ROWS)
        q0 = pl.multiple_of(r * GRID_W, GRID_W)
        k0 = pl.multiple_of(r0 * GRID_W, GRID_W)
        d0 = r0 - r + NA_WIN_ROWS - 1
        bias = jnp.concatenate(
            [jnp.concatenate([bias_ref[0, d0 + 2 * j], bias_ref[1, d0 + 2 * j]], axis=0)
             for j in range(NA_WIN_ROWS // 2)], axis=1)
        qs = both_heads(qs_ref, q0)
        half = band // 2
        k1 = pl.multiple_of(k0 + half, GRID_W)
        s = jnp.concatenate([_dot_t(qs, k_ref[pl.ds(k0, half), :]), _dot_t(qs, k_ref[pl.ds(k1, half), :])],
                            axis=1) + bias
        mc = both_heads(mc_ref, q0)
        m = jnp.maximum(_row_max(_block_max(s)), mc)
        acc = _dot(jnp.exp2(s - _lanes(m, band)).astype(BF16), vext_ref[pl.ds(k0, band), :])
        alpha = jnp.exp2(mc - m)
        num = acc[:, :LANES] + alpha * both_heads(numc_ref, q0)
        den = acc[:, LANES:] + alpha * both_heads(lc_ref, q0)
        o = num / den
        o_ref[pl.ds(q0, GRID_W), :] = jnp.where(lane_q < 64, o[:GRID_W], o[GRID_W:]).astype(BF16)
        return carry

    lax.fori_loop(0, rows, one_row, 0, unroll=NA_UNROLL)


def _na_bias_tables(rel_bias):
    n_h = rel_bias.shape[0]
    col = jnp.arange(GRID_W)
    col_start = jnp.clip(col - NA_WIN_COLS // 2, 0, GRID_W - NA_WIN_COLS)
    in_win = (col[None, :] >= col_start[:, None]) & (col[None, :] < col_start[:, None] + NA_WIN_COLS)
    col_idx = jnp.clip(col[None, :] - col[:, None] + NA_WIN_COLS - 1, 0, 2 * NA_WIN_COLS - 2)
    masked = jnp.where(in_win[None, None], rel_bias[:, :, col_idx] * LOG2E, NEG_BIG)
    pairs = jnp.concatenate([masked[:, :-1], masked[:, 1:]], axis=-1)
    return pairs.reshape((n_h // 2, 2) + pairs.shape[1:])


def _na_attention(q, k, v, rel_bias, dims):
    b, s_len, n_ctx = dims
    rows = s_len // GRID_W
    n_pairs = q.shape[1] // LANES
    bias = _na_bias_tables(rel_bias)
    ctx_blk = b * s_len // n_ctx
    lat = pl.BlockSpec((s_len, LANES), lambda p, i: (i, p))
    ctx = pl.BlockSpec((n_ctx, LANES), lambda p, i: (ctx_blk + i, p))
    return pl.pallas_call(
        functools.partial(_na_kernel, rows=rows),
        grid=(n_pairs, b),
        in_specs=[lat, lat, lat, ctx, ctx,
                  pl.BlockSpec((None,) + bias.shape[1:], lambda p, i: (p, 0, 0, 0, 0))],
        out_specs=lat,
        out_shape=jax.ShapeDtypeStruct((b * s_len, q.shape[1]), BF16),
        scratch_shapes=[pltpu.VMEM((2 * s_len, LANES), BF16),
                        pltpu.VMEM((s_len, 2 * LANES), BF16),
                        pltpu.VMEM((2 * s_len, LANES), F32),
                        pltpu.VMEM((2 * s_len, LANES), F32),
                        pltpu.VMEM((2 * s_len, LANES), F32)],
        compiler_params=_params("arbitrary", "arbitrary"),
        name="attn_na",
    )(q, k, v, k, v, bias)


def _swa_kernel(sink_ref, q_ref, k_ref, v_ref, kc_ref, vc_ref, o_ref, kwin_ref, vwin_ref, *, nh, tq, s_len):
    n_sub = q_ref.shape[0] // tq
    span = tq + 2 * SW_WINDOW
    p_blk, t = pl.program_id(1), pl.program_id(2)

    @pl.when(t == 0)
    def _():
        for j in range(n_sub):
            kwin_ref[j, span:, :] = kc_ref[...]
            vwin_ref[j, span:, 0:LANES] = vc_ref[...]
            vwin_ref[j, :, LANES:] = jnp.ones((vwin_ref.shape[1], LANES), BF16)

    row = lax.broadcasted_iota(jnp.int32, (tq, span), 0)
    col = lax.broadcasted_iota(jnp.int32, (tq, span), 1)
    sink_all = _sink_rows(sink_ref, p_blk * nh, nh, tq)
    kd = 2 * LANES
    for j in range(n_sub):
        q0 = (t * n_sub + j) * tq
        k0 = pl.multiple_of(jnp.clip(q0 - SW_WINDOW, 0, s_len - span), SW_WINDOW)
        kwin_ref[j, 0:span, :] = k_ref[pl.ds(k0, span), :]
        vwin_ref[j, 0:span, 0:LANES] = v_ref[pl.ds(k0, span), :]
        bias = jnp.where(jnp.abs(col - row + (k0 - q0)) <= SW_WINDOW, 0.0, NEG_BIG)
        qs_all = _stack_heads(q_ref[j * tq:(j + 1) * tq, :], 64, nh)
        vwin = vwin_ref[j]
        outs = []
        for h in range(nh):
            qs = qs_all[h * tq:(h + 1) * tq]
            sink = sink_all[h * tq:(h + 1) * tq]
            s = jnp.concatenate([_dot_t(qs, kwin_ref[j, c * kd:(c + 1) * kd, :])
                                 for c in range(kwin_ref.shape[1] // kd)], axis=1)
            s = jnp.concatenate([s[:, :span] + bias, s[:, span:]], axis=1)
            m = jnp.maximum(_row_max(_block_max(s)), sink)
            acc = _dot(jnp.exp2(s - _lanes(m, s.shape[1])).astype(BF16), vwin)
            outs.append(acc[:, :LANES] / (acc[:, LANES:] + jnp.exp2(sink - m)))
        o_ref[j * tq:(j + 1) * tq, :] = _unstack_heads(jnp.concatenate(outs, axis=0), 64, nh).astype(BF16)


def _swa_attention(q, k, v, sink, dims):
    b, s_len, n_ctx = dims
    tq = SWA_TQ
    rows = tq * SWA_TILES_PER_STEP
    n_t = s_len // rows
    n_kvblk = k.shape[1] // LANES
    nh = q.shape[1] // k.shape[1] * 2
    qw = nh * 64
    ctx_blk = b * s_len // n_ctx
    n_win = tq + 2 * SW_WINDOW + n_ctx
    assert n_win % (2 * LANES) == 0
    qspec = pl.BlockSpec((rows, qw), lambda i, p, t, *_: (i * n_t + t, p))
    lat = pl.BlockSpec((s_len, LANES), lambda i, p, t, *_: (i, p))
    ctx = pl.BlockSpec((n_ctx, LANES), lambda i, p, t, *_: (ctx_blk + i, p))
    return pl.pallas_call(
        functools.partial(_swa_kernel, nh=nh, tq=tq, s_len=s_len),
        grid_spec=pltpu.PrefetchScalarGridSpec(
            num_scalar_prefetch=1,
            grid=(b, n_kvblk, n_t),
            in_specs=[qspec, lat, lat, ctx, ctx],
            out_specs=qspec,
            scratch_shapes=[pltpu.VMEM((SWA_TILES_PER_STEP, n_win, LANES), BF16),
                            pltpu.VMEM((SWA_TILES_PER_STEP, n_win, 2 * LANES), BF16)]),
        out_shape=jax.ShapeDtypeStruct((b * s_len, q.shape[1]), BF16),
        compiler_params=_params("arbitrary", "arbitrary", "arbitrary"),
        name="attn_swa",
    )(sink, q, k, v, k, v)


def _global_kernel(q_ref, k_ref, v_ref, kc_ref, vc_ref, o_ref, kall_ref, vext_ref, s_ref, p_ref, *, nh, s_len):
    @pl.when(pl.program_id(2) == 0)
    def _():
        kall_ref[0:s_len, :] = k_ref[...]
        kall_ref[s_len:, :] = kc_ref[...]
        vext_ref[0:s_len, 0:LANES] = v_ref[...]
        vext_ref[s_len:, 0:LANES] = vc_ref[...]
        vext_ref[:, LANES:] = jnp.ones((vext_ref.shape[0], LANES), BF16)

    qs = _stack_heads(q_ref[...], LANES, nh)
    n_chunks = kall_ref.shape[0] // GLB_CHUNK
    mrun = None
    for c in range(n_chunks):
        ks = slice(c * GLB_CHUNK, (c + 1) * GLB_CHUNK)
        s = _dot_t(qs, kall_ref[ks, :])
        s_ref[:, ks] = s
        blk = _block_max(s)
        mrun = blk if mrun is None else jnp.maximum(mrun, blk)
    m = _lanes(_row_max(mrun), GLB_CHUNK)
    for c in range(n_chunks):
        ks = slice(c * GLB_CHUNK, (c + 1) * GLB_CHUNK)
        p_ref[:, ks] = jnp.exp2(s_ref[:, ks] - m).astype(BF16)
    acc = _dot_row_halves(p_ref, vext_ref[...])
    o_ref[...] = _unstack_heads(acc[:, :LANES] / acc[:, LANES:], LANES, nh).astype(BF16)


def _global_attention(q, k, v, dims):
    b, s_len, n_ctx = dims
    tq = GLB_TQ
    n_t = s_len // tq
    n_kvblk = k.shape[1] // LANES
    nh = q.shape[1] // k.shape[1]
    qw = nh * LANES
    ctx_blk = b * s_len // n_ctx
    n_keys = s_len + n_ctx
    qspec = pl.BlockSpec((tq, qw), lambda i, p, t: (i * n_t + t, p))
    lat = pl.BlockSpec((s_len, LANES), lambda i, p, t: (i, p))
    ctx = pl.BlockSpec((n_ctx, LANES), lambda i, p, t: (ctx_blk + i, p))
    return pl.pallas_call(
        functools.partial(_global_kernel, nh=nh, s_len=s_len),
        grid=(b, n_kvblk, n_t),
        in_specs=[qspec, lat, lat, ctx, ctx],
        out_specs=qspec,
        out_shape=jax.ShapeDtypeStruct((b * s_len, q.shape[1]), BF16),
        scratch_shapes=[pltpu.VMEM((n_keys, LANES), BF16),
                        pltpu.VMEM((n_keys, 2 * LANES), BF16),
                        pltpu.VMEM((nh * tq, n_keys), F32),
                        pltpu.VMEM((nh * tq, n_keys), BF16)],
        compiler_params=_params("arbitrary", "arbitrary", "arbitrary"),
        name="attn_global",
    )(q, k, v, k, v)


def _ctx_kernel(sink_ref, q_ref, kc_ref, vc_ref, o_ref, *, dh, nh, use_sink):
    tq = q_ref.shape[0]
    qw = nh * dh
    for p in range(kc_ref.shape[1] // LANES):
        kv = slice(p * LANES, (p + 1) * LANES)
        qs = _stack_heads(q_ref[:, p * qw:(p + 1) * qw], dh, nh)
        s = _dot_t(qs, kc_ref[:, kv])
        m = _row_max(_block_max(s))
        if use_sink:
            sink = _sink_rows(sink_ref, p * nh, nh, tq)
            m = jnp.maximum(m, sink)
        acc = _dot(jnp.exp2(s - _lanes(m, s.shape[1])).astype(BF16), _ones_ext(vc_ref[:, kv]))
        den = acc[:, LANES:]
        if use_sink:
            den = den + jnp.exp2(sink - m)
        o_ref[:, p * qw:(p + 1) * qw] = _unstack_heads(acc[:, :LANES] / den, dh, nh).astype(BF16)


def _ctx_attention(q, k, v, sink, dims, mixer):
    b, s_len, n_ctx = dims
    n_qh, n_kvh, dh = MIXER_HEADS[mixer]
    n_kvblk = k.shape[1] // LANES
    nh = n_qh // n_kvblk
    ctx_blk = b * s_len // n_ctx
    use_sink = sink is not None
    if not use_sink:
        sink = jnp.zeros((n_qh,), F32)
    return pl.pallas_call(
        functools.partial(_ctx_kernel, dh=dh, nh=nh, use_sink=use_sink),
        grid_spec=pltpu.PrefetchScalarGridSpec(
            num_scalar_prefetch=1,
            grid=(b,),
            in_specs=[pl.BlockSpec((n_ctx, q.shape[1]), lambda i, *_: (ctx_blk + i, 0)),
                      pl.BlockSpec((n_ctx, k.shape[1]), lambda i, *_: (ctx_blk + i, 0)),
                      pl.BlockSpec((n_ctx, v.shape[1]), lambda i, *_: (ctx_blk + i, 0))],
            out_specs=pl.BlockSpec((n_ctx, q.shape[1]), lambda i, *_: (i, 0))),
        out_shape=jax.ShapeDtypeStruct((b * n_ctx, q.shape[1]), BF16),
        compiler_params=_params("arbitrary"),
        name="attn_ctx_m%d" % mixer,
    )(sink, q, k, v)


def _oproj_kernel(*refs, n_lat_tiles, n_y, n_x, moe):
    y = _pick_rows(refs[:n_y], n_lat_tiles)
    x_in = _pick_rows(refs[n_y:n_y + n_x], n_lat_tiles)
    refs = refs[n_y + n_x:]
    mod_ref, n2_ref, wo_ref = refs[:3]
    refs = refs[3:]
    if moe:
        wr_ref, br_ref = refs[:2]
        refs = refs[2:]
    xo_ref, h_ref = refs[:2]
    x = x_in + mod_ref[2:3, :] * _dot(y, wo_ref[...])
    xo_ref[...] = x
    h = _rms_mod(x, n2_ref[...], mod_ref[4:5, :], mod_ref[3:4, :])
    h_ref[...] = h.astype(BF16)
    if moe:
        route_ref, counts_ref, base_ref = refs[2:5]
        h_hi, h_lo = _split_bf16(h)
        w_hi, w_lo = _split_bf16(wr_ref[...])
        hi_terms = _dot(h_hi, jnp.concatenate([w_hi, w_lo], axis=1))
        lo_terms = _dot(h_lo, jnp.concatenate([w_hi, jnp.zeros_like(w_hi)], axis=1))
        logits = hi_terms[:, :LANES] + hi_terms[:, LANES:] + lo_terms[:, :LANES] + br_ref[...]
        lane = lax.broadcasted_iota(jnp.int32, logits.shape, 1)
        v1 = jnp.max(logits, axis=-1, keepdims=True)
        i1 = jnp.min(jnp.where(logits == v1, lane, LANES), axis=-1, keepdims=True)
        rest = jnp.where(lane == i1, NEG_BIG, logits)
        v2 = jnp.max(rest, axis=-1, keepdims=True)
        i2 = jnp.min(jnp.where(rest == v2, lane, LANES), axis=-1, keepdims=True)
        e = jnp.exp(v2 - v1)
        w1 = 1.0 / (1.0 + e)
        w2 = e / (1.0 + e)
        @pl.when(pl.program_id(0) == 0)
        def _():
            base_ref[...] = jnp.zeros_like(base_ref)

        tm = logits.shape[0]
        pick1 = lane == i1
        pick2 = lane == i2
        cnt = jnp.where(pick1, 1.0, 0.0) + jnp.where(pick2, 1.0, 0.0)
        earlier = lax.broadcasted_iota(jnp.int32, (tm, tm), 0) > lax.broadcasted_iota(jnp.int32, (tm, tm), 1)
        before = _dot(jnp.where(earlier, 1.0, 0.0).astype(BF16), cnt.astype(BF16)) + base_ref[0:1, :]
        r1 = jnp.sum(jnp.where(pick1, before, 0.0), axis=-1, keepdims=True)
        r2 = jnp.sum(jnp.where(pick2, before, 0.0), axis=-1, keepdims=True)
        base_ref[...] = base_ref[...] + jnp.sum(cnt, axis=0, keepdims=True)
        counts_ref[...] = base_ref[...]
        route = jnp.where(lane == 0, i1.astype(F32), 0.0)
        route = jnp.where(lane == 1, i2.astype(F32), route)
        route = jnp.where(lane == 2, w1, route)
        route = jnp.where(lane == 3, w2, route)
        route = jnp.where(lane == 4, r1, route)
        route = jnp.where(lane == 5, r2, route)
        route_ref[...] = route


def _oproj(y_lat, y_ctx, xs, mod, norm2, w_o, router, dims, last):
    b, s_len, n_ctx = dims
    x_parts = list(xs) if isinstance(xs, tuple) else [xs]
    d = x_parts[0].shape[1]
    tm = _row_tile(s_len, b * n_ctx, 512)
    n_lat_tiles = b * s_len // tm
    tiles_per_sample = s_len // tm
    n_rows = b * s_len if last else b * (s_len + n_ctx)
    n_tiles = n_rows // tm
    moe = router is not None

    def mod_idx(i):
        return jnp.where(i < n_lat_tiles, i // tiles_per_sample, b)

    row = pl.BlockSpec((tm, d), lambda i: (i, 0))
    y_parts = [y_lat] if last else [y_lat, y_ctx]
    in_specs = ([pl.BlockSpec((tm, y_lat.shape[1]), lambda i: (i, 0))] if last
                else _row_sources(y_parts, tm, n_lat_tiles))
    in_specs += [row] if len(x_parts) == 1 else _row_sources(x_parts, tm, n_lat_tiles)
    in_specs += [pl.BlockSpec((None, 6, d), lambda i: (mod_idx(i), 0, 0)),
                 pl.BlockSpec((1, d), lambda i: (0, 0)),
                 pl.BlockSpec(w_o.shape, lambda i: (0, 0))]
    args = y_parts + x_parts + [mod, norm2.reshape(1, d), w_o.astype(BF16)]
    out_specs = [row, row]
    out_shape = [jax.ShapeDtypeStruct((n_rows, d), F32), jax.ShapeDtypeStruct((n_rows, d), BF16)]
    if moe:
        w_router, b_router = router
        n_e = w_router.shape[1]
        wr = jnp.pad(w_router, ((0, 0), (0, LANES - n_e)))
        br = jnp.pad(b_router, (0, LANES - n_e), constant_values=NEG_BIG).reshape(1, LANES)
        in_specs += [pl.BlockSpec(wr.shape, lambda i: (0, 0)), pl.BlockSpec(br.shape, lambda i: (0, 0))]
        args += [wr, br]
        out_specs += [pl.BlockSpec((tm, LANES), lambda i: (i, 0)), pl.BlockSpec((8, LANES), lambda i: (0, 0))]
        out_shape += [jax.ShapeDtypeStruct((n_rows, LANES), F32), jax.ShapeDtypeStruct((8, LANES), F32)]
    return pl.pallas_call(
        functools.partial(_oproj_kernel, n_lat_tiles=n_lat_tiles, n_y=len(y_parts), n_x=len(x_parts), moe=moe),
        grid=(n_tiles,),
        in_specs=in_specs,
        out_specs=out_specs,
        out_shape=out_shape,
        scratch_shapes=[pltpu.VMEM((8, LANES), F32)] if moe else [],
        compiler_params=_params("arbitrary"),
        name="oproj_moe" if moe else "oproj",
    )(*args)


def _swiglu_partial(h_ref, wg_ref, wu_ref, wd_ref):
    h = h_ref[...]
    a = _silu(_dot(h, wg_ref[...].astype(BF16))) * _dot(h, wu_ref[...].astype(BF16))
    return _dot(a.astype(BF16), wd_ref[...].astype(BF16))


def _accumulate_over_hidden(f, n_f, acc_ref, partial, finish):
    @pl.when(f == 0)
    def _():
        acc_ref[...] = partial()

    @pl.when(jnp.logical_and(f > 0, f < n_f - 1))
    def _():
        acc_ref[...] += partial()

    @pl.when(f == n_f - 1)
    def _():
        finish(acc_ref[...] + partial())


def _ffn_kernel(h_ref, wg_ref, wu_ref, wd_ref, x_ref, mod_ref, o_ref, acc_ref):
    def finish(total):
        o_ref[...] = x_ref[...] + mod_ref[5:6, :] * total

    _accumulate_over_hidden(pl.program_id(1), pl.num_programs(1), acc_ref,
                            functools.partial(_swiglu_partial, h_ref, wg_ref, wu_ref, wd_ref), finish)


def _ffn(h, xs, mod, w_in, w_out, dims):
    b, s_len, n_ctx = dims
    n_rows, d = h.shape
    d_ff = w_out.shape[0]
    tm = _row_tile(s_len, b * n_ctx, 1024)
    tf = FFN_TF
    n_f = d_ff // tf
    n_lat_tiles = b * s_len // tm
    tiles_per_sample = s_len // tm

    def mod_idx(i):
        return jnp.where(i < n_lat_tiles, i // tiles_per_sample, b)

    w_in = _to_bf16(w_in[None])[0]
    w_out = _to_bf16(w_out[None])[0]
    return pl.pallas_call(
        _ffn_kernel,
        grid=(n_rows // tm, n_f),
        in_specs=[pl.BlockSpec((tm, d), lambda i, f: (i, 0)),
                  pl.BlockSpec((d, tf), lambda i, f: (0, f)),
                  pl.BlockSpec((d, tf), lambda i, f: (0, n_f + f)),
                  pl.BlockSpec((tf, d), lambda i, f: (f, 0)),
                  pl.BlockSpec((tm, d), lambda i, f: (i, 0)),
                  pl.BlockSpec((None, 6, d), lambda i, f: (mod_idx(i), 0, 0))],
        out_specs=pl.BlockSpec((tm, d), lambda i, f: (i, 0)),
        out_shape=jax.ShapeDtypeStruct((n_rows, d), F32),
        scratch_shapes=[pltpu.VMEM((tm, d), F32)],
        compiler_params=_params("arbitrary", "arbitrary"),
        name="ffn_dense",
    )(h, w_in, w_in, w_out, xs, mod)


def _moe_ffn_kernel(te_ref, na_ref, h_ref, wg_ref, wu_ref, wd_ref, *rest):
    o_ref, acc_ref = rest[-2:]
    i, f = pl.program_id(0), pl.program_id(1)

    def finish(total):
        o_ref[...] = total.astype(o_ref.dtype)

    @pl.when(i < na_ref[0])
    def _():
        _accumulate_over_hidden(f, pl.num_programs(1), acc_ref,
                                functools.partial(_swiglu_partial, h_ref, wg_ref, wu_ref, wd_ref), finish)


def _moe_ffn(hs, ys, n_rows_total, first_tile, tile_expert, n_active, w_in, w_out):
    n_rows, d = hs.shape
    d_ff = w_out.shape[1]
    tm, tf = MOE_TM, FFN_TF
    n_f = d_ff // tf

    def tile(i, na):
        return jnp.minimum(i, jnp.maximum(na[0] - 1, 0))

    def expert(i, te, na):
        return te[tile(i, na)]

    in_specs = [pl.BlockSpec((tm, d), lambda i, f, te, na: (tile(i, na), 0)),
                pl.BlockSpec((None, d, tf), lambda i, f, te, na: (expert(i, te, na), 0, f)),
                pl.BlockSpec((None, d, tf), lambda i, f, te, na: (expert(i, te, na), 0, n_f + f)),
                pl.BlockSpec((None, tf, d), lambda i, f, te, na: (expert(i, te, na), f, 0))]
    args = [tile_expert, n_active, hs, w_in, w_in, w_out]
    aliases = {}
    if ys is not None:
        in_specs.append(pl.BlockSpec(memory_space=pl.ANY))
        aliases = {len(args): 0}
        args.append(ys)
    return pl.pallas_call(
        _moe_ffn_kernel,
        grid_spec=pltpu.PrefetchScalarGridSpec(
            num_scalar_prefetch=2,
            grid=(n_rows // tm, n_f),
            in_specs=in_specs,
            out_specs=pl.BlockSpec((tm, d), lambda i, f, te, na: (first_tile + tile(i, na), 0)),
            scratch_shapes=[pltpu.VMEM((tm, d), F32)]),
        out_shape=jax.ShapeDtypeStruct((n_rows_total, d), hs.dtype),
        input_output_aliases=aliases,
        compiler_params=_params("arbitrary", "arbitrary"),
        name="moe_ffn",
    )(*args)


def _combine_kernel(x_ref, a_ref, b_ref, route_ref, mod_ref, o_ref):
    w1 = route_ref[:, 2:3]
    w2 = route_ref[:, 3:4]
    mix = w1 * a_ref[...].astype(F32) + w2 * b_ref[...].astype(F32)
    o_ref[...] = x_ref[...] + mod_ref[5:6, :] * mix


def _combine(xs, ya, yb, route, mod, dims):
    b, s_len, n_ctx = dims
    n_rows, d = ya.shape
    tm = _row_tile(s_len, b * n_ctx, 512)
    n_lat_tiles = b * s_len // tm
    tiles_per_sample = s_len // tm

    def mod_idx(i):
        return jnp.where(i < n_lat_tiles, i // tiles_per_sample, b)

    row = pl.BlockSpec((tm, d), lambda i: (i, 0))
    return pl.pallas_call(
        _combine_kernel,
        grid=(n_rows // tm,),
        in_specs=[row, row, row,
                  pl.BlockSpec((tm, LANES), lambda i: (i, 0)),
                  pl.BlockSpec((None, 6, d), lambda i: (mod_idx(i), 0, 0))],
        out_specs=row,
        out_shape=jax.ShapeDtypeStruct((n_rows, d), F32),
        compiler_params=_params("arbitrary"),
        name="moe_combine",
    )(xs, ya, yb, route, mod)


def _route_plan(idx, rank, counts, tm):
    n = idx.shape[0]
    tiles_per = (counts + tm - 1) // tm
    tile_end = jnp.cumsum(tiles_per)
    tile_start = tile_end - tiles_per
    experts = jnp.arange(N_EXPERTS, dtype=jnp.int32)
    start = jnp.sum(jnp.where(idx[:, :, None] == experts, tile_start, 0), axis=-1)
    slot = start * tm + rank
    n_tiles = (2 * n) // tm + N_EXPERTS
    tile_ids = jnp.arange(n_tiles, dtype=jnp.int32)
    tile_expert = jnp.minimum(jnp.sum((tile_end[None, :] <= tile_ids[:, None]).astype(jnp.int32), axis=1),
                              N_EXPERTS - 1)
    token = jnp.arange(2 * n, dtype=jnp.int32) // 2
    _, token_sorted = lax.sort((slot.reshape(-1), token), num_keys=1)
    first_pair = jnp.cumsum(counts) - counts
    row_in_expert = ((tile_ids - tile_start[tile_expert]) * tm)[:, None] + jnp.arange(tm, dtype=jnp.int32)[None, :]
    pair = jnp.clip(first_pair[tile_expert][:, None] + row_in_expert, 0, 2 * n - 1)
    filler = (tile_ids[:, None] * tm + jnp.arange(tm, dtype=jnp.int32)[None, :]) % n
    src = jnp.where(row_in_expert < counts[tile_expert][:, None], token_sorted[pair], filler).reshape(-1)
    return slot, src, tile_expert, tile_end[-1:].astype(jnp.int32)


def _cast_kernel(w_ref, o_ref):
    o_ref[...] = w_ref[...].astype(o_ref.dtype)


def _to_bf16(w):
    n_e, n_r, n_c = w.shape
    rows = n_r
    while rows * n_c * 4 > CAST_BLOCK_BYTES and rows % 2 == 0 and (rows // 2) % 16 == 0:
        rows //= 2
    return pl.pallas_call(
        _cast_kernel,
        grid=(n_e, n_r // rows),
        in_specs=[pl.BlockSpec((None, rows, n_c), lambda e, r: (e, r, 0))],
        out_specs=pl.BlockSpec((None, rows, n_c), lambda e, r: (e, r, 0)),
        out_shape=jax.ShapeDtypeStruct(w.shape, BF16),
        compiler_params=_params("arbitrary", "arbitrary"),
        name="cast_bf16",
    )(w)


def _take_rows(a, rows):
    return a.at[rows].get(mode="promise_in_bounds")


def _moe(h, xs, route, counts, mod, w_exp_in, w_exp_out, dims):
    idx = route[:, 0:2].astype(jnp.int32)
    rank = route[:, 4:6].astype(jnp.int32)
    slot, src, tile_expert, n_active = _route_plan(idx, rank, counts[0, :N_EXPERTS].astype(jnp.int32), MOE_TM)
    n_tiles = src.shape[0] // MOE_TM
    first = max(1, n_tiles // MOE_FIRST_CHUNK_DIV)
    rest = -(-(n_tiles - first) // (MOE_CHUNKS - 1))
    bounds = [0] + [min(first + j * rest, n_tiles) for j in range(MOE_CHUNKS)]
    ys = None
    for t0, t1 in zip(bounds[:-1], bounds[1:]):
        if t1 == t0:
            continue
        hs = _take_rows(h, src[t0 * MOE_TM:t1 * MOE_TM])
        ys = _moe_ffn(hs, ys, src.shape[0], t0, tile_expert[t0:t1], jnp.clip(n_active - t0, 0, t1 - t0),
                      w_exp_in, w_exp_out)
    ya = _take_rows(ys, slot[:, 0])
    yb = _take_rows(ys, slot[:, 1])
    return _combine(xs, ya, yb, route, mod, dims)


def _layer(xs, cond, p, dims, mixer, last):
    b, s_len, n_ctx = dims
    mod = _adaln(cond, p["w_mod"], p["b_mod"])
    q, k, v = _qkv(xs, mod, p["norm1"], p["w_qkv"], p["q_norm"], p["k_norm"], dims, mixer)
    if mixer == 0:
        y_lat = _na_attention(q, k, v, p["rel_bias"], dims)
    elif mixer == 1:
        y_lat = _swa_attention(q, k, v, p["sink"], dims)
    else:
        y_lat = _global_attention(q, k, v, dims)
    y_ctx = None if last else _ctx_attention(q, k, v, p.get("sink"), dims, mixer)
    router = (p["w_router"], p["b_router"]) if "w_router" in p else None
    outs = _oproj(y_lat, y_ctx, xs, mod, p["norm2"], p["w_o"], router, dims, last)
    if router is None:
        xs, h = outs
        return _ffn(h, xs, mod, p["w_ffn_in"], p["w_ffn_out"], dims)
    xs, h, route, counts = outs
    return _moe(h, xs, route, counts, mod, p["w_exp_in"], p["w_exp_out"], dims)


def kernel(x, c, ctx, c_ctx, l0_w_mod, l0_b_mod, l0_norm1, l0_norm2, l0_w_qkv, l0_q_norm, l0_k_norm, l0_rel_bias, l0_w_o, l0_w_ffn_in, l0_w_ffn_out, l1_w_mod, l1_b_mod, l1_norm1, l1_norm2, l1_w_qkv, l1_q_norm, l1_k_norm, l1_sink, l1_w_o, l1_w_router, l1_b_router, l1_w_exp_in, l1_w_exp_out, l2_w_mod, l2_b_mod, l2_norm1, l2_norm2, l2_w_qkv, l2_q_norm, l2_k_norm, l2_w_o, l2_w_ffn_in, l2_w_ffn_out, l3_w_mod, l3_b_mod, l3_norm1, l3_norm2, l3_w_qkv, l3_q_norm, l3_k_norm, l3_rel_bias, l3_w_o, l3_w_router, l3_b_router, l3_w_exp_in, l3_w_exp_out):
    b, s_len, d = x.shape
    n_ctx = ctx.shape[1]
    dims = (b, s_len, n_ctx)
    layers = (
        dict(w_mod=l0_w_mod, b_mod=l0_b_mod, norm1=l0_norm1, norm2=l0_norm2, w_qkv=l0_w_qkv, q_norm=l0_q_norm,
             k_norm=l0_k_norm, rel_bias=l0_rel_bias, w_o=l0_w_o, w_ffn_in=l0_w_ffn_in, w_ffn_out=l0_w_ffn_out),
        dict(w_mod=l1_w_mod, b_mod=l1_b_mod, norm1=l1_norm1, norm2=l1_norm2, w_qkv=l1_w_qkv, q_norm=l1_q_norm,
             k_norm=l1_k_norm, sink=l1_sink, w_o=l1_w_o, w_router=l1_w_router, b_router=l1_b_router,
             w_exp_in=l1_w_exp_in, w_exp_out=l1_w_exp_out),
        dict(w_mod=l2_w_mod, b_mod=l2_b_mod, norm1=l2_norm1, norm2=l2_norm2, w_qkv=l2_w_qkv, q_norm=l2_q_norm,
             k_norm=l2_k_norm, w_o=l2_w_o, w_ffn_in=l2_w_ffn_in, w_ffn_out=l2_w_ffn_out),
        dict(w_mod=l3_w_mod, b_mod=l3_b_mod, norm1=l3_norm1, norm2=l3_norm2, w_qkv=l3_w_qkv, q_norm=l3_q_norm,
             k_norm=l3_k_norm, rel_bias=l3_rel_bias, w_o=l3_w_o, w_router=l3_w_router, b_router=l3_b_router,
             w_exp_in=l3_w_exp_in, w_exp_out=l3_w_exp_out),
    )
    xs = (x.reshape(b * s_len, d), ctx.reshape(b * n_ctx, d))
    pad_rows = -(b + 1) % 8
    cond = jnp.concatenate([c, c_ctx[None, :], jnp.zeros((pad_rows, d), F32)], axis=0)
    n_layers = len(layers)
    for i, p in enumerate(layers):
        xs = _layer(xs, cond, p, dims, i % 3, i == n_layers - 1)
    return xs.reshape(b, s_len, d)
```

```python
import functools

import jax
import jax.numpy as jnp
from jax import lax
from jax.experimental import pallas as pl
from jax.experimental.pallas import tpu as pltpu

F32 = jnp.float32
BF16 = jnp.bfloat16

GRID_W = 64
NORM_EPS = 1e-6
ROPE_THETA = 10000.0
NA_WIN_ROWS = 8
NA_WIN_COLS = 16
SW_WINDOW = 128
N_EXPERTS = 8
MIXER_HEADS = ((16, 16, 64), (16, 4, 64), (8, 4, 128))

LANES = 128
VMEM_LIMIT_BYTES = 56 * 1024 * 1024
NEG_BIG = -1e30
LOG2E = 1.4426950408889634

QKV_CHUNK = 512
GLB_TQ = 1024
GLB_CHUNK = 256
SWA_TQ = 256
SWA_TILES_PER_STEP = 4
NA_CTX_ROWS = 1024
NA_UNROLL = 32
MOE_TM = 1024
MOE_CHUNKS = 4
MOE_FIRST_CHUNK_DIV = 16
CAST_BLOCK_BYTES = 8 * 1024 * 1024
FFN_TF = 512


def _params(*sem):
    return pltpu.CompilerParams(dimension_semantics=sem, vmem_limit_bytes=VMEM_LIMIT_BYTES)


def _row_tile(n_lat_per_sample, n_ctx_rows, cap):
    for tm in (1024, 512, 256, 128):
        if tm <= cap and n_lat_per_sample % tm == 0 and n_ctx_rows % tm == 0:
            return tm
    raise ValueError("no row tile fits")


def _split_bf16(a):
    hi = a.astype(BF16)
    lo = (a - hi.astype(F32)).astype(BF16)
    return hi, lo


def _dot(a, b):
    return jnp.dot(a, b, preferred_element_type=F32)


def _dot_t(a, b):
    return lax.dot_general(a, b, (((1,), (1,)), ((), ())), preferred_element_type=F32)


def _silu(g):
    return g / (1.0 + jnp.exp(-g))


def _rms_mod(x, gain, scale, shift):
    ms = jnp.mean(x * x, axis=-1, keepdims=True)
    return x * lax.rsqrt(ms + NORM_EPS) * gain * (1.0 + scale) + shift


def _adaln_kernel(c_ref, w_ref, b_ref, o_ref):
    a_hi, a_lo = _split_bf16(_silu(c_ref[...]))
    w_hi, w_lo = _split_bf16(w_ref[...])
    o_ref[...] = _dot(a_hi, w_hi) + _dot(a_hi, w_lo) + _dot(a_lo, w_hi) + b_ref[...]


def _adaln(cond, w_mod, b_mod):
    r, d = cond.shape
    n = w_mod.shape[1]
    tn = 1536
    out = pl.pallas_call(
        _adaln_kernel,
        grid=(n // tn,),
        in_specs=[pl.BlockSpec((r, d), lambda j: (0, 0)),
                  pl.BlockSpec((d, tn), lambda j: (0, j)),
                  pl.BlockSpec((1, tn), lambda j: (0, j))],
        out_specs=pl.BlockSpec((r, tn), lambda j: (0, j)),
        out_shape=jax.ShapeDtypeStruct((r, n), F32),
        compiler_params=_params("arbitrary"),
        name="adaln",
    )(cond, w_mod, b_mod.reshape(1, n))
    return out.reshape(r, 6, d)


def _rot_half(z, dh):
    if dh == LANES:
        return pltpu.roll(z, LANES // 2, axis=1)
    lane = lax.broadcasted_iota(jnp.int32, z.shape, 1)
    from_right = pltpu.roll(z, LANES - dh // 2, axis=1)
    from_left = pltpu.roll(z, dh // 2, axis=1)
    return jnp.where((lane % dh) < dh // 2, from_right, from_left)


def _head_mean_sq(z, dh):
    z2 = z * z
    if dh == LANES:
        return jnp.broadcast_to(jnp.sum(z2, axis=-1, keepdims=True), z.shape) * (1.0 / dh)
    low = lax.broadcasted_iota(jnp.int32, z.shape, 1) < dh
    s_low = jnp.sum(jnp.where(low, z2, 0.0), axis=-1, keepdims=True)
    s_high = jnp.sum(jnp.where(low, 0.0, z2), axis=-1, keepdims=True)
    return jnp.where(low, s_low, s_high) * (1.0 / dh)


def _row_sources(arrays, tm, n_lat_tiles):
    lat, ctx = arrays
    return [pl.BlockSpec((tm, lat.shape[1]), lambda i: (jnp.minimum(i, n_lat_tiles - 1), 0)),
            pl.BlockSpec((tm, ctx.shape[1]), lambda i: (jnp.maximum(i - n_lat_tiles, 0), 0))]


def _pick_rows(refs, n_lat_tiles):
    if len(refs) == 1:
        return refs[0][...]
    return jnp.where(pl.program_id(0) < n_lat_tiles, refs[0][...], refs[1][...])


def _qkv_kernel(*refs, n_q, n_kv, dh, rope, n_x, n_lat_tiles):
    x = _pick_rows(refs[:n_x], n_lat_tiles)
    refs = refs[n_x:]
    if rope:
        mod_ref, n1_ref, w_ref, g_ref, cos_ref, sin_ref, q_ref, k_ref, v_ref = refs
    else:
        mod_ref, n1_ref, w_ref, g_ref, q_ref, k_ref, v_ref = refs
    h = _rms_mod(x, n1_ref[...], mod_ref[1:2, :], mod_ref[0:1, :]).astype(BF16)
    n_qk = n_q + n_kv
    cw = QKV_CHUNK
    for c in range((n_qk + n_kv) // cw):
        y = _dot(h, w_ref[:, c * cw:(c + 1) * cw])
        for s in range(cw // LANES):
            col = c * cw + s * LANES
            z = y[:, s * LANES:(s + 1) * LANES]
            if col < n_qk:
                z = z * lax.rsqrt(_head_mean_sq(z, dh) + NORM_EPS) * g_ref[:, col:col + LANES]
                if rope:
                    z = z * cos_ref[...] + _rot_half(z, dh) * sin_ref[...]
            z = z.astype(BF16)
            if col < n_q:
                q_ref[:, col:col + LANES] = z
            elif col < n_qk:
                k_ref[:, col - n_q:col - n_q + LANES] = z
            else:
                v_ref[:, col - n_qk:col - n_qk + LANES] = z


def _rope_tables(s_len, dh, tm):
    n_freq = dh // 4
    inv_freq = ROPE_THETA ** (-jnp.arange(n_freq, dtype=F32) / n_freq)
    t = jnp.arange(s_len)
    row = (t // GRID_W).astype(F32)
    col = (t % GRID_W).astype(F32)
    ang = jnp.concatenate([row[:, None] * inv_freq, col[:, None] * inv_freq], axis=-1)
    cos, sin = jnp.cos(ang), jnp.sin(ang)
    reps = LANES // dh
    cos_t = jnp.tile(jnp.concatenate([cos, cos], axis=-1), (1, reps))
    sin_t = jnp.tile(jnp.concatenate([-sin, sin], axis=-1), (1, reps))
    cos_t = jnp.concatenate([cos_t, jnp.ones((tm, LANES), F32)], axis=0)
    sin_t = jnp.concatenate([sin_t, jnp.zeros((tm, LANES), F32)], axis=0)
    return cos_t, sin_t


def _qkv(xs, mod, norm1, w_qkv, q_gain, k_gain, dims, mixer):
    b, s_len, n_ctx = dims
    x_parts = list(xs) if isinstance(xs, tuple) else [xs]
    n_rows, d = sum(a.shape[0] for a in x_parts), x_parts[0].shape[1]
    n_qh, n_kvh, dh = MIXER_HEADS[mixer]
    n_q, n_kv = n_qh * dh, n_kvh * dh
    rope = mixer != 0
    tm = _row_tile(s_len, b * n_ctx, 512)
    n_lat_tiles = b * s_len // tm
    tiles_per_sample = s_len // tm

    gains = jnp.concatenate([jnp.tile(q_gain * (dh ** -0.5 * LOG2E), n_qh),
                             jnp.tile(k_gain, n_kvh)]).reshape(1, n_q + n_kv)

    def mod_idx(i):
        return jnp.where(i < n_lat_tiles, i // tiles_per_sample, b)

    x_specs = [pl.BlockSpec((tm, d), lambda i: (i, 0))] if len(x_parts) == 1 else _row_sources(x_parts, tm, n_lat_tiles)
    in_specs = x_specs + [pl.BlockSpec((None, 6, d), lambda i: (mod_idx(i), 0, 0)),
                          pl.BlockSpec((1, d), lambda i: (0, 0)),
                          pl.BlockSpec(w_qkv.shape, lambda i: (0, 0)),
                          pl.BlockSpec(gains.shape, lambda i: (0, 0))]
    args = x_parts + [mod, norm1.reshape(1, d), w_qkv.astype(BF16), gains]
    if rope:
        cos_t, sin_t = _rope_tables(s_len, dh, tm)

        def pos_idx(i):
            return jnp.where(i < n_lat_tiles, i % tiles_per_sample, tiles_per_sample)

        in_specs += [pl.BlockSpec((tm, LANES), lambda i: (pos_idx(i), 0))] * 2
        args += [cos_t, sin_t]
    return pl.pallas_call(
        functools.partial(_qkv_kernel, n_q=n_q, n_kv=n_kv, dh=dh, rope=rope, n_x=len(x_parts),
                          n_lat_tiles=n_lat_tiles),
        grid=(n_rows // tm,),
        in_specs=in_specs,
        out_specs=[pl.BlockSpec((tm, n_q), lambda i: (i, 0)),
                   pl.BlockSpec((tm, n_kv), lambda i: (i, 0)),
                   pl.BlockSpec((tm, n_kv), lambda i: (i, 0))],
        out_shape=[jax.ShapeDtypeStruct((n_rows, n_q), BF16),
                   jax.ShapeDtypeStruct((n_rows, n_kv), BF16),
                   jax.ShapeDtypeStruct((n_rows, n_kv), BF16)],
        compiler_params=_params("arbitrary"),
        name="qkv_m%d" % mixer,
    )(*args)


def _stack_heads(q, dh, nh):
    if dh == LANES:
        return jnp.concatenate([q[:, h * LANES:(h + 1) * LANES] for h in range(nh)], axis=0)
    per_kv = nh // 2
    tq = q.shape[0]
    lane = lax.broadcasted_iota(jnp.int32, (tq, LANES), 1)
    ops = []
    for h in range(nh):
        slot = h // per_kv
        chunk = q[:, (h // 2) * LANES:(h // 2 + 1) * LANES].astype(F32)
        if h % 2 != slot:
            chunk = pltpu.roll(chunk, dh, axis=1)
        keep = lane < dh if slot == 0 else lane >= dh
        ops.append(jnp.where(keep, chunk, 0.0).astype(BF16))
    return jnp.concatenate(ops, axis=0)


def _unstack_heads(o, dh, nh):
    tq = o.shape[0] // nh
    if dh == LANES:
        return jnp.concatenate([o[h * tq:(h + 1) * tq] for h in range(nh)], axis=1)
    per_kv = nh // 2
    lane = lax.broadcasted_iota(jnp.int32, (tq, LANES), 1)
    chunks = []
    for c in range(nh // 2):
        parts = []
        for h in (2 * c, 2 * c + 1):
            oh = o[h * tq:(h + 1) * tq]
            if h % 2 != h // per_kv:
                oh = pltpu.roll(oh, dh, axis=1)
            parts.append(oh)
        chunks.append(jnp.where(lane < dh, parts[0], parts[1]))
    return jnp.concatenate(chunks, axis=1)


def _lanes(x, n):
    return x if n == LANES else jnp.concatenate([x] * (n // LANES), axis=1)


def _block_max(s):
    return functools.reduce(jnp.maximum, [s[:, j * LANES:(j + 1) * LANES] for j in range(s.shape[1] // LANES)])


def _row_max(blk):
    return jnp.broadcast_to(jnp.max(blk, axis=-1, keepdims=True), blk.shape)


def _dot_row_halves(p, v):
    half = p.shape[0] // 2
    return jnp.concatenate([_dot(p[0:half, :], v), _dot(p[half:, :], v)], axis=0)


def _ones_ext(v):
    return jnp.concatenate([v, jnp.ones_like(v)], axis=1)


def _sink_rows(sink_ref, first_head, nh, tq):
    return jnp.concatenate([jnp.full((tq, LANES), sink_ref[first_head + h] * LOG2E, F32) for h in range(nh)], axis=0)


def _na_kernel(q_ref, k_ref, v_ref, kc_ref, vc_ref, bias_ref, o_ref,
               qs_ref, vext_ref, mc_ref, numc_ref, lc_ref, *, rows):
    s_len = q_ref.shape[0]
    band = NA_WIN_ROWS * GRID_W
    lane = lax.broadcasted_iota(jnp.int32, (s_len, LANES), 1)
    q = q_ref[...]
    qs_ref[0:s_len, :] = jnp.where(lane < 64, q, jnp.zeros_like(q))
    qs_ref[s_len:, :] = jnp.where(lane >= 64, q, jnp.zeros_like(q))
    vext_ref[:, 0:LANES] = v_ref[...]
    vext_ref[:, LANES:] = jnp.ones((s_len, LANES), BF16)

    kc = kc_ref[...]
    vcx = _ones_ext(vc_ref[...])
    for c in range(2 * s_len // NA_CTX_ROWS):
        rs = slice(c * NA_CTX_ROWS, (c + 1) * NA_CTX_ROWS)
        s = _dot_t(qs_ref[rs, :], kc)
        m = _row_max(_block_max(s))
        acc = _dot(jnp.exp2(s - _lanes(m, s.shape[1])).astype(BF16), vcx)
        mc_ref[rs, :] = m
        numc_ref[rs, :] = acc[:, :LANES]
        lc_ref[rs, :] = acc[:, LANES:]

    lane_q = lax.broadcasted_iota(jnp.int32, (GRID_W, LANES), 1)

    def both_heads(ref, q0):
        return jnp.concatenate([ref[pl.ds(q0, GRID_W), :], ref[pl.ds(s_len + q0, GRID_W), :]], axis=0)

    def one_row(r, carry):
        r0 = jnp.clip(r - NA_WIN_ROWS // 2, 0, rows - NA_WIN_ROWS)
        q0 = pl.multiple_of(r * GRID_W, GRID_W)
        k0 = pl.multiple_of(r0 * GRID_W, GRID_W)
        d0 = r0 - r + NA_WIN_ROWS - 1
        bias = jnp.concatenate(
            [jnp.concatenate([bias_ref[0, d0 + 2 * j], bias_ref[1, d0 + 2 * j]], axis=0)
             for j in range(NA_WIN_ROWS // 2)], axis=1)
        qs = both_heads(qs_ref, q0)
        half = band // 2
        k1 = pl.multiple_of(k0 + half, GRID_W)
        s = jnp.concatenate([_dot_t(qs, k_ref[pl.ds(k0, half), :]), _dot_t(qs, k_ref[pl.ds(k1, half), :])],
                            axis=1) + bias
        mc = both_heads(mc_ref, q0)
        m = jnp.maximum(_row_max(_block_max(s)), mc)
        acc = _dot(jnp.exp2(s - _lanes(m, band)).astype(BF16), vext_ref[pl.ds(k0, band), :])
        alpha = jnp.exp2(mc - m)
        num = acc[:, :LANES] + alpha * both_heads(numc_ref, q0)
        den = acc[:, LANES:] + alpha * both_heads(lc_ref, q0)
        o = num / den
        o_ref[pl.ds(q0, GRID_W), :] = jnp.where(lane_q < 64, o[:GRID_W], o[GRID_W:]).astype(BF16)
        return carry

    lax.fori_loop(0, rows, one_row, 0, unroll=NA_UNROLL)


def _na_bias_tables(rel_bias):
    n_h = rel_bias.shape[0]
    col = jnp.arange(GRID_W)
    col_start = jnp.clip(col - NA_WIN_COLS // 2, 0, GRID_W - NA_WIN_COLS)
    in_win = (col[None, :] >= col_start[:, None]) & (col[None, :] < col_start[:, None] + NA_WIN_COLS)
    col_idx = jnp.clip(col[None, :] - col[:, None] + NA_WIN_COLS - 1, 0, 2 * NA_WIN_COLS - 2)
    masked = jnp.where(in_win[None, None], rel_bias[:, :, col_idx] * LOG2E, NEG_BIG)
    pairs = jnp.concatenate([masked[:, :-1], masked[:, 1:]], axis=-1)
    return pairs.reshape((n_h // 2, 2) + pairs.shape[1:])


def _na_attention(q, k, v, rel_bias, dims):
    b, s_len, n_ctx = dims
    rows = s_len // GRID_W
    n_pairs = q.shape[1] // LANES
    bias = _na_bias_tables(rel_bias)
    ctx_blk = b * s_len // n_ctx
    lat = pl.BlockSpec((s_len, LANES), lambda p, i: (i, p))
    ctx = pl.BlockSpec((n_ctx, LANES), lambda p, i: (ctx_blk + i, p))
    return pl.pallas_call(
        functools.partial(_na_kernel, rows=rows),
        grid=(n_pairs, b),
        in_specs=[lat, lat, lat, ctx, ctx,
                  pl.BlockSpec((None,) + bias.shape[1:], lambda p, i: (p, 0, 0, 0, 0))],
        out_specs=lat,
        out_shape=jax.ShapeDtypeStruct((b * s_len, q.shape[1]), BF16),
        scratch_shapes=[pltpu.VMEM((2 * s_len, LANES), BF16),
                        pltpu.VMEM((s_len, 2 * LANES), BF16),
                        pltpu.VMEM((2 * s_len, LANES), F32),
                        pltpu.VMEM((2 * s_len, LANES), F32),
                        pltpu.VMEM((2 * s_len, LANES), F32)],
        compiler_params=_params("arbitrary", "arbitrary"),
        name="attn_na",
    )(q, k, v, k, v, bias)


def _swa_kernel(sink_ref, q_ref, k_ref, v_ref, kc_ref, vc_ref, o_ref, kwin_ref, vwin_ref, *, nh, tq, s_len):
    n_sub = q_ref.shape[0] // tq
    span = tq + 2 * SW_WINDOW
    p_blk, t = pl.program_id(1), pl.program_id(2)

    @pl.when(t == 0)
    def _():
        for j in range(n_sub):
            kwin_ref[j, span:, :] = kc_ref[...]
            vwin_ref[j, span:, 0:LANES] = vc_ref[...]
            vwin_ref[j, :, LANES:] = jnp.ones((vwin_ref.shape[1], LANES), BF16)

    row = lax.broadcasted_iota(jnp.int32, (tq, span), 0)
    col = lax.broadcasted_iota(jnp.int32, (tq, span), 1)
    sink_all = _sink_rows(sink_ref, p_blk * nh, nh, tq)
    kd = 2 * LANES
    for j in range(n_sub):
        q0 = (t * n_sub + j) * tq
        k0 = pl.multiple_of(jnp.clip(q0 - SW_WINDOW, 0, s_len - span), SW_WINDOW)
        kwin_ref[j, 0:span, :] = k_ref[pl.ds(k0, span), :]
        vwin_ref[j, 0:span, 0:LANES] = v_ref[pl.ds(k0, span), :]
        bias = jnp.where(jnp.abs(col - row + (k0 - q0)) <= SW_WINDOW, 0.0, NEG_BIG)
        qs_all = _stack_heads(q_ref[j * tq:(j + 1) * tq, :], 64, nh)
        vwin = vwin_ref[j]
        outs = []
        for h in range(nh):
            qs = qs_all[h * tq:(h + 1) * tq]
            sink = sink_all[h * tq:(h + 1) * tq]
            s = jnp.concatenate([_dot_t(qs, kwin_ref[j, c * kd:(c + 1) * kd, :])
                                 for c in range(kwin_ref.shape[1] // kd)], axis=1)
            s = jnp.concatenate([s[:, :span] + bias, s[:, span:]], axis=1)
            m = jnp.maximum(_row_max(_block_max(s)), sink)
            acc = _dot(jnp.exp2(s - _lanes(m, s.shape[1])).astype(BF16), vwin)
            outs.append(acc[:, :LANES] / (acc[:, LANES:] + jnp.exp2(sink - m)))
        o_ref[j * tq:(j + 1) * tq, :] = _unstack_heads(jnp.concatenate(outs, axis=0), 64, nh).astype(BF16)


def _swa_attention(q, k, v, sink, dims):
    b, s_len, n_ctx = dims
    tq = SWA_TQ
    rows = tq * SWA_TILES_PER_STEP
    n_t = s_len // rows
    n_kvblk = k.shape[1] // LANES
    nh = q.shape[1] // k.shape[1] * 2
    qw = nh * 64
    ctx_blk = b * s_len // n_ctx
    n_win = tq + 2 * SW_WINDOW + n_ctx
    assert n_win % (2 * LANES) == 0
    qspec = pl.BlockSpec((rows, qw), lambda i, p, t, *_: (i * n_t + t, p))
    lat = pl.BlockSpec((s_len, LANES), lambda i, p, t, *_: (i, p))
    ctx = pl.BlockSpec((n_ctx, LANES), lambda i, p, t, *_: (ctx_blk + i, p))
    return pl.pallas_call(
        functools.partial(_swa_kernel, nh=nh, tq=tq, s_len=s_len),
        grid_spec=pltpu.PrefetchScalarGridSpec(
            num_scalar_prefetch=1,
            grid=(b, n_kvblk, n_t),
            in_specs=[qspec, lat, lat, ctx, ctx],
            out_specs=qspec,
            scratch_shapes=[pltpu.VMEM((SWA_TILES_PER_STEP, n_win, LANES), BF16),
                            pltpu.VMEM((SWA_TILES_PER_STEP, n_win, 2 * LANES), BF16)]),
        out_shape=jax.ShapeDtypeStruct((b * s_len, q.shape[1]), BF16),
        compiler_params=_params("arbitrary", "arbitrary", "arbitrary"),
        name="attn_swa",
    )(sink, q, k, v, k, v)


def _global_kernel(q_ref, k_ref, v_ref, kc_ref, vc_ref, o_ref, kall_ref, vext_ref, s_ref, p_ref, *, nh, s_len):
    @pl.when(pl.program_id(2) == 0)
    def _():
        kall_ref[0:s_len, :] = k_ref[...]
        kall_ref[s_len:, :] = kc_ref[...]
        vext_ref[0:s_len, 0:LANES] = v_ref[...]
        vext_ref[s_len:, 0:LANES] = vc_ref[...]
        vext_ref[:, LANES:] = jnp.ones((vext_ref.shape[0], LANES), BF16)

    qs = _stack_heads(q_ref[...], LANES, nh)
    n_chunks = kall_ref.shape[0] // GLB_CHUNK
    mrun = None
    for c in range(n_chunks):
        ks = slice(c * GLB_CHUNK, (c + 1) * GLB_CHUNK)
        s = _dot_t(qs, kall_ref[ks, :])
        s_ref[:, ks] = s
        blk = _block_max(s)
        mrun = blk if mrun is None else jnp.maximum(mrun, blk)
    m = _lanes(_row_max(mrun), GLB_CHUNK)
    for c in range(n_chunks):
        ks = slice(c * GLB_CHUNK, (c + 1) * GLB_CHUNK)
        p_ref[:, ks] = jnp.exp2(s_ref[:, ks] - m).astype(BF16)
    acc = _dot_row_halves(p_ref, vext_ref[...])
    o_ref[...] = _unstack_heads(acc[:, :LANES] / acc[:, LANES:], LANES, nh).astype(BF16)


def _global_attention(q, k, v, dims):
    b, s_len, n_ctx = dims
    tq = GLB_TQ
    n_t = s_len // tq
    n_kvblk = k.shape[1] // LANES
    nh = q.shape[1] // k.shape[1]
    qw = nh * LANES
    ctx_blk = b * s_len // n_ctx
    n_keys = s_len + n_ctx
    qspec = pl.BlockSpec((tq, qw), lambda i, p, t: (i * n_t + t, p))
    lat = pl.BlockSpec((s_len, LANES), lambda i, p, t: (i, p))
    ctx = pl.BlockSpec((n_ctx, LANES), lambda i, p, t: (ctx_blk + i, p))
    return pl.pallas_call(
        functools.partial(_global_kernel, nh=nh, s_len=s_len),
        grid=(b, n_kvblk, n_t),
        in_specs=[qspec, lat, lat, ctx, ctx],
        out_specs=qspec,
        out_shape=jax.ShapeDtypeStruct((b * s_len, q.shape[1]), BF16),
        scratch_shapes=[pltpu.VMEM((n_keys, LANES), BF16),
                        pltpu.VMEM((n_keys, 2 * LANES), BF16),
                        pltpu.VMEM((nh * tq, n_keys), F32),
                        pltpu.VMEM((nh * tq, n_keys), BF16)],
        compiler_params=_params("arbitrary", "arbitrary", "arbitrary"),
        name="attn_global",
    )(q, k, v, k, v)


def _ctx_kernel(sink_ref, q_ref, kc_ref, vc_ref, o_ref, *, dh, nh, use_sink):
    tq = q_ref.shape[0]
    qw = nh * dh
    for p in range(kc_ref.shape[1] // LANES):
        kv = slice(p * LANES, (p + 1) * LANES)
        qs = _stack_heads(q_ref[:, p * qw:(p + 1) * qw], dh, nh)
        s = _dot_t(qs, kc_ref[:, kv])
        m = _row_max(_block_max(s))
        if use_sink:
            sink = _sink_rows(sink_ref, p * nh, nh, tq)
            m = jnp.maximum(m, sink)
        acc = _dot(jnp.exp2(s - _lanes(m, s.shape[1])).astype(BF16), _ones_ext(vc_ref[:, kv]))
        den = acc[:, LANES:]
        if use_sink:
            den = den + jnp.exp2(sink - m)
        o_ref[:, p * qw:(p + 1) * qw] = _unstack_heads(acc[:, :LANES] / den, dh, nh).astype(BF16)


def _ctx_attention(q, k, v, sink, dims, mixer):
    b, s_len, n_ctx = dims
    n_qh, n_kvh, dh = MIXER_HEADS[mixer]
    n_kvblk = k.shape[1] // LANES
    nh = n_qh // n_kvblk
    ctx_blk = b * s_len // n_ctx
    use_sink = sink is not None
    if not use_sink:
        sink = jnp.zeros((n_qh,), F32)
    return pl.pallas_call(
        functools.partial(_ctx_kernel, dh=dh, nh=nh, use_sink=use_sink),
        grid_spec=pltpu.PrefetchScalarGridSpec(
            num_scalar_prefetch=1,
            grid=(b,),
            in_specs=[pl.BlockSpec((n_ctx, q.shape[1]), lambda i, *_: (ctx_blk + i, 0)),
                      pl.BlockSpec((n_ctx, k.shape[1]), lambda i, *_: (ctx_blk + i, 0)),
                      pl.BlockSpec((n_ctx, v.shape[1]), lambda i, *_: (ctx_blk + i, 0))],
            out_specs=pl.BlockSpec((n_ctx, q.shape[1]), lambda i, *_: (i, 0))),
        out_shape=jax.ShapeDtypeStruct((b * n_ctx, q.shape[1]), BF16),
        compiler_params=_params("arbitrary"),
        name="attn_ctx_m%d" % mixer,
    )(sink, q, k, v)


def _oproj_kernel(*refs, n_lat_tiles, n_y, n_x, moe):
    y = _pick_rows(refs[:n_y], n_lat_tiles)
    x_in = _pick_rows(refs[n_y:n_y + n_x], n_lat_tiles)
    refs = refs[n_y + n_x:]
    mod_ref, n2_ref, wo_ref = refs[:3]
    refs = refs[3:]
    if moe:
        wr_ref, br_ref = refs[:2]
        refs = refs[2:]
    xo_ref, h_ref = refs[:2]
    x = x_in + mod_ref[2:3, :] * _dot(y, wo_ref[...])
    xo_ref[...] = x
    h = _rms_mod(x, n2_ref[...], mod_ref[4:5, :], mod_ref[3:4, :])
    h_ref[...] = h.astype(BF16)
    if moe:
        route_ref, counts_ref, base_ref = refs[2:5]
        h_hi, h_lo = _split_bf16(h)
        w_hi, w_lo = _split_bf16(wr_ref[...])
        hi_terms = _dot(h_hi, jnp.concatenate([w_hi, w_lo], axis=1))
        lo_terms = _dot(h_lo, jnp.concatenate([w_hi, jnp.zeros_like(w_hi)], axis=1))
        logits = hi_terms[:, :LANES] + hi_terms[:, LANES:] + lo_terms[:, :LANES] + br_ref[...]
        lane = lax.broadcasted_iota(jnp.int32, logits.shape, 1)
        v1 = jnp.max(logits, axis=-1, keepdims=True)
        i1 = jnp.min(jnp.where(logits == v1, lane, LANES), axis=-1, keepdims=True)
        rest = jnp.where(lane == i1, NEG_BIG, logits)
        v2 = jnp.max(rest, axis=-1, keepdims=True)
        i2 = jnp.min(jnp.where(rest == v2, lane, LANES), axis=-1, keepdims=True)
        e = jnp.exp(v2 - v1)
        w1 = 1.0 / (1.0 + e)
        w2 = e / (1.0 + e)
        @pl.when(pl.program_id(0) == 0)
        def _():
            base_ref[...] = jnp.zeros_like(base_ref)

        tm = logits.shape[0]
        pick1 = lane == i1
        pick2 = lane == i2
        cnt = jnp.where(pick1, 1.0, 0.0) + jnp.where(pick2, 1.0, 0.0)
        earlier = lax.broadcasted_iota(jnp.int32, (tm, tm), 0) > lax.broadcasted_iota(jnp.int32, (tm, tm), 1)
        before = _dot(jnp.where(earlier, 1.0, 0.0).astype(BF16), cnt.astype(BF16)) + base_ref[0:1, :]
        r1 = jnp.sum(jnp.where(pick1, before, 0.0), axis=-1, keepdims=True)
        r2 = jnp.sum(jnp.where(pick2, before, 0.0), axis=-1, keepdims=True)
        base_ref[...] = base_ref[...] + jnp.sum(cnt, axis=0, keepdims=True)
        counts_ref[...] = base_ref[...]
        route = jnp.where(lane == 0, i1.astype(F32), 0.0)
        route = jnp.where(lane == 1, i2.astype(F32), route)
        route = jnp.where(lane == 2, w1, route)
        route = jnp.where(lane == 3, w2, route)
        route = jnp.where(lane == 4, r1, route)
        route = jnp.where(lane == 5, r2, route)
        route_ref[...] = route


def _oproj(y_lat, y_ctx, xs, mod, norm2, w_o, router, dims, last):
    b, s_len, n_ctx = dims
    x_parts = list(xs) if isinstance(xs, tuple) else [xs]
    d = x_parts[0].shape[1]
    tm = _row_tile(s_len, b * n_ctx, 512)
    n_lat_tiles = b * s_len // tm
    tiles_per_sample = s_len // tm
    n_rows = b * s_len if last else b * (s_len + n_ctx)
    n_tiles = n_rows // tm
    moe = router is not None

    def mod_idx(i):
        return jnp.where(i < n_lat_tiles, i // tiles_per_sample, b)

    row = pl.BlockSpec((tm, d), lambda i: (i, 0))
    y_parts = [y_lat] if last else [y_lat, y_ctx]
    in_specs = ([pl.BlockSpec((tm, y_lat.shape[1]), lambda i: (i, 0))] if last
                else _row_sources(y_parts, tm, n_lat_tiles))
    in_specs += [row] if len(x_parts) == 1 else _row_sources(x_parts, tm, n_lat_tiles)
    in_specs += [pl.BlockSpec((None, 6, d), lambda i: (mod_idx(i), 0, 0)),
                 pl.BlockSpec((1, d), lambda i: (0, 0)),
                 pl.BlockSpec(w_o.shape, lambda i: (0, 0))]
    args = y_parts + x_parts + [mod, norm2.reshape(1, d), w_o.astype(BF16)]
    out_specs = [row, row]
    out_shape = [jax.ShapeDtypeStruct((n_rows, d), F32), jax.ShapeDtypeStruct((n_rows, d), BF16)]
    if moe:
        w_router, b_router = router
        n_e = w_router.shape[1]
        wr = jnp.pad(w_router, ((0, 0), (0, LANES - n_e)))
        br = jnp.pad(b_router, (0, LANES - n_e), constant_values=NEG_BIG).reshape(1, LANES)
        in_specs += [pl.BlockSpec(wr.shape, lambda i: (0, 0)), pl.BlockSpec(br.shape, lambda i: (0, 0))]
        args += [wr, br]
        out_specs += [pl.BlockSpec((tm, LANES), lambda i: (i, 0)), pl.BlockSpec((8, LANES), lambda i: (0, 0))]
        out_shape += [jax.ShapeDtypeStruct((n_rows, LANES), F32), jax.ShapeDtypeStruct((8, LANES), F32)]
    return pl.pallas_call(
        functools.partial(_oproj_kernel, n_lat_tiles=n_lat_tiles, n_y=len(y_parts), n_x=len(x_parts), moe=moe),
        grid=(n_tiles,),
        in_specs=in_specs,
        out_specs=out_specs,
        out_shape=out_shape,
        scratch_shapes=[pltpu.VMEM((8, LANES), F32)] if moe else [],
        compiler_params=_params("arbitrary"),
        name="oproj_moe" if moe else "oproj",
    )(*args)


def _swiglu_partial(h_ref, wg_ref, wu_ref, wd_ref):
    h = h_ref[...]
    a = _silu(_dot(h, wg_ref[...].astype(BF16))) * _dot(h, wu_ref[...].astype(BF16))
    return _dot(a.astype(BF16), wd_ref[...].astype(BF16))


def _accumulate_over_hidden(f, n_f, acc_ref, partial, finish):
    @pl.when(f == 0)
    def _():
        acc_ref[...] = partial()

    @pl.when(jnp.logical_and(f > 0, f < n_f - 1))
    def _():
        acc_ref[...] += partial()

    @pl.when(f == n_f - 1)
    def _():
        finish(acc_ref[...] + partial())


def _ffn_kernel(h_ref, wg_ref, wu_ref, wd_ref, x_ref, mod_ref, o_ref, acc_ref):
    def finish(total):
        o_ref[...] = x_ref[...] + mod_ref[5:6, :] * total

    _accumulate_over_hidden(pl.program_id(1), pl.num_programs(1), acc_ref,
                            functools.partial(_swiglu_partial, h_ref, wg_ref, wu_ref, wd_ref), finish)


def _ffn(h, xs, mod, w_in, w_out, dims):
    b, s_len, n_ctx = dims
    n_rows, d = h.shape
    d_ff = w_out.shape[0]
    tm = _row_tile(s_len, b * n_ctx, 1024)
    tf = FFN_TF
    n_f = d_ff // tf
    n_lat_tiles = b * s_len // tm
    tiles_per_sample = s_len // tm

    def mod_idx(i):
        return jnp.where(i < n_lat_tiles, i // tiles_per_sample, b)

    w_in = _to_bf16(w_in[None])[0]
    w_out = _to_bf16(w_out[None])[0]
    return pl.pallas_call(
        _ffn_kernel,
        grid=(n_rows // tm, n_f),
        in_specs=[pl.BlockSpec((tm, d), lambda i, f: (i, 0)),
                  pl.BlockSpec((d, tf), lambda i, f: (0, f)),
                  pl.BlockSpec((d, tf), lambda i, f: (0, n_f + f)),
                  pl.BlockSpec((tf, d), lambda i, f: (f, 0)),
                  pl.BlockSpec((tm, d), lambda i, f: (i, 0)),
                  pl.BlockSpec((None, 6, d), lambda i, f: (mod_idx(i), 0, 0))],
        out_specs=pl.BlockSpec((tm, d), lambda i, f: (i, 0)),
        out_shape=jax.ShapeDtypeStruct((n_rows, d), F32),
        scratch_shapes=[pltpu.VMEM((tm, d), F32)],
        compiler_params=_params("arbitrary", "arbitrary"),
        name="ffn_dense",
    )(h, w_in, w_in, w_out, xs, mod)


def _moe_ffn_kernel(te_ref, na_ref, h_ref, wg_ref, wu_ref, wd_ref, *rest):
    o_ref, acc_ref = rest[-2:]
    i, f = pl.program_id(0), pl.program_id(1)

    def finish(total):
        o_ref[...] = total.astype(o_ref.dtype)

    @pl.when(i < na_ref[0])
    def _():
        _accumulate_over_hidden(f, pl.num_programs(1), acc_ref,
                                functools.partial(_swiglu_partial, h_ref, wg_ref, wu_ref, wd_ref), finish)


def _moe_ffn(hs, ys, n_rows_total, first_tile, tile_expert, n_active, w_in, w_out):
    n_rows, d = hs.shape
    d_ff = w_out.shape[1]
    tm, tf = MOE_TM, FFN_TF
    n_f = d_ff // tf

    def tile(i, na):
        return jnp.minimum(i, jnp.maximum(na[0] - 1, 0))

    def expert(i, te, na):
        return te[tile(i, na)]

    in_specs = [pl.BlockSpec((tm, d), lambda i, f, te, na: (tile(i, na), 0)),
                pl.BlockSpec((None, d, tf), lambda i, f, te, na: (expert(i, te, na), 0, f)),
                pl.BlockSpec((None, d, tf), lambda i, f, te, na: (expert(i, te, na), 0, n_f + f)),
                pl.BlockSpec((None, tf, d), lambda i, f, te, na: (expert(i, te, na), f, 0))]
    args = [tile_expert, n_active, hs, w_in, w_in, w_out]
    aliases = {}
    if ys is not None:
        in_specs.append(pl.BlockSpec(memory_space=pl.ANY))
        aliases = {len(args): 0}
        args.append(ys)
    return pl.pallas_call(
        _moe_ffn_kernel,
        grid_spec=pltpu.PrefetchScalarGridSpec(
            num_scalar_prefetch=2,
            grid=(n_rows // tm, n_f),
            in_specs=in_specs,
            out_specs=pl.BlockSpec((tm, d), lambda i, f, te, na: (first_tile + tile(i, na), 0)),
            scratch_shapes=[pltpu.VMEM((tm, d), F32)]),
        out_shape=jax.ShapeDtypeStruct((n_rows_total, d), hs.dtype),
        input_output_aliases=aliases,
        compiler_params=_params("arbitrary", "arbitrary"),
        name="moe_ffn",
    )(*args)


def _combine_kernel(x_ref, a_ref, b_ref, route_ref, mod_ref, o_ref):
    w1 = route_ref[:, 2:3]
    w2 = route_ref[:, 3:4]
    mix = w1 * a_ref[...].astype(F32) + w2 * b_ref[...].astype(F32)
    o_ref[...] = x_ref[...] + mod_ref[5:6, :] * mix


def _combine(xs, ya, yb, route, mod, dims):
    b, s_len, n_ctx = dims
    n_rows, d = ya.shape
    tm = _row_tile(s_len, b * n_ctx, 512)
    n_lat_tiles = b * s_len // tm
    tiles_per_sample = s_len // tm

    def mod_idx(i):
        return jnp.where(i < n_lat_tiles, i // tiles_per_sample, b)

    row = pl.BlockSpec((tm, d), lambda i: (i, 0))
    return pl.pallas_call(
        _combine_kernel,
        grid=(n_rows // tm,),
        in_specs=[row, row, row,
                  pl.BlockSpec((tm, LANES), lambda i: (i, 0)),
                  pl.BlockSpec((None, 6, d), lambda i: (mod_idx(i), 0, 0))],
        out_specs=row,
        out_shape=jax.ShapeDtypeStruct((n_rows, d), F32),
        compiler_params=_params("arbitrary"),
        name="moe_combine",
    )(xs, ya, yb, route, mod)


def _route_plan(idx, rank, counts, tm):
    n = idx.shape[0]
    tiles_per = (counts + tm - 1) // tm
    tile_end = jnp.cumsum(tiles_per)
    tile_start = tile_end - tiles_per
    experts = jnp.arange(N_EXPERTS, dtype=jnp.int32)
    start = jnp.sum(jnp.where(idx[:, :, None] == experts, tile_start, 0), axis=-1)
    slot = start * tm + rank
    n_tiles = (2 * n) // tm + N_EXPERTS
    tile_ids = jnp.arange(n_tiles, dtype=jnp.int32)
    tile_expert = jnp.minimum(jnp.sum((tile_end[None, :] <= tile_ids[:, None]).astype(jnp.int32), axis=1),
                              N_EXPERTS - 1)
    token = jnp.arange(2 * n, dtype=jnp.int32) // 2
    _, token_sorted = lax.sort((slot.reshape(-1), token), num_keys=1)
    first_pair = jnp.cumsum(counts) - counts
    row_in_expert = ((tile_ids - tile_start[tile_expert]) * tm)[:, None] + jnp.arange(tm, dtype=jnp.int32)[None, :]
    pair = jnp.clip(first_pair[tile_expert][:, None] + row_in_expert, 0, 2 * n - 1)
    filler = (tile_ids[:, None] * tm + jnp.arange(tm, dtype=jnp.int32)[None, :]) % n
    src = jnp.where(row_in_expert < counts[tile_expert][:, None], token_sorted[pair], filler).reshape(-1)
    return slot, src, tile_expert, tile_end[-1:].astype(jnp.int32)


def _cast_kernel(w_ref, o_ref):
    o_ref[...] = w_ref[...].astype(o_ref.dtype)


def _to_bf16(w):
    n_e, n_r, n_c = w.shape
    rows = n_r
    while rows * n_c * 4 > CAST_BLOCK_BYTES and rows % 2 == 0 and (rows // 2) % 16 == 0:
        rows //= 2
    return pl.pallas_call(
        _cast_kernel,
        grid=(n_e, n_r // rows),
        in_specs=[pl.BlockSpec((None, rows, n_c), lambda e, r: (e, r, 0))],
        out_specs=pl.BlockSpec((None, rows, n_c), lambda e, r: (e, r, 0)),
        out_shape=jax.ShapeDtypeStruct(w.shape, BF16),
        compiler_params=_params("arbitrary", "arbitrary"),
        name="cast_bf16",
    )(w)


def _take_rows(a, rows):
    return a.at[rows].get(mode="promise_in_bounds")


def _moe(h, xs, route, counts, mod, w_exp_in, w_exp_out, dims):
    idx = route[:, 0:2].astype(jnp.int32)
    rank = route[:, 4:6].astype(jnp.int32)
    slot, src, tile_expert, n_active = _route_plan(idx, rank, counts[0, :N_EXPERTS].astype(jnp.int32), MOE_TM)
    n_tiles = src.shape[0] // MOE_TM
    first = max(1, n_tiles // MOE_FIRST_CHUNK_DIV)
    rest = -(-(n_tiles - first) // (MOE_CHUNKS - 1))
    bounds = [0] + [min(first + j * rest, n_tiles) for j in range(MOE_CHUNKS)]
    ys = None
    for t0, t1 in zip(bounds[:-1], bounds[1:]):
        if t1 == t0:
            continue
        hs = _take_rows(h, src[t0 * MOE_TM:t1 * MOE_TM])
        ys = _moe_ffn(hs, ys, src.shape[0], t0, tile_expert[t0:t1], jnp.clip(n_active - t0, 0, t1 - t0),
                      w_exp_in, w_exp_out)
    ya = _take_rows(ys, slot[:, 0])
    yb = _take_rows(ys, slot[:, 1])
    return _combine(xs, ya, yb, route, mod, dims)


def _layer(xs, cond, p, dims, mixer, last):
    b, s_len, n_ctx = dims
    mod = _adaln(cond, p["w_mod"], p["b_mod"])
    q, k, v = _qkv(xs, mod, p["norm1"], p["w_qkv"], p["q_norm"], p["k_norm"], dims, mixer)
    if mixer == 0:
        y_lat = _na_attention(q, k, v, p["rel_bias"], dims)
    elif mixer == 1:
        y_lat = _swa_attention(q, k, v, p["sink"], dims)
    else:
        y_lat = _global_attention(q, k, v, dims)
    y_ctx = None if last else _ctx_attention(q, k, v, p.get("sink"), dims, mixer)
    router = (p["w_router"], p["b_router"]) if "w_router" in p else None
    outs = _oproj(y_lat, y_ctx, xs, mod, p["norm2"], p["w_o"], router, dims, last)
    if router is None:
        xs, h = outs
        return _ffn(h, xs, mod, p["w_ffn_in"], p["w_ffn_out"], dims)
    xs, h, route, counts = outs
    return _moe(h, xs, route, counts, mod, p["w_exp_in"], p["w_exp_out"], dims)


def kernel(x, c, ctx, c_ctx, l0_w_mod, l0_b_mod, l0_norm1, l0_norm2, l0_w_qkv, l0_q_norm, l0_k_norm, l0_rel_bias, l0_w_o, l0_w_ffn_in, l0_w_ffn_out, l1_w_mod, l1_b_mod, l1_norm1, l1_norm2, l1_w_qkv, l1_q_norm, l1_k_norm, l1_sink, l1_w_o, l1_w_router, l1_b_router, l1_w_exp_in, l1_w_exp_out, l2_w_mod, l2_b_mod, l2_norm1, l2_norm2, l2_w_qkv, l2_q_norm, l2_k_norm, l2_w_o, l2_w_ffn_in, l2_w_ffn_out, l3_w_mod, l3_b_mod, l3_norm1, l3_norm2, l3_w_qkv, l3_q_norm, l3_k_norm, l3_rel_bias, l3_w_o, l3_w_router, l3_b_router, l3_w_exp_in, l3_w_exp_out):
    b, s_len, d = x.shape
    n_ctx = ctx.shape[1]
    dims = (b, s_len, n_ctx)
    layers = (
        dict(w_mod=l0_w_mod, b_mod=l0_b_mod, norm1=l0_norm1, norm2=l0_norm2, w_qkv=l0_w_qkv, q_norm=l0_q_norm,
             k_norm=l0_k_norm, rel_bias=l0_rel_bias, w_o=l0_w_o, w_ffn_in=l0_w_ffn_in, w_ffn_out=l0_w_ffn_out),
        dict(w_mod=l1_w_mod, b_mod=l1_b_mod, norm1=l1_norm1, norm2=l1_norm2, w_qkv=l1_w_qkv, q_norm=l1_q_norm,
             k_norm=l1_k_norm, sink=l1_sink, w_o=l1_w_o, w_router=l1_w_router, b_router=l1_b_router,
             w_exp_in=l1_w_exp_in, w_exp_out=l1_w_exp_out),
        dict(w_mod=l2_w_mod, b_mod=l2_b_mod, norm1=l2_norm1, norm2=l2_norm2, w_qkv=l2_w_qkv, q_norm=l2_q_norm,
             k_norm=l2_k_norm, w_o=l2_w_o, w_ffn_in=l2_w_ffn_in, w_ffn_out=l2_w_ffn_out),
        dict(w_mod=l3_w_mod, b_mod=l3_b_mod, norm1=l3_norm1, norm2=l3_norm2, w_qkv=l3_w_qkv, q_norm=l3_q_norm,
             k_norm=l3_k_norm, rel_bias=l3_rel_bias, w_o=l3_w_o, w_router=l3_w_router, b_router=l3_b_router,
             w_exp_in=l3_w_exp_in, w_exp_out=l3_w_exp_out),
    )
    xs = (x.reshape(b * s_len, d), ctx.reshape(b * n_ctx, d))
    pad_rows = -(b + 1) % 8
    cond = jnp.concatenate([c, c_ctx[None, :], jnp.zeros((pad_rows, d), F32)], axis=0)
    n_layers = len(layers)
    for i, p in enumerate(layers):
        xs = _layer(xs, cond, p, dims, i % 3, i == n_layers - 1)
    return xs.reshape(b, s_len, d)
```

```python
import functools

import jax
import jax.numpy as jnp
from jax import lax
from jax.experimental import pallas as pl
from jax.experimental.pallas import tpu as pltpu

F32 = jnp.float32
BF16 = jnp.bfloat16

GRID_W = 64
NORM_EPS = 1e-6
ROPE_THETA = 10000.0
NA_WIN_ROWS = 8
NA_WIN_COLS = 16
SW_WINDOW = 128
N_EXPERTS = 8
MIXER_HEADS = ((16, 16, 64), (16, 4, 64), (8, 4, 128))

LANES = 128
VMEM_LIMIT_BYTES = 56 * 1024 * 1024
NEG_BIG = -1e30
LOG2E = 1.4426950408889634

QKV_CHUNK = 512
GLB_TQ = 1024
GLB_CHUNK = 256
SWA_TQ = 256
SWA_TILES_PER_STEP = 4
NA_CTX_ROWS = 1024
MOE_TM = 1024
MOE_CHUNKS = 4
MOE_FIRST_CHUNK_DIV = 16
CAST_BLOCK_BYTES = 8 * 1024 * 1024
FFN_TF = 512


def _params(*sem):
    return pltpu.CompilerParams(dimension_semantics=sem, vmem_limit_bytes=VMEM_LIMIT_BYTES)


def _row_tile(n_lat_per_sample, n_ctx_rows, cap):
    for tm in (1024, 512, 256, 128):
        if tm <= cap and n_lat_per_sample % tm == 0 and n_ctx_rows % tm == 0:
            return tm
    raise ValueError("no row tile fits")


def _split_bf16(a):
    hi = a.astype(BF16)
    lo = (a - hi.astype(F32)).astype(BF16)
    return hi, lo


def _dot(a, b):
    return jnp.dot(a, b, preferred_element_type=F32)


def _dot_t(a, b):
    return lax.dot_general(a, b, (((1,), (1,)), ((), ())), preferred_element_type=F32)


def _silu(g):
    return g / (1.0 + jnp.exp(-g))


def _rms_mod(x, gain, scale, shift):
    ms = jnp.mean(x * x, axis=-1, keepdims=True)
    return x * lax.rsqrt(ms + NORM_EPS) * gain * (1.0 + scale) + shift


def _adaln_kernel(c_ref, w_ref, b_ref, o_ref):
    a_hi, a_lo = _split_bf16(_silu(c_ref[...]))
    w_hi, w_lo = _split_bf16(w_ref[...])
    o_ref[...] = _dot(a_hi, w_hi) + _dot(a_hi, w_lo) + _dot(a_lo, w_hi) + b_ref[...]


def _adaln(cond, w_mod, b_mod):
    r, d = cond.shape
    n = w_mod.shape[1]
    tn = 1536
    out = pl.pallas_call(
        _adaln_kernel,
        grid=(n // tn,),
        in_specs=[pl.BlockSpec((r, d), lambda j: (0, 0)),
                  pl.BlockSpec((d, tn), lambda j: (0, j)),
                  pl.BlockSpec((1, tn), lambda j: (0, j))],
        out_specs=pl.BlockSpec((r, tn), lambda j: (0, j)),
        out_shape=jax.ShapeDtypeStruct((r, n), F32),
        compiler_params=_params("arbitrary"),
        name="adaln",
    )(cond, w_mod, b_mod.reshape(1, n))
    return out.reshape(r, 6, d)


def _rot_half(z, dh):
    if dh == LANES:
        return pltpu.roll(z, LANES // 2, axis=1)
    lane = lax.broadcasted_iota(jnp.int32, z.shape, 1)
    from_right = pltpu.roll(z, LANES - dh // 2, axis=1)
    from_left = pltpu.roll(z, dh // 2, axis=1)
    return jnp.where((lane % dh) < dh // 2, from_right, from_left)


def _head_mean_sq(z, dh):
    z2 = z * z
    if dh == LANES:
        return jnp.broadcast_to(jnp.sum(z2, axis=-1, keepdims=True), z.shape) * (1.0 / dh)
    low = lax.broadcasted_iota(jnp.int32, z.shape, 1) < dh
    s_low = jnp.sum(jnp.where(low, z2, 0.0), axis=-1, keepdims=True)
    s_high = jnp.sum(jnp.where(low, 0.0, z2), axis=-1, keepdims=True)
    return jnp.where(low, s_low, s_high) * (1.0 / dh)


def _row_sources(arrays, tm, n_lat_tiles):
    lat, ctx = arrays
    return [pl.BlockSpec((tm, lat.shape[1]), lambda i: (jnp.minimum(i, n_lat_tiles - 1), 0)),
            pl.BlockSpec((tm, ctx.shape[1]), lambda i: (jnp.maximum(i - n_lat_tiles, 0), 0))]


def _pick_rows(refs, n_lat_tiles):
    if len(refs) == 1:
        return refs[0][...]
    return jnp.where(pl.program_id(0) < n_lat_tiles, refs[0][...], refs[1][...])


def _qkv_kernel(*refs, n_q, n_kv, dh, rope, n_x, n_lat_tiles):
    x = _pick_rows(refs[:n_x], n_lat_tiles)
    refs = refs[n_x:]
    if rope:
        mod_ref, n1_ref, w_ref, g_ref, cos_ref, sin_ref, q_ref, k_ref, v_ref = refs
    else:
        mod_ref, n1_ref, w_ref, g_ref, q_ref, k_ref, v_ref = refs
    h = _rms_mod(x, n1_ref[...], mod_ref[1:2, :], mod_ref[0:1, :]).astype(BF16)
    n_qk = n_q + n_kv
    cw = QKV_CHUNK
    for c in range((n_qk + n_kv) // cw):
        y = _dot(h, w_ref[:, c * cw:(c + 1) * cw])
        for s in range(cw // LANES):
            col = c * cw + s * LANES
            z = y[:, s * LANES:(s + 1) * LANES]
            if col < n_qk:
                z = z * lax.rsqrt(_head_mean_sq(z, dh) + NORM_EPS) * g_ref[:, col:col + LANES]
                if rope:
                    z = z * cos_ref[...] + _rot_half(z, dh) * sin_ref[...]
            z = z.astype(BF16)
            if col < n_q:
                q_ref[:, col:col + LANES] = z
            elif col < n_qk:
                k_ref[:, col - n_q:col - n_q + LANES] = z
            else:
                v_ref[:, col - n_qk:col - n_qk + LANES] = z


def _rope_tables(s_len, dh, tm):
    n_freq = dh // 4
    inv_freq = ROPE_THETA ** (-jnp.arange(n_freq, dtype=F32) / n_freq)
    t = jnp.arange(s_len)
    row = (t // GRID_W).astype(F32)
    col = (t % GRID_W).astype(F32)
    ang = jnp.concatenate([row[:, None] * inv_freq, col[:, None] * inv_freq], axis=-1)
    cos, sin = jnp.cos(ang), jnp.sin(ang)
    reps = LANES // dh
    cos_t = jnp.tile(jnp.concatenate([cos, cos], axis=-1), (1, reps))
    sin_t = jnp.tile(jnp.concatenate([-sin, sin], axis=-1), (1, reps))
    cos_t = jnp.concatenate([cos_t, jnp.ones((tm, LANES), F32)], axis=0)
    sin_t = jnp.concatenate([sin_t, jnp.zeros((tm, LANES), F32)], axis=0)
    return cos_t, sin_t


def _qkv(xs, mod, norm1, w_qkv, q_gain, k_gain, dims, mixer):
    b, s_len, n_ctx = dims
    x_parts = list(xs) if isinstance(xs, tuple) else [xs]
    n_rows, d = sum(a.shape[0] for a in x_parts), x_parts[0].shape[1]
    n_qh, n_kvh, dh = MIXER_HEADS[mixer]
    n_q, n_kv = n_qh * dh, n_kvh * dh
    rope = mixer != 0
    tm = _row_tile(s_len, b * n_ctx, 512)
    n_lat_tiles = b * s_len // tm
    tiles_per_sample = s_len // tm

    gains = jnp.concatenate([jnp.tile(q_gain * (dh ** -0.5 * LOG2E), n_qh),
                             jnp.tile(k_gain, n_kvh)]).reshape(1, n_q + n_kv)

    def mod_idx(i):
        return jnp.where(i < n_lat_tiles, i // tiles_per_sample, b)

    x_specs = [pl.BlockSpec((tm, d), lambda i: (i, 0))] if len(x_parts) == 1 else _row_sources(x_parts, tm, n_lat_tiles)
    in_specs = x_specs + [pl.BlockSpec((None, 6, d), lambda i: (mod_idx(i), 0, 0)),
                          pl.BlockSpec((1, d), lambda i: (0, 0)),
                          pl.BlockSpec(w_qkv.shape, lambda i: (0, 0)),
                          pl.BlockSpec(gains.shape, lambda i: (0, 0))]
    args = x_parts + [mod, norm1.reshape(1, d), w_qkv.astype(BF16), gains]
    if rope:
        cos_t, sin_t = _rope_tables(s_len, dh, tm)

        def pos_idx(i):
            return jnp.where(i < n_lat_tiles, i % tiles_per_sample, tiles_per_sample)

        in_specs += [pl.BlockSpec((tm, LANES), lambda i: (pos_idx(i), 0))] * 2
        args += [cos_t, sin_t]
    return pl.pallas_call(
        functools.partial(_qkv_kernel, n_q=n_q, n_kv=n_kv, dh=dh, rope=rope, n_x=len(x_parts),
                          n_lat_tiles=n_lat_tiles),
        grid=(n_rows // tm,),
        in_specs=in_specs,
        out_specs=[pl.BlockSpec((tm, n_q), lambda i: (i, 0)),
                   pl.BlockSpec((tm, n_kv), lambda i: (i, 0)),
                   pl.BlockSpec((tm, n_kv), lambda i: (i, 0))],
        out_shape=[jax.ShapeDtypeStruct((n_rows, n_q), BF16),
                   jax.ShapeDtypeStruct((n_rows, n_kv), BF16),
                   jax.ShapeDtypeStruct((n_rows, n_kv), BF16)],
        compiler_params=_params("arbitrary"),
        name="qkv_m%d" % mixer,
    )(*args)


def _stack_heads(q, dh, nh):
    if dh == LANES:
        return jnp.concatenate([q[:, h * LANES:(h + 1) * LANES] for h in range(nh)], axis=0)
    per_kv = nh // 2
    tq = q.shape[0]
    lane = lax.broadcasted_iota(jnp.int32, (tq, LANES), 1)
    ops = []
    for h in range(nh):
        slot = h // per_kv
        chunk = q[:, (h // 2) * LANES:(h // 2 + 1) * LANES].astype(F32)
        if h % 2 != slot:
            chunk = pltpu.roll(chunk, dh, axis=1)
        keep = lane < dh if slot == 0 else lane >= dh
        ops.append(jnp.where(keep, chunk, 0.0).astype(BF16))
    return jnp.concatenate(ops, axis=0)


def _unstack_heads(o, dh, nh):
    tq = o.shape[0] // nh
    if dh == LANES:
        return jnp.concatenate([o[h * tq:(h + 1) * tq] for h in range(nh)], axis=1)
    per_kv = nh // 2
    lane = lax.broadcasted_iota(jnp.int32, (tq, LANES), 1)
    chunks = []
    for c in range(nh // 2):
        parts = []
        for h in (2 * c, 2 * c + 1):
            oh = o[h * tq:(h + 1) * tq]
            if h % 2 != h // per_kv:
                oh = pltpu.roll(oh, dh, axis=1)
            parts.append(oh)
        chunks.append(jnp.where(lane < dh, parts[0], parts[1]))
    return jnp.concatenate(chunks, axis=1)


def _lanes(x, n):
    return x if n == LANES else jnp.concatenate([x] * (n // LANES), axis=1)


def _block_max(s):
    return functools.reduce(jnp.maximum, [s[:, j * LANES:(j + 1) * LANES] for j in range(s.shape[1] // LANES)])


def _row_max(blk):
    return jnp.broadcast_to(jnp.max(blk, axis=-1, keepdims=True), blk.shape)


def _dot_row_halves(p, v):
    half = p.shape[0] // 2
    return jnp.concatenate([_dot(p[0:half, :], v), _dot(p[half:, :], v)], axis=0)


def _ones_ext(v):
    return jnp.concatenate([v, jnp.ones_like(v)], axis=1)


def _sink_rows(sink_ref, first_head, nh, tq):
    return jnp.concatenate([jnp.full((tq, LANES), sink_ref[first_head + h] * LOG2E, F32) for h in range(nh)], axis=0)


def _na_kernel(q_ref, k_ref, v_ref, kc_ref, vc_ref, bias_ref, o_ref,
               qs_ref, vext_ref, mc_ref, numc_ref, lc_ref, *, rows):
    s_len = q_ref.shape[0]
    band = NA_WIN_ROWS * GRID_W
    lane = lax.broadcasted_iota(jnp.int32, (s_len, LANES), 1)
    q = q_ref[...]
    qs_ref[0:s_len, :] = jnp.where(lane < 64, q, jnp.zeros_like(q))
    qs_ref[s_len:, :] = jnp.where(lane >= 64, q, jnp.zeros_like(q))
    vext_ref[:, 0:LANES] = v_ref[...]
    vext_ref[:, LANES:] = jnp.ones((s_len, LANES), BF16)

    kc = kc_ref[...]
    vcx = _ones_ext(vc_ref[...])
    lane_q = lax.broadcasted_iota(jnp.int32, (GRID_W, LANES), 1)
    half = band // 2

    def both_heads(ref, q0, n):
        return jnp.concatenate([ref[q0:q0 + n, :], ref[s_len + q0:s_len + q0 + n, :]], axis=0)

    def context_block(q0, n):
        s = _dot_t(both_heads(qs_ref, q0, n), kc)
        m = _row_max(_block_max(s))
        acc = _dot(jnp.exp2(s - _lanes(m, s.shape[1])).astype(BF16), vcx)
        for i, base in enumerate((q0, s_len + q0)):
            mc_ref[base:base + n, :] = m[i * n:(i + 1) * n]
            numc_ref[base:base + n, :] = acc[i * n:(i + 1) * n, :LANES]
            lc_ref[base:base + n, :] = acc[i * n:(i + 1) * n, LANES:]

    def one_row(r):
        r0 = min(max(r - NA_WIN_ROWS // 2, 0), rows - NA_WIN_ROWS)
        q0, k0, d0 = r * GRID_W, r0 * GRID_W, r0 - r + NA_WIN_ROWS - 1
        bias = jnp.concatenate(
            [jnp.concatenate([bias_ref[0, d0 + 2 * j], bias_ref[1, d0 + 2 * j]], axis=0)
             for j in range(NA_WIN_ROWS // 2)], axis=1)
        qs = both_heads(qs_ref, q0, GRID_W)
        s = jnp.concatenate([_dot_t(qs, k_ref[k0:k0 + half, :]), _dot_t(qs, k_ref[k0 + half:k0 + band, :])],
                            axis=1) + bias
        mc = both_heads(mc_ref, q0, GRID_W)
        m = jnp.maximum(_row_max(_block_max(s)), mc)
        acc = _dot(jnp.exp2(s - _lanes(m, band)).astype(BF16), vext_ref[k0:k0 + band, :])
        alpha = jnp.exp2(mc - m)
        num = acc[:, :LANES] + alpha * both_heads(numc_ref, q0, GRID_W)
        den = acc[:, LANES:] + alpha * both_heads(lc_ref, q0, GRID_W)
        o = num / den
        o_ref[q0:q0 + GRID_W, :] = jnp.where(lane_q < 64, o[:GRID_W], o[GRID_W:]).astype(BF16)

    group = NA_CTX_ROWS // (2 * GRID_W)
    for g in range(rows // group):
        context_block(g * group * GRID_W, group * GRID_W)
        for r in range(g * group, (g + 1) * group):
            one_row(r)


def _na_bias_tables(rel_bias):
    n_h = rel_bias.shape[0]
    col = jnp.arange(GRID_W)
    col_start = jnp.clip(col - NA_WIN_COLS // 2, 0, GRID_W - NA_WIN_COLS)
    in_win = (col[None, :] >= col_start[:, None]) & (col[None, :] < col_start[:, None] + NA_WIN_COLS)
    col_idx = jnp.clip(col[None, :] - col[:, None] + NA_WIN_COLS - 1, 0, 2 * NA_WIN_COLS - 2)
    masked = jnp.where(in_win[None, None], rel_bias[:, :, col_idx] * LOG2E, NEG_BIG)
    pairs = jnp.concatenate([masked[:, :-1], masked[:, 1:]], axis=-1)
    return pairs.reshape((n_h // 2, 2) + pairs.shape[1:])


def _na_attention(q, k, v, rel_bias, dims):
    b, s_len, n_ctx = dims
    rows = s_len // GRID_W
    n_pairs = q.shape[1] // LANES
    bias = _na_bias_tables(rel_bias)
    ctx_blk = b * s_len // n_ctx
    lat = pl.BlockSpec((s_len, LANES), lambda p, i: (i, p))
    ctx = pl.BlockSpec((n_ctx, LANES), lambda p, i: (ctx_blk + i, p))
    return pl.pallas_call(
        functools.partial(_na_kernel, rows=rows),
        grid=(n_pairs, b),
        in_specs=[lat, lat, lat, ctx, ctx,
                  pl.BlockSpec((None,) + bias.shape[1:], lambda p, i: (p, 0, 0, 0, 0))],
        out_specs=lat,
        out_shape=jax.ShapeDtypeStruct((b * s_len, q.shape[1]), BF16),
        scratch_shapes=[pltpu.VMEM((2 * s_len, LANES), BF16),
                        pltpu.VMEM((s_len, 2 * LANES), BF16),
                        pltpu.VMEM((2 * s_len, LANES), F32),
                        pltpu.VMEM((2 * s_len, LANES), F32),
                        pltpu.VMEM((2 * s_len, LANES), F32)],
        compiler_params=_params("arbitrary", "arbitrary"),
        name="attn_na",
    )(q, k, v, k, v, bias)


def _swa_kernel(sink_ref, q_ref, k_ref, v_ref, kc_ref, vc_ref, o_ref, kwin_ref, vwin_ref, *, nh, tq, s_len):
    n_sub = q_ref.shape[0] // tq
    span = tq + 2 * SW_WINDOW
    p_blk, t = pl.program_id(1), pl.program_id(2)

    @pl.when(t == 0)
    def _():
        for j in range(n_sub):
            kwin_ref[j, span:, :] = kc_ref[...]
            vwin_ref[j, span:, 0:LANES] = vc_ref[...]
            vwin_ref[j, :, LANES:] = jnp.ones((vwin_ref.shape[1], LANES), BF16)

    row = lax.broadcasted_iota(jnp.int32, (tq, span), 0)
    col = lax.broadcasted_iota(jnp.int32, (tq, span), 1)
    sink_all = _sink_rows(sink_ref, p_blk * nh, nh, tq)
    kd = 2 * LANES
    for j in range(n_sub):
        q0 = (t * n_sub + j) * tq
        k0 = pl.multiple_of(jnp.clip(q0 - SW_WINDOW, 0, s_len - span), SW_WINDOW)
        kwin_ref[j, 0:span, :] = k_ref[pl.ds(k0, span), :]
        vwin_ref[j, 0:span, 0:LANES] = v_ref[pl.ds(k0, span), :]
        bias = jnp.where(jnp.abs(col - row + (k0 - q0)) <= SW_WINDOW, 0.0, NEG_BIG)
        qs_all = _stack_heads(q_ref[j * tq:(j + 1) * tq, :], 64, nh)
        vwin = vwin_ref[j]
        outs = []
        for h in range(nh):
            qs = qs_all[h * tq:(h + 1) * tq]
            sink = sink_all[h * tq:(h + 1) * tq]
            s = jnp.concatenate([_dot_t(qs, kwin_ref[j, c * kd:(c + 1) * kd, :])
                                 for c in range(kwin_ref.shape[1] // kd)], axis=1)
            s = jnp.concatenate([s[:, :span] + bias, s[:, span:]], axis=1)
            m = jnp.maximum(_row_max(_block_max(s)), sink)
            acc = _dot(jnp.exp2(s - _lanes(m, s.shape[1])).astype(BF16), vwin)
            outs.append(acc[:, :LANES] / (acc[:, LANES:] + jnp.exp2(sink - m)))
        o_ref[j * tq:(j + 1) * tq, :] = _unstack_heads(jnp.concatenate(outs, axis=0), 64, nh).astype(BF16)


def _swa_attention(q, k, v, sink, dims):
    b, s_len, n_ctx = dims
    tq = SWA_TQ
    rows = tq * SWA_TILES_PER_STEP
    n_t = s_len // rows
    n_kvblk = k.shape[1] // LANES
    nh = q.shape[1] // k.shape[1] * 2
    qw = nh * 64
    ctx_blk = b * s_len // n_ctx
    n_win = tq + 2 * SW_WINDOW + n_ctx
    assert n_win % (2 * LANES) == 0
    qspec = pl.BlockSpec((rows, qw), lambda i, p, t, *_: (i * n_t + t, p))
    lat = pl.BlockSpec((s_len, LANES), lambda i, p, t, *_: (i, p))
    ctx = pl.BlockSpec((n_ctx, LANES), lambda i, p, t, *_: (ctx_blk + i, p))
    return pl.pallas_call(
        functools.partial(_swa_kernel, nh=nh, tq=tq, s_len=s_len),
        grid_spec=pltpu.PrefetchScalarGridSpec(
            num_scalar_prefetch=1,
            grid=(b, n_kvblk, n_t),
            in_specs=[qspec, lat, lat, ctx, ctx],
            out_specs=qspec,
            scratch_shapes=[pltpu.VMEM((SWA_TILES_PER_STEP, n_win, LANES), BF16),
                            pltpu.VMEM((SWA_TILES_PER_STEP, n_win, 2 * LANES), BF16)]),
        out_shape=jax.ShapeDtypeStruct((b * s_len, q.shape[1]), BF16),
        compiler_params=_params("arbitrary", "arbitrary", "arbitrary"),
        name="attn_swa",
    )(sink, q, k, v, k, v)


def _global_kernel(q_ref, k_ref, v_ref, kc_ref, vc_ref, o_ref, kall_ref, vext_ref, s_ref, p_ref, *, nh, s_len):
    @pl.when(pl.program_id(2) == 0)
    def _():
        kall_ref[0:s_len, :] = k_ref[...]
        kall_ref[s_len:, :] = kc_ref[...]
        vext_ref[0:s_len, 0:LANES] = v_ref[...]
        vext_ref[s_len:, 0:LANES] = vc_ref[...]
        vext_ref[:, LANES:] = jnp.ones((vext_ref.shape[0], LANES), BF16)

    qs = _stack_heads(q_ref[...], LANES, nh)
    n_chunks = kall_ref.shape[0] // GLB_CHUNK
    mrun = None
    for c in range(n_chunks):
        ks = slice(c * GLB_CHUNK, (c + 1) * GLB_CHUNK)
        s = _dot_t(qs, kall_ref[ks, :])
        s_ref[:, ks] = s
        blk = _block_max(s)
        mrun = blk if mrun is None else jnp.maximum(mrun, blk)
    m = _lanes(_row_max(mrun), GLB_CHUNK)
    for c in range(n_chunks):
        ks = slice(c * GLB_CHUNK, (c + 1) * GLB_CHUNK)
        p_ref[:, ks] = jnp.exp2(s_ref[:, ks] - m).astype(BF16)
    acc = _dot_row_halves(p_ref, vext_ref[...])
    o_ref[...] = _unstack_heads(acc[:, :LANES] / acc[:, LANES:], LANES, nh).astype(BF16)


def _global_attention(q, k, v, dims):
    b, s_len, n_ctx = dims
    tq = GLB_TQ
    n_t = s_len // tq
    n_kvblk = k.shape[1] // LANES
    nh = q.shape[1] // k.shape[1]
    qw = nh * LANES
    ctx_blk = b * s_len // n_ctx
    n_keys = s_len + n_ctx
    qspec = pl.BlockSpec((tq, qw), lambda i, p, t: (i * n_t + t, p))
    lat = pl.BlockSpec((s_len, LANES), lambda i, p, t: (i, p))
    ctx = pl.BlockSpec((n_ctx, LANES), lambda i, p, t: (ctx_blk + i, p))
    return pl.pallas_call(
        functools.partial(_global_kernel, nh=nh, s_len=s_len),
        grid=(b, n_kvblk, n_t),
        in_specs=[qspec, lat, lat, ctx, ctx],
        out_specs=qspec,
        out_shape=jax.ShapeDtypeStruct((b * s_len, q.shape[1]), BF16),
        scratch_shapes=[pltpu.VMEM((n_keys, LANES), BF16),
                        pltpu.VMEM((n_keys, 2 * LANES), BF16),
                        pltpu.VMEM((nh * tq, n_keys), F32),
                        pltpu.VMEM((nh * tq, n_keys), BF16)],
        compiler_params=_params("arbitrary", "arbitrary", "arbitrary"),
        name="attn_global",
    )(q, k, v, k, v)


def _ctx_kernel(sink_ref, q_ref, kc_ref, vc_ref, o_ref, *, dh, nh, use_sink):
    tq = q_ref.shape[0]
    qw = nh * dh
    for p in range(kc_ref.shape[1] // LANES):
        kv = slice(p * LANES, (p + 1) * LANES)
        qs = _stack_heads(q_ref[:, p * qw:(p + 1) * qw], dh, nh)
        s = _dot_t(qs, kc_ref[:, kv])
        m = _row_max(_block_max(s))
        if use_sink:
            sink = _sink_rows(sink_ref, p * nh, nh, tq)
            m = jnp.maximum(m, sink)
        acc = _dot(jnp.exp2(s - _lanes(m, s.shape[1])).astype(BF16), _ones_ext(vc_ref[:, kv]))
        den = acc[:, LANES:]
        if use_sink:
            den = den + jnp.exp2(sink - m)
        o_ref[:, p * qw:(p + 1) * qw] = _unstack_heads(acc[:, :LANES] / den, dh, nh).astype(BF16)


def _ctx_attention(q, k, v, sink, dims, mixer):
    b, s_len, n_ctx = dims
    n_qh, n_kvh, dh = MIXER_HEADS[mixer]
    n_kvblk = k.shape[1] // LANES
    nh = n_qh // n_kvblk
    ctx_blk = b * s_len // n_ctx
    use_sink = sink is not None
    if not use_sink:
        sink = jnp.zeros((n_qh,), F32)
    return pl.pallas_call(
        functools.partial(_ctx_kernel, dh=dh, nh=nh, use_sink=use_sink),
        grid_spec=pltpu.PrefetchScalarGridSpec(
            num_scalar_prefetch=1,
            grid=(b,),
            in_specs=[pl.BlockSpec((n_ctx, q.shape[1]), lambda i, *_: (ctx_blk + i, 0)),
                      pl.BlockSpec((n_ctx, k.shape[1]), lambda i, *_: (ctx_blk + i, 0)),
                      pl.BlockSpec((n_ctx, v.shape[1]), lambda i, *_: (ctx_blk + i, 0))],
            out_specs=pl.BlockSpec((n_ctx, q.shape[1]), lambda i, *_: (i, 0))),
        out_shape=jax.ShapeDtypeStruct((b * n_ctx, q.shape[1]), BF16),
        compiler_params=_params("arbitrary"),
        name="attn_ctx_m%d" % mixer,
    )(sink, q, k, v)


def _oproj_kernel(*refs, n_lat_tiles, n_y, n_x, moe):
    y = _pick_rows(refs[:n_y], n_lat_tiles)
    x_in = _pick_rows(refs[n_y:n_y + n_x], n_lat_tiles)
    refs = refs[n_y + n_x:]
    mod_ref, n2_ref, wo_ref = refs[:3]
    refs = refs[3:]
    if moe:
        wr_ref, br_ref = refs[:2]
        refs = refs[2:]
    xo_ref, h_ref = refs[:2]
    x = x_in + mod_ref[2:3, :] * _dot(y, wo_ref[...])
    xo_ref[...] = x
    h = _rms_mod(x, n2_ref[...], mod_ref[4:5, :], mod_ref[3:4, :])
    h_ref[...] = h.astype(BF16)
    if moe:
        route_ref, counts_ref, base_ref = refs[2:5]
        h_hi, h_lo = _split_bf16(h)
        w_hi, w_lo = _split_bf16(wr_ref[...])
        hi_terms = _dot(h_hi, jnp.concatenate([w_hi, w_lo], axis=1))
        lo_terms = _dot(h_lo, jnp.concatenate([w_hi, jnp.zeros_like(w_hi)], axis=1))
        logits = hi_terms[:, :LANES] + hi_terms[:, LANES:] + lo_terms[:, :LANES] + br_ref[...]
        lane = lax.broadcasted_iota(jnp.int32, logits.shape, 1)
        v1 = jnp.max(logits, axis=-1, keepdims=True)
        i1 = jnp.min(jnp.where(logits == v1, lane, LANES), axis=-1, keepdims=True)
        rest = jnp.where(lane == i1, NEG_BIG, logits)
        v2 = jnp.max(rest, axis=-1, keepdims=True)
        i2 = jnp.min(jnp.where(rest == v2, lane, LANES), axis=-1, keepdims=True)
        e = jnp.exp(v2 - v1)
        w1 = 1.0 / (1.0 + e)
        w2 = e / (1.0 + e)
        @pl.when(pl.program_id(0) == 0)
        def _():
            base_ref[...] = jnp.zeros_like(base_ref)

        tm = logits.shape[0]
        pick1 = lane == i1
        pick2 = lane == i2
        cnt = jnp.where(pick1, 1.0, 0.0) + jnp.where(pick2, 1.0, 0.0)
        earlier = lax.broadcasted_iota(jnp.int32, (tm, tm), 0) > lax.broadcasted_iota(jnp.int32, (tm, tm), 1)
        before = _dot(jnp.where(earlier, 1.0, 0.0).astype(BF16), cnt.astype(BF16)) + base_ref[0:1, :]
        r1 = jnp.sum(jnp.where(pick1, before, 0.0), axis=-1, keepdims=True)
        r2 = jnp.sum(jnp.where(pick2, before, 0.0), axis=-1, keepdims=True)
        base_ref[...] = base_ref[...] + jnp.sum(cnt, axis=0, keepdims=True)
        counts_ref[...] = base_ref[...]
        route = jnp.where(lane == 0, i1.astype(F32), 0.0)
        route = jnp.where(lane == 1, i2.astype(F32), route)
        route = jnp.where(lane == 2, w1, route)
        route = jnp.where(lane == 3, w2, route)
        route = jnp.where(lane == 4, r1, route)
        route = jnp.where(lane == 5, r2, route)
        route_ref[...] = route


def _oproj(y_lat, y_ctx, xs, mod, norm2, w_o, router, dims, last):
    b, s_len, n_ctx = dims
    x_parts = list(xs) if isinstance(xs, tuple) else [xs]
    d = x_parts[0].shape[1]
    tm = _row_tile(s_len, b * n_ctx, 512)
    n_lat_tiles = b * s_len // tm
    tiles_per_sample = s_len // tm
    n_rows = b * s_len if last else b * (s_len + n_ctx)
    n_tiles = n_rows // tm
    moe = router is not None

    def mod_idx(i):
        return jnp.where(i < n_lat_tiles, i // tiles_per_sample, b)

    row = pl.BlockSpec((tm, d), lambda i: (i, 0))
    y_parts = [y_lat] if last else [y_lat, y_ctx]
    in_specs = ([pl.BlockSpec((tm, y_lat.shape[1]), lambda i: (i, 0))] if last
                else _row_sources(y_parts, tm, n_lat_tiles))
    in_specs += [row] if len(x_parts) == 1 else _row_sources(x_parts, tm, n_lat_tiles)
    in_specs += [pl.BlockSpec((None, 6, d), lambda i: (mod_idx(i), 0, 0)),
                 pl.BlockSpec((1, d), lambda i: (0, 0)),
                 pl.BlockSpec(w_o.shape, lambda i: (0, 0))]
    args = y_parts + x_parts + [mod, norm2.reshape(1, d), w_o.astype(BF16)]
    out_specs = [row, row]
    out_shape = [jax.ShapeDtypeStruct((n_rows, d), F32), jax.ShapeDtypeStruct((n_rows, d), BF16)]
    if moe:
        w_router, b_router = router
        n_e = w_router.shape[1]
        wr = jnp.pad(w_router, ((0, 0), (0, LANES - n_e)))
        br = jnp.pad(b_router, (0, LANES - n_e), constant_values=NEG_BIG).reshape(1, LANES)
        in_specs += [pl.BlockSpec(wr.shape, lambda i: (0, 0)), pl.BlockSpec(br.shape, lambda i: (0, 0))]
        args += [wr, br]
        out_specs += [pl.BlockSpec((tm, LANES), lambda i: (i, 0)), pl.BlockSpec((8, LANES), lambda i: (0, 0))]
        out_shape += [jax.ShapeDtypeStruct((n_rows, LANES), F32), jax.ShapeDtypeStruct((8, LANES), F32)]
    return pl.pallas_call(
        functools.partial(_oproj_kernel, n_lat_tiles=n_lat_tiles, n_y=len(y_parts), n_x=len(x_parts), moe=moe),
        grid=(n_tiles,),
        in_specs=in_specs,
        out_specs=out_specs,
        out_shape=out_shape,
        scratch_shapes=[pltpu.VMEM((8, LANES), F32)] if moe else [],
        compiler_params=_params("arbitrary"),
        name="oproj_moe" if moe else "oproj",
    )(*args)


def _swiglu_partial(h_ref, wg_ref, wu_ref, wd_ref):
    h = h_ref[...]
    a = _silu(_dot(h, wg_ref[...].astype(BF16))) * _dot(h, wu_ref[...].astype(BF16))
    return _dot(a.astype(BF16), wd_ref[...].astype(BF16))


def _accumulate_over_hidden(f, n_f, acc_ref, partial, finish):
    @pl.when(f == 0)
    def _():
        acc_ref[...] = partial()

    @pl.when(jnp.logical_and(f > 0, f < n_f - 1))
    def _():
        acc_ref[...] += partial()

    @pl.when(f == n_f - 1)
    def _():
        finish(acc_ref[...] + partial())


def _ffn_kernel(h_ref, wg_ref, wu_ref, wd_ref, x_ref, mod_ref, o_ref, acc_ref):
    def finish(total):
        o_ref[...] = x_ref[...] + mod_ref[5:6, :] * total

    _accumulate_over_hidden(pl.program_id(1), pl.num_programs(1), acc_ref,
                            functools.partial(_swiglu_partial, h_ref, wg_ref, wu_ref, wd_ref), finish)


def _ffn(h, xs, mod, w_in, w_out, dims):
    b, s_len, n_ctx = dims
    n_rows, d = h.shape
    d_ff = w_out.shape[0]
    tm = _row_tile(s_len, b * n_ctx, 1024)
    tf = FFN_TF
    n_f = d_ff // tf
    n_lat_tiles = b * s_len // tm
    tiles_per_sample = s_len // tm

    def mod_idx(i):
        return jnp.where(i < n_lat_tiles, i // tiles_per_sample, b)

    w_in = _to_bf16(w_in[None])[0]
    w_out = _to_bf16(w_out[None])[0]
    return pl.pallas_call(
        _ffn_kernel,
        grid=(n_rows // tm, n_f),
        in_specs=[pl.BlockSpec((tm, d), lambda i, f: (i, 0)),
                  pl.BlockSpec((d, tf), lambda i, f: (0, f)),
                  pl.BlockSpec((d, tf), lambda i, f: (0, n_f + f)),
                  pl.BlockSpec((tf, d), lambda i, f: (f, 0)),
                  pl.BlockSpec((tm, d), lambda i, f: (i, 0)),
                  pl.BlockSpec((None, 6, d), lambda i, f: (mod_idx(i), 0, 0))],
        out_specs=pl.BlockSpec((tm, d), lambda i, f: (i, 0)),
        out_shape=jax.ShapeDtypeStruct((n_rows, d), F32),
        scratch_shapes=[pltpu.VMEM((tm, d), F32)],
        compiler_params=_params("arbitrary", "arbitrary"),
        name="ffn_dense",
    )(h, w_in, w_in, w_out, xs, mod)


def _moe_ffn_kernel(te_ref, na_ref, h_ref, wg_ref, wu_ref, wd_ref, *rest):
    o_ref, acc_ref = rest[-2:]
    i, f = pl.program_id(0), pl.program_id(1)

    def finish(total):
        o_ref[...] = total.astype(o_ref.dtype)

    @pl.when(i < na_ref[0])
    def _():
        _accumulate_over_hidden(f, pl.num_programs(1), acc_ref,
                                functools.partial(_swiglu_partial, h_ref, wg_ref, wu_ref, wd_ref), finish)


def _moe_ffn(hs, ys, n_rows_total, first_tile, tile_expert, n_active, w_in, w_out):
    n_rows, d = hs.shape
    d_ff = w_out.shape[1]
    tm, tf = MOE_TM, FFN_TF
    n_f = d_ff // tf

    def tile(i, na):
        return jnp.minimum(i, jnp.maximum(na[0] - 1, 0))

    def expert(i, te, na):
        return te[tile(i, na)]

    in_specs = [pl.BlockSpec((tm, d), lambda i, f, te, na: (tile(i, na), 0)),
                pl.BlockSpec((None, d, tf), lambda i, f, te, na: (expert(i, te, na), 0, f)),
                pl.BlockSpec((None, d, tf), lambda i, f, te, na: (expert(i, te, na), 0, n_f + f)),
                pl.BlockSpec((None, tf, d), lambda i, f, te, na: (expert(i, te, na), f, 0))]
    args = [tile_expert, n_active, hs, w_in, w_in, w_out]
    aliases = {}
    if ys is not None:
        in_specs.append(pl.BlockSpec(memory_space=pl.ANY))
        aliases = {len(args): 0}
        args.append(ys)
    return pl.pallas_call(
        _moe_ffn_kernel,
        grid_spec=pltpu.PrefetchScalarGridSpec(
            num_scalar_prefetch=2,
            grid=(n_rows // tm, n_f),
            in_specs=in_specs,
            out_specs=pl.BlockSpec((tm, d), lambda i, f, te, na: (first_tile + tile(i, na), 0)),
            scratch_shapes=[pltpu.VMEM((tm, d), F32)]),
        out_shape=jax.ShapeDtypeStruct((n_rows_total, d), hs.dtype),
        input_output_aliases=aliases,
        compiler_params=_params("arbitrary", "arbitrary"),
        name="moe_ffn",
    )(*args)


def _combine_kernel(x_ref, a_ref, b_ref, route_ref, mod_ref, o_ref):
    w1 = route_ref[:, 2:3]
    w2 = route_ref[:, 3:4]
    mix = w1 * a_ref[...].astype(F32) + w2 * b_ref[...].astype(F32)
    o_ref[...] = x_ref[...] + mod_ref[5:6, :] * mix


def _combine(xs, ya, yb, route, mod, dims):
    b, s_len, n_ctx = dims
    n_rows, d = ya.shape
    tm = _row_tile(s_len, b * n_ctx, 512)
    n_lat_tiles = b * s_len // tm
    tiles_per_sample = s_len // tm

    def mod_idx(i):
        return jnp.where(i < n_lat_tiles, i // tiles_per_sample, b)

    row = pl.BlockSpec((tm, d), lambda i: (i, 0))
    return pl.pallas_call(
        _combine_kernel,
        grid=(n_rows // tm,),
        in_specs=[row, row, row,
                  pl.BlockSpec((tm, LANES), lambda i: (i, 0)),
                  pl.BlockSpec((None, 6, d), lambda i: (mod_idx(i), 0, 0))],
        out_specs=row,
        out_shape=jax.ShapeDtypeStruct((n_rows, d), F32),
        compiler_params=_params("arbitrary"),
        name="moe_combine",
    )(xs, ya, yb, route, mod)


def _route_plan(idx, rank, counts, tm):
    n = idx.shape[0]
    tiles_per = (counts + tm - 1) // tm
    tile_end = jnp.cumsum(tiles_per)
    tile_start = tile_end - tiles_per
    experts = jnp.arange(N_EXPERTS, dtype=jnp.int32)
    start = jnp.sum(jnp.where(idx[:, :, None] == experts, tile_start, 0), axis=-1)
    slot = start * tm + rank
    n_tiles = (2 * n) // tm + N_EXPERTS
    tile_ids = jnp.arange(n_tiles, dtype=jnp.int32)
    tile_expert = jnp.minimum(jnp.sum((tile_end[None, :] <= tile_ids[:, None]).astype(jnp.int32), axis=1),
                              N_EXPERTS - 1)
    token = jnp.arange(2 * n, dtype=jnp.int32) // 2
    _, token_sorted = lax.sort((slot.reshape(-1), token), num_keys=1)
    first_pair = jnp.cumsum(counts) - counts
    row_in_expert = ((tile_ids - tile_start[tile_expert]) * tm)[:, None] + jnp.arange(tm, dtype=jnp.int32)[None, :]
    pair = jnp.clip(first_pair[tile_expert][:, None] + row_in_expert, 0, 2 * n - 1)
    filler = (tile_ids[:, None] * tm + jnp.arange(tm, dtype=jnp.int32)[None, :]) % n
    src = jnp.where(row_in_expert < counts[tile_expert][:, None], token_sorted[pair], filler).reshape(-1)
    return slot, src, tile_expert, tile_end[-1:].astype(jnp.int32)


def _cast_kernel(w_ref, o_ref):
    o_ref[...] = w_ref[...].astype(o_ref.dtype)


def _to_bf16(w):
    n_e, n_r, n_c = w.shape
    rows = n_r
    while rows * n_c * 4 > CAST_BLOCK_BYTES and rows % 2 == 0 and (rows // 2) % 16 == 0:
        rows //= 2
    return pl.pallas_call(
        _cast_kernel,
        grid=(n_e, n_r // rows),
        in_specs=[pl.BlockSpec((None, rows, n_c), lambda e, r: (e, r, 0))],
        out_specs=pl.BlockSpec((None, rows, n_c), lambda e, r: (e, r, 0)),
        out_shape=jax.ShapeDtypeStruct(w.shape, BF16),
        compiler_params=_params("arbitrary", "arbitrary"),
        name="cast_bf16",
    )(w)


def _take_rows(a, rows):
    return a.at[rows].get(mode="promise_in_bounds")


def _moe(h, xs, route, counts, mod, w_exp_in, w_exp_out, dims):
    idx = route[:, 0:2].astype(jnp.int32)
    rank = route[:, 4:6].astype(jnp.int32)
    slot, src, tile_expert, n_active = _route_plan(idx, rank, counts[0, :N_EXPERTS].astype(jnp.int32), MOE_TM)
    n_tiles = src.shape[0] // MOE_TM
    first = max(1, n_tiles // MOE_FIRST_CHUNK_DIV)
    rest = -(-(n_tiles - first) // (MOE_CHUNKS - 1))
    bounds = [0] + [min(first + j * rest, n_tiles) for j in range(MOE_CHUNKS)]
    ys = None
    for t0, t1 in zip(bounds[:-1], bounds[1:]):
        if t1 == t0:
            continue
        hs = _take_rows(h, src[t0 * MOE_TM:t1 * MOE_TM])
        ys = _moe_ffn(hs, ys, src.shape[0], t0, tile_expert[t0:t1], jnp.clip(n_active - t0, 0, t1 - t0),
                      w_exp_in, w_exp_out)
    ya = _take_rows(ys, slot[:, 0])
    yb = _take_rows(ys, slot[:, 1])
    return _combine(xs, ya, yb, route, mod, dims)


def _layer(xs, cond, p, dims, mixer, last):
    b, s_len, n_ctx = dims
    mod = _adaln(cond, p["w_mod"], p["b_mod"])
    q, k, v = _qkv(xs, mod, p["norm1"], p["w_qkv"], p["q_norm"], p["k_norm"], dims, mixer)
    if mixer == 0:
        y_lat = _na_attention(q, k, v, p["rel_bias"], dims)
    elif mixer == 1:
        y_lat = _swa_attention(q, k, v, p["sink"], dims)
    else:
        y_lat = _global_attention(q, k, v, dims)
    y_ctx = None if last else _ctx_attention(q, k, v, p.get("sink"), dims, mixer)
    router = (p["w_router"], p["b_router"]) if "w_router" in p else None
    outs = _oproj(y_lat, y_ctx, xs, mod, p["norm2"], p["w_o"], router, dims, last)
    if router is None:
        xs, h = outs
        return _ffn(h, xs, mod, p["w_ffn_in"], p["w_ffn_out"], dims)
    xs, h, route, counts = outs
    return _moe(h, xs, route, counts, mod, p["w_exp_in"], p["w_exp_out"], dims)


def kernel(x, c, ctx, c_ctx, l0_w_mod, l0_b_mod, l0_norm1, l0_norm2, l0_w_qkv, l0_q_norm, l0_k_norm, l0_rel_bias, l0_w_o, l0_w_ffn_in, l0_w_ffn_out, l1_w_mod, l1_b_mod, l1_norm1, l1_norm2, l1_w_qkv, l1_q_norm, l1_k_norm, l1_sink, l1_w_o, l1_w_router, l1_b_router, l1_w_exp_in, l1_w_exp_out, l2_w_mod, l2_b_mod, l2_norm1, l2_norm2, l2_w_qkv, l2_q_norm, l2_k_norm, l2_w_o, l2_w_ffn_in, l2_w_ffn_out, l3_w_mod, l3_b_mod, l3_norm1, l3_norm2, l3_w_qkv, l3_q_norm, l3_k_norm, l3_rel_bias, l3_w_o, l3_w_router, l3_b_router, l3_w_exp_in, l3_w_exp_out):
    b, s_len, d = x.shape
    n_ctx = ctx.shape[1]
    dims = (b, s_len, n_ctx)
    layers = (
        dict(w_mod=l0_w_mod, b_mod=l0_b_mod, norm1=l0_norm1, norm2=l0_norm2, w_qkv=l0_w_qkv, q_norm=l0_q_norm,
             k_norm=l0_k_norm, rel_bias=l0_rel_bias, w_o=l0_w_o, w_ffn_in=l0_w_ffn_in, w_ffn_out=l0_w_ffn_out),
        dict(w_mod=l1_w_mod, b_mod=l1_b_mod, norm1=l1_norm1, norm2=l1_norm2, w_qkv=l1_w_qkv, q_norm=l1_q_norm,
             k_norm=l1_k_norm, sink=l1_sink, w_o=l1_w_o, w_router=l1_w_router, b_router=l1_b_router,
             w_exp_in=l1_w_exp_in, w_exp_out=l1_w_exp_out),
        dict(w_mod=l2_w_mod, b_mod=l2_b_mod, norm1=l2_norm1, norm2=l2_norm2, w_qkv=l2_w_qkv, q_norm=l2_q_norm,
             k_norm=l2_k_norm, w_o=l2_w_o, w_ffn_in=l2_w_ffn_in, w_ffn_out=l2_w_ffn_out),
        dict(w_mod=l3_w_mod, b_mod=l3_b_mod, norm1=l3_norm1, norm2=l3_norm2, w_qkv=l3_w_qkv, q_norm=l3_q_norm,
             k_norm=l3_k_norm, rel_bias=l3_rel_bias, w_o=l3_w_o, w_router=l3_w_router, b_router=l3_b_router,
             w_exp_in=l3_w_exp_in, w_exp_out=l3_w_exp_out),
    )
    xs = (x.reshape(b * s_len, d), ctx.reshape(b * n_ctx, d))
    pad_rows = -(b + 1) % 8
    cond = jnp.concatenate([c, c_ctx[None, :], jnp.zeros((pad_rows, d), F32)], axis=0)
    n_layers = len(layers)
    for i, p in enumerate(layers):
        xs = _layer(xs, cond, p, dims, i % 3, i == n_layers - 1)
    return xs.reshape(b, s_len, d)
```

```python
import functools

import jax
import jax.numpy as jnp
from jax import lax
from jax.experimental import pallas as pl
from jax.experimental.pallas import tpu as pltpu

F32 = jnp.float32
BF16 = jnp.bfloat16

GRID_W = 64
NORM_EPS = 1e-6
ROPE_THETA = 10000.0
NA_WIN_ROWS = 8
NA_WIN_COLS = 16
SW_WINDOW = 128
N_EXPERTS = 8
MIXER_HEADS = ((16, 16, 64), (16, 4, 64), (8, 4, 128))

LANES = 128
VMEM_LIMIT_BYTES = 56 * 1024 * 1024
NEG_BIG = -1e30
LOG2E = 1.4426950408889634

QKV_CHUNK = 512
GLB_TQ = 1024
GLB_CHUNK = 256
SWA_TQ = 256
SWA_TILES_PER_STEP = 4
NA_CTX_ROWS = 1024
MOE_TM = 1024
MOE_CHUNKS = 4
MOE_FIRST_CHUNK_DIV = 16
CAST_BLOCK_BYTES = 8 * 1024 * 1024
FFN_TF = 512


def _params(*sem):
    return pltpu.CompilerParams(dimension_semantics=sem, vmem_limit_bytes=VMEM_LIMIT_BYTES)


def _row_tile(n_lat_per_sample, n_ctx_rows, cap):
    for tm in (1024, 512, 256, 128):
        if tm <= cap and n_lat_per_sample % tm == 0 and n_ctx_rows % tm == 0:
            return tm
    raise ValueError("no row tile fits")


def _split_bf16(a):
    hi = a.astype(BF16)
    lo = (a - hi.astype(F32)).astype(BF16)
    return hi, lo


def _dot(a, b):
    return jnp.dot(a, b, preferred_element_type=F32)


def _dot_t(a, b):
    return lax.dot_general(a, b, (((1,), (1,)), ((), ())), preferred_element_type=F32)


def _silu(g):
    return g / (1.0 + jnp.exp(-g))


def _rms_mod(x, gain, scale, shift):
    ms = jnp.mean(x * x, axis=-1, keepdims=True)
    return x * lax.rsqrt(ms + NORM_EPS) * gain * (1.0 + scale) + shift


def _adaln_kernel(c_ref, w_ref, b_ref, o_ref):
    a_hi, a_lo = _split_bf16(_silu(c_ref[...]))
    w_hi, w_lo = _split_bf16(w_ref[...])
    o_ref[...] = _dot(a_hi, w_hi) + _dot(a_hi, w_lo) + _dot(a_lo, w_hi) + b_ref[...]


def _adaln(cond, w_mod, b_mod):
    r, d = cond.shape
    n = w_mod.shape[1]
    tn = 1536
    out = pl.pallas_call(
        _adaln_kernel,
        grid=(n // tn,),
        in_specs=[pl.BlockSpec((r, d), lambda j: (0, 0)),
                  pl.BlockSpec((d, tn), lambda j: (0, j)),
                  pl.BlockSpec((1, tn), lambda j: (0, j))],
        out_specs=pl.BlockSpec((r, tn), lambda j: (0, j)),
        out_shape=jax.ShapeDtypeStruct((r, n), F32),
        compiler_params=_params("arbitrary"),
        name="adaln",
    )(cond, w_mod, b_mod.reshape(1, n))
    return out.reshape(r, 6, d)


def _rot_half(z, dh):
    if dh == LANES:
        return pltpu.roll(z, LANES // 2, axis=1)
    lane = lax.broadcasted_iota(jnp.int32, z.shape, 1)
    from_right = pltpu.roll(z, LANES - dh // 2, axis=1)
    from_left = pltpu.roll(z, dh // 2, axis=1)
    return jnp.where((lane % dh) < dh // 2, from_right, from_left)


def _head_mean_sq(z, dh):
    z2 = z * z
    if dh == LANES:
        return jnp.broadcast_to(jnp.sum(z2, axis=-1, keepdims=True), z.shape) * (1.0 / dh)
    low = lax.broadcasted_iota(jnp.int32, z.shape, 1) < dh
    s_low = jnp.sum(jnp.where(low, z2, 0.0), axis=-1, keepdims=True)
    s_high = jnp.sum(jnp.where(low, 0.0, z2), axis=-1, keepdims=True)
    return jnp.where(low, s_low, s_high) * (1.0 / dh)


def _row_sources(arrays, tm, n_lat_tiles):
    lat, ctx = arrays
    return [pl.BlockSpec((tm, lat.shape[1]), lambda i: (jnp.minimum(i, n_lat_tiles - 1), 0)),
            pl.BlockSpec((tm, ctx.shape[1]), lambda i: (jnp.maximum(i - n_lat_tiles, 0), 0))]


def _pick_rows(refs, n_lat_tiles):
    if len(refs) == 1:
        return refs[0][...]
    return jnp.where(pl.program_id(0) < n_lat_tiles, refs[0][...], refs[1][...])


def _qkv_kernel(*refs, n_q, n_kv, dh, rope, n_x, n_lat_tiles):
    x = _pick_rows(refs[:n_x], n_lat_tiles)
    refs = refs[n_x:]
    if rope:
        mod_ref, n1_ref, w_ref, g_ref, cos_ref, sin_ref, q_ref, k_ref, v_ref = refs
    else:
        mod_ref, n1_ref, w_ref, g_ref, q_ref, k_ref, v_ref = refs
    h = _rms_mod(x, n1_ref[...], mod_ref[1:2, :], mod_ref[0:1, :]).astype(BF16)
    n_qk = n_q + n_kv
    cw = QKV_CHUNK

    def store(col, z):
        z = z.astype(BF16)
        if col < n_q:
            q_ref[:, col:col + LANES] = z
        elif col < n_qk:
            k_ref[:, col - n_q:col - n_q + LANES] = z
        else:
            v_ref[:, col - n_qk:col - n_qk + LANES] = z

    normed = []
    for c in range((n_qk + n_kv) // cw):
        y = _dot(h, w_ref[:, c * cw:(c + 1) * cw])
        for s in range(cw // LANES):
            col = c * cw + s * LANES
            z = y[:, s * LANES:(s + 1) * LANES]
            if col < n_qk:
                z = z * lax.rsqrt(_head_mean_sq(z, dh) + NORM_EPS) * g_ref[:, col:col + LANES]
                if rope:
                    normed.append((col, z))
                    continue
            store(col, z)
    for col, z in normed:
        store(col, z * cos_ref[...] + _rot_half(z, dh) * sin_ref[...])


def _rope_tables(s_len, dh, tm):
    n_freq = dh // 4
    inv_freq = ROPE_THETA ** (-jnp.arange(n_freq, dtype=F32) / n_freq)
    t = jnp.arange(s_len)
    row = (t // GRID_W).astype(F32)
    col = (t % GRID_W).astype(F32)
    ang = jnp.concatenate([row[:, None] * inv_freq, col[:, None] * inv_freq], axis=-1)
    cos, sin = jnp.cos(ang), jnp.sin(ang)
    reps = LANES // dh
    cos_t = jnp.tile(jnp.concatenate([cos, cos], axis=-1), (1, reps))
    sin_t = jnp.tile(jnp.concatenate([-sin, sin], axis=-1), (1, reps))
    cos_t = jnp.concatenate([cos_t, jnp.ones((tm, LANES), F32)], axis=0)
    sin_t = jnp.concatenate([sin_t, jnp.zeros((tm, LANES), F32)], axis=0)
    return cos_t, sin_t


def _qkv(xs, mod, norm1, w_qkv, q_gain, k_gain, dims, mixer):
    b, s_len, n_ctx = dims
    x_parts = list(xs) if isinstance(xs, tuple) else [xs]
    n_rows, d = sum(a.shape[0] for a in x_parts), x_parts[0].shape[1]
    n_qh, n_kvh, dh = MIXER_HEADS[mixer]
    n_q, n_kv = n_qh * dh, n_kvh * dh
    rope = mixer != 0
    tm = _row_tile(s_len, b * n_ctx, 512)
    n_lat_tiles = b * s_len // tm
    tiles_per_sample = s_len // tm

    gains = jnp.concatenate([jnp.tile(q_gain * (dh ** -0.5 * LOG2E), n_qh),
                             jnp.tile(k_gain, n_kvh)]).reshape(1, n_q + n_kv)

    def mod_idx(i):
        return jnp.where(i < n_lat_tiles, i // tiles_per_sample, b)

    x_specs = [pl.BlockSpec((tm, d), lambda i: (i, 0))] if len(x_parts) == 1 else _row_sources(x_parts, tm, n_lat_tiles)
    in_specs = x_specs + [pl.BlockSpec((None, 6, d), lambda i: (mod_idx(i), 0, 0)),
                          pl.BlockSpec((1, d), lambda i: (0, 0)),
                          pl.BlockSpec(w_qkv.shape, lambda i: (0, 0)),
                          pl.BlockSpec(gains.shape, lambda i: (0, 0))]
    args = x_parts + [mod, norm1.reshape(1, d), w_qkv.astype(BF16), gains]
    if rope:
        cos_t, sin_t = _rope_tables(s_len, dh, tm)

        def pos_idx(i):
            return jnp.where(i < n_lat_tiles, i % tiles_per_sample, tiles_per_sample)

        in_specs += [pl.BlockSpec((tm, LANES), lambda i: (pos_idx(i), 0))] * 2
        args += [cos_t, sin_t]
    return pl.pallas_call(
        functools.partial(_qkv_kernel, n_q=n_q, n_kv=n_kv, dh=dh, rope=rope, n_x=len(x_parts),
                          n_lat_tiles=n_lat_tiles),
        grid=(n_rows // tm,),
        in_specs=in_specs,
        out_specs=[pl.BlockSpec((tm, n_q), lambda i: (i, 0)),
                   pl.BlockSpec((tm, n_kv), lambda i: (i, 0)),
                   pl.BlockSpec((tm, n_kv), lambda i: (i, 0))],
        out_shape=[jax.ShapeDtypeStruct((n_rows, n_q), BF16),
                   jax.ShapeDtypeStruct((n_rows, n_kv), BF16),
                   jax.ShapeDtypeStruct((n_rows, n_kv), BF16)],
        compiler_params=_params("arbitrary"),
        name="qkv_m%d" % mixer,
    )(*args)


def _stack_heads(q, dh, nh):
    if dh == LANES:
        return jnp.concatenate([q[:, h * LANES:(h + 1) * LANES] for h in range(nh)], axis=0)
    per_kv = nh // 2
    tq = q.shape[0]
    lane = lax.broadcasted_iota(jnp.int32, (tq, LANES), 1)
    ops = []
    for h in range(nh):
        slot = h // per_kv
        chunk = q[:, (h // 2) * LANES:(h // 2 + 1) * LANES].astype(F32)
        if h % 2 != slot:
            chunk = pltpu.roll(chunk, dh, axis=1)
        keep = lane < dh if slot == 0 else lane >= dh
        ops.append(jnp.where(keep, chunk, 0.0).astype(BF16))
    return jnp.concatenate(ops, axis=0)


def _unstack_heads(o, dh, nh):
    tq = o.shape[0] // nh
    if dh == LANES:
        return jnp.concatenate([o[h * tq:(h + 1) * tq] for h in range(nh)], axis=1)
    per_kv = nh // 2
    lane = lax.broadcasted_iota(jnp.int32, (tq, LANES), 1)
    chunks = []
    for c in range(nh // 2):
        parts = []
        for h in (2 * c, 2 * c + 1):
            oh = o[h * tq:(h + 1) * tq]
            if h % 2 != h // per_kv:
                oh = pltpu.roll(oh, dh, axis=1)
            parts.append(oh)
        chunks.append(jnp.where(lane < dh, parts[0], parts[1]))
    return jnp.concatenate(chunks, axis=1)


def _lanes(x, n):
    return x if n == LANES else jnp.concatenate([x] * (n // LANES), axis=1)


def _block_max(s):
    return functools.reduce(jnp.maximum, [s[:, j * LANES:(j + 1) * LANES] for j in range(s.shape[1] // LANES)])


def _row_max(blk):
    return jnp.broadcast_to(jnp.max(blk, axis=-1, keepdims=True), blk.shape)


def _dot_row_halves(p, v):
    half = p.shape[0] // 2
    return jnp.concatenate([_dot(p[0:half, :], v), _dot(p[half:, :], v)], axis=0)


def _ones_ext(v):
    return jnp.concatenate([v, jnp.ones_like(v)], axis=1)


def _sink_rows(sink_ref, first_head, nh, tq):
    return jnp.concatenate([jnp.full((tq, LANES), sink_ref[first_head + h] * LOG2E, F32) for h in range(nh)], axis=0)


def _na_kernel(q_ref, k_ref, v_ref, kc_ref, vc_ref, bias_ref, o_ref,
               qs_ref, vext_ref, mc_ref, numc_ref, lc_ref, *, rows):
    s_len = q_ref.shape[0]
    band = NA_WIN_ROWS * GRID_W
    lane = lax.broadcasted_iota(jnp.int32, (s_len, LANES), 1)
    q = q_ref[...]
    qs_ref[0:s_len, :] = jnp.where(lane < 64, q, jnp.zeros_like(q))
    qs_ref[s_len:, :] = jnp.where(lane >= 64, q, jnp.zeros_like(q))
    vext_ref[:, 0:LANES] = v_ref[...]
    vext_ref[:, LANES:] = jnp.ones((s_len, LANES), BF16)

    kc = kc_ref[...]
    vcx = _ones_ext(vc_ref[...])
    lane_q = lax.broadcasted_iota(jnp.int32, (GRID_W, LANES), 1)
    half = band // 2

    def both_heads(ref, q0, n):
        return jnp.concatenate([ref[q0:q0 + n, :], ref[s_len + q0:s_len + q0 + n, :]], axis=0)

    def context_block(q0, n):
        s = _dot_t(both_heads(qs_ref, q0, n), kc)
        m = _row_max(_block_max(s))
        acc = _dot(jnp.exp2(s - _lanes(m, s.shape[1])).astype(BF16), vcx)
        for i, base in enumerate((q0, s_len + q0)):
            mc_ref[base:base + n, :] = m[i * n:(i + 1) * n]
            numc_ref[base:base + n, :] = acc[i * n:(i + 1) * n, :LANES]
            lc_ref[base:base + n, :] = acc[i * n:(i + 1) * n, LANES:]

    def one_row(r):
        r0 = min(max(r - NA_WIN_ROWS // 2, 0), rows - NA_WIN_ROWS)
        q0, k0, d0 = r * GRID_W, r0 * GRID_W, r0 - r + NA_WIN_ROWS - 1
        bias = jnp.concatenate(
            [jnp.concatenate([bias_ref[0, d0 + 2 * j], bias_ref[1, d0 + 2 * j]], axis=0)
             for j in range(NA_WIN_ROWS // 2)], axis=1)
        qs = both_heads(qs_ref, q0, GRID_W)
        s = jnp.concatenate([_dot_t(qs, k_ref[k0:k0 + half, :]), _dot_t(qs, k_ref[k0 + half:k0 + band, :])],
                            axis=1) + bias
        mc = both_heads(mc_ref, q0, GRID_W)
        m = jnp.maximum(_row_max(_block_max(s)), mc)
        acc = _dot(jnp.exp2(s - _lanes(m, band)).astype(BF16), vext_ref[k0:k0 + band, :])
        alpha = jnp.exp2(mc - m)
        num = acc[:, :LANES] + alpha * both_heads(numc_ref, q0, GRID_W)
        den = acc[:, LANES:] + alpha * both_heads(lc_ref, q0, GRID_W)
        o = num / den
        o_ref[q0:q0 + GRID_W, :] = jnp.where(lane_q < 64, o[:GRID_W], o[GRID_W:]).astype(BF16)

    group = NA_CTX_ROWS // (2 * GRID_W)
    for g in range(rows // group):
        context_block(g * group * GRID_W, group * GRID_W)
        for r in range(g * group, (g + 1) * group):
            one_row(r)


def _na_bias_tables(rel_bias):
    n_h = rel_bias.shape[0]
    col = jnp.arange(GRID_W)
    col_start = jnp.clip(col - NA_WIN_COLS // 2, 0, GRID_W - NA_WIN_COLS)
    in_win = (col[None, :] >= col_start[:, None]) & (col[None, :] < col_start[:, None] + NA_WIN_COLS)
    col_idx = jnp.clip(col[None, :] - col[:, None] + NA_WIN_COLS - 1, 0, 2 * NA_WIN_COLS - 2)
    masked = jnp.where(in_win[None, None], rel_bias[:, :, col_idx] * LOG2E, NEG_BIG)
    pairs = jnp.concatenate([masked[:, :-1], masked[:, 1:]], axis=-1)
    return pairs.reshape((n_h // 2, 2) + pairs.shape[1:])


def _na_attention(q, k, v, rel_bias, dims):
    b, s_len, n_ctx = dims
    rows = s_len // GRID_W
    n_pairs = q.shape[1] // LANES
    bias = _na_bias_tables(rel_bias)
    ctx_blk = b * s_len // n_ctx
    lat = pl.BlockSpec((s_len, LANES), lambda p, i: (i, p))
    ctx = pl.BlockSpec((n_ctx, LANES), lambda p, i: (ctx_blk + i, p))
    return pl.pallas_call(
        functools.partial(_na_kernel, rows=rows),
        grid=(n_pairs, b),
        in_specs=[lat, lat, lat, ctx, ctx,
                  pl.BlockSpec((None,) + bias.shape[1:], lambda p, i: (p, 0, 0, 0, 0))],
        out_specs=lat,
        out_shape=jax.ShapeDtypeStruct((b * s_len, q.shape[1]), BF16),
        scratch_shapes=[pltpu.VMEM((2 * s_len, LANES), BF16),
                        pltpu.VMEM((s_len, 2 * LANES), BF16),
                        pltpu.VMEM((2 * s_len, LANES), F32),
                        pltpu.VMEM((2 * s_len, LANES), F32),
                        pltpu.VMEM((2 * s_len, LANES), F32)],
        compiler_params=_params("arbitrary", "arbitrary"),
        name="attn_na",
    )(q, k, v, k, v, bias)


def _swa_kernel(sink_ref, q_ref, k_ref, v_ref, kc_ref, vc_ref, o_ref, kwin_ref, vwin_ref, *, nh, tq, s_len):
    n_sub = q_ref.shape[0] // tq
    span = tq + 2 * SW_WINDOW
    p_blk, t = pl.program_id(1), pl.program_id(2)

    @pl.when(t == 0)
    def _():
        for j in range(n_sub):
            kwin_ref[j, span:, :] = kc_ref[...]
            vwin_ref[j, span:, 0:LANES] = vc_ref[...]
            vwin_ref[j, :, LANES:] = jnp.ones((vwin_ref.shape[1], LANES), BF16)

    row = lax.broadcasted_iota(jnp.int32, (tq, span), 0)
    col = lax.broadcasted_iota(jnp.int32, (tq, span), 1)
    sink_all = _sink_rows(sink_ref, p_blk * nh, nh, tq)
    kd = 2 * LANES
    for j in range(n_sub):
        q0 = (t * n_sub + j) * tq
        k0 = pl.multiple_of(jnp.clip(q0 - SW_WINDOW, 0, s_len - span), SW_WINDOW)
        kwin_ref[j, 0:span, :] = k_ref[pl.ds(k0, span), :]
        vwin_ref[j, 0:span, 0:LANES] = v_ref[pl.ds(k0, span), :]
        bias = jnp.where(jnp.abs(col - row + (k0 - q0)) <= SW_WINDOW, 0.0, NEG_BIG)
        qs_all = _stack_heads(q_ref[j * tq:(j + 1) * tq, :], 64, nh)
        vwin = vwin_ref[j]
        outs = []
        for h in range(nh):
            qs = qs_all[h * tq:(h + 1) * tq]
            sink = sink_all[h * tq:(h + 1) * tq]
            s = jnp.concatenate([_dot_t(qs, kwin_ref[j, c * kd:(c + 1) * kd, :])
                                 for c in range(kwin_ref.shape[1] // kd)], axis=1)
            s = jnp.concatenate([s[:, :span] + bias, s[:, span:]], axis=1)
            m = jnp.maximum(_row_max(_block_max(s)), sink)
            acc = _dot(jnp.exp2(s - _lanes(m, s.shape[1])).astype(BF16), vwin)
            outs.append(acc[:, :LANES] / (acc[:, LANES:] + jnp.exp2(sink - m)))
        o_ref[j * tq:(j + 1) * tq, :] = _unstack_heads(jnp.concatenate(outs, axis=0), 64, nh).astype(BF16)


def _swa_attention(q, k, v, sink, dims):
    b, s_len, n_ctx = dims
    tq = SWA_TQ
    rows = tq * SWA_TILES_PER_STEP
    n_t = s_len // rows
    n_kvblk = k.shape[1] // LANES
    nh = q.shape[1] // k.shape[1] * 2
    qw = nh * 64
    ctx_blk = b * s_len // n_ctx
    n_win = tq + 2 * SW_WINDOW + n_ctx
    assert n_win % (2 * LANES) == 0
    qspec = pl.BlockSpec((rows, qw), lambda i, p, t, *_: (i * n_t + t, p))
    lat = pl.BlockSpec((s_len, LANES), lambda i, p, t, *_: (i, p))
    ctx = pl.BlockSpec((n_ctx, LANES), lambda i, p, t, *_: (ctx_blk + i, p))
    return pl.pallas_call(
        functools.partial(_swa_kernel, nh=nh, tq=tq, s_len=s_len),
        grid_spec=pltpu.PrefetchScalarGridSpec(
            num_scalar_prefetch=1,
            grid=(b, n_kvblk, n_t),
            in_specs=[qspec, lat, lat, ctx, ctx],
            out_specs=qspec,
            scratch_shapes=[pltpu.VMEM((SWA_TILES_PER_STEP, n_win, LANES), BF16),
                            pltpu.VMEM((SWA_TILES_PER_STEP, n_win, 2 * LANES), BF16)]),
        out_shape=jax.ShapeDtypeStruct((b * s_len, q.shape[1]), BF16),
        compiler_params=_params("arbitrary", "arbitrary", "arbitrary"),
        name="attn_swa",
    )(sink, q, k, v, k, v)


def _global_kernel(q_ref, k_ref, v_ref, kc_ref, vc_ref, o_ref, kall_ref, vext_ref, s_ref, p_ref, *, nh, s_len):
    @pl.when(pl.program_id(2) == 0)
    def _():
        kall_ref[0:s_len, :] = k_ref[...]
        kall_ref[s_len:, :] = kc_ref[...]
        vext_ref[0:s_len, 0:LANES] = v_ref[...]
        vext_ref[s_len:, 0:LANES] = vc_ref[...]
        vext_ref[:, LANES:] = jnp.ones((vext_ref.shape[0], LANES), BF16)

    qs = _stack_heads(q_ref[...], LANES, nh)
    n_chunks = kall_ref.shape[0] // GLB_CHUNK
    mrun = None
    for c in range(n_chunks):
        ks = slice(c * GLB_CHUNK, (c + 1) * GLB_CHUNK)
        s = _dot_t(qs, kall_ref[ks, :])
        s_ref[:, ks] = s
        blk = _block_max(s)
        mrun = blk if mrun is None else jnp.maximum(mrun, blk)
    m = _lanes(_row_max(mrun), GLB_CHUNK)
    for c in range(n_chunks):
        ks = slice(c * GLB_CHUNK, (c + 1) * GLB_CHUNK)
        p_ref[:, ks] = jnp.exp2(s_ref[:, ks] - m).astype(BF16)
    acc = _dot_row_halves(p_ref, vext_ref[...])
    o_ref[...] = _unstack_heads(acc[:, :LANES] / acc[:, LANES:], LANES, nh).astype(BF16)


def _global_attention(q, k, v, dims):
    b, s_len, n_ctx = dims
    tq = GLB_TQ
    n_t = s_len // tq
    n_kvblk = k.shape[1] // LANES
    nh = q.shape[1] // k.shape[1]
    qw = nh * LANES
    ctx_blk = b * s_len // n_ctx
    n_keys = s_len + n_ctx
    qspec = pl.BlockSpec((tq, qw), lambda i, p, t: (i * n_t + t, p))
    lat = pl.BlockSpec((s_len, LANES), lambda i, p, t: (i, p))
    ctx = pl.BlockSpec((n_ctx, LANES), lambda i, p, t: (ctx_blk + i, p))
    return pl.pallas_call(
        functools.partial(_global_kernel, nh=nh, s_len=s_len),
        grid=(b, n_kvblk, n_t),
        in_specs=[qspec, lat, lat, ctx, ctx],
        out_specs=qspec,
        out_shape=jax.ShapeDtypeStruct((b * s_len, q.shape[1]), BF16),
        scratch_shapes=[pltpu.VMEM((n_keys, LANES), BF16),
                        pltpu.VMEM((n_keys, 2 * LANES), BF16),
                        pltpu.VMEM((nh * tq, n_keys), F32),
                        pltpu.VMEM((nh * tq, n_keys), BF16)],
        compiler_params=_params("arbitrary", "arbitrary", "arbitrary"),
        name="attn_global",
    )(q, k, v, k, v)


def _ctx_kernel(sink_ref, q_ref, kc_ref, vc_ref, o_ref, *, dh, nh, use_sink):
    tq = q_ref.shape[0]
    qw = nh * dh
    for p in range(kc_ref.shape[1] // LANES):
        kv = slice(p * LANES, (p + 1) * LANES)
        qs = _stack_heads(q_ref[:, p * qw:(p + 1) * qw], dh, nh)
        s = _dot_t(qs, kc_ref[:, kv])
        m = _row_max(_block_max(s))
        if use_sink:
            sink = _sink_rows(sink_ref, p * nh, nh, tq)
            m = jnp.maximum(m, sink)
        acc = _dot(jnp.exp2(s - _lanes(m, s.shape[1])).astype(BF16), _ones_ext(vc_ref[:, kv]))
        den = acc[:, LANES:]
        if use_sink:
            den = den + jnp.exp2(sink - m)
        o_ref[:, p * qw:(p + 1) * qw] = _unstack_heads(acc[:, :LANES] / den, dh, nh).astype(BF16)


def _ctx_attention(q, k, v, sink, dims, mixer):
    b, s_len, n_ctx = dims
    n_qh, n_kvh, dh = MIXER_HEADS[mixer]
    n_kvblk = k.shape[1] // LANES
    nh = n_qh // n_kvblk
    ctx_blk = b * s_len // n_ctx
    use_sink = sink is not None
    if not use_sink:
        sink = jnp.zeros((n_qh,), F32)
    return pl.pallas_call(
        functools.partial(_ctx_kernel, dh=dh, nh=nh, use_sink=use_sink),
        grid_spec=pltpu.PrefetchScalarGridSpec(
            num_scalar_prefetch=1,
            grid=(b,),
            in_specs=[pl.BlockSpec((n_ctx, q.shape[1]), lambda i, *_: (ctx_blk + i, 0)),
                      pl.BlockSpec((n_ctx, k.shape[1]), lambda i, *_: (ctx_blk + i, 0)),
                      pl.BlockSpec((n_ctx, v.shape[1]), lambda i, *_: (ctx_blk + i, 0))],
            out_specs=pl.BlockSpec((n_ctx, q.shape[1]), lambda i, *_: (i, 0))),
        out_shape=jax.ShapeDtypeStruct((b * n_ctx, q.shape[1]), BF16),
        compiler_params=_params("arbitrary"),
        name="attn_ctx_m%d" % mixer,
    )(sink, q, k, v)


def _oproj_kernel(*refs, n_lat_tiles, n_y, n_x, moe):
    y = _pick_rows(refs[:n_y], n_lat_tiles)
    x_in = _pick_rows(refs[n_y:n_y + n_x], n_lat_tiles)
    refs = refs[n_y + n_x:]
    mod_ref, n2_ref, wo_ref = refs[:3]
    refs = refs[3:]
    if moe:
        wr_ref, br_ref = refs[:2]
        refs = refs[2:]
    xo_ref, h_ref = refs[:2]
    x = x_in + mod_ref[2:3, :] * _dot(y, wo_ref[...])
    xo_ref[...] = x
    h = _rms_mod(x, n2_ref[...], mod_ref[4:5, :], mod_ref[3:4, :])
    h_ref[...] = h.astype(BF16)
    if moe:
        route_ref, counts_ref, base_ref = refs[2:5]
        h_hi, h_lo = _split_bf16(h)
        w_hi, w_lo = _split_bf16(wr_ref[...])
        hi_terms = _dot(h_hi, jnp.concatenate([w_hi, w_lo], axis=1))
        lo_terms = _dot(h_lo, jnp.concatenate([w_hi, jnp.zeros_like(w_hi)], axis=1))
        logits = hi_terms[:, :LANES] + hi_terms[:, LANES:] + lo_terms[:, :LANES] + br_ref[...]
        lane = lax.broadcasted_iota(jnp.int32, logits.shape, 1)
        v1 = jnp.max(logits, axis=-1, keepdims=True)
        i1 = jnp.min(jnp.where(logits == v1, lane, LANES), axis=-1, keepdims=True)
        rest = jnp.where(lane == i1, NEG_BIG, logits)
        v2 = jnp.max(rest, axis=-1, keepdims=True)
        i2 = jnp.min(jnp.where(rest == v2, lane, LANES), axis=-1, keepdims=True)
        e = jnp.exp(v2 - v1)
        w1 = 1.0 / (1.0 + e)
        w2 = e / (1.0 + e)
        @pl.when(pl.program_id(0) == 0)
        def _():
            base_ref[...] = jnp.zeros_like(base_ref)

        tm = logits.shape[0]
        pick1 = lane == i1
        pick2 = lane == i2
        cnt = jnp.where(pick1, 1.0, 0.0) + jnp.where(pick2, 1.0, 0.0)
        earlier = lax.broadcasted_iota(jnp.int32, (tm, tm), 0) > lax.broadcasted_iota(jnp.int32, (tm, tm), 1)
        before = _dot(jnp.where(earlier, 1.0, 0.0).astype(BF16), cnt.astype(BF16)) + base_ref[0:1, :]
        r1 = jnp.sum(jnp.where(pick1, before, 0.0), axis=-1, keepdims=True)
        r2 = jnp.sum(jnp.where(pick2, before, 0.0), axis=-1, keepdims=True)
        base_ref[...] = base_ref[...] + jnp.sum(cnt, axis=0, keepdims=True)
        counts_ref[...] = base_ref[...]
        route = jnp.where(lane == 0, i1.astype(F32), 0.0)
        route = jnp.where(lane == 1, i2.astype(F32), route)
        route = jnp.where(lane == 2, w1, route)
        route = jnp.where(lane == 3, w2, route)
        route = jnp.where(lane == 4, r1, route)
        route = jnp.where(lane == 5, r2, route)
        route_ref[...] = route


def _oproj(y_lat, y_ctx, xs, mod, norm2, w_o, router, dims, last):
    b, s_len, n_ctx = dims
    x_parts = list(xs) if isinstance(xs, tuple) else [xs]
    d = x_parts[0].shape[1]
    tm = _row_tile(s_len, b * n_ctx, 512)
    n_lat_tiles = b * s_len // tm
    tiles_per_sample = s_len // tm
    n_rows = b * s_len if last else b * (s_len + n_ctx)
    n_tiles = n_rows // tm
    moe = router is not None

    def mod_idx(i):
        return jnp.where(i < n_lat_tiles, i // tiles_per_sample, b)

    row = pl.BlockSpec((tm, d), lambda i: (i, 0))
    y_parts = [y_lat] if last else [y_lat, y_ctx]
    in_specs = ([pl.BlockSpec((tm, y_lat.shape[1]), lambda i: (i, 0))] if last
                else _row_sources(y_parts, tm, n_lat_tiles))
    in_specs += [row] if len(x_parts) == 1 else _row_sources(x_parts, tm, n_lat_tiles)
    in_specs += [pl.BlockSpec((None, 6, d), lambda i: (mod_idx(i), 0, 0)),
                 pl.BlockSpec((1, d), lambda i: (0, 0)),
                 pl.BlockSpec(w_o.shape, lambda i: (0, 0))]
    args = y_parts + x_parts + [mod, norm2.reshape(1, d), w_o.astype(BF16)]
    out_specs = [row, row]
    out_shape = [jax.ShapeDtypeStruct((n_rows, d), F32), jax.ShapeDtypeStruct((n_rows, d), BF16)]
    if moe:
        w_router, b_router = router
        n_e = w_router.shape[1]
        wr = jnp.pad(w_router, ((0, 0), (0, LANES - n_e)))
        br = jnp.pad(b_router, (0, LANES - n_e), constant_values=NEG_BIG).reshape(1, LANES)
        in_specs += [pl.BlockSpec(wr.shape, lambda i: (0, 0)), pl.BlockSpec(br.shape, lambda i: (0, 0))]
        args += [wr, br]
        out_specs += [pl.BlockSpec((tm, LANES), lambda i: (i, 0)), pl.BlockSpec((8, LANES), lambda i: (0, 0))]
        out_shape += [jax.ShapeDtypeStruct((n_rows, LANES), F32), jax.ShapeDtypeStruct((8, LANES), F32)]
    return pl.pallas_call(
        functools.partial(_oproj_kernel, n_lat_tiles=n_lat_tiles, n_y=len(y_parts), n_x=len(x_parts), moe=moe),
        grid=(n_tiles,),
        in_specs=in_specs,
        out_specs=out_specs,
        out_shape=out_shape,
        scratch_shapes=[pltpu.VMEM((8, LANES), F32)] if moe else [],
        compiler_params=_params("arbitrary"),
        name="oproj_moe" if moe else "oproj",
    )(*args)


def _swiglu_partial(h_ref, wg_ref, wu_ref, wd_ref):
    h = h_ref[...]
    a = _silu(_dot(h, wg_ref[...].astype(BF16))) * _dot(h, wu_ref[...].astype(BF16))
    return _dot(a.astype(BF16), wd_ref[...].astype(BF16))


def _accumulate_over_hidden(f, n_f, acc_ref, partial, finish):
    @pl.when(f == 0)
    def _():
        acc_ref[...] = partial()

    @pl.when(jnp.logical_and(f > 0, f < n_f - 1))
    def _():
        acc_ref[...] += partial()

    @pl.when(f == n_f - 1)
    def _():
        finish(acc_ref[...] + partial())


def _ffn_kernel(h_ref, wg_ref, wu_ref, wd_ref, x_ref, mod_ref, o_ref, acc_ref):
    def finish(total):
        o_ref[...] = x_ref[...] + mod_ref[5:6, :] * total

    _accumulate_over_hidden(pl.program_id(1), pl.num_programs(1), acc_ref,
                            functools.partial(_swiglu_partial, h_ref, wg_ref, wu_ref, wd_ref), finish)


def _ffn(h, xs, mod, w_in, w_out, dims):
    b, s_len, n_ctx = dims
    n_rows, d = h.shape
    d_ff = w_out.shape[0]
    tm = _row_tile(s_len, b * n_ctx, 1024)
    tf = FFN_TF
    n_f = d_ff // tf
    n_lat_tiles = b * s_len // tm
    tiles_per_sample = s_len // tm

    def mod_idx(i):
        return jnp.where(i < n_lat_tiles, i // tiles_per_sample, b)

    w_in = _to_bf16(w_in[None])[0]
    w_out = _to_bf16(w_out[None])[0]
    return pl.pallas_call(
        _ffn_kernel,
        grid=(n_rows // tm, n_f),
        in_specs=[pl.BlockSpec((tm, d), lambda i, f: (i, 0)),
                  pl.BlockSpec((d, tf), lambda i, f: (0, f)),
                  pl.BlockSpec((d, tf), lambda i, f: (0, n_f + f)),
                  pl.BlockSpec((tf, d), lambda i, f: (f, 0)),
                  pl.BlockSpec((tm, d), lambda i, f: (i, 0)),
                  pl.BlockSpec((None, 6, d), lambda i, f: (mod_idx(i), 0, 0))],
        out_specs=pl.BlockSpec((tm, d), lambda i, f: (i, 0)),
        out_shape=jax.ShapeDtypeStruct((n_rows, d), F32),
        scratch_shapes=[pltpu.VMEM((tm, d), F32)],
        compiler_params=_params("arbitrary", "arbitrary"),
        name="ffn_dense",
    )(h, w_in, w_in, w_out, xs, mod)


def _moe_ffn_kernel(te_ref, na_ref, h_ref, wg_ref, wu_ref, wd_ref, *rest):
    o_ref, acc_ref = rest[-2:]
    i, f = pl.program_id(0), pl.program_id(1)

    def finish(total):
        o_ref[...] = total.astype(o_ref.dtype)

    @pl.when(i < na_ref[0])
    def _():
        _accumulate_over_hidden(f, pl.num_programs(1), acc_ref,
                                functools.partial(_swiglu_partial, h_ref, wg_ref, wu_ref, wd_ref), finish)


def _moe_ffn(hs, ys, n_rows_total, first_tile, tile_expert, n_active, w_in, w_out):
    n_rows, d = hs.shape
    d_ff = w_out.shape[1]
    tm, tf = MOE_TM, FFN_TF
    n_f = d_ff // tf

    def tile(i, na):
        return jnp.minimum(i, jnp.maximum(na[0] - 1, 0))

    def expert(i, te, na):
        return te[tile(i, na)]

    in_specs = [pl.BlockSpec((tm, d), lambda i, f, te, na: (tile(i, na), 0)),
                pl.BlockSpec((None, d, tf), lambda i, f, te, na: (expert(i, te, na), 0, f)),
                pl.BlockSpec((None, d, tf), lambda i, f, te, na: (expert(i, te, na), 0, n_f + f)),
                pl.BlockSpec((None, tf, d), lambda i, f, te, na: (expert(i, te, na), f, 0))]
    args = [tile_expert, n_active, hs, w_in, w_in, w_out]
    aliases = {}
    if ys is not None:
        in_specs.append(pl.BlockSpec(memory_space=pl.ANY))
        aliases = {len(args): 0}
        args.append(ys)
    return pl.pallas_call(
        _moe_ffn_kernel,
        grid_spec=pltpu.PrefetchScalarGridSpec(
            num_scalar_prefetch=2,
            grid=(n_rows // tm, n_f),
            in_specs=in_specs,
            out_specs=pl.BlockSpec((tm, d), lambda i, f, te, na: (first_tile + tile(i, na), 0)),
            scratch_shapes=[pltpu.VMEM((tm, d), F32)]),
        out_shape=jax.ShapeDtypeStruct((n_rows_total, d), hs.dtype),
        input_output_aliases=aliases,
        compiler_params=_params("arbitrary", "arbitrary"),
        name="moe_ffn",
    )(*args)


def _combine_kernel(x_ref, a_ref, b_ref, route_ref, mod_ref, o_ref):
    w1 = route_ref[:, 2:3]
    w2 = route_ref[:, 3:4]
    mix = w1 * a_ref[...].astype(F32) + w2 * b_ref[...].astype(F32)
    o_ref[...] = x_ref[...] + mod_ref[5:6, :] * mix


def _combine(xs, ya, yb, route, mod, dims):
    b, s_len, n_ctx = dims
    n_rows, d = ya.shape
    tm = _row_tile(s_len, b * n_ctx, 512)
    n_lat_tiles = b * s_len // tm
    tiles_per_sample = s_len // tm

    def mod_idx(i):
        return jnp.where(i < n_lat_tiles, i // tiles_per_sample, b)

    row = pl.BlockSpec((tm, d), lambda i: (i, 0))
    return pl.pallas_call(
        _combine_kernel,
        grid=(n_rows // tm,),
        in_specs=[row, row, row,
                  pl.BlockSpec((tm, LANES), lambda i: (i, 0)),
                  pl.BlockSpec((None, 6, d), lambda i: (mod_idx(i), 0, 0))],
        out_specs=row,
        out_shape=jax.ShapeDtypeStruct((n_rows, d), F32),
        compiler_params=_params("arbitrary"),
        name="moe_combine",
    )(xs, ya, yb, route, mod)


def _route_plan(idx, rank, counts, tm):
    n = idx.shape[0]
    tiles_per = (counts + tm - 1) // tm
    tile_end = jnp.cumsum(tiles_per)
    tile_start = tile_end - tiles_per
    experts = jnp.arange(N_EXPERTS, dtype=jnp.int32)
    start = jnp.sum(jnp.where(idx[:, :, None] == experts, tile_start, 0), axis=-1)
    slot = start * tm + rank
    n_tiles = (2 * n) // tm + N_EXPERTS
    tile_ids = jnp.arange(n_tiles, dtype=jnp.int32)
    tile_expert = jnp.minimum(jnp.sum((tile_end[None, :] <= tile_ids[:, None]).astype(jnp.int32), axis=1),
                              N_EXPERTS - 1)
    assert n_tiles * tm * n < 2 ** 32
    key = slot.reshape(-1).astype(jnp.uint32) * jnp.uint32(n) + jnp.arange(2 * n, dtype=jnp.uint32) // 2
    token_sorted = (lax.sort(key) % jnp.uint32(n)).astype(jnp.int32)
    first_pair = jnp.cumsum(counts) - counts
    row_in_expert = ((tile_ids - tile_start[tile_expert]) * tm)[:, None] + jnp.arange(tm, dtype=jnp.int32)[None, :]
    pair = jnp.clip(first_pair[tile_expert][:, None] + row_in_expert, 0, 2 * n - 1)
    filler = (tile_ids[:, None] * tm + jnp.arange(tm, dtype=jnp.int32)[None, :]) % n
    src = jnp.where(row_in_expert < counts[tile_expert][:, None], token_sorted[pair], filler).reshape(-1)
    return slot, src, tile_expert, tile_end[-1:].astype(jnp.int32)


def _cast_kernel(w_ref, o_ref):
    o_ref[...] = w_ref[...].astype(o_ref.dtype)


def _to_bf16(w):
    n_e, n_r, n_c = w.shape
    rows = n_r
    while rows * n_c * 4 > CAST_BLOCK_BYTES and rows % 2 == 0 and (rows // 2) % 16 == 0:
        rows //= 2
    return pl.pallas_call(
        _cast_kernel,
        grid=(n_e, n_r // rows),
        in_specs=[pl.BlockSpec((None, rows, n_c), lambda e, r: (e, r, 0))],
        out_specs=pl.BlockSpec((None, rows, n_c), lambda e, r: (e, r, 0)),
        out_shape=jax.ShapeDtypeStruct(w.shape, BF16),
        compiler_params=_params("arbitrary", "arbitrary"),
        name="cast_bf16",
    )(w)


def _take_rows(a, rows):
    return a.at[rows].get(mode="promise_in_bounds")


def _moe(h, xs, route, counts, mod, w_exp_in, w_exp_out, dims):
    idx = route[:, 0:2].astype(jnp.int32)
    rank = route[:, 4:6].astype(jnp.int32)
    slot, src, tile_expert, n_active = _route_plan(idx, rank, counts[0, :N_EXPERTS].astype(jnp.int32), MOE_TM)
    n_tiles = src.shape[0] // MOE_TM
    first = max(1, n_tiles // MOE_FIRST_CHUNK_DIV)
    rest = -(-(n_tiles - first) // (MOE_CHUNKS - 1))
    bounds = [0] + [min(first + j * rest, n_tiles) for j in range(MOE_CHUNKS)]
    ys = None
    for t0, t1 in zip(bounds[:-1], bounds[1:]):
        if t1 == t0:
            continue
        hs = _take_rows(h, src[t0 * MOE_TM:t1 * MOE_TM])
        ys = _moe_ffn(hs, ys, src.shape[0], t0, tile_expert[t0:t1], jnp.clip(n_active - t0, 0, t1 - t0),
                      w_exp_in, w_exp_out)
    ya = _take_rows(ys, slot[:, 0])
    yb = _take_rows(ys, slot[:, 1])
    return _combine(xs, ya, yb, route, mod, dims)


def _layer(xs, cond, p, dims, mixer, last):
    b, s_len, n_ctx = dims
    mod = _adaln(cond, p["w_mod"], p["b_mod"])
    q, k, v = _qkv(xs, mod, p["norm1"], p["w_qkv"], p["q_norm"], p["k_norm"], dims, mixer)
    if mixer == 0:
        y_lat = _na_attention(q, k, v, p["rel_bias"], dims)
    elif mixer == 1:
        y_lat = _swa_attention(q, k, v, p["sink"], dims)
    else:
        y_lat = _global_attention(q, k, v, dims)
    y_ctx = None if last else _ctx_attention(q, k, v, p.get("sink"), dims, mixer)
    router = (p["w_router"], p["b_router"]) if "w_router" in p else None
    outs = _oproj(y_lat, y_ctx, xs, mod, p["norm2"], p["w_o"], router, dims, last)
    if router is None:
        xs, h = outs
        return _ffn(h, xs, mod, p["w_ffn_in"], p["w_ffn_out"], dims)
    xs, h, route, counts = outs
    return _moe(h, xs, route, counts, mod, p["w_exp_in"], p["w_exp_out"], dims)


def kernel(x, c, ctx, c_ctx, l0_w_mod, l0_b_mod, l0_norm1, l0_norm2, l0_w_qkv, l0_q_norm, l0_k_norm, l0_rel_bias, l0_w_o, l0_w_ffn_in, l0_w_ffn_out, l1_w_mod, l1_b_mod, l1_norm1, l1_norm2, l1_w_qkv, l1_q_norm, l1_k_norm, l1_sink, l1_w_o, l1_w_router, l1_b_router, l1_w_exp_in, l1_w_exp_out, l2_w_mod, l2_b_mod, l2_norm1, l2_norm2, l2_w_qkv, l2_q_norm, l2_k_norm, l2_w_o, l2_w_ffn_in, l2_w_ffn_out, l3_w_mod, l3_b_mod, l3_norm1, l3_norm2, l3_w_qkv, l3_q_norm, l3_k_norm, l3_rel_bias, l3_w_o, l3_w_router, l3_b_router, l3_w_exp_in, l3_w_exp_out):
    b, s_len, d = x.shape
    n_ctx = ctx.shape[1]
    dims = (b, s_len, n_ctx)
    layers = (
        dict(w_mod=l0_w_mod, b_mod=l0_b_mod, norm1=l0_norm1, norm2=l0_norm2, w_qkv=l0_w_qkv, q_norm=l0_q_norm,
             k_norm=l0_k_norm, rel_bias=l0_rel_bias, w_o=l0_w_o, w_ffn_in=l0_w_ffn_in, w_ffn_out=l0_w_ffn_out),
        dict(w_mod=l1_w_mod, b_mod=l1_b_mod, norm1=l1_norm1, norm2=l1_norm2, w_qkv=l1_w_qkv, q_norm=l1_q_norm,
             k_norm=l1_k_norm, sink=l1_sink, w_o=l1_w_o, w_router=l1_w_router, b_router=l1_b_router,
             w_exp_in=l1_w_exp_in, w_exp_out=l1_w_exp_out),
        dict(w_mod=l2_w_mod, b_mod=l2_b_mod, norm1=l2_norm1, norm2=l2_norm2, w_qkv=l2_w_qkv, q_norm=l2_q_norm,
             k_norm=l2_k_norm, w_o=l2_w_o, w_ffn_in=l2_w_ffn_in, w_ffn_out=l2_w_ffn_out),
        dict(w_mod=l3_w_mod, b_mod=l3_b_mod, norm1=l3_norm1, norm2=l3_norm2, w_qkv=l3_w_qkv, q_norm=l3_q_norm,
             k_norm=l3_k_norm, rel_bias=l3_rel_bias, w_o=l3_w_o, w_router=l3_w_router, b_router=l3_b_router,
             w_exp_in=l3_w_exp_in, w_exp_out=l3_w_exp_out),
    )
    xs = (x.reshape(b * s_len, d), ctx.reshape(b * n_ctx, d))
    pad_rows = -(b + 1) % 8
    cond = jnp.concatenate([c, c_ctx[None, :], jnp.zeros((pad_rows, d), F32)], axis=0)
    n_layers = len(layers)
    for i, p in enumerate(layers):
        xs = _layer(xs, cond, p, dims, i % 3, i == n_layers - 1)
    return xs.reshape(b, s_len, d)
```

```python
import functools

import jax
import jax.numpy as jnp
from jax import lax
from jax.experimental import pallas as pl
from jax.experimental.pallas import tpu as pltpu

F32 = jnp.float32
BF16 = jnp.bfloat16

GRID_W = 64
NORM_EPS = 1e-6
ROPE_THETA = 10000.0
NA_WIN_ROWS = 8
NA_WIN_COLS = 16
SW_WINDOW = 128
N_EXPERTS = 8
MIXER_HEADS = ((16, 16, 64), (16, 4, 64), (8, 4, 128))

LANES = 128
VMEM_LIMIT_BYTES = 56 * 1024 * 1024
NEG_BIG = -1e30
LOG2E = 1.4426950408889634

QKV_CHUNK = 512
GLB_TQ = 1024
GLB_CHUNK = 256
SWA_TQ = 256
SWA_TILES_PER_STEP = 4
NA_CTX_ROWS = 1024
MOE_TM = 1024
MOE_CHUNKS = 4
MOE_FIRST_CHUNK_DIV = 16
CAST_BLOCK_BYTES = 8 * 1024 * 1024
FFN_TF = 512


def _params(*sem):
    return pltpu.CompilerParams(dimension_semantics=sem, vmem_limit_bytes=VMEM_LIMIT_BYTES)


def _row_tile(n_lat_per_sample, n_ctx_rows, cap):
    for tm in (1024, 512, 256, 128):
        if tm <= cap and n_lat_per_sample % tm == 0 and n_ctx_rows % tm == 0:
            return tm
    raise ValueError("no row tile fits")


def _split_bf16(a):
    hi = a.astype(BF16)
    lo = (a - hi.astype(F32)).astype(BF16)
    return hi, lo


def _dot(a, b):
    return jnp.dot(a, b, preferred_element_type=F32)


def _dot_t(a, b):
    return lax.dot_general(a, b, (((1,), (1,)), ((), ())), preferred_element_type=F32)


def _silu(g):
    return g / (1.0 + jnp.exp(-g))


def _rms_mod(x, gain, scale, shift):
    ms = jnp.mean(x * x, axis=-1, keepdims=True)
    return x * lax.rsqrt(ms + NORM_EPS) * gain * (1.0 + scale) + shift


def _adaln_kernel(c_ref, w_ref, b_ref, o_ref):
    a_hi, a_lo = _split_bf16(_silu(c_ref[...]))
    w_hi, w_lo = _split_bf16(w_ref[...])
    o_ref[...] = _dot(a_hi, w_hi) + _dot(a_hi, w_lo) + _dot(a_lo, w_hi) + b_ref[...]


def _adaln(cond, w_mod, b_mod):
    r, d = cond.shape
    n = w_mod.shape[1]
    tn = 1536
    out = pl.pallas_call(
        _adaln_kernel,
        grid=(n // tn,),
        in_specs=[pl.BlockSpec((r, d), lambda j: (0, 0)),
                  pl.BlockSpec((d, tn), lambda j: (0, j)),
                  pl.BlockSpec((1, tn), lambda j: (0, j))],
        out_specs=pl.BlockSpec((r, tn), lambda j: (0, j)),
        out_shape=jax.ShapeDtypeStruct((r, n), F32),
        compiler_params=_params("arbitrary"),
        name="adaln",
    )(cond, w_mod, b_mod.reshape(1, n))
    return out.reshape(r, 6, d)


def _rot_half(z, dh):
    if dh == LANES:
        return pltpu.roll(z, LANES // 2, axis=1)
    lane = lax.broadcasted_iota(jnp.int32, z.shape, 1)
    from_right = pltpu.roll(z, LANES - dh // 2, axis=1)
    from_left = pltpu.roll(z, dh // 2, axis=1)
    return jnp.where((lane % dh) < dh // 2, from_right, from_left)


def _head_mean_sq(z, dh):
    z2 = z * z
    if dh == LANES:
        return jnp.broadcast_to(jnp.sum(z2, axis=-1, keepdims=True), z.shape) * (1.0 / dh)
    low = lax.broadcasted_iota(jnp.int32, z.shape, 1) < dh
    s_low = jnp.sum(jnp.where(low, z2, 0.0), axis=-1, keepdims=True)
    s_high = jnp.sum(jnp.where(low, 0.0, z2), axis=-1, keepdims=True)
    return jnp.where(low, s_low, s_high) * (1.0 / dh)


def _row_sources(arrays, tm, n_lat_tiles):
    lat, ctx = arrays
    return [pl.BlockSpec((tm, lat.shape[1]), lambda i: (jnp.minimum(i, n_lat_tiles - 1), 0)),
            pl.BlockSpec((tm, ctx.shape[1]), lambda i: (jnp.maximum(i - n_lat_tiles, 0), 0))]


def _pick_rows(refs, n_lat_tiles):
    if len(refs) == 1:
        return refs[0][...]
    return jnp.where(pl.program_id(0) < n_lat_tiles, refs[0][...], refs[1][...])


def _qkv_kernel(*refs, n_q, n_kv, dh, rope, n_x, n_lat_tiles):
    x = _pick_rows(refs[:n_x], n_lat_tiles)
    refs = refs[n_x:]
    if rope:
        mod_ref, n1_ref, w_ref, g_ref, cos_ref, sin_ref, q_ref, k_ref, v_ref = refs
    else:
        mod_ref, n1_ref, w_ref, g_ref, q_ref, k_ref, v_ref = refs
    h = _rms_mod(x, n1_ref[...], mod_ref[1:2, :], mod_ref[0:1, :]).astype(BF16)
    n_qk = n_q + n_kv
    cw = QKV_CHUNK

    def store(col, z):
        z = z.astype(BF16)
        if col < n_q:
            q_ref[:, col:col + LANES] = z
        elif col < n_qk:
            k_ref[:, col - n_q:col - n_q + LANES] = z
        else:
            v_ref[:, col - n_qk:col - n_qk + LANES] = z

    normed = []
    for c in range((n_qk + n_kv) // cw):
        y = _dot(h, w_ref[:, c * cw:(c + 1) * cw])
        for s in range(cw // LANES):
            col = c * cw + s * LANES
            z = y[:, s * LANES:(s + 1) * LANES]
            if col < n_qk:
                z = z * lax.rsqrt(_head_mean_sq(z, dh) + NORM_EPS) * g_ref[:, col:col + LANES]
                if rope:
                    normed.append((col, z))
                    continue
            store(col, z)
    for col, z in normed:
        store(col, z * cos_ref[...] + _rot_half(z, dh) * sin_ref[...])


def _rope_tables(s_len, dh, tm):
    n_freq = dh // 4
    inv_freq = ROPE_THETA ** (-jnp.arange(n_freq, dtype=F32) / n_freq)
    t = jnp.arange(s_len)
    row = (t // GRID_W).astype(F32)
    col = (t % GRID_W).astype(F32)
    ang = jnp.concatenate([row[:, None] * inv_freq, col[:, None] * inv_freq], axis=-1)
    cos, sin = jnp.cos(ang), jnp.sin(ang)
    reps = LANES // dh
    cos_t = jnp.tile(jnp.concatenate([cos, cos], axis=-1), (1, reps))
    sin_t = jnp.tile(jnp.concatenate([-sin, sin], axis=-1), (1, reps))
    cos_t = jnp.concatenate([cos_t, jnp.ones((tm, LANES), F32)], axis=0)
    sin_t = jnp.concatenate([sin_t, jnp.zeros((tm, LANES), F32)], axis=0)
    return cos_t, sin_t


def _qkv(xs, mod, norm1, w_qkv, q_gain, k_gain, dims, mixer):
    b, s_len, n_ctx = dims
    x_parts = list(xs) if isinstance(xs, tuple) else [xs]
    n_rows, d = sum(a.shape[0] for a in x_parts), x_parts[0].shape[1]
    n_qh, n_kvh, dh = MIXER_HEADS[mixer]
    n_q, n_kv = n_qh * dh, n_kvh * dh
    rope = mixer != 0
    assert w_qkv.shape[1] == n_q + 2 * n_kv and (n_q + 2 * n_kv) % QKV_CHUNK == 0 and n_q % LANES == 0 and n_kv % LANES == 0
    tm = _row_tile(s_len, b * n_ctx, 512)
    n_lat_tiles = b * s_len // tm
    tiles_per_sample = s_len // tm

    gains = jnp.concatenate([jnp.tile(q_gain * (dh ** -0.5 * LOG2E), n_qh),
                             jnp.tile(k_gain, n_kvh)]).reshape(1, n_q + n_kv)

    def mod_idx(i):
        return jnp.where(i < n_lat_tiles, i // tiles_per_sample, b)

    x_specs = [pl.BlockSpec((tm, d), lambda i: (i, 0))] if len(x_parts) == 1 else _row_sources(x_parts, tm, n_lat_tiles)
    in_specs = x_specs + [pl.BlockSpec((None, 6, d), lambda i: (mod_idx(i), 0, 0)),
                          pl.BlockSpec((1, d), lambda i: (0, 0)),
                          pl.BlockSpec(w_qkv.shape, lambda i: (0, 0)),
                          pl.BlockSpec(gains.shape, lambda i: (0, 0))]
    args = x_parts + [mod, norm1.reshape(1, d), w_qkv.astype(BF16), gains]
    if rope:
        cos_t, sin_t = _rope_tables(s_len, dh, tm)

        def pos_idx(i):
            return jnp.where(i < n_lat_tiles, i % tiles_per_sample, tiles_per_sample)

        in_specs += [pl.BlockSpec((tm, LANES), lambda i: (pos_idx(i), 0))] * 2
        args += [cos_t, sin_t]
    return pl.pallas_call(
        functools.partial(_qkv_kernel, n_q=n_q, n_kv=n_kv, dh=dh, rope=rope, n_x=len(x_parts),
                          n_lat_tiles=n_lat_tiles),
        grid=(n_rows // tm,),
        in_specs=in_specs,
        out_specs=[pl.BlockSpec((tm, n_q), lambda i: (i, 0)),
                   pl.BlockSpec((tm, n_kv), lambda i: (i, 0)),
                   pl.BlockSpec((tm, n_kv), lambda i: (i, 0))],
        out_shape=[jax.ShapeDtypeStruct((n_rows, n_q), BF16),
                   jax.ShapeDtypeStruct((n_rows, n_kv), BF16),
                   jax.ShapeDtypeStruct((n_rows, n_kv), BF16)],
        compiler_params=_params("arbitrary"),
        name="qkv_m%d" % mixer,
    )(*args)


def _stack_heads(q, dh, nh):
    if dh == LANES:
        return jnp.concatenate([q[:, h * LANES:(h + 1) * LANES] for h in range(nh)], axis=0)
    per_kv = nh // 2
    tq = q.shape[0]
    lane = lax.broadcasted_iota(jnp.int32, (tq, LANES), 1)
    ops = []
    for h in range(nh):
        slot = h // per_kv
        chunk = q[:, (h // 2) * LANES:(h // 2 + 1) * LANES].astype(F32)
        if h % 2 != slot:
            chunk = pltpu.roll(chunk, dh, axis=1)
        keep = lane < dh if slot == 0 else lane >= dh
        ops.append(jnp.where(keep, chunk, 0.0).astype(BF16))
    return jnp.concatenate(ops, axis=0)


def _unstack_heads(o, dh, nh):
    tq = o.shape[0] // nh
    if dh == LANES:
        return jnp.concatenate([o[h * tq:(h + 1) * tq] for h in range(nh)], axis=1)
    per_kv = nh // 2
    lane = lax.broadcasted_iota(jnp.int32, (tq, LANES), 1)
    chunks = []
    for c in range(nh // 2):
        parts = []
        for h in (2 * c, 2 * c + 1):
            oh = o[h * tq:(h + 1) * tq]
            if h % 2 != h // per_kv:
                oh = pltpu.roll(oh, dh, axis=1)
            parts.append(oh)
        chunks.append(jnp.where(lane < dh, parts[0], parts[1]))
    return jnp.concatenate(chunks, axis=1)


def _lanes(x, n):
    return x if n == LANES else jnp.concatenate([x] * (n // LANES), axis=1)


def _block_max(s):
    return functools.reduce(jnp.maximum, [s[:, j * LANES:(j + 1) * LANES] for j in range(s.shape[1] // LANES)])


def _row_max(blk):
    return jnp.broadcast_to(jnp.max(blk, axis=-1, keepdims=True), blk.shape)


def _dot_row_halves(p, v):
    half = p.shape[0] // 2
    return jnp.concatenate([_dot(p[0:half, :], v), _dot(p[half:, :], v)], axis=0)


def _ones_ext(v):
    return jnp.concatenate([v, jnp.ones_like(v)], axis=1)


def _sink_rows(sink_ref, first_head, nh, tq):
    return jnp.concatenate([jnp.full((tq, LANES), sink_ref[first_head + h] * LOG2E, F32) for h in range(nh)], axis=0)


def _na_kernel(q_ref, k_ref, v_ref, kc_ref, vc_ref, bias_ref, o_ref,
               qs_ref, vext_ref, mc_ref, numc_ref, lc_ref, *, rows):
    s_len = q_ref.shape[0]
    band = NA_WIN_ROWS * GRID_W
    lane = lax.broadcasted_iota(jnp.int32, (s_len, LANES), 1)
    q = q_ref[...]
    qs_ref[0:s_len, :] = jnp.where(lane < 64, q, jnp.zeros_like(q))
    qs_ref[s_len:, :] = jnp.where(lane >= 64, q, jnp.zeros_like(q))
    vext_ref[:, 0:LANES] = v_ref[...]
    vext_ref[:, LANES:] = jnp.ones((s_len, LANES), BF16)

    kc = kc_ref[...]
    vcx = _ones_ext(vc_ref[...])
    lane_q = lax.broadcasted_iota(jnp.int32, (GRID_W, LANES), 1)
    half = band // 2

    def both_heads(ref, q0, n):
        return jnp.concatenate([ref[q0:q0 + n, :], ref[s_len + q0:s_len + q0 + n, :]], axis=0)

    def context_block(q0, n):
        s = _dot_t(both_heads(qs_ref, q0, n), kc)
        m = _row_max(_block_max(s))
        acc = _dot(jnp.exp2(s - _lanes(m, s.shape[1])).astype(BF16), vcx)
        for i, base in enumerate((q0, s_len + q0)):
            mc_ref[base:base + n, :] = m[i * n:(i + 1) * n]
            numc_ref[base:base + n, :] = acc[i * n:(i + 1) * n, :LANES]
            lc_ref[base:base + n, :] = acc[i * n:(i + 1) * n, LANES:]

    def one_row(r):
        r0 = min(max(r - NA_WIN_ROWS // 2, 0), rows - NA_WIN_ROWS)
        q0, k0, d0 = r * GRID_W, r0 * GRID_W, r0 - r + NA_WIN_ROWS - 1
        bias = jnp.concatenate(
            [jnp.concatenate([bias_ref[0, d0 + 2 * j], bias_ref[1, d0 + 2 * j]], axis=0)
             for j in range(NA_WIN_ROWS // 2)], axis=1)
        qs = both_heads(qs_ref, q0, GRID_W)
        s = jnp.concatenate([_dot_t(qs, k_ref[k0:k0 + half, :]), _dot_t(qs, k_ref[k0 + half:k0 + band, :])],
                            axis=1) + bias
        mc = both_heads(mc_ref, q0, GRID_W)
        m = jnp.maximum(_row_max(_block_max(s)), mc)
        acc = _dot(jnp.exp2(s - _lanes(m, band)).astype(BF16), vext_ref[k0:k0 + band, :])
        alpha = jnp.exp2(mc - m)
        num = acc[:, :LANES] + alpha * both_heads(numc_ref, q0, GRID_W)
        den = acc[:, LANES:] + alpha * both_heads(lc_ref, q0, GRID_W)
        o = num / den
        o_ref[q0:q0 + GRID_W, :] = jnp.where(lane_q < 64, o[:GRID_W], o[GRID_W:]).astype(BF16)

    group = NA_CTX_ROWS // (2 * GRID_W)
    for g in range(rows // group):
        context_block(g * group * GRID_W, group * GRID_W)
        for r in range(g * group, (g + 1) * group):
            one_row(r)


def _na_bias_tables(rel_bias):
    n_h = rel_bias.shape[0]
    col = jnp.arange(GRID_W)
    col_start = jnp.clip(col - NA_WIN_COLS // 2, 0, GRID_W - NA_WIN_COLS)
    in_win = (col[None, :] >= col_start[:, None]) & (col[None, :] < col_start[:, None] + NA_WIN_COLS)
    col_idx = jnp.clip(col[None, :] - col[:, None] + NA_WIN_COLS - 1, 0, 2 * NA_WIN_COLS - 2)
    masked = jnp.where(in_win[None, None], rel_bias[:, :, col_idx] * LOG2E, NEG_BIG)
    pairs = jnp.concatenate([masked[:, :-1], masked[:, 1:]], axis=-1)
    return pairs.reshape((n_h // 2, 2) + pairs.shape[1:])


def _na_attention(q, k, v, rel_bias, dims):
    b, s_len, n_ctx = dims
    rows = s_len // GRID_W
    assert s_len % GRID_W == 0 and rows >= NA_WIN_ROWS and (rows * 2 * GRID_W) % NA_CTX_ROWS == 0
    assert (b * s_len) % n_ctx == 0
    n_pairs = q.shape[1] // LANES
    bias = _na_bias_tables(rel_bias)
    ctx_blk = b * s_len // n_ctx
    lat = pl.BlockSpec((s_len, LANES), lambda p, i: (i, p))
    ctx = pl.BlockSpec((n_ctx, LANES), lambda p, i: (ctx_blk + i, p))
    return pl.pallas_call(
        functools.partial(_na_kernel, rows=rows),
        grid=(n_pairs, b),
        in_specs=[lat, lat, lat, ctx, ctx,
                  pl.BlockSpec((None,) + bias.shape[1:], lambda p, i: (p, 0, 0, 0, 0))],
        out_specs=lat,
        out_shape=jax.ShapeDtypeStruct((b * s_len, q.shape[1]), BF16),
        scratch_shapes=[pltpu.VMEM((2 * s_len, LANES), BF16),
                        pltpu.VMEM((s_len, 2 * LANES), BF16),
                        pltpu.VMEM((2 * s_len, LANES), F32),
                        pltpu.VMEM((2 * s_len, LANES), F32),
                        pltpu.VMEM((2 * s_len, LANES), F32)],
        compiler_params=_params("arbitrary", "arbitrary"),
        name="attn_na",
    )(q, k, v, k, v, bias)


def _swa_kernel(sink_ref, q_ref, k_ref, v_ref, kc_ref, vc_ref, o_ref, kwin_ref, vwin_ref, *, nh, tq, s_len):
    n_sub = q_ref.shape[0] // tq
    span = tq + 2 * SW_WINDOW
    p_blk, t = pl.program_id(1), pl.program_id(2)

    @pl.when(t == 0)
    def _():
        for j in range(n_sub):
            kwin_ref[j, span:, :] = kc_ref[...]
            vwin_ref[j, span:, 0:LANES] = vc_ref[...]
            vwin_ref[j, :, LANES:] = jnp.ones((vwin_ref.shape[1], LANES), BF16)

    row = lax.broadcasted_iota(jnp.int32, (tq, span), 0)
    col = lax.broadcasted_iota(jnp.int32, (tq, span), 1)
    sink_all = _sink_rows(sink_ref, p_blk * nh, nh, tq)
    kd = 2 * LANES
    for j in range(n_sub):
        q0 = (t * n_sub + j) * tq
        k0 = pl.multiple_of(jnp.clip(q0 - SW_WINDOW, 0, s_len - span), SW_WINDOW)
        kwin_ref[j, 0:span, :] = k_ref[pl.ds(k0, span), :]
        vwin_ref[j, 0:span, 0:LANES] = v_ref[pl.ds(k0, span), :]
        bias = jnp.where(jnp.abs(col - row + (k0 - q0)) <= SW_WINDOW, 0.0, NEG_BIG)
        qs_all = _stack_heads(q_ref[j * tq:(j + 1) * tq, :], 64, nh)
        vwin = vwin_ref[j]
        outs = []
        for h in range(nh):
            qs = qs_all[h * tq:(h + 1) * tq]
            sink = sink_all[h * tq:(h + 1) * tq]
            s = jnp.concatenate([_dot_t(qs, kwin_ref[j, c * kd:(c + 1) * kd, :])
                                 for c in range(kwin_ref.shape[1] // kd)], axis=1)
            s = jnp.concatenate([s[:, :span] + bias, s[:, span:]], axis=1)
            m = jnp.maximum(_row_max(_block_max(s)), sink)
            acc = _dot(jnp.exp2(s - _lanes(m, s.shape[1])).astype(BF16), vwin)
            outs.append(acc[:, :LANES] / (acc[:, LANES:] + jnp.exp2(sink - m)))
        o_ref[j * tq:(j + 1) * tq, :] = _unstack_heads(jnp.concatenate(outs, axis=0), 64, nh).astype(BF16)


def _swa_attention(q, k, v, sink, dims):
    b, s_len, n_ctx = dims
    tq = SWA_TQ
    rows = tq * SWA_TILES_PER_STEP
    n_t = s_len // rows
    assert s_len % rows == 0 and tq % SW_WINDOW == 0 and s_len >= tq + 2 * SW_WINDOW and (b * s_len) % n_ctx == 0
    n_kvblk = k.shape[1] // LANES
    nh = q.shape[1] // k.shape[1] * 2
    qw = nh * 64
    ctx_blk = b * s_len // n_ctx
    n_win = tq + 2 * SW_WINDOW + n_ctx
    assert n_win % (2 * LANES) == 0
    qspec = pl.BlockSpec((rows, qw), lambda i, p, t, *_: (i * n_t + t, p))
    lat = pl.BlockSpec((s_len, LANES), lambda i, p, t, *_: (i, p))
    ctx = pl.BlockSpec((n_ctx, LANES), lambda i, p, t, *_: (ctx_blk + i, p))
    return pl.pallas_call(
        functools.partial(_swa_kernel, nh=nh, tq=tq, s_len=s_len),
        grid_spec=pltpu.PrefetchScalarGridSpec(
            num_scalar_prefetch=1,
            grid=(b, n_kvblk, n_t),
            in_specs=[qspec, lat, lat, ctx, ctx],
            out_specs=qspec,
            scratch_shapes=[pltpu.VMEM((SWA_TILES_PER_STEP, n_win, LANES), BF16),
                            pltpu.VMEM((SWA_TILES_PER_STEP, n_win, 2 * LANES), BF16)]),
        out_shape=jax.ShapeDtypeStruct((b * s_len, q.shape[1]), BF16),
        compiler_params=_params("arbitrary", "arbitrary", "arbitrary"),
        name="attn_swa",
    )(sink, q, k, v, k, v)


def _global_kernel(q_ref, k_ref, v_ref, kc_ref, vc_ref, o_ref, kall_ref, vext_ref, s_ref, p_ref, *, nh, s_len):
    @pl.when(pl.program_id(2) == 0)
    def _():
        kall_ref[0:s_len, :] = k_ref[...]
        kall_ref[s_len:, :] = kc_ref[...]
        vext_ref[0:s_len, 0:LANES] = v_ref[...]
        vext_ref[s_len:, 0:LANES] = vc_ref[...]
        vext_ref[:, LANES:] = jnp.ones((vext_ref.shape[0], LANES), BF16)

    qs = _stack_heads(q_ref[...], LANES, nh)
    n_chunks = kall_ref.shape[0] // GLB_CHUNK
    mrun = None
    for c in range(n_chunks):
        ks = slice(c * GLB_CHUNK, (c + 1) * GLB_CHUNK)
        s = _dot_t(qs, kall_ref[ks, :])
        s_ref[:, ks] = s
        blk = _block_max(s)
        mrun = blk if mrun is None else jnp.maximum(mrun, blk)
    m = _lanes(_row_max(mrun), GLB_CHUNK)
    for c in range(n_chunks):
        ks = slice(c * GLB_CHUNK, (c + 1) * GLB_CHUNK)
        p_ref[:, ks] = jnp.exp2(s_ref[:, ks] - m).astype(BF16)
    acc = _dot_row_halves(p_ref, vext_ref[...])
    o_ref[...] = _unstack_heads(acc[:, :LANES] / acc[:, LANES:], LANES, nh).astype(BF16)


def _global_attention(q, k, v, dims):
    b, s_len, n_ctx = dims
    tq = GLB_TQ
    n_t = s_len // tq
    n_kvblk = k.shape[1] // LANES
    nh = q.shape[1] // k.shape[1]
    qw = nh * LANES
    ctx_blk = b * s_len // n_ctx
    n_keys = s_len + n_ctx
    assert s_len % tq == 0 and n_keys % GLB_CHUNK == 0 and (b * s_len) % n_ctx == 0
    qspec = pl.BlockSpec((tq, qw), lambda i, p, t: (i * n_t + t, p))
    lat = pl.BlockSpec((s_len, LANES), lambda i, p, t: (i, p))
    ctx = pl.BlockSpec((n_ctx, LANES), lambda i, p, t: (ctx_blk + i, p))
    return pl.pallas_call(
        functools.partial(_global_kernel, nh=nh, s_len=s_len),
        grid=(b, n_kvblk, n_t),
        in_specs=[qspec, lat, lat, ctx, ctx],
        out_specs=qspec,
        out_shape=jax.ShapeDtypeStruct((b * s_len, q.shape[1]), BF16),
        scratch_shapes=[pltpu.VMEM((n_keys, LANES), BF16),
                        pltpu.VMEM((n_keys, 2 * LANES), BF16),
                        pltpu.VMEM((nh * tq, n_keys), F32),
                        pltpu.VMEM((nh * tq, n_keys), BF16)],
        compiler_params=_params("arbitrary", "arbitrary", "arbitrary"),
        name="attn_global",
    )(q, k, v, k, v)


def _ctx_kernel(sink_ref, q_ref, kc_ref, vc_ref, o_ref, *, dh, nh, use_sink):
    tq = q_ref.shape[0]
    qw = nh * dh
    for p in range(kc_ref.shape[1] // LANES):
        kv = slice(p * LANES, (p + 1) * LANES)
        qs = _stack_heads(q_ref[:, p * qw:(p + 1) * qw], dh, nh)
        s = _dot_t(qs, kc_ref[:, kv])
        m = _row_max(_block_max(s))
        if use_sink:
            sink = _sink_rows(sink_ref, p * nh, nh, tq)
            m = jnp.maximum(m, sink)
        acc = _dot(jnp.exp2(s - _lanes(m, s.shape[1])).astype(BF16), _ones_ext(vc_ref[:, kv]))
        den = acc[:, LANES:]
        if use_sink:
            den = den + jnp.exp2(sink - m)
        o_ref[:, p * qw:(p + 1) * qw] = _unstack_heads(acc[:, :LANES] / den, dh, nh).astype(BF16)


def _ctx_attention(q, k, v, sink, dims, mixer):
    b, s_len, n_ctx = dims
    n_qh, n_kvh, dh = MIXER_HEADS[mixer]
    n_kvblk = k.shape[1] // LANES
    nh = n_qh // n_kvblk
    ctx_blk = b * s_len // n_ctx
    use_sink = sink is not None
    if not use_sink:
        sink = jnp.zeros((n_qh,), F32)
    return pl.pallas_call(
        functools.partial(_ctx_kernel, dh=dh, nh=nh, use_sink=use_sink),
        grid_spec=pltpu.PrefetchScalarGridSpec(
            num_scalar_prefetch=1,
            grid=(b,),
            in_specs=[pl.BlockSpec((n_ctx, q.shape[1]), lambda i, *_: (ctx_blk + i, 0)),
                      pl.BlockSpec((n_ctx, k.shape[1]), lambda i, *_: (ctx_blk + i, 0)),
                      pl.BlockSpec((n_ctx, v.shape[1]), lambda i, *_: (ctx_blk + i, 0))],
            out_specs=pl.BlockSpec((n_ctx, q.shape[1]), lambda i, *_: (i, 0))),
        out_shape=jax.ShapeDtypeStruct((b * n_ctx, q.shape[1]), BF16),
        compiler_params=_params("arbitrary"),
        name="attn_ctx_m%d" % mixer,
    )(sink, q, k, v)


def _oproj_kernel(*refs, n_lat_tiles, n_y, n_x, moe):
    y = _pick_rows(refs[:n_y], n_lat_tiles)
    x_in = _pick_rows(refs[n_y:n_y + n_x], n_lat_tiles)
    refs = refs[n_y + n_x:]
    mod_ref, n2_ref, wo_ref = refs[:3]
    refs = refs[3:]
    if moe:
        wr_ref, br_ref = refs[:2]
        refs = refs[2:]
    xo_ref, h_ref = refs[:2]
    x = x_in + mod_ref[2:3, :] * _dot(y, wo_ref[...])
    xo_ref[...] = x
    h = _rms_mod(x, n2_ref[...], mod_ref[4:5, :], mod_ref[3:4, :])
    h_ref[...] = h.astype(BF16)
    if moe:
        route_ref, counts_ref, base_ref = refs[2:5]
        h_hi, h_lo = _split_bf16(h)
        w_hi, w_lo = _split_bf16(wr_ref[...])
        hi_terms = _dot(h_hi, jnp.concatenate([w_hi, w_lo], axis=1))
        lo_terms = _dot(h_lo, jnp.concatenate([w_hi, jnp.zeros_like(w_hi)], axis=1))
        logits = hi_terms[:, :LANES] + hi_terms[:, LANES:] + lo_terms[:, :LANES] + br_ref[...]
        lane = lax.broadcasted_iota(jnp.int32, logits.shape, 1)
        v1 = jnp.max(logits, axis=-1, keepdims=True)
        i1 = jnp.min(jnp.where(logits == v1, lane, LANES), axis=-1, keepdims=True)
        rest = jnp.where(lane == i1, NEG_BIG, logits)
        v2 = jnp.max(rest, axis=-1, keepdims=True)
        i2 = jnp.min(jnp.where(rest == v2, lane, LANES), axis=-1, keepdims=True)
        e = jnp.exp(v2 - v1)
        w1 = 1.0 / (1.0 + e)
        w2 = e / (1.0 + e)
        @pl.when(pl.program_id(0) == 0)
        def _():
            base_ref[...] = jnp.zeros_like(base_ref)

        tm = logits.shape[0]
        pick1 = lane == i1
        pick2 = lane == i2
        cnt = jnp.where(pick1, 1.0, 0.0) + jnp.where(pick2, 1.0, 0.0)
        earlier = lax.broadcasted_iota(jnp.int32, (tm, tm), 0) > lax.broadcasted_iota(jnp.int32, (tm, tm), 1)
        before = _dot(jnp.where(earlier, 1.0, 0.0).astype(BF16), cnt.astype(BF16)) + base_ref[0:1, :]
        r1 = jnp.sum(jnp.where(pick1, before, 0.0), axis=-1, keepdims=True)
        r2 = jnp.sum(jnp.where(pick2, before, 0.0), axis=-1, keepdims=True)
        base_ref[...] = base_ref[...] + jnp.sum(cnt, axis=0, keepdims=True)
        counts_ref[...] = base_ref[...]
        route = jnp.where(lane == 0, i1.astype(F32), 0.0)
        route = jnp.where(lane == 1, i2.astype(F32), route)
        route = jnp.where(lane == 2, w1, route)
        route = jnp.where(lane == 3, w2, route)
        route = jnp.where(lane == 4, r1, route)
        route = jnp.where(lane == 5, r2, route)
        route_ref[...] = route


def _oproj(y_lat, y_ctx, xs, mod, norm2, w_o, router, dims, last):
    b, s_len, n_ctx = dims
    x_parts = list(xs) if isinstance(xs, tuple) else [xs]
    d = x_parts[0].shape[1]
    tm = _row_tile(s_len, b * n_ctx, 512)
    n_lat_tiles = b * s_len // tm
    tiles_per_sample = s_len // tm
    n_rows = b * s_len if last else b * (s_len + n_ctx)
    n_tiles = n_rows // tm
    moe = router is not None

    def mod_idx(i):
        return jnp.where(i < n_lat_tiles, i // tiles_per_sample, b)

    row = pl.BlockSpec((tm, d), lambda i: (i, 0))
    y_parts = [y_lat] if last else [y_lat, y_ctx]
    in_specs = ([pl.BlockSpec((tm, y_lat.shape[1]), lambda i: (i, 0))] if last
                else _row_sources(y_parts, tm, n_lat_tiles))
    in_specs += [row] if len(x_parts) == 1 else _row_sources(x_parts, tm, n_lat_tiles)
    in_specs += [pl.BlockSpec((None, 6, d), lambda i: (mod_idx(i), 0, 0)),
                 pl.BlockSpec((1, d), lambda i: (0, 0)),
                 pl.BlockSpec(w_o.shape, lambda i: (0, 0))]
    args = y_parts + x_parts + [mod, norm2.reshape(1, d), w_o.astype(BF16)]
    out_specs = [row, row]
    out_shape = [jax.ShapeDtypeStruct((n_rows, d), F32), jax.ShapeDtypeStruct((n_rows, d), BF16)]
    if moe:
        w_router, b_router = router
        n_e = w_router.shape[1]
        wr = jnp.pad(w_router, ((0, 0), (0, LANES - n_e)))
        br = jnp.pad(b_router, (0, LANES - n_e), constant_values=NEG_BIG).reshape(1, LANES)
        in_specs += [pl.BlockSpec(wr.shape, lambda i: (0, 0)), pl.BlockSpec(br.shape, lambda i: (0, 0))]
        args += [wr, br]
        out_specs += [pl.BlockSpec((tm, LANES), lambda i: (i, 0)), pl.BlockSpec((8, LANES), lambda i: (0, 0))]
        out_shape += [jax.ShapeDtypeStruct((n_rows, LANES), F32), jax.ShapeDtypeStruct((8, LANES), F32)]
    return pl.pallas_call(
        functools.partial(_oproj_kernel, n_lat_tiles=n_lat_tiles, n_y=len(y_parts), n_x=len(x_parts), moe=moe),
        grid=(n_tiles,),
        in_specs=in_specs,
        out_specs=out_specs,
        out_shape=out_shape,
        scratch_shapes=[pltpu.VMEM((8, LANES), F32)] if moe else [],
        compiler_params=_params("arbitrary"),
        name="oproj_moe" if moe else "oproj",
    )(*args)


def _swiglu_partial(h_ref, wg_ref, wu_ref, wd_ref):
    h = h_ref[...]
    a = _silu(_dot(h, wg_ref[...].astype(BF16))) * _dot(h, wu_ref[...].astype(BF16))
    return _dot(a.astype(BF16), wd_ref[...].astype(BF16))


def _accumulate_over_hidden(f, n_f, acc_ref, partial, finish):
    @pl.when(f == 0)
    def _():
        acc_ref[...] = partial()

    @pl.when(jnp.logical_and(f > 0, f < n_f - 1))
    def _():
        acc_ref[...] += partial()

    @pl.when(f == n_f - 1)
    def _():
        finish(acc_ref[...] + partial())


def _ffn_kernel(h_ref, wg_ref, wu_ref, wd_ref, x_ref, mod_ref, o_ref, acc_ref):
    def finish(total):
        o_ref[...] = x_ref[...] + mod_ref[5:6, :] * total

    _accumulate_over_hidden(pl.program_id(1), pl.num_programs(1), acc_ref,
                            functools.partial(_swiglu_partial, h_ref, wg_ref, wu_ref, wd_ref), finish)


def _ffn(h, xs, mod, w_in, w_out, dims):
    b, s_len, n_ctx = dims
    n_rows, d = h.shape
    d_ff = w_out.shape[0]
    tm = _row_tile(s_len, b * n_ctx, 1024)
    tf = FFN_TF
    n_f = d_ff // tf
    assert d_ff % tf == 0 and n_f >= 2 and n_rows % tm == 0
    n_lat_tiles = b * s_len // tm
    tiles_per_sample = s_len // tm

    def mod_idx(i):
        return jnp.where(i < n_lat_tiles, i // tiles_per_sample, b)

    w_in = _to_bf16(w_in[None])[0]
    w_out = _to_bf16(w_out[None])[0]
    return pl.pallas_call(
        _ffn_kernel,
        grid=(n_rows // tm, n_f),
        in_specs=[pl.BlockSpec((tm, d), lambda i, f: (i, 0)),
                  pl.BlockSpec((d, tf), lambda i, f: (0, f)),
                  pl.BlockSpec((d, tf), lambda i, f: (0, n_f + f)),
                  pl.BlockSpec((tf, d), lambda i, f: (f, 0)),
                  pl.BlockSpec((tm, d), lambda i, f: (i, 0)),
                  pl.BlockSpec((None, 6, d), lambda i, f: (mod_idx(i), 0, 0))],
        out_specs=pl.BlockSpec((tm, d), lambda i, f: (i, 0)),
        out_shape=jax.ShapeDtypeStruct((n_rows, d), F32),
        scratch_shapes=[pltpu.VMEM((tm, d), F32)],
        compiler_params=_params("arbitrary", "arbitrary"),
        name="ffn_dense",
    )(h, w_in, w_in, w_out, xs, mod)


def _moe_ffn_kernel(te_ref, na_ref, h_ref, wg_ref, wu_ref, wd_ref, prev_ref, o_ref, acc_ref):
    del prev_ref
    i, f = pl.program_id(0), pl.program_id(1)

    def finish(total):
        o_ref[...] = total.astype(o_ref.dtype)

    @pl.when(i < na_ref[0])
    def _():
        _accumulate_over_hidden(f, pl.num_programs(1), acc_ref,
                                functools.partial(_swiglu_partial, h_ref, wg_ref, wu_ref, wd_ref), finish)


def _moe_ffn(hs, ys, n_rows_total, first_tile, tile_expert, n_active, w_in, w_out):
    assert ys.shape == (n_rows_total, hs.shape[1])
    n_rows, d = hs.shape
    d_ff = w_out.shape[1]
    tm, tf = MOE_TM, FFN_TF
    n_f = d_ff // tf
    assert d_ff % tf == 0 and n_f >= 2 and n_rows % tm == 0

    def tile(i, na):
        return jnp.minimum(i, jnp.maximum(na[0] - 1, 0))

    def expert(i, te, na):
        return te[tile(i, na)]

    in_specs = [pl.BlockSpec((tm, d), lambda i, f, te, na: (tile(i, na), 0)),
                pl.BlockSpec((None, d, tf), lambda i, f, te, na: (expert(i, te, na), 0, f)),
                pl.BlockSpec((None, d, tf), lambda i, f, te, na: (expert(i, te, na), 0, n_f + f)),
                pl.BlockSpec((None, tf, d), lambda i, f, te, na: (expert(i, te, na), f, 0))]
    in_specs.append(pl.BlockSpec(memory_space=pl.ANY))
    args = [tile_expert, n_active, hs, w_in, w_in, w_out, ys]
    aliases = {len(args) - 1: 0}
    return pl.pallas_call(
        _moe_ffn_kernel,
        grid_spec=pltpu.PrefetchScalarGridSpec(
            num_scalar_prefetch=2,
            grid=(n_rows // tm, n_f),
            in_specs=in_specs,
            out_specs=pl.BlockSpec((tm, d), lambda i, f, te, na: (first_tile + tile(i, na), 0)),
            scratch_shapes=[pltpu.VMEM((tm, d), F32)]),
        out_shape=jax.ShapeDtypeStruct((n_rows_total, d), hs.dtype),
        input_output_aliases=aliases,
        compiler_params=_params("arbitrary", "arbitrary"),
        name="moe_ffn",
    )(*args)


def _combine_kernel(x_ref, a_ref, b_ref, route_ref, mod_ref, o_ref):
    w1 = route_ref[:, 2:3]
    w2 = route_ref[:, 3:4]
    mix = w1 * a_ref[...].astype(F32) + w2 * b_ref[...].astype(F32)
    o_ref[...] = x_ref[...] + mod_ref[5:6, :] * mix


def _combine(xs, ya, yb, route, mod, dims):
    b, s_len, n_ctx = dims
    n_rows, d = ya.shape
    tm = _row_tile(s_len, b * n_ctx, 512)
    n_lat_tiles = b * s_len // tm
    tiles_per_sample = s_len // tm

    def mod_idx(i):
        return jnp.where(i < n_lat_tiles, i // tiles_per_sample, b)

    row = pl.BlockSpec((tm, d), lambda i: (i, 0))
    return pl.pallas_call(
        _combine_kernel,
        grid=(n_rows // tm,),
        in_specs=[row, row, row,
                  pl.BlockSpec((tm, LANES), lambda i: (i, 0)),
                  pl.BlockSpec((None, 6, d), lambda i: (mod_idx(i), 0, 0))],
        out_specs=row,
        out_shape=jax.ShapeDtypeStruct((n_rows, d), F32),
        compiler_params=_params("arbitrary"),
        name="moe_combine",
    )(xs, ya, yb, route, mod)


def _route_plan(idx, rank, counts, tm):
    n = idx.shape[0]
    tiles_per = (counts + tm - 1) // tm
    tile_end = jnp.cumsum(tiles_per)
    tile_start = tile_end - tiles_per
    experts = jnp.arange(N_EXPERTS, dtype=jnp.int32)
    start = jnp.sum(jnp.where(idx[:, :, None] == experts, tile_start, 0), axis=-1)
    slot = start * tm + rank
    n_tiles = (2 * n) // tm + N_EXPERTS
    tile_ids = jnp.arange(n_tiles, dtype=jnp.int32)
    tile_expert = jnp.minimum(jnp.sum((tile_end[None, :] <= tile_ids[:, None]).astype(jnp.int32), axis=1),
                              N_EXPERTS - 1)
    assert n_tiles * tm * n < 2 ** 32
    key = slot.reshape(-1).astype(jnp.uint32) * jnp.uint32(n) + jnp.arange(2 * n, dtype=jnp.uint32) // 2
    token_sorted = (lax.sort(key) % jnp.uint32(n)).astype(jnp.int32)
    first_pair = jnp.cumsum(counts) - counts
    row_in_expert = ((tile_ids - tile_start[tile_expert]) * tm)[:, None] + jnp.arange(tm, dtype=jnp.int32)[None, :]
    pair = jnp.clip(first_pair[tile_expert][:, None] + row_in_expert, 0, 2 * n - 1)
    filler = (tile_ids[:, None] * tm + jnp.arange(tm, dtype=jnp.int32)[None, :]) % n
    src = jnp.where(row_in_expert < counts[tile_expert][:, None], token_sorted[pair], filler).reshape(-1)
    return slot, src, tile_expert, tile_end[-1:].astype(jnp.int32)


def _cast_kernel(w_ref, o_ref):
    o_ref[...] = w_ref[...].astype(o_ref.dtype)


def _to_bf16(w):
    n_e, n_r, n_c = w.shape
    rows = n_r
    while rows * n_c * 4 > CAST_BLOCK_BYTES and rows % 2 == 0 and (rows // 2) % 16 == 0:
        rows //= 2
    return pl.pallas_call(
        _cast_kernel,
        grid=(n_e, n_r // rows),
        in_specs=[pl.BlockSpec((None, rows, n_c), lambda e, r: (e, r, 0))],
        out_specs=pl.BlockSpec((None, rows, n_c), lambda e, r: (e, r, 0)),
        out_shape=jax.ShapeDtypeStruct(w.shape, BF16),
        compiler_params=_params("arbitrary", "arbitrary"),
        name="cast_bf16",
    )(w)


def _take_rows(a, rows):
    return a.at[rows].get(mode="promise_in_bounds")


def _moe(h, xs, route, counts, mod, w_exp_in, w_exp_out, dims):
    idx = route[:, 0:2].astype(jnp.int32)
    rank = route[:, 4:6].astype(jnp.int32)
    slot, src, tile_expert, n_active = _route_plan(idx, rank, counts[0, :N_EXPERTS].astype(jnp.int32), MOE_TM)
    n_tiles = src.shape[0] // MOE_TM
    first = max(1, n_tiles // MOE_FIRST_CHUNK_DIV)
    rest = -(-(n_tiles - first) // (MOE_CHUNKS - 1))
    bounds = [0] + [min(first + j * rest, n_tiles) for j in range(MOE_CHUNKS)]
    ys = jnp.zeros((src.shape[0], h.shape[1]), h.dtype)
    for t0, t1 in zip(bounds[:-1], bounds[1:]):
        if t1 == t0:
            continue
        hs = _take_rows(h, src[t0 * MOE_TM:t1 * MOE_TM])
        ys = _moe_ffn(hs, ys, src.shape[0], t0, tile_expert[t0:t1], jnp.clip(n_active - t0, 0, t1 - t0),
                      w_exp_in, w_exp_out)
    ya = _take_rows(ys, slot[:, 0])
    yb = _take_rows(ys, slot[:, 1])
    return _combine(xs, ya, yb, route, mod, dims)


def _layer(xs, cond, p, dims, mixer, last):
    b, s_len, n_ctx = dims
    mod = _adaln(cond, p["w_mod"], p["b_mod"])
    q, k, v = _qkv(xs, mod, p["norm1"], p["w_qkv"], p["q_norm"], p["k_norm"], dims, mixer)
    if mixer == 0:
        y_lat = _na_attention(q, k, v, p["rel_bias"], dims)
    elif mixer == 1:
        y_lat = _swa_attention(q, k, v, p["sink"], dims)
    else:
        y_lat = _global_attention(q, k, v, dims)
    y_ctx = None if last else _ctx_attention(q, k, v, p.get("sink"), dims, mixer)
    router = (p["w_router"], p["b_router"]) if "w_router" in p else None
    outs = _oproj(y_lat, y_ctx, xs, mod, p["norm2"], p["w_o"], router, dims, last)
    if router is None:
        xs, h = outs
        return _ffn(h, xs, mod, p["w_ffn_in"], p["w_ffn_out"], dims)
    xs, h, route, counts = outs
    return _moe(h, xs, route, counts, mod, p["w_exp_in"], p["w_exp_out"], dims)


def kernel(x, c, ctx, c_ctx, l0_w_mod, l0_b_mod, l0_norm1, l0_norm2, l0_w_qkv, l0_q_norm, l0_k_norm, l0_rel_bias, l0_w_o, l0_w_ffn_in, l0_w_ffn_out, l1_w_mod, l1_b_mod, l1_norm1, l1_norm2, l1_w_qkv, l1_q_norm, l1_k_norm, l1_sink, l1_w_o, l1_w_router, l1_b_router, l1_w_exp_in, l1_w_exp_out, l2_w_mod, l2_b_mod, l2_norm1, l2_norm2, l2_w_qkv, l2_q_norm, l2_k_norm, l2_w_o, l2_w_ffn_in, l2_w_ffn_out, l3_w_mod, l3_b_mod, l3_norm1, l3_norm2, l3_w_qkv, l3_q_norm, l3_k_norm, l3_rel_bias, l3_w_o, l3_w_router, l3_b_router, l3_w_exp_in, l3_w_exp_out):
    b, s_len, d = x.shape
    n_ctx = ctx.shape[1]
    dims = (b, s_len, n_ctx)
    layers = (
        dict(w_mod=l0_w_mod, b_mod=l0_b_mod, norm1=l0_norm1, norm2=l0_norm2, w_qkv=l0_w_qkv, q_norm=l0_q_norm,
             k_norm=l0_k_norm, rel_bias=l0_rel_bias, w_o=l0_w_o, w_ffn_in=l0_w_ffn_in, w_ffn_out=l0_w_ffn_out),
        dict(w_mod=l1_w_mod, b_mod=l1_b_mod, norm1=l1_norm1, norm2=l1_norm2, w_qkv=l1_w_qkv, q_norm=l1_q_norm,
             k_norm=l1_k_norm, sink=l1_sink, w_o=l1_w_o, w_router=l1_w_router, b_router=l1_b_router,
             w_exp_in=l1_w_exp_in, w_exp_out=l1_w_exp_out),
        dict(w_mod=l2_w_mod, b_mod=l2_b_mod, norm1=l2_norm1, norm2=l2_norm2, w_qkv=l2_w_qkv, q_norm=l2_q_norm,
             k_norm=l2_k_norm, w_o=l2_w_o, w_ffn_in=l2_w_ffn_in, w_ffn_out=l2_w_ffn_out),
        dict(w_mod=l3_w_mod, b_mod=l3_b_mod, norm1=l3_norm1, norm2=l3_norm2, w_qkv=l3_w_qkv, q_norm=l3_q_norm,
             k_norm=l3_k_norm, rel_bias=l3_rel_bias, w_o=l3_w_o, w_router=l3_w_router, b_router=l3_b_router,
             w_exp_in=l3_w_exp_in, w_exp_out=l3_w_exp_out),
    )
    xs = (x.reshape(b * s_len, d), ctx.reshape(b * n_ctx, d))
    pad_rows = -(b + 1) % 8
    cond = jnp.concatenate([c, c_ctx[None, :], jnp.zeros((pad_rows, d), F32)], axis=0)
    n_layers = len(layers)
    for i, p in enumerate(layers):
        xs = _layer(xs, cond, p, dims, i % 3, i == n_layers - 1)
    return xs.reshape(b, s_len, d)
```
